```python
import math
import jax, jax.numpy as jnp
from jax import lax
import numpy as np

D_MODEL = 2048
BATCH = 8
SEQ = 8192
DEPTH = 4

GRID_W = 64
CTX_LEN = 256
N_MIXERS = 2
N_SSD_LAYERS = (DEPTH + 1) // 2
N_ATTN_LAYERS = DEPTH // 2

SSD_EXPAND = 2
SSD_D_INNER = SSD_EXPAND * D_MODEL
SSD_HEAD_DIM = 64
SSD_HEADS = SSD_D_INNER // SSD_HEAD_DIM
SSD_GROUPS = 8
SSD_STATE = 128
SSD_CHUNK = 128
SSD_CONV = 3
SSD_XBC = SSD_D_INNER + 2 * SSD_GROUPS * SSD_STATE
SSD_IN = SSD_D_INNER + SSD_XBC + 2 * SSD_HEADS

ATTN_HEAD_DIM = 128
ATTN_HEADS = D_MODEL // ATTN_HEAD_DIM
ATTN_KV_HEADS = 4
ATTN_GROUP = ATTN_HEADS // ATTN_KV_HEADS
ATTN_Q_DIM = ATTN_HEADS * ATTN_HEAD_DIM
ATTN_KV_DIM = ATTN_KV_HEADS * ATTN_HEAD_DIM
ATTN_QKV = ATTN_Q_DIM + 2 * ATTN_KV_DIM
ATTN_WINDOW = 128
ATTN_BLOCK = 128
ROPE_THETA = 10000.0
ROPE_AXIS_DIM = ATTN_HEAD_DIM // 2

D_FF = 5632
FFN_CONV = 3

NORM_EPS = 1e-6

kernel_name = "hybrid_ssd_swa_convffn_prefix_dit"


def _rms_norm(x, w):
    xf = x.astype(jnp.float32)
    y = xf * lax.rsqrt(jnp.mean(xf * xf, axis=-1, keepdims=True) + NORM_EPS)
    return y.astype(x.dtype) * w


def _modulation(cvec, w, b):
    m = jax.nn.silu(cvec) @ w + b
    return jnp.split(m, 6, axis=-1)


def _modulate(h, shift, scale):
    return h * (1.0 + scale) + shift


def _dwconv(x, w, b):
    k = w.shape[0]
    pad = k // 2
    y = lax.conv_general_dilated(x, w[:, None, :], window_strides=(1,), padding=[(pad, pad)],
                                 dimension_numbers=('NWC', 'WIO', 'NWC'),
                                 feature_group_count=x.shape[-1])
    return y + b


def _ssd_chunked(x, dA, B, C, h0):
    b, l, _, p = x.shape
    g, n, e = SSD_GROUPS, SSD_STATE, SSD_HEADS // SSD_GROUPS
    nc = l // SSD_CHUNK
    xc = x.reshape(b, nc, SSD_CHUNK, g, e, p)
    Bc = B.reshape(b, nc, SSD_CHUNK, g, n)
    Cc = C.reshape(b, nc, SSD_CHUNK, g, n)
    a_cs = jnp.cumsum(dA.reshape(b, nc, SSD_CHUNK, g, e), axis=2)
    lower = jnp.tril(jnp.ones((SSD_CHUNK, SSD_CHUNK), dtype=bool))[None, None, :, :, None, None]
    diff = a_cs[:, :, :, None] - a_cs[:, :, None, :]
    seg = jnp.exp(jnp.where(lower, diff, -jnp.inf))
    cb = jnp.einsum('bclgn,bcsgn->bclsg', Cc, Bc)
    w = (cb[..., None] * seg).astype(x.dtype)
    y_diag = jnp.einsum('bclsge,bcsgep->bclgep', w, xc)
    decay_to_end = jnp.exp(a_cs[:, :, -1:] - a_cs).astype(x.dtype)
    states = jnp.einsum('bclgn,bclgep->bcgepn', Bc, xc * decay_to_end[..., None])
    chunk_decay = jnp.exp(a_cs[:, :, -1]).astype(states.dtype)
    if h0 is None:
        h0 = jnp.zeros((b, g, e, p, n), states.dtype)

    def step(h, inp):
        dec, st = inp
        return dec[..., None, None] * h + st, h

    h_last, h_in = lax.scan(step, h0.astype(states.dtype),
                            (jnp.moveaxis(chunk_decay, 1, 0), jnp.moveaxis(states, 1, 0)))
    h_in = jnp.moveaxis(h_in, 0, 1)
    decay_from_start = jnp.exp(a_cs).astype(x.dtype)
    y_off = jnp.einsum('bclgn,bcgepn->bclgep', Cc, h_in) * decay_from_start[..., None]
    y = (y_diag + y_off).reshape(b, l, g * e, p)
    return y, h_last


def _ssd_project(h, w_in, conv_w, conv_b):
    b, l, _ = h.shape
    zxbcdt = h @ w_in
    o1 = SSD_D_INNER
    o2 = o1 + SSD_XBC
    o3 = o2 + SSD_HEADS
    z = zxbcdt[..., :o1]
    xbc = jax.nn.silu(_dwconv(zxbcdt[..., o1:o2], conv_w, conv_b))
    dt_f = zxbcdt[..., o2:o3]
    dt_b = zxbcdt[..., o3:]
    gn = SSD_GROUPS * SSD_STATE
    xs = xbc[..., :SSD_D_INNER].reshape(b, l, SSD_HEADS, SSD_HEAD_DIM)
    B = xbc[..., SSD_D_INNER:SSD_D_INNER + gn].reshape(b, l, SSD_GROUPS, SSD_STATE)
    C = xbc[..., SSD_D_INNER + gn:].reshape(b, l, SSD_GROUPS, SSD_STATE)
    return z, xs, B, C, dt_f, dt_b


def _ssd_direction(xs, B, C, dt_raw, dt_bias, a_log, h0, reverse):
    dt = jax.nn.softplus((dt_raw + dt_bias).astype(jnp.float32))
    dA = -jnp.exp(a_log.astype(jnp.float32)) * dt
    xdt = xs * dt[..., None].astype(xs.dtype)
    if reverse:
        xdt, dA, B, C = (jnp.flip(t, axis=1) for t in (xdt, dA, B, C))
    y, h_last = _ssd_chunked(xdt, dA, B, C, h0)
    if reverse:
        y = jnp.flip(y, axis=1)
    return y, h_last


def _ssd_mixer(h_lat, h_ctx, w_in, conv_w, conv_b, dt_bias_f, dt_bias_b, a_log_f, a_log_b,
               d_skip, norm_w, w_out, need_ctx_out):
    z_c, xs_c, B_c, C_c, dtf_c, dtb_c = _ssd_project(h_ctx, w_in, conv_w, conv_b)
    yf_c, hf_c = _ssd_direction(xs_c, B_c, C_c, dtf_c, dt_bias_f, a_log_f, None, False)
    yb_c, hb_c = _ssd_direction(xs_c, B_c, C_c, dtb_c, dt_bias_b, a_log_b, None, True)
    z, xs, B, C, dtf, dtb = _ssd_project(h_lat, w_in, conv_w, conv_b)
    yf, _ = _ssd_direction(xs, B, C, dtf, dt_bias_f, a_log_f, hf_c, False)
    yb, _ = _ssd_direction(xs, B, C, dtb, dt_bias_b, a_log_b, hb_c, True)

    def finish(y_f, y_b, x_s, zz):
        y = (y_f + y_b + x_s * d_skip[:, None]).reshape(zz.shape)
        return _rms_norm(y * jax.nn.silu(zz), norm_w) @ w_out

    out_lat = finish(yf, yb, xs, z)
    out_ctx = finish(yf_c, yb_c, xs_c, z_c) if need_ctx_out else None
    return out_lat, out_ctx


def _axial_angles(rows):
    row = jnp.repeat(jnp.arange(rows, dtype=jnp.float32), GRID_W)
    col = jnp.tile(jnp.arange(GRID_W, dtype=jnp.float32), rows)
    inv_freq = ROPE_THETA ** (-jnp.arange(0, ROPE_AXIS_DIM, 2, dtype=jnp.float32) / ROPE_AXIS_DIM)
    return row[:, None] * inv_freq[None, :], col[:, None] * inv_freq[None, :]


def _rope_half(x, ang):
    m = ang.shape[-1]
    shape = (ang.shape[0],) + (1,) * (x.ndim - 3) + (m,)
    cos = jnp.cos(ang).reshape(shape).astype(x.dtype)
    sin = jnp.sin(ang).reshape(shape).astype(x.dtype)
    x1, x2 = x[..., :m], x[..., m:]
    return jnp.concatenate([x1 * cos - x2 * sin, x2 * cos + x1 * sin], axis=-1)


def _rope_2d(x, ang_row, ang_col):
    return jnp.concatenate([_rope_half(x[..., :ROPE_AXIS_DIM], ang_row),
                            _rope_half(x[..., ROPE_AXIS_DIM:], ang_col)], axis=-1)


def _attn_qkv(h, w_qkv, q_gain, k_gain):
    b, l, _ = h.shape
    qkv = h @ w_qkv
    q = qkv[..., :ATTN_Q_DIM].reshape(b, l, ATTN_KV_HEADS, ATTN_GROUP, ATTN_HEAD_DIM)
    k = qkv[..., ATTN_Q_DIM:ATTN_Q_DIM + ATTN_KV_DIM].reshape(b, l, ATTN_KV_HEADS, ATTN_HEAD_DIM)
    v = qkv[..., ATTN_Q_DIM + ATTN_KV_DIM:].reshape(b, l, ATTN_KV_HEADS, ATTN_HEAD_DIM)
    return _rms_norm(q, q_gain), _rms_norm(k, k_gain), v


def _attn_mixer(h_lat, h_ctx, w_qkv, q_gain, k_gain, sinks, w_o, ang_row, ang_col, need_ctx_out):
    scale = ATTN_HEAD_DIM ** -0.5
    q_c, k_c, v_c = _attn_qkv(h_ctx, w_qkv, q_gain, k_gain)
    q, k, v = _attn_qkv(h_lat, w_qkv, q_gain, k_gain)
    q = _rope_2d(q, ang_row, ang_col)
    k = _rope_2d(k, ang_row, ang_col)
    b, n = q.shape[:2]
    lc = k_c.shape[1]
    nb = n // ATTN_BLOCK
    sink = sinks.astype(jnp.float32).reshape(ATTN_KV_HEADS, ATTN_GROUP)

    qb = q.reshape(b, nb, ATTN_BLOCK, ATTN_KV_HEADS, ATTN_GROUP, ATTN_HEAD_DIM)

    def band(t):
        tb = t.reshape(b, nb, ATTN_BLOCK, ATTN_KV_HEADS, ATTN_HEAD_DIM)
        tp = jnp.pad(tb, ((0, 0), (1, 1), (0, 0), (0, 0), (0, 0)))
        return jnp.concatenate([tp[:, :-2], tp[:, 1:-1], tp[:, 2:]], axis=2)

    k_band, v_band = band(k), band(v)
    qi = jnp.arange(ATTN_BLOCK)[:, None]
    kj = jnp.arange(3 * ATTN_BLOCK)[None, :]
    kpos = (jnp.arange(nb)[:, None, None] - 1) * ATTN_BLOCK + kj[None]
    band_mask = (jnp.abs(kj - ATTN_BLOCK - qi) <= ATTN_WINDOW)[None] & (kpos >= 0) & (kpos < n)

    s_ctx = jnp.einsum('bkqhgd,bshd->bkhgqs', qb, k_c).astype(jnp.float32) * scale
    s_band = jnp.einsum('bkqhgd,bkshd->bkhgqs', qb, k_band).astype(jnp.float32) * scale
    s_band = jnp.where(band_mask[None, :, None, None], s_band, -jnp.inf)
    s_sink = jnp.broadcast_to(sink[None, None, :, :, None, None], s_band.shape[:-1] + (1,))
    p = jax.nn.softmax(jnp.concatenate([s_ctx, s_band, s_sink], axis=-1), axis=-1)
    p_ctx = p[..., :lc].astype(v.dtype)
    p_band = p[..., lc:-1].astype(v.dtype)
    o = (jnp.einsum('bkhgqs,bshd->bkqhgd', p_ctx, v_c)
         + jnp.einsum('bkhgqs,bkshd->bkqhgd', p_band, v_band))
    out_lat = o.reshape(b, n, ATTN_Q_DIM) @ w_o

    if need_ctx_out:
        s = jnp.einsum('bqhgd,bshd->bhgqs', q_c, k_c).astype(jnp.float32) * scale
        s_sk = jnp.broadcast_to(sink[None, :, :, None, None], s.shape[:-1] + (1,))
        pc = jax.nn.softmax(jnp.concatenate([s, s_sk], axis=-1), axis=-1)[..., :-1].astype(v_c.dtype)
        oc = jnp.einsum('bhgqs,bshd->bqhgd', pc, v_c)
        out_ctx = oc.reshape(b, lc, ATTN_Q_DIM) @ w_o
    else:
        out_ctx = None
    return out_lat, out_ctx


def _conv_ffn(h, w_up, conv_w, conv_b, w_down):
    u = _dwconv(h @ w_up, conv_w, conv_b)
    val, gate = u[..., :D_FF], u[..., D_FF:]
    return (jax.nn.silu(gate) * val) @ w_down


def _fwd_setup_inputs(seed: int = 0) -> dict:
    key = jax.random.key(seed)
    ks = jax.random.split(key, 32)
    f32 = jnp.float32

    def nrm(k, shape, s):
        return jax.random.normal(k, shape, f32) * s

    L, S, A = DEPTH, N_SSD_LAYERS, N_ATTN_LAYERS

    def dt_bias(k):
        dt = jnp.exp(jax.random.uniform(k, (S, SSD_HEADS), f32, math.log(1e-3), math.log(1e-1)))
        return dt + jnp.log(-jnp.expm1(-dt))

    return {
        "x": nrm(ks[0], (BATCH, SEQ, D_MODEL), 1.0),
        "c": nrm(ks[1], (BATCH, D_MODEL), 1.0),
        "ctx": nrm(ks[2], (BATCH, CTX_LEN, D_MODEL), 1.0),
        "c_ctx": nrm(ks[3], (D_MODEL,), 1.0),
        "ada_w": nrm(ks[4], (L, D_MODEL, 6 * D_MODEL), 0.5 * D_MODEL ** -0.5),
        "ada_b": nrm(ks[5], (L, 6 * D_MODEL), 0.01),
        "norm1_w": 1.0 + nrm(ks[6], (L, D_MODEL), 0.05),
        "norm2_w": 1.0 + nrm(ks[7], (L, D_MODEL), 0.05),
        "ssd_w_in": nrm(ks[8], (S, D_MODEL, SSD_IN), D_MODEL ** -0.5),
        "ssd_conv_w": nrm(ks[9], (S, SSD_CONV, SSD_XBC), SSD_CONV ** -0.5),
        "ssd_conv_b": nrm(ks[10], (S, SSD_XBC), 0.01),
        "ssd_dt_bias_f": dt_bias(ks[11]),
        "ssd_dt_bias_b": dt_bias(ks[12]),
        "ssd_a_log_f": jnp.log(jax.random.uniform(ks[13], (S, SSD_HEADS), f32, 1.0, 16.0)),
        "ssd_a_log_b": jnp.log(jax.random.uniform(ks[14], (S, SSD_HEADS), f32, 1.0, 16.0)),
        "ssd_d": 1.0 + nrm(ks[15], (S, SSD_HEADS), 0.1),
        "ssd_norm_w": 1.0 + nrm(ks[16], (S, SSD_D_INNER), 0.05),
        "ssd_w_out": nrm(ks[17], (S, SSD_D_INNER, D_MODEL), SSD_D_INNER ** -0.5),
        "attn_w_qkv": nrm(ks[18], (A, D_MODEL, ATTN_QKV), D_MODEL ** -0.5),
        "attn_q_gain": 1.0 + nrm(ks[19], (A, ATTN_HEAD_DIM), 0.05),
        "attn_k_gain": 1.0 + nrm(ks[20], (A, ATTN_HEAD_DIM), 0.05),
        "attn_sinks": nrm(ks[21], (A, ATTN_HEADS), 0.5),
        "attn_w_o": nrm(ks[22], (A, ATTN_Q_DIM, D_MODEL), ATTN_Q_DIM ** -0.5),
        "ffn_w_up": nrm(ks[23], (L, D_MODEL, 2 * D_FF), D_MODEL ** -0.5),
        "ffn_conv_w": nrm(ks[24], (L, FFN_CONV, 2 * D_FF), FFN_CONV ** -0.5),
        "ffn_conv_b": nrm(ks[25], (L, 2 * D_FF), 0.01),
        "ffn_w_down": nrm(ks[26], (L, D_FF, D_MODEL), D_FF ** -0.5),
    }


def _fwd_reference(x, c, ctx, c_ctx, ada_w, ada_b, norm1_w, norm2_w, ssd_w_in, ssd_conv_w, ssd_conv_b,
              ssd_dt_bias_f, ssd_dt_bias_b, ssd_a_log_f, ssd_a_log_b, ssd_d, ssd_norm_w, ssd_w_out,
              attn_w_qkv, attn_q_gain, attn_k_gain, attn_sinks, attn_w_o,
              ffn_w_up, ffn_conv_w, ffn_conv_b, ffn_w_down):
    n = x.shape[1]
    rows = n // GRID_W
    ang_row, ang_col = _axial_angles(rows)
    xc = ctx
    for i in range(DEPTH):
        last = i == DEPTH - 1
        sh1, sc1, g1, sh2, sc2, g2 = [t[:, None, :] for t in _modulation(c, ada_w[i], ada_b[i])]
        csh1, csc1, cg1, csh2, csc2, cg2 = _modulation(c_ctx, ada_w[i], ada_b[i])
        h = _modulate(_rms_norm(x, norm1_w[i]), sh1, sc1)
        hc = _modulate(_rms_norm(xc, norm1_w[i]), csh1, csc1)
        j = i // N_MIXERS
        if i % N_MIXERS == 0:
            mix, mix_c = _ssd_mixer(h, hc, ssd_w_in[j], ssd_conv_w[j], ssd_conv_b[j],
                                    ssd_dt_bias_f[j], ssd_dt_bias_b[j], ssd_a_log_f[j], ssd_a_log_b[j],
                                    ssd_d[j], ssd_norm_w[j], ssd_w_out[j], not last)
        else:
            mix, mix_c = _attn_mixer(h, hc, attn_w_qkv[j], attn_q_gain[j], attn_k_gain[j],
                                     attn_sinks[j], attn_w_o[j], ang_row, ang_col, not last)
        x = x + g1 * mix
        h = _modulate(_rms_norm(x, norm2_w[i]), sh2, sc2)
        x = x + g2 * _conv_ffn(h, ffn_w_up[i], ffn_conv_w[i], ffn_conv_b[i], ffn_w_down[i])
        if not last:
            xc = xc + cg1 * mix_c
            hc = _modulate(_rms_norm(xc, norm2_w[i]), csh2, csc2)
            xc = xc + cg2 * _conv_ffn(hc, ffn_w_up[i], ffn_conv_w[i], ffn_conv_b[i], ffn_w_down[i])
    return x


import jax as _jax
import jax.numpy as _jnp

TWIN_FORMAT = 'train_step'
FWD_PARAMS = ['x', 'c', 'ctx', 'c_ctx', 'ada_w', 'ada_b', 'norm1_w', 'norm2_w', 'ssd_w_in', 'ssd_conv_w', 'ssd_conv_b', 'ssd_dt_bias_f', 'ssd_dt_bias_b', 'ssd_a_log_f', 'ssd_a_log_b', 'ssd_d', 'ssd_norm_w', 'ssd_w_out', 'attn_w_qkv', 'attn_q_gain', 'attn_k_gain', 'attn_sinks', 'attn_w_o', 'ffn_w_up', 'ffn_conv_w', 'ffn_conv_b', 'ffn_w_down']
TWIN_WEIGHTS = ['c_ctx', 'ada_w', 'ada_b', 'norm1_w', 'norm2_w', 'ssd_w_in', 'ssd_conv_w', 'ssd_conv_b', 'ssd_dt_bias_f', 'ssd_dt_bias_b', 'ssd_a_log_f', 'ssd_a_log_b', 'ssd_d', 'ssd_norm_w', 'ssd_w_out', 'attn_w_qkv', 'attn_q_gain', 'attn_k_gain', 'attn_sinks', 'attn_w_o', 'ffn_w_up', 'ffn_conv_w', 'ffn_conv_b', 'ffn_w_down']
TWIN_DIFF_INPUT = 'x'
TWIN_INPUTS = ['x', 'c', 'ctx', 'c_ctx', 'ada_w', 'ada_b', 'norm1_w', 'norm2_w', 'ssd_w_in', 'ssd_conv_w', 'ssd_conv_b', 'ssd_dt_bias_f', 'ssd_dt_bias_b', 'ssd_a_log_f', 'ssd_a_log_b', 'ssd_d', 'ssd_norm_w', 'ssd_w_out', 'attn_w_qkv', 'attn_q_gain', 'attn_k_gain', 'attn_sinks', 'attn_w_o', 'ffn_w_up', 'ffn_conv_w', 'ffn_conv_b', 'ffn_w_down', 'loss_target', 'm_c_ctx', 'm_ada_w', 'm_ada_b', 'm_norm1_w', 'm_norm2_w', 'm_ssd_w_in', 'm_ssd_conv_w', 'm_ssd_conv_b', 'm_ssd_dt_bias_f', 'm_ssd_dt_bias_b', 'm_ssd_a_log_f', 'm_ssd_a_log_b', 'm_ssd_d', 'm_ssd_norm_w', 'm_ssd_w_out', 'm_attn_w_qkv', 'm_attn_q_gain', 'm_attn_k_gain', 'm_attn_sinks', 'm_attn_w_o', 'm_ffn_w_up', 'm_ffn_conv_w', 'm_ffn_conv_b', 'm_ffn_w_down', 'v_c_ctx', 'v_ada_w', 'v_ada_b', 'v_norm1_w', 'v_norm2_w', 'v_ssd_w_in', 'v_ssd_conv_w', 'v_ssd_conv_b', 'v_ssd_dt_bias_f', 'v_ssd_dt_bias_b', 'v_ssd_a_log_f', 'v_ssd_a_log_b', 'v_ssd_d', 'v_ssd_norm_w', 'v_ssd_w_out', 'v_attn_w_qkv', 'v_attn_q_gain', 'v_attn_k_gain', 'v_attn_sinks', 'v_attn_w_o', 'v_ffn_w_up', 'v_ffn_conv_w', 'v_ffn_conv_b', 'v_ffn_w_down']
TWIN_OUTPUTS = ['loss', 'grad_x', 'grad_c_ctx', 'grad_ada_w', 'grad_ada_b', 'grad_norm1_w', 'grad_norm2_w', 'grad_ssd_w_in', 'grad_ssd_conv_w', 'grad_ssd_conv_b', 'grad_ssd_dt_bias_f', 'grad_ssd_dt_bias_b', 'grad_ssd_a_log_f', 'grad_ssd_a_log_b', 'grad_ssd_d', 'grad_ssd_norm_w', 'grad_ssd_w_out', 'grad_attn_w_qkv', 'grad_attn_q_gain', 'grad_attn_k_gain', 'grad_attn_sinks', 'grad_attn_w_o', 'grad_ffn_w_up', 'grad_ffn_conv_w', 'grad_ffn_conv_b', 'grad_ffn_w_down', 'delta_c_ctx', 'delta_ada_w', 'delta_ada_b', 'delta_norm1_w', 'delta_norm2_w', 'delta_ssd_w_in', 'delta_ssd_conv_w', 'delta_ssd_conv_b', 'delta_ssd_dt_bias_f', 'delta_ssd_dt_bias_b', 'delta_ssd_a_log_f', 'delta_ssd_a_log_b', 'delta_ssd_d', 'delta_ssd_norm_w', 'delta_ssd_w_out', 'delta_attn_w_qkv', 'delta_attn_q_gain', 'delta_attn_k_gain', 'delta_attn_sinks', 'delta_attn_w_o', 'delta_ffn_w_up', 'delta_ffn_conv_w', 'delta_ffn_conv_b', 'delta_ffn_w_down', 'new_m_c_ctx', 'new_m_ada_w', 'new_m_ada_b', 'new_m_norm1_w', 'new_m_norm2_w', 'new_m_ssd_w_in', 'new_m_ssd_conv_w', 'new_m_ssd_conv_b', 'new_m_ssd_dt_bias_f', 'new_m_ssd_dt_bias_b', 'new_m_ssd_a_log_f', 'new_m_ssd_a_log_b', 'new_m_ssd_d', 'new_m_ssd_norm_w', 'new_m_ssd_w_out', 'new_m_attn_w_qkv', 'new_m_attn_q_gain', 'new_m_attn_k_gain', 'new_m_attn_sinks', 'new_m_attn_w_o', 'new_m_ffn_w_up', 'new_m_ffn_conv_w', 'new_m_ffn_conv_b', 'new_m_ffn_w_down', 'new_v_c_ctx', 'new_v_ada_w', 'new_v_ada_b', 'new_v_norm1_w', 'new_v_norm2_w', 'new_v_ssd_w_in', 'new_v_ssd_conv_w', 'new_v_ssd_conv_b', 'new_v_ssd_dt_bias_f', 'new_v_ssd_dt_bias_b', 'new_v_ssd_a_log_f', 'new_v_ssd_a_log_b', 'new_v_ssd_d', 'new_v_ssd_norm_w', 'new_v_ssd_w_out', 'new_v_attn_w_qkv', 'new_v_attn_q_gain', 'new_v_attn_k_gain', 'new_v_attn_sinks', 'new_v_attn_w_o', 'new_v_ffn_w_up', 'new_v_ffn_conv_w', 'new_v_ffn_conv_b', 'new_v_ffn_w_down']
TWIN_LEAF_KINDS = {'loss': 'loss', 'grad_x': 'grad_x', 'grad_c_ctx': 'grad_w', 'grad_ada_w': 'grad_w', 'grad_ada_b': 'grad_w', 'grad_norm1_w': 'grad_w', 'grad_norm2_w': 'grad_w', 'grad_ssd_w_in': 'grad_w', 'grad_ssd_conv_w': 'grad_w', 'grad_ssd_conv_b': 'grad_w', 'grad_ssd_dt_bias_f': 'grad_w', 'grad_ssd_dt_bias_b': 'grad_w', 'grad_ssd_a_log_f': 'grad_w', 'grad_ssd_a_log_b': 'grad_w', 'grad_ssd_d': 'grad_w', 'grad_ssd_norm_w': 'grad_w', 'grad_ssd_w_out': 'grad_w', 'grad_attn_w_qkv': 'grad_w', 'grad_attn_q_gain': 'grad_w', 'grad_attn_k_gain': 'grad_w', 'grad_attn_sinks': 'grad_w', 'grad_attn_w_o': 'grad_w', 'grad_ffn_w_up': 'grad_w', 'grad_ffn_conv_w': 'grad_w', 'grad_ffn_conv_b': 'grad_w', 'grad_ffn_w_down': 'grad_w', 'delta_c_ctx': 'delta_w', 'delta_ada_w': 'delta_w', 'delta_ada_b': 'delta_w', 'delta_norm1_w': 'delta_w', 'delta_norm2_w': 'delta_w', 'delta_ssd_w_in': 'delta_w', 'delta_ssd_conv_w': 'delta_w', 'delta_ssd_conv_b': 'delta_w', 'delta_ssd_dt_bias_f': 'delta_w', 'delta_ssd_dt_bias_b': 'delta_w', 'delta_ssd_a_log_f': 'delta_w', 'delta_ssd_a_log_b': 'delta_w', 'delta_ssd_d': 'delta_w', 'delta_ssd_norm_w': 'delta_w', 'delta_ssd_w_out': 'delta_w', 'delta_attn_w_qkv': 'delta_w', 'delta_attn_q_gain': 'delta_w', 'delta_attn_k_gain': 'delta_w', 'delta_attn_sinks': 'delta_w', 'delta_attn_w_o': 'delta_w', 'delta_ffn_w_up': 'delta_w', 'delta_ffn_conv_w': 'delta_w', 'delta_ffn_conv_b': 'delta_w', 'delta_ffn_w_down': 'delta_w', 'new_m_c_ctx': 'new_m', 'new_m_ada_w': 'new_m', 'new_m_ada_b': 'new_m', 'new_m_norm1_w': 'new_m', 'new_m_norm2_w': 'new_m', 'new_m_ssd_w_in': 'new_m', 'new_m_ssd_conv_w': 'new_m', 'new_m_ssd_conv_b': 'new_m', 'new_m_ssd_dt_bias_f': 'new_m', 'new_m_ssd_dt_bias_b': 'new_m', 'new_m_ssd_a_log_f': 'new_m', 'new_m_ssd_a_log_b': 'new_m', 'new_m_ssd_d': 'new_m', 'new_m_ssd_norm_w': 'new_m', 'new_m_ssd_w_out': 'new_m', 'new_m_attn_w_qkv': 'new_m', 'new_m_attn_q_gain': 'new_m', 'new_m_attn_k_gain': 'new_m', 'new_m_attn_sinks': 'new_m', 'new_m_attn_w_o': 'new_m', 'new_m_ffn_w_up': 'new_m', 'new_m_ffn_conv_w': 'new_m', 'new_m_ffn_conv_b': 'new_m', 'new_m_ffn_w_down': 'new_m', 'new_v_c_ctx': 'new_v', 'new_v_ada_w': 'new_v', 'new_v_ada_b': 'new_v', 'new_v_norm1_w': 'new_v', 'new_v_norm2_w': 'new_v', 'new_v_ssd_w_in': 'new_v', 'new_v_ssd_conv_w': 'new_v', 'new_v_ssd_conv_b': 'new_v', 'new_v_ssd_dt_bias_f': 'new_v', 'new_v_ssd_dt_bias_b': 'new_v', 'new_v_ssd_a_log_f': 'new_v', 'new_v_ssd_a_log_b': 'new_v', 'new_v_ssd_d': 'new_v', 'new_v_ssd_norm_w': 'new_v', 'new_v_ssd_w_out': 'new_v', 'new_v_attn_w_qkv': 'new_v', 'new_v_attn_q_gain': 'new_v', 'new_v_attn_k_gain': 'new_v', 'new_v_attn_sinks': 'new_v', 'new_v_attn_w_o': 'new_v', 'new_v_ffn_w_up': 'new_v', 'new_v_ffn_conv_w': 'new_v', 'new_v_ffn_conv_b': 'new_v', 'new_v_ffn_w_down': 'new_v'}


def _forward(args):
    return _fwd_reference(*[args[k] for k in FWD_PARAMS])


def _output_shape():
    def fwd():
        inp = _fwd_setup_inputs(0)
        return _fwd_reference(*[inp[k] for k in FWD_PARAMS])
    out = _jax.eval_shape(fwd)
    return out.shape, out.dtype

N_MICROBATCH = 1
ADAM_LR = 0.001
ADAM_B1 = 0.9
ADAM_B2 = 0.999
ADAM_EPS = 1e-08
ADAM_WD = 0.01
ADAM_STEP = 10
PER_EXAMPLE_BATCH_AXIS = {'x': 0, 'c': 0, 'ctx': 0, 'loss_target': 0}
SHARED_INPUTS = []
_WEIGHT_DTYPES = {'c_ctx': _jnp.float32, 'ada_w': _jnp.float32, 'ada_b': _jnp.float32, 'norm1_w': _jnp.float32, 'norm2_w': _jnp.float32, 'ssd_w_in': _jnp.float32, 'ssd_conv_w': _jnp.float32, 'ssd_conv_b': _jnp.float32, 'ssd_dt_bias_f': _jnp.float32, 'ssd_dt_bias_b': _jnp.float32, 'ssd_a_log_f': _jnp.float32, 'ssd_a_log_b': _jnp.float32, 'ssd_d': _jnp.float32, 'ssd_norm_w': _jnp.float32, 'ssd_w_out': _jnp.float32, 'attn_w_qkv': _jnp.float32, 'attn_q_gain': _jnp.float32, 'attn_k_gain': _jnp.float32, 'attn_sinks': _jnp.float32, 'attn_w_o': _jnp.float32, 'ffn_w_up': _jnp.float32, 'ffn_conv_w': _jnp.float32, 'ffn_conv_b': _jnp.float32, 'ffn_w_down': _jnp.float32}
MOMENT_SCALE = {'c_ctx': 4.021935e-01, 'ada_w': 8.605848e-01, 'ada_b': 2.156292e+00, 'norm1_w': 9.472596e-02, 'norm2_w': 3.179384e+00, 'ssd_w_in': 8.014086e-02, 'ssd_conv_w': 1.552269e-01, 'ssd_conv_b': 3.046320e-01, 'ssd_dt_bias_f': 2.620939e-01, 'ssd_dt_bias_b': 3.925067e-01, 'ssd_a_log_f': 5.527967e-01, 'ssd_a_log_b': 4.652402e-01, 'ssd_d': 4.989385e-01, 'ssd_norm_w': 2.762452e+00, 'ssd_w_out': 2.547538e-01, 'attn_w_qkv': 2.179240e-01, 'attn_q_gain': 2.812184e-01, 'attn_k_gain': 2.774417e-01, 'attn_sinks': 2.387310e-02, 'attn_w_o': 1.861747e-01, 'ffn_w_up': 8.348636e-02, 'ffn_conv_w': 4.718926e-01, 'ffn_conv_b': 3.947407e-01, 'ffn_w_down': 7.079205e-02}


def _to_microbatches(a, axis):
    t = _jnp.moveaxis(a, axis, 0)
    t = t.reshape((N_MICROBATCH, t.shape[0] // N_MICROBATCH) + t.shape[1:])
    return _jnp.moveaxis(t, 1, axis + 1)


def setup_inputs(seed: int = 0) -> dict:
    inp = _fwd_setup_inputs(seed)
    key = _jax.random.fold_in(_jax.random.key(seed), 7919)
    shape, _ = _output_shape()
    out = dict(inp)
    out["loss_target"] = _jax.random.normal(_jax.random.fold_in(key, 0), shape, _jnp.float32)
    for i, name in enumerate(TWIN_WEIGHTS):
        w = inp[name].astype(_jnp.float32)
        if MOMENT_SCALE is None:
            s = _jnp.sqrt(_jnp.mean(_jnp.square(w)) + 1e-30)
        else:
            s = MOMENT_SCALE[name]
        km, kv = _jax.random.split(_jax.random.fold_in(key, i + 1))
        out[name] = w
        out["m_" + name] = s * _jax.random.normal(km, w.shape, _jnp.float32)
        out["v_" + name] = (s * s) * _jax.random.uniform(kv, w.shape, _jnp.float32, 0.5, 1.5)
    if N_MICROBATCH > 1:
        for name, axis in PER_EXAMPLE_BATCH_AXIS.items():
            out[name] = _to_microbatches(out[name], axis)
    return {'x': out['x'], 'c': out['c'], 'ctx': out['ctx'], 'c_ctx': out['c_ctx'], 'ada_w': out['ada_w'], 'ada_b': out['ada_b'], 'norm1_w': out['norm1_w'], 'norm2_w': out['norm2_w'], 'ssd_w_in': out['ssd_w_in'], 'ssd_conv_w': out['ssd_conv_w'], 'ssd_conv_b': out['ssd_conv_b'], 'ssd_dt_bias_f': out['ssd_dt_bias_f'], 'ssd_dt_bias_b': out['ssd_dt_bias_b'], 'ssd_a_log_f': out['ssd_a_log_f'], 'ssd_a_log_b': out['ssd_a_log_b'], 'ssd_d': out['ssd_d'], 'ssd_norm_w': out['ssd_norm_w'], 'ssd_w_out': out['ssd_w_out'], 'attn_w_qkv': out['attn_w_qkv'], 'attn_q_gain': out['attn_q_gain'], 'attn_k_gain': out['attn_k_gain'], 'attn_sinks': out['attn_sinks'], 'attn_w_o': out['attn_w_o'], 'ffn_w_up': out['ffn_w_up'], 'ffn_conv_w': out['ffn_conv_w'], 'ffn_conv_b': out['ffn_conv_b'], 'ffn_w_down': out['ffn_w_down'], 'loss_target': out['loss_target'], 'm_c_ctx': out['m_c_ctx'], 'm_ada_w': out['m_ada_w'], 'm_ada_b': out['m_ada_b'], 'm_norm1_w': out['m_norm1_w'], 'm_norm2_w': out['m_norm2_w'], 'm_ssd_w_in': out['m_ssd_w_in'], 'm_ssd_conv_w': out['m_ssd_conv_w'], 'm_ssd_conv_b': out['m_ssd_conv_b'], 'm_ssd_dt_bias_f': out['m_ssd_dt_bias_f'], 'm_ssd_dt_bias_b': out['m_ssd_dt_bias_b'], 'm_ssd_a_log_f': out['m_ssd_a_log_f'], 'm_ssd_a_log_b': out['m_ssd_a_log_b'], 'm_ssd_d': out['m_ssd_d'], 'm_ssd_norm_w': out['m_ssd_norm_w'], 'm_ssd_w_out': out['m_ssd_w_out'], 'm_attn_w_qkv': out['m_attn_w_qkv'], 'm_attn_q_gain': out['m_attn_q_gain'], 'm_attn_k_gain': out['m_attn_k_gain'], 'm_attn_sinks': out['m_attn_sinks'], 'm_attn_w_o': out['m_attn_w_o'], 'm_ffn_w_up': out['m_ffn_w_up'], 'm_ffn_conv_w': out['m_ffn_conv_w'], 'm_ffn_conv_b': out['m_ffn_conv_b'], 'm_ffn_w_down': out['m_ffn_w_down'], 'v_c_ctx': out['v_c_ctx'], 'v_ada_w': out['v_ada_w'], 'v_ada_b': out['v_ada_b'], 'v_norm1_w': out['v_norm1_w'], 'v_norm2_w': out['v_norm2_w'], 'v_ssd_w_in': out['v_ssd_w_in'], 'v_ssd_conv_w': out['v_ssd_conv_w'], 'v_ssd_conv_b': out['v_ssd_conv_b'], 'v_ssd_dt_bias_f': out['v_ssd_dt_bias_f'], 'v_ssd_dt_bias_b': out['v_ssd_dt_bias_b'], 'v_ssd_a_log_f': out['v_ssd_a_log_f'], 'v_ssd_a_log_b': out['v_ssd_a_log_b'], 'v_ssd_d': out['v_ssd_d'], 'v_ssd_norm_w': out['v_ssd_norm_w'], 'v_ssd_w_out': out['v_ssd_w_out'], 'v_attn_w_qkv': out['v_attn_w_qkv'], 'v_attn_q_gain': out['v_attn_q_gain'], 'v_attn_k_gain': out['v_attn_k_gain'], 'v_attn_sinks': out['v_attn_sinks'], 'v_attn_w_o': out['v_attn_w_o'], 'v_ffn_w_up': out['v_ffn_w_up'], 'v_ffn_conv_w': out['v_ffn_conv_w'], 'v_ffn_conv_b': out['v_ffn_conv_b'], 'v_ffn_w_down': out['v_ffn_w_down']}


def _loss(weights, diff, rest, loss_target):
    with _jax.named_scope("forward"):
        args = {**rest, TWIN_DIFF_INPUT: diff, **{k: w.astype(_WEIGHT_DTYPES[k]) for k, w in weights.items()}}
        y = _forward(args)
    with _jax.named_scope("loss_head"):
        err = _jnp.square(y.astype(_jnp.float32) - loss_target)
        return 0.5 * _jnp.sum(_jnp.mean(err, axis=-1)) if err.ndim else 0.5 * err


def _adamw(w, g, m, v):
    m = ADAM_B1 * m + (1.0 - ADAM_B1) * g
    v = ADAM_B2 * v + (1.0 - ADAM_B2) * _jnp.square(g)
    m_hat = m / (1.0 - ADAM_B1 ** ADAM_STEP)
    v_hat = v / (1.0 - ADAM_B2 ** ADAM_STEP)
    delta = -ADAM_LR * (m_hat / (_jnp.sqrt(v_hat) + ADAM_EPS) + ADAM_WD * w)
    return delta, m, v


def reference(x, c, ctx, c_ctx, ada_w, ada_b, norm1_w, norm2_w, ssd_w_in, ssd_conv_w, ssd_conv_b, ssd_dt_bias_f, ssd_dt_bias_b, ssd_a_log_f, ssd_a_log_b, ssd_d, ssd_norm_w, ssd_w_out, attn_w_qkv, attn_q_gain, attn_k_gain, attn_sinks, attn_w_o, ffn_w_up, ffn_conv_w, ffn_conv_b, ffn_w_down, loss_target, m_c_ctx, m_ada_w, m_ada_b, m_norm1_w, m_norm2_w, m_ssd_w_in, m_ssd_conv_w, m_ssd_conv_b, m_ssd_dt_bias_f, m_ssd_dt_bias_b, m_ssd_a_log_f, m_ssd_a_log_b, m_ssd_d, m_ssd_norm_w, m_ssd_w_out, m_attn_w_qkv, m_attn_q_gain, m_attn_k_gain, m_attn_sinks, m_attn_w_o, m_ffn_w_up, m_ffn_conv_w, m_ffn_conv_b, m_ffn_w_down, v_c_ctx, v_ada_w, v_ada_b, v_norm1_w, v_norm2_w, v_ssd_w_in, v_ssd_conv_w, v_ssd_conv_b, v_ssd_dt_bias_f, v_ssd_dt_bias_b, v_ssd_a_log_f, v_ssd_a_log_b, v_ssd_d, v_ssd_norm_w, v_ssd_w_out, v_attn_w_qkv, v_attn_q_gain, v_attn_k_gain, v_attn_sinks, v_attn_w_o, v_ffn_w_up, v_ffn_conv_w, v_ffn_conv_b, v_ffn_w_down):
    given = dict(x=x, c=c, ctx=ctx, c_ctx=c_ctx, ada_w=ada_w, ada_b=ada_b, norm1_w=norm1_w, norm2_w=norm2_w, ssd_w_in=ssd_w_in, ssd_conv_w=ssd_conv_w, ssd_conv_b=ssd_conv_b, ssd_dt_bias_f=ssd_dt_bias_f, ssd_dt_bias_b=ssd_dt_bias_b, ssd_a_log_f=ssd_a_log_f, ssd_a_log_b=ssd_a_log_b, ssd_d=ssd_d, ssd_norm_w=ssd_norm_w, ssd_w_out=ssd_w_out, attn_w_qkv=attn_w_qkv, attn_q_gain=attn_q_gain, attn_k_gain=attn_k_gain, attn_sinks=attn_sinks, attn_w_o=attn_w_o, ffn_w_up=ffn_w_up, ffn_conv_w=ffn_conv_w, ffn_conv_b=ffn_conv_b, ffn_w_down=ffn_w_down, loss_target=loss_target, m_c_ctx=m_c_ctx, m_ada_w=m_ada_w, m_ada_b=m_ada_b, m_norm1_w=m_norm1_w, m_norm2_w=m_norm2_w, m_ssd_w_in=m_ssd_w_in, m_ssd_conv_w=m_ssd_conv_w, m_ssd_conv_b=m_ssd_conv_b, m_ssd_dt_bias_f=m_ssd_dt_bias_f, m_ssd_dt_bias_b=m_ssd_dt_bias_b, m_ssd_a_log_f=m_ssd_a_log_f, m_ssd_a_log_b=m_ssd_a_log_b, m_ssd_d=m_ssd_d, m_ssd_norm_w=m_ssd_norm_w, m_ssd_w_out=m_ssd_w_out, m_attn_w_qkv=m_attn_w_qkv, m_attn_q_gain=m_attn_q_gain, m_attn_k_gain=m_attn_k_gain, m_attn_sinks=m_attn_sinks, m_attn_w_o=m_attn_w_o, m_ffn_w_up=m_ffn_w_up, m_ffn_conv_w=m_ffn_conv_w, m_ffn_conv_b=m_ffn_conv_b, m_ffn_w_down=m_ffn_w_down, v_c_ctx=v_c_ctx, v_ada_w=v_ada_w, v_ada_b=v_ada_b, v_norm1_w=v_norm1_w, v_norm2_w=v_norm2_w, v_ssd_w_in=v_ssd_w_in, v_ssd_conv_w=v_ssd_conv_w, v_ssd_conv_b=v_ssd_conv_b, v_ssd_dt_bias_f=v_ssd_dt_bias_f, v_ssd_dt_bias_b=v_ssd_dt_bias_b, v_ssd_a_log_f=v_ssd_a_log_f, v_ssd_a_log_b=v_ssd_a_log_b, v_ssd_d=v_ssd_d, v_ssd_norm_w=v_ssd_norm_w, v_ssd_w_out=v_ssd_w_out, v_attn_w_qkv=v_attn_w_qkv, v_attn_q_gain=v_attn_q_gain, v_attn_k_gain=v_attn_k_gain, v_attn_sinks=v_attn_sinks, v_attn_w_o=v_attn_w_o, v_ffn_w_up=v_ffn_w_up, v_ffn_conv_w=v_ffn_conv_w, v_ffn_conv_b=v_ffn_conv_b, v_ffn_w_down=v_ffn_w_down)
    weights = {n: given[n] for n in TWIN_WEIGHTS}
    shared = {n: given[n] for n in SHARED_INPUTS}
    per_example = {n: given[n] for n in ['x', 'c', 'ctx']}
    grad_fn = _jax.value_and_grad(_loss, argnums=(0, 1))

    def one_microbatch(ex, loss_target):
        ex = dict(ex)
        diff = ex.pop(TWIN_DIFF_INPUT)
        return grad_fn(weights, diff, {**shared, **ex}, loss_target)

    if N_MICROBATCH == 1:
        loss, (grad_w, grad_x) = one_microbatch(per_example, given["loss_target"])
    else:
        def body(carry, xs):
            loss_sum, grad_sum = carry
            l_k, (gw_k, gx_k) = one_microbatch(xs[0], xs[1])
            with _jax.named_scope("update"):
                return (loss_sum + l_k, _jax.tree.map(_jnp.add, grad_sum, gw_k)), gx_k

        init = (_jnp.zeros((), _jnp.float32), _jax.tree.map(_jnp.zeros_like, weights))
        (loss, grad_w), grad_x = _jax.lax.scan(body, init, (per_example, given["loss_target"]))
    with _jax.named_scope("update"):
        delta_w, new_m, new_v = {}, {}, {}
        for n in TWIN_WEIGHTS:
            delta_w[n], new_m[n], new_v[n] = _adamw(weights[n], grad_w[n], given["m_" + n], given["v_" + n])
    return (loss, grad_x, *[grad_w[n] for n in TWIN_WEIGHTS], *[delta_w[n] for n in TWIN_WEIGHTS],
            *[new_m[n] for n in TWIN_WEIGHTS], *[new_v[n] for n in TWIN_WEIGHTS])
```

```python
import functools
import math

import jax
import jax.numpy as jnp
from jax import lax
from jax.experimental import pallas as pl
from jax.experimental.pallas import tpu as pltpu

F32 = jnp.float32
BF16 = jnp.bfloat16

SSD_HEAD_DIM = 64
SSD_GROUPS = 8
SSD_STATE = 128
CHUNK = 128
ATTN_HEAD_DIM = 128
ATTN_KV_HEADS = 4
GRID_W = 64
ROPE_THETA = 10000.0
NORM_EPS = 1e-6
ADAM_LR, ADAM_B1, ADAM_B2, ADAM_EPS, ADAM_WD, ADAM_STEP = 0.001, 0.9, 0.999, 1e-08, 0.01, 10

LANES = 128
SUBLANES = 8
VMEM_LIMIT = 56 * 1024 * 1024
MESH = pl.DeviceIdType.MESH
N_DEV = 8
N_CHIP = 4


def _cp(**kw):
    return pltpu.CompilerParams(vmem_limit_bytes=VMEM_LIMIT, **kw)


def _pick(n, cands):
    for c in cands:
        if n % c == 0:
            return c
    return n


def _silu(x):
    return x * jax.nn.sigmoid(x)


def _dsilu(x):
    s = jax.nn.sigmoid(x)
    return s * (1.0 + x * (1.0 - s))


_DIMS = {"nn": (((1,), (0,)), ((), ())), "nt": (((1,), (1,)), ((), ())), "tn": (((0,), (0,)), ((), ()))}


def _mm_tiles(mode, m, n, k):
    tm = _pick(m, (768, 512, 384, 256, 128))
    tn = _pick(n, (1024, 1152, 768, 512, 384, 256, 128))
    tk = _pick(k, (2048, 1408, 1024, 768, 512, 256, 128))
    return tm, tn, tk


def matmul(a, b, mode, name, out_dtype=F32):
    if mode == "nn":
        (m, k), (_, n) = a.shape, b.shape
    elif mode == "nt":
        (m, k), (n, _) = a.shape, b.shape
    else:
        (k, m), (_, n) = a.shape, b.shape
    tm, tn, tk = _mm_tiles(mode, m, n, k)
    nk = k // tk
    a_spec = pl.BlockSpec((tk, tm), lambda j, i, kk: (kk, i)) if mode == "tn" else pl.BlockSpec((tm, tk), lambda j, i, kk: (i, kk))
    b_spec = pl.BlockSpec((tn, tk), lambda j, i, kk: (j, kk)) if mode == "nt" else pl.BlockSpec((tk, tn), lambda j, i, kk: (kk, j))

    def body(a_ref, b_ref, o_ref, acc_ref):
        part = lax.dot_general(a_ref[...], b_ref[...], _DIMS[mode], preferred_element_type=F32)
        if nk == 1:
            o_ref[...] = part.astype(out_dtype)
        else:
            kk = pl.program_id(2)

            @pl.when(kk == 0)
            def _():
                acc_ref[...] = part

            @pl.when(kk > 0)
            def _():
                acc_ref[...] += part

            @pl.when(kk == nk - 1)
            def _():
                o_ref[...] = acc_ref[...].astype(out_dtype)

    return pl.pallas_call(
        body, name=name, grid=(n // tn, m // tm, nk),
        in_specs=[a_spec, b_spec], out_specs=pl.BlockSpec((tm, tn), lambda j, i, kk: (i, j)),
        out_shape=jax.ShapeDtypeStruct((m, n), out_dtype),
        scratch_shapes=[pltpu.VMEM((tm, tn) if nk > 1 else (SUBLANES, LANES), F32)],
        compiler_params=_cp(dimension_semantics=("parallel", "parallel", "arbitrary")),
    )(a, b)


def matmul_gate_res(a, w, res, gate, n_ctx, name):
    (m, k), (_, n) = a.shape, w.shape
    tm, tn, tk = _mm_tiles("nn", m, n, k)
    nk = k // tk

    def body(a_ref, b_ref, r_ref, g_ref, x_ref, y_ref, acc_ref):
        kk = pl.program_id(2)
        row0 = pl.program_id(1) * tm
        part = jnp.dot(a_ref[...], b_ref[...], preferred_element_type=F32)

        @pl.when(kk == 0)
        def _():
            acc_ref[...] = part

        @pl.when(kk > 0)
        def _():
            acc_ref[...] += part

        @pl.when(kk == nk - 1)
        def _():
            y = acc_ref[...]
            row = row0 + lax.broadcasted_iota(jnp.int32, (tm, 1), 0)
            g = jnp.where(row < n_ctx, g_ref[0:1, :], g_ref[1:2, :])
            y_ref[...] = y.astype(y_ref.dtype)
            x_ref[...] = r_ref[...] + g * y

    return pl.pallas_call(
        body, name=name, grid=(n // tn, m // tm, nk),
        in_specs=[pl.BlockSpec((tm, tk), lambda j, i, kk: (i, kk)), pl.BlockSpec((tk, tn), lambda j, i, kk: (kk, j)),
                  pl.BlockSpec((tm, tn), lambda j, i, kk: (i, j)), pl.BlockSpec((2, tn), lambda j, i, kk: (0, j))],
        out_specs=[pl.BlockSpec((tm, tn), lambda j, i, kk: (i, j)), pl.BlockSpec((tm, tn), lambda j, i, kk: (i, j))],
        out_shape=[jax.ShapeDtypeStruct((m, n), F32), jax.ShapeDtypeStruct((m, n), F32)],
        scratch_shapes=[pltpu.VMEM((tm, tn), F32)],
        compiler_params=_cp(dimension_semantics=("parallel", "parallel", "arbitrary")),
    )(a, w, res, gate)


ROW_TILE = 256


def _seg_row(ref2, i, n_ctx_tiles):
    return jnp.where(i < n_ctx_tiles, ref2[0:1, :], ref2[1:2, :])


def _acc_rows(ref, step, rows):
    @pl.when(step == 0)
    def _():
        ref[...] = jnp.zeros_like(ref)

    for r, v in enumerate(rows):
        ref[r:r + 1, :] += v


def norm_mod(x, w, shift, scale, n_ctx, name):
    t, d = x.shape
    tm = _pick(n_ctx, (ROW_TILE, 128))
    nct = n_ctx // tm

    def body(x_ref, w_ref, sh_ref, sc_ref, h_ref):
        i = pl.program_id(0)
        xv = x_ref[...]
        r = lax.rsqrt(jnp.mean(xv * xv, axis=-1, keepdims=True) + NORM_EPS)
        h_ref[...] = ((xv * r) * w_ref[...] * (1.0 + _seg_row(sc_ref, i, nct)) + _seg_row(sh_ref, i, nct)).astype(BF16)

    return pl.pallas_call(
        body, name=name, grid=(t // tm,),
        in_specs=[pl.BlockSpec((tm, d), lambda i: (i, 0)), pl.BlockSpec((1, d), lambda i: (0, 0)),
                  pl.BlockSpec((2, d), lambda i: (0, 0)), pl.BlockSpec((2, d), lambda i: (0, 0))],
        out_specs=pl.BlockSpec((tm, d), lambda i: (i, 0)),
        out_shape=jax.ShapeDtypeStruct((t, d), BF16), compiler_params=_cp(),
    )(x, w, shift, scale)


def norm_mod_bwd(dh, dres, x, w, scale, n_ctx, name):
    t, d = x.shape
    tm = _pick(n_ctx, (ROW_TILE, 128))
    nct = n_ctx // tm

    def body(dh_ref, dr_ref, x_ref, w_ref, sc_ref, dx_ref, st_ref):
        i = pl.program_id(0)
        xv, g = x_ref[...], dh_ref[...]
        r = lax.rsqrt(jnp.mean(xv * xv, axis=-1, keepdims=True) + NORM_EPS)
        xn = xv * r
        one_sc = 1.0 + _seg_row(sc_ref, i, nct)
        dxn = g * (w_ref[...] * one_sc)
        dx_ref[...] = dr_ref[...] + r * (dxn - xn * jnp.mean(dxn * xn, axis=-1, keepdims=True))
        gx = g * xn
        s_shift = jnp.sum(g, axis=0, keepdims=True)
        s_scale = jnp.sum(gx * w_ref[...], axis=0, keepdims=True)
        s_w = jnp.sum(gx * one_sc, axis=0, keepdims=True)
        _acc_rows(st_ref, i, [jnp.where(i < nct, s_shift, 0.0), jnp.where(i < nct, 0.0, s_shift),
                              jnp.where(i < nct, s_scale, 0.0), jnp.where(i < nct, 0.0, s_scale), s_w])

    return pl.pallas_call(
        body, name=name, grid=(t // tm,),
        in_specs=[pl.BlockSpec((tm, d), lambda i: (i, 0)), pl.BlockSpec((tm, d), lambda i: (i, 0)),
                  pl.BlockSpec((tm, d), lambda i: (i, 0)), pl.BlockSpec((1, d), lambda i: (0, 0)),
                  pl.BlockSpec((2, d), lambda i: (0, 0))],
        out_specs=[pl.BlockSpec((tm, d), lambda i: (i, 0)), pl.BlockSpec((SUBLANES, d), lambda i: (0, 0))],
        out_shape=[jax.ShapeDtypeStruct((t, d), F32), jax.ShapeDtypeStruct((SUBLANES, d), F32)],
        compiler_params=_cp(dimension_semantics=("arbitrary",)),
    )(dh, dres, x, w, scale)


def gate_bwd(dx, y, gate, n_ctx, name):
    t, d = dx.shape
    tm = _pick(n_ctx, (ROW_TILE, 128))
    nct = n_ctx // tm

    def body(dx_ref, y_ref, g_ref, dy_ref, dg_ref):
        i = pl.program_id(0)
        dxv = dx_ref[...]
        dy_ref[...] = (dxv * _seg_row(g_ref, i, nct)).astype(BF16)
        s = jnp.sum(dxv * y_ref[...], axis=0, keepdims=True)
        _acc_rows(dg_ref, i, [jnp.where(i < nct, s, 0.0), jnp.where(i < nct, 0.0, s)])

    return pl.pallas_call(
        body, name=name, grid=(t // tm,),
        in_specs=[pl.BlockSpec((tm, d), lambda i: (i, 0)), pl.BlockSpec((tm, d), lambda i: (i, 0)),
                  pl.BlockSpec((2, d), lambda i: (0, 0))],
        out_specs=[pl.BlockSpec((tm, d), lambda i: (i, 0)), pl.BlockSpec((SUBLANES, d), lambda i: (0, 0))],
        out_shape=[jax.ShapeDtypeStruct((t, d), BF16), jax.ShapeDtypeStruct((SUBLANES, d), F32)],
        compiler_params=_cp(dimension_semantics=("arbitrary",)),
    )(dx, y, gate)


def _seg_edges(i, tm, n_ctx, t):
    first = (i == 0) | (i == n_ctx // tm)
    last = (i == n_ctx // tm - 1) | (i == t // tm - 1)
    return first, last


def _shift_rows(x, prev8, next8, first, last):
    tm = x.shape[0]
    row = lax.broadcasted_iota(jnp.int32, (tm, 1), 0)
    prev_row = jnp.where(first, 0.0, prev8[SUBLANES - 1:SUBLANES, :])
    next_row = jnp.where(last, 0.0, next8[0:1, :])
    xp = jnp.where(row == 0, prev_row, pltpu.roll(x, 1, 0))
    xn = jnp.where(row == tm - 1, next_row, pltpu.roll(x, tm - 1, 0))
    return xp, xn


def _halo_specs(tm, tc, t, col):
    r, nblk8 = tm // SUBLANES, t // SUBLANES
    return [pl.BlockSpec((tm, tc), lambda j, i: (i, col(j))),
            pl.BlockSpec((SUBLANES, tc), lambda j, i: (jnp.maximum(i * r - 1, 0), col(j))),
            pl.BlockSpec((SUBLANES, tc), lambda j, i: (jnp.minimum((i + 1) * r, nblk8 - 1), col(j)))]


def dwconv_act(x, col0, c, wb, n_ctx, mode, act_dtype, name):
    t = x.shape[0]
    tm = _pick(n_ctx, (ROW_TILE, 128))
    nparts = 2 if mode == "glu" else 1
    cw = c // nparts
    tc = _pick(math.gcd(cw, col0) if col0 else cw, (512, 384, 256, 128))

    def conv(xm, xp8, xn8, w_ref, i):
        first, last = _seg_edges(i, tm, n_ctx, t)
        xp, xn = _shift_rows(xm[...], xp8[...], xn8[...], first, last)
        return w_ref[0:1, :] * xp + w_ref[1:2, :] * xm[...] + w_ref[2:3, :] * xn + w_ref[3:4, :]

    def body(*refs):
        i = pl.program_id(1)
        if mode == "silu":
            xm, xp8, xn8, w_ref, pre_ref, act_ref = refs
            pre = conv(xm, xp8, xn8, w_ref, i)
            pre_ref[...] = pre
            act_ref[...] = _silu(pre).astype(act_dtype)
        else:
            xm, xp8, xn8, w_ref, gm, gp8, gn8, gw_ref, pv_ref, pg_ref, act_ref = refs
            val = conv(xm, xp8, xn8, w_ref, i)
            gat = conv(gm, gp8, gn8, gw_ref, i)
            pv_ref[...] = val
            pg_ref[...] = gat
            act_ref[...] = (_silu(gat) * val).astype(act_dtype)

    in_specs, args = [], []
    for p in range(nparts):
        in_specs += _halo_specs(tm, tc, t, lambda j, p=p: (col0 + p * cw) // tc + j)
        in_specs.append(pl.BlockSpec((SUBLANES, tc), lambda j, i, p=p: (0, p * cw // tc + j)))
        args += [x, x, x, wb]
    out = pl.BlockSpec((tm, tc), lambda j, i: (i, j))
    out_shape = [jax.ShapeDtypeStruct((t, cw), F32)] * nparts + [jax.ShapeDtypeStruct((t, cw), act_dtype)]
    return pl.pallas_call(
        body, name=name, grid=(cw // tc, t // tm), in_specs=in_specs, out_specs=[out] * (nparts + 1), out_shape=out_shape,
        compiler_params=_cp(dimension_semantics=("parallel", "parallel")),
    )(*args)


def dwconv_act_bwd(dact, pres, x, col0, c, wb, n_ctx, mode, name):
    t = x.shape[0]
    tm = _pick(n_ctx, (ROW_TILE, 128))
    nparts = 2 if mode == "glu" else 1
    cw = c // nparts
    tc = _pick(math.gcd(cw, col0) if col0 else cw, (512, 384, 256, 128))

    def dpre_of(dact_v, pres_v):
        if mode == "silu":
            return [dact_v * _dsilu(pres_v[0])]
        val, gat = pres_v
        return [dact_v * _silu(gat), dact_v * val * _dsilu(gat)]

    def body(*refs):
        i = pl.program_id(1)
        first, last = _seg_edges(i, tm, n_ctx, t)
        da = refs[0:3]
        pr = [refs[3 + 3 * p:6 + 3 * p] for p in range(nparts)]
        rest = refs[3 + 3 * nparts:]
        xs, ws = rest[:nparts], rest[nparts:2 * nparts]
        dxs, dws = rest[2 * nparts:3 * nparts], rest[3 * nparts:]
        dm = dpre_of(da[0][...], [p[0][...] for p in pr])
        dp8 = dpre_of(da[1][...], [p[1][...] for p in pr])
        dn8 = dpre_of(da[2][...], [p[2][...] for p in pr])
        for p in range(nparts):
            d_prev, d_next = _shift_rows(dm[p], dp8[p], dn8[p], first, last)
            w_ref, xv = ws[p], xs[p][...]
            dxs[p][...] = (w_ref[0:1, :] * d_next + w_ref[1:2, :] * dm[p] + w_ref[2:3, :] * d_prev).astype(BF16)
            _acc_rows(dws[p], i, [jnp.sum(d_next * xv, axis=0, keepdims=True), jnp.sum(dm[p] * xv, axis=0, keepdims=True),
                                  jnp.sum(d_prev * xv, axis=0, keepdims=True), jnp.sum(dm[p], axis=0, keepdims=True)])

    plain = lambda j: j
    in_specs = _halo_specs(tm, tc, t, plain)
    args = [dact] * 3
    for p in range(nparts):
        in_specs += _halo_specs(tm, tc, t, plain)
        args += [pres[p]] * 3
    for p in range(nparts):
        in_specs.append(pl.BlockSpec((tm, tc), lambda j, i, p=p: (i, (col0 + p * cw) // tc + j)))
        args.append(x)
    for p in range(nparts):
        in_specs.append(pl.BlockSpec((SUBLANES, tc), lambda j, i, p=p: (0, p * cw // tc + j)))
        args.append(wb)
    out_specs = [pl.BlockSpec((tm, tc), lambda j, i: (i, j))] * nparts + [pl.BlockSpec((SUBLANES, tc), lambda j, i: (0, j))] * nparts
    res = pl.pallas_call(
        body, name=name, grid=(cw // tc, t // tm), in_specs=in_specs, out_specs=out_specs,
        out_shape=[jax.ShapeDtypeStruct((t, cw), BF16)] * nparts + [jax.ShapeDtypeStruct((SUBLANES, cw), F32)] * nparts,
        compiler_params=_cp(dimension_semantics=("parallel", "arbitrary")),
    )(*args)
    return res[:nparts], res[nparts:]


HI = lax.Precision.HIGHEST
_NT = (((1,), (1,)), ((), ()))
_TN = (((0,), (0,)), ((), ()))


def _dot(a, b, dims=None, precision=None):
    if dims is None:
        return jnp.dot(a, b, preferred_element_type=F32, precision=precision)
    return lax.dot_general(a, b, dims, preferred_element_type=F32, precision=precision)


def _softplus(x):
    y = jnp.exp(-jnp.abs(x))
    u = 1.0 + y
    log1p = jnp.where(u == 1.0, y, y * jnp.log(u) / jnp.where(u == 1.0, 1.0, u - 1.0))
    return jnp.maximum(x, 0.0) + log1p


def _tri(n, upper):
    r = lax.broadcasted_iota(jnp.int32, (n, n), 0)
    c = lax.broadcasted_iota(jnp.int32, (n, n), 1)
    return ((r <= c) if upper else (r >= c)).astype(F32)


def _group_select(heads, e):
    g = jnp.arange(SSD_GROUPS)[:, None, None]
    src = jnp.arange(LANES)[None, :, None]
    dst = jnp.arange(LANES)[None, None, :]
    d, k = dst // e, dst % e
    return ((dst < 2 * e) & (src == d * heads + g * e + k)).astype(F32)


def ssd_prep(zx, col0, bias, alog, sel, heads, name):
    t = zx.shape[0]
    assert 2 * heads == LANES and col0 % LANES == 0
    nc = t // CHUNK

    def body(zx_ref, b_ref, al_ref, sel_ref, dt_ref, da_ref, dtg_ref, ag_ref, agt_ref):
        dt = _softplus(zx_ref[...] + b_ref[...])
        da = -jnp.exp(al_ref[...]) * dt
        lane = lax.broadcasted_iota(jnp.int32, (CHUNK, LANES), 1)
        a = jnp.where(lane < heads, _dot(_tri(CHUNK, False), da, precision=HI), _dot(_tri(CHUNK, True), da, precision=HI))
        dt_ref[...] = dt
        da_ref[...] = da
        for g in range(SSD_GROUPS):
            s = sel_ref[g]
            dtg_ref[g] = _dot(dt, s, precision=HI)
            a_g = _dot(a, s, precision=HI)
            ag_ref[g] = a_g
            agt_ref[g, 0] = a_g.T

    return pl.pallas_call(
        body, name=name, grid=(nc,),
        in_specs=[pl.BlockSpec((CHUNK, LANES), lambda c: (c, col0 // LANES)), pl.BlockSpec((1, LANES), lambda c: (0, 0)),
                  pl.BlockSpec((1, LANES), lambda c: (0, 0)), pl.BlockSpec((SSD_GROUPS, LANES, LANES), lambda c: (0, 0, 0))],
        out_specs=[pl.BlockSpec((CHUNK, LANES), lambda c: (c, 0)), pl.BlockSpec((CHUNK, LANES), lambda c: (c, 0)),
                   pl.BlockSpec((SSD_GROUPS, CHUNK, LANES), lambda c: (0, c, 0)), pl.BlockSpec((SSD_GROUPS, CHUNK, LANES), lambda c: (0, c, 0)),
                   pl.BlockSpec((SSD_GROUPS, 1, LANES, CHUNK), lambda c: (0, c, 0, 0))],
        out_shape=[jax.ShapeDtypeStruct((t, LANES), F32), jax.ShapeDtypeStruct((t, LANES), F32),
                   jax.ShapeDtypeStruct((SSD_GROUPS, t, LANES), F32), jax.ShapeDtypeStruct((SSD_GROUPS, t, LANES), F32),
                   jax.ShapeDtypeStruct((SSD_GROUPS, nc, LANES, CHUNK), F32)],
        compiler_params=_cp(),
    )(zx, bias, alog, sel)


def ssd_prep_bwd(zx, col0, bias, alog, sel, ddtg, dag, dagt, da_comp, heads, name):
    t = zx.shape[0]
    nc = t // CHUNK

    def body(zx_ref, b_ref, al_ref, sel_ref, ddtg_ref, dag_ref, dagt_ref, da_ref, draw_ref, st_ref):
        c = pl.program_id(0)
        ddt = jnp.zeros((CHUNK, LANES), F32)
        dacc = jnp.zeros((CHUNK, LANES), F32)
        for g in range(SSD_GROUPS):
            s = sel_ref[g]
            ddt += _dot(ddtg_ref[g], s, _NT, precision=HI)
            dacc += _dot(dag_ref[g] + dagt_ref[g, 0].T, s, _NT, precision=HI)
        lane = lax.broadcasted_iota(jnp.int32, (CHUNK, LANES), 1)
        dda = jnp.where(lane < heads, _dot(_tri(CHUNK, True), dacc, precision=HI), _dot(_tri(CHUNK, False), dacc, precision=HI))
        xin = zx_ref[...] + b_ref[...]
        ddt_tot = ddt - dda * jnp.exp(al_ref[...])
        draw = ddt_tot * jax.nn.sigmoid(xin)
        draw_ref[...] = draw.astype(BF16)
        _acc_rows(st_ref, c, [jnp.sum(draw, axis=0, keepdims=True), jnp.sum(dda * da_ref[...], axis=0, keepdims=True)])

    g3 = pl.BlockSpec((SSD_GROUPS, CHUNK, LANES), lambda c: (0, c, 0))
    return pl.pallas_call(
        body, name=name, grid=(nc,),
        in_specs=[pl.BlockSpec((CHUNK, LANES), lambda c: (c, col0 // LANES)), pl.BlockSpec((1, LANES), lambda c: (0, 0)),
                  pl.BlockSpec((1, LANES), lambda c: (0, 0)), pl.BlockSpec((SSD_GROUPS, LANES, LANES), lambda c: (0, 0, 0)),
                  g3, g3, pl.BlockSpec((SSD_GROUPS, 1, LANES, CHUNK), lambda c: (0, c, 0, 0)), pl.BlockSpec((CHUNK, LANES), lambda c: (c, 0))],
        out_specs=[pl.BlockSpec((CHUNK, LANES), lambda c: (c, 0)), pl.BlockSpec((SUBLANES, LANES), lambda c: (0, 0))],
        out_shape=[jax.ShapeDtypeStruct((t, LANES), BF16), jax.ShapeDtypeStruct((SUBLANES, LANES), F32)],
        compiler_params=_cp(dimension_semantics=("arbitrary",)),
    )(zx, bias, alog, sel, ddtg, dag, dagt, da_comp)


def _chunk_row(k, nctx_c, nc, rev):
    if not rev:
        return k
    return jnp.where(k < nctx_c, nctx_c - 1 - k, nc + nctx_c - 1 - k)


def _pair_consts(dtg, ag, agt, l0, end):
    lane = lax.broadcasted_iota(jnp.int32, (CHUNK, LANES), 1)
    lo = lane < SSD_HEAD_DIM
    a0, a1 = ag[:, l0:l0 + 1], ag[:, l0 + 1:l0 + 2]
    dtp = jnp.where(lo, dtg[:, l0:l0 + 1], dtg[:, l0 + 1:l0 + 2])
    acol = jnp.where(lo, a0, a1)
    aend = acol[end:end + 1, :]
    return lo, a0, a1, agt[l0:l0 + 1, :], agt[l0 + 1:l0 + 2, :], dtp, acol, aend


def ssd_scan(xbc, dtg, ag, agt, d_inner, n_ctx, rev, name):
    t = xbc.shape[0]
    e = d_inner // SSD_HEAD_DIM // SSD_GROUPS
    npair, gw = e // 2, e * SSD_HEAD_DIM
    assert e % 2 == 0 and gw % LANES == 0
    nc, nctx_c = t // CHUNK, n_ctx // CHUNK
    dirn = 1 if rev else 0
    end = 0 if rev else CHUNK - 1
    ridx = lambda k: _chunk_row(k, nctx_c, nc, rev)

    def body(x_ref, b_ref, c_ref, dtg_ref, ag_ref, agt_ref, y_ref, h_ref, state):
        k = pl.program_id(1)

        @pl.when(k == 0)
        def _():
            state[...] = jnp.zeros_like(state)

        bb, cbf = b_ref[...].astype(BF16), c_ref[...].astype(BF16)
        cb = _dot(cbf, bb, _NT)
        row = lax.broadcasted_iota(jnp.int32, (CHUNK, CHUNK), 0)
        col = lax.broadcasted_iota(jnp.int32, (CHUNK, CHUNK), 1)
        mask = (row <= col) if rev else (row >= col)
        rlo = lax.broadcasted_iota(jnp.int32, (LANES, 1), 0) < SSD_HEAD_DIM
        dtg_v, ag_v, agt_v = dtg_ref[0], ag_ref[0], agt_ref[0, 0]
        for p in range(npair):
            sl = slice(p * LANES, (p + 1) * LANES)
            lo, a0, a1, a0r, a1r, dtp, acol, aend = _pair_consts(dtg_v, ag_v, agt_v, dirn * e + 2 * p, end)
            xdt = x_ref[:, sl] * dtp
            xdtb = xdt.astype(BF16)
            w0 = (cb * jnp.exp(jnp.where(mask, a0 - a0r, -jnp.inf))).astype(BF16)
            w1 = (cb * jnp.exp(jnp.where(mask, a1 - a1r, -jnp.inf))).astype(BF16)
            yd = jnp.where(lo, _dot(w0, xdtb), _dot(w1, xdtb))
            xw = (xdt * jnp.exp(aend - acol)).astype(BF16)
            st = _dot(xw, bb, _TN)
            s_in = state[sl, :]
            h_ref[0, 0, sl, :] = s_in
            yo = _dot(cbf, s_in.astype(BF16), _NT)
            y_ref[:, sl] = yd + yo * jnp.exp(acol)
            cd = jnp.where(rlo, jnp.exp(aend[:, 0:1]), jnp.exp(aend[:, LANES - 1:LANES]))
            state[sl, :] = cd * s_in + st

    xcol = d_inner // LANES
    g3 = pl.BlockSpec((1, CHUNK, LANES), lambda g, k: (g, ridx(k), 0))
    return pl.pallas_call(
        body, name=name, grid=(SSD_GROUPS, nc),
        in_specs=[pl.BlockSpec((CHUNK, gw), lambda g, k: (ridx(k), g)),
                  pl.BlockSpec((CHUNK, SSD_STATE), lambda g, k: (ridx(k), xcol + g)),
                  pl.BlockSpec((CHUNK, SSD_STATE), lambda g, k: (ridx(k), xcol + SSD_GROUPS + g)),
                  g3, g3, pl.BlockSpec((1, 1, LANES, CHUNK), lambda g, k: (g, ridx(k), 0, 0))],
        out_specs=[pl.BlockSpec((CHUNK, gw), lambda g, k: (ridx(k), g)),
                   pl.BlockSpec((1, 1, npair * LANES, SSD_STATE), lambda g, k: (ridx(k), g, 0, 0))],
        out_shape=[jax.ShapeDtypeStruct((t, d_inner), F32), jax.ShapeDtypeStruct((nc, SSD_GROUPS, npair * LANES, SSD_STATE), F32)],
        scratch_shapes=[pltpu.VMEM((npair * LANES, SSD_STATE), F32)],
        compiler_params=_cp(dimension_semantics=("parallel", "arbitrary")),
    )(xbc, xbc, xbc, dtg, ag, agt)


def ssd_scan_bwd(dy, xbc, dtg, ag, agt, hst, dskip, prev, d_inner, n_ctx, rev, name):
    t = xbc.shape[0]
    e = d_inner // SSD_HEAD_DIM // SSD_GROUPS
    npair, gw = e // 2, e * SSD_HEAD_DIM
    nc, nctx_c = t // CHUNK, n_ctx // CHUNK
    dirn = 1 if rev else 0
    end = 0 if rev else CHUNK - 1
    ridx = lambda kk: _chunk_row(nc - 1 - kk, nctx_c, nc, rev)
    has_skip, has_prev = dskip is not None, prev is not None

    def body(*refs):
        dy_ref, x_ref, b_ref, c_ref, dtg_ref, ag_ref, agt_ref, h_ref = refs[:8]
        pos = 8
        if has_skip:
            ds_ref = refs[pos]
            pos += 1
        if has_prev:
            pdx_ref, pdb_ref, pdc_ref = refs[pos:pos + 3]
            pos += 3
        dx_ref, db_ref, dc_ref, ddtg_ref, dag_ref, dagt_ref, dd_ref, dstate = refs[pos:]
        kk = pl.program_id(1)

        @pl.when(kk == 0)
        def _():
            dstate[...] = jnp.zeros_like(dstate)

        bb, cbf = b_ref[...].astype(BF16), c_ref[...].astype(BF16)
        cb = _dot(cbf, bb, _NT)
        row = lax.broadcasted_iota(jnp.int32, (CHUNK, CHUNK), 0)
        col = lax.broadcasted_iota(jnp.int32, (CHUNK, CHUNK), 1)
        mask = (row <= col) if rev else (row >= col)
        rlo = lax.broadcasted_iota(jnp.int32, (LANES, 1), 0) < SSD_HEAD_DIM
        is_end = lax.broadcasted_iota(jnp.int32, (CHUNK, 1), 0) == end
        dtg_v, ag_v, agt_v = dtg_ref[0], ag_ref[0], agt_ref[0, 0]
        dcb = jnp.zeros((CHUNK, CHUNK), F32)
        d_c = jnp.zeros((CHUNK, SSD_STATE), F32)
        d_b = jnp.zeros((CHUNK, SSD_STATE), F32)
        ddt_out = jnp.zeros((CHUNK, LANES), F32)
        da_out = jnp.zeros((CHUNK, LANES), F32)
        dat_out = jnp.zeros((LANES, CHUNK), F32)
        for p in range(npair):
            sl = slice(p * LANES, (p + 1) * LANES)
            l0 = dirn * e + 2 * p
            lo, a0, a1, a0r, a1r, dtp, acol, aend = _pair_consts(dtg_v, ag_v, agt_v, l0, end)
            x = x_ref[:, sl]
            dyv = dy_ref[:, sl]
            xdt = x * dtp
            xdtb = xdt.astype(BF16)
            seg0 = jnp.exp(jnp.where(mask, a0 - a0r, -jnp.inf))
            seg1 = jnp.exp(jnp.where(mask, a1 - a1r, -jnp.inf))
            w0, w1 = cb * seg0, cb * seg1
            efs, dte = jnp.exp(acol), jnp.exp(aend - acol)
            xw = (xdt * dte).astype(BF16)
            s_in = h_ref[0, 0, sl, :]
            sb = s_in.astype(BF16)
            dyo = (dyv * efs).astype(BF16)
            da_exp = dyv * _dot(cbf, sb, _NT) * efs
            d_c += _dot(dyo, sb)
            ds_y = _dot(dyo, cbf, _TN)
            dyb = dyv.astype(BF16)
            dy0 = jnp.where(lo, dyv, 0.0).astype(BF16)
            dy1 = jnp.where(lo, 0.0, dyv).astype(BF16)
            dw0, dw1 = _dot(dy0, xdtb, _NT), _dot(dy1, xdtb, _NT)
            dxdt = jnp.where(lo, _dot(w0.astype(BF16), dyb, _TN), _dot(w1.astype(BF16), dyb, _TN))
            dcb += dw0 * seg0 + dw1 * seg1
            t0, t1 = dw0 * w0, dw1 * w1
            d_s = dstate[sl, :]
            dsb = d_s.astype(BF16)
            dxw = _dot(bb, dsb, _NT)
            d_b += _dot(xw, dsb)
            dxdt += dxw * dte
            tmp = dxw * xdt * dte
            da_exp -= tmp
            end_row = jnp.sum(tmp, axis=0, keepdims=True)
            prod = d_s * s_in
            e0, e1 = jnp.exp(aend[:, 0:1]), jnp.exp(aend[:, LANES - 1:LANES])
            sc0 = jnp.sum(jnp.where(rlo, prod, 0.0), keepdims=True) * e0
            sc1 = jnp.sum(jnp.where(rlo, 0.0, prod), keepdims=True) * e1
            dstate[sl, :] = jnp.where(rlo, e0, e1) * d_s + ds_y
            dxv = dxdt * dtp
            if has_skip:
                dxv += dyv * ds_ref[:, sl]
                _acc_rows(dd_ref.at[:, sl], kk, [jnp.sum(dyv * x, axis=0, keepdims=True)])
            if has_prev:
                dxv += pdx_ref[:, sl]
            dx_ref[:, sl] = dxv
            ddt_exp = dxdt * x
            lane = lax.broadcasted_iota(jnp.int32, (CHUNK, LANES), 1)
            sub = lax.broadcasted_iota(jnp.int32, (LANES, CHUNK), 0)
            for j, (sel, tj, scj) in enumerate(((lo, t0, sc0), (~lo, t1, sc1))):
                ddt_col = jnp.sum(jnp.where(sel, ddt_exp, 0.0), axis=1, keepdims=True)
                da_col = jnp.sum(jnp.where(sel, da_exp, 0.0), axis=1, keepdims=True) + jnp.sum(tj, axis=1, keepdims=True)
                da_end = jnp.sum(jnp.where(sel[0:1, :], end_row, 0.0), axis=1, keepdims=True) + scj
                da_col = da_col + jnp.where(is_end, da_end, 0.0)
                ddt_out += jnp.where(lane == l0 + j, ddt_col, 0.0)
                da_out += jnp.where(lane == l0 + j, da_col, 0.0)
                dat_out -= jnp.where(sub == l0 + j, jnp.sum(tj, axis=0, keepdims=True), 0.0)
        dcbb = dcb.astype(BF16)
        d_c += _dot(dcbb, bb)
        d_b += _dot(dcbb, cbf, _TN)
        if has_prev:
            d_b += pdb_ref[...]
            d_c += pdc_ref[...]
        db_ref[...] = d_b
        dc_ref[...] = d_c
        ddtg_ref[0] = ddt_out
        dag_ref[0] = da_out
        dagt_ref[0, 0] = dat_out
        if not has_skip:
            dd_ref[...] = jnp.zeros_like(dd_ref)

    xcol = d_inner // LANES
    xs_spec = pl.BlockSpec((CHUNK, gw), lambda g, kk: (ridx(kk), g))
    bc_spec = pl.BlockSpec((CHUNK, SSD_STATE), lambda g, kk: (ridx(kk), g))
    g3 = pl.BlockSpec((1, CHUNK, LANES), lambda g, kk: (g, ridx(kk), 0))
    g4 = pl.BlockSpec((1, 1, LANES, CHUNK), lambda g, kk: (g, ridx(kk), 0, 0))
    in_specs = [xs_spec, xs_spec,
                pl.BlockSpec((CHUNK, SSD_STATE), lambda g, kk: (ridx(kk), xcol + g)),
                pl.BlockSpec((CHUNK, SSD_STATE), lambda g, kk: (ridx(kk), xcol + SSD_GROUPS + g)),
                g3, g3, g4, pl.BlockSpec((1, 1, npair * LANES, SSD_STATE), lambda g, kk: (ridx(kk), g, 0, 0))]
    args = [dy, xbc, xbc, xbc, dtg, ag, agt, hst]
    if has_skip:
        in_specs.append(pl.BlockSpec((1, gw), lambda g, kk: (0, g)))
        args.append(dskip)
    if has_prev:
        in_specs += [xs_spec, bc_spec, bc_spec]
        args += list(prev)
    gn = SSD_GROUPS * SSD_STATE
    return pl.pallas_call(
        body, name=name, grid=(SSD_GROUPS, nc), in_specs=in_specs,
        out_specs=[xs_spec, bc_spec, bc_spec, g3, g3, g4, pl.BlockSpec((SUBLANES, gw), lambda g, kk: (0, g))],
        out_shape=[jax.ShapeDtypeStruct((t, d_inner), F32), jax.ShapeDtypeStruct((t, gn), F32), jax.ShapeDtypeStruct((t, gn), F32),
                   jax.ShapeDtypeStruct((SSD_GROUPS, t, LANES), F32), jax.ShapeDtypeStruct((SSD_GROUPS, t, LANES), F32),
                   jax.ShapeDtypeStruct((SSD_GROUPS, nc, LANES, CHUNK), F32), jax.ShapeDtypeStruct((SUBLANES, d_inner), F32)],
        scratch_shapes=[pltpu.VMEM((npair * LANES, SSD_STATE), F32)],
        compiler_params=_cp(dimension_semantics=("parallel", "arbitrary")),
    )(*args)


def ssd_gate_norm(yf, yb, xbc, zx, dskip, w, name):
    t, di = yf.shape
    tm = CHUNK

    def body(yf_ref, yb_ref, x_ref, z_ref, d_ref, w_ref, y_ref, o_ref):
        y = yf_ref[...] + yb_ref[...] + x_ref[...] * d_ref[...]
        y_ref[...] = y
        gz = y * _silu(z_ref[...])
        o_ref[...] = (gz * lax.rsqrt(jnp.mean(gz * gz, axis=-1, keepdims=True) + NORM_EPS) * w_ref[...]).astype(BF16)

    blk = pl.BlockSpec((tm, di), lambda i: (i, 0))
    vec = pl.BlockSpec((1, di), lambda i: (0, 0))
    return pl.pallas_call(
        body, name=name, grid=(t // tm,), in_specs=[blk, blk, blk, blk, vec, vec], out_specs=[blk, blk],
        out_shape=[jax.ShapeDtypeStruct((t, di), F32), jax.ShapeDtypeStruct((t, di), BF16)], compiler_params=_cp(),
    )(yf, yb, xbc, zx, dskip, w)


def ssd_gate_norm_bwd(dout, y, zx, w, name):
    t, di = y.shape
    tm = CHUNK

    def body(do_ref, y_ref, z_ref, w_ref, dy_ref, dz_ref, st_ref):
        i = pl.program_id(0)
        z, yv, g = z_ref[...], y_ref[...], do_ref[...]
        sz = _silu(z)
        gz = yv * sz
        r = lax.rsqrt(jnp.mean(gz * gz, axis=-1, keepdims=True) + NORM_EPS)
        n = gz * r
        dn = g * w_ref[...]
        dgz = r * (dn - n * jnp.mean(dn * n, axis=-1, keepdims=True))
        dy_ref[...] = dgz * sz
        dz_ref[...] = (dgz * yv * _dsilu(z)).astype(BF16)
        _acc_rows(st_ref, i, [jnp.sum(g * n, axis=0, keepdims=True)])

    blk = pl.BlockSpec((tm, di), lambda i: (i, 0))
    return pl.pallas_call(
        body, name=name, grid=(t // tm,), in_specs=[blk, blk, blk, pl.BlockSpec((1, di), lambda i: (0, 0))],
        out_specs=[blk, blk, pl.BlockSpec((SUBLANES, di), lambda i: (0, 0))],
        out_shape=[jax.ShapeDtypeStruct((t, di), F32), jax.ShapeDtypeStruct((t, di), BF16), jax.ShapeDtypeStruct((SUBLANES, di), F32)],
        compiler_params=_cp(dimension_semantics=("arbitrary",)),
    )(dout, y, zx, w)


def rope_tables(n_ctx, seq):
    pos = jnp.arange(seq)
    half = ATTN_HEAD_DIM // 4
    inv = ROPE_THETA ** (-jnp.arange(0, 2 * half, 2, dtype=F32) / (2 * half))
    ar = (pos // GRID_W).astype(F32)[:, None] * inv[None, :]
    ac = (pos % GRID_W).astype(F32)[:, None] * inv[None, :]
    cos = jnp.concatenate([jnp.cos(ar), jnp.cos(ar), jnp.cos(ac), jnp.cos(ac)], axis=-1)
    sin = jnp.concatenate([-jnp.sin(ar), jnp.sin(ar), -jnp.sin(ac), jnp.sin(ac)], axis=-1)
    cos = jnp.concatenate([jnp.ones((n_ctx, ATTN_HEAD_DIM), F32), cos], axis=0)
    sin = jnp.concatenate([jnp.zeros((n_ctx, ATTN_HEAD_DIM), F32), sin], axis=0)
    return cos, sin


def _rot(x):
    lane = lax.broadcasted_iota(jnp.int32, x.shape, 1)
    q = ATTN_HEAD_DIM // 4
    return jnp.where((lane % (2 * q)) < q, pltpu.roll(x, ATTN_HEAD_DIM - q, 1), pltpu.roll(x, q, 1))


def qk_prep(qkv, gains, cos, sin, n_q, n_k, name):
    t, c = qkv.shape
    tm = ROW_TILE
    hd = ATTN_HEAD_DIM

    def body(x_ref, g_ref, cos_ref, sin_ref, o_ref):
        cs, sn = cos_ref[...], sin_ref[...]
        for h in range(c // hd):
            sl = slice(h * hd, (h + 1) * hd)
            x = x_ref[:, sl]
            if h < n_q + n_k:
                gain = g_ref[0:1, :] if h < n_q else g_ref[1:2, :]
                xn = x * lax.rsqrt(jnp.mean(x * x, axis=-1, keepdims=True) + NORM_EPS) * gain
                x = xn * cs + _rot(xn) * sn
            o_ref[:, sl] = x.astype(BF16)

    return pl.pallas_call(
        body, name=name, grid=(t // tm,),
        in_specs=[pl.BlockSpec((tm, c), lambda i: (i, 0)), pl.BlockSpec((SUBLANES, hd), lambda i: (0, 0)),
                  pl.BlockSpec((tm, hd), lambda i: (i, 0)), pl.BlockSpec((tm, hd), lambda i: (i, 0))],
        out_specs=pl.BlockSpec((tm, c), lambda i: (i, 0)), out_shape=jax.ShapeDtypeStruct((t, c), BF16), compiler_params=_cp(),
    )(qkv, gains, cos, sin)


def qk_prep_bwd(dq, dk, dv, qkv, gains, cos, sin, name):
    t, c = qkv.shape
    tm = ROW_TILE
    hd = ATTN_HEAD_DIM
    n_q, n_k = dq.shape[1] // hd, dk.shape[1] // hd

    def body(dq_ref, dk_ref, dv_ref, x_ref, g_ref, cos_ref, sin_ref, o_ref, st_ref):
        i = pl.program_id(0)
        cs, sn = cos_ref[...], sin_ref[...]
        dgq = jnp.zeros((1, hd), F32)
        dgk = jnp.zeros((1, hd), F32)
        for h in range(c // hd):
            sl = slice(h * hd, (h + 1) * hd)
            if h >= n_q + n_k:
                hv = h - n_q - n_k
                o_ref[:, sl] = dv_ref[:, hv * hd:(hv + 1) * hd].astype(BF16)
                continue
            is_q = h < n_q
            dy = dq_ref[:, sl] if is_q else dk_ref[:, (h - n_q) * hd:(h - n_q + 1) * hd]
            gain = g_ref[0:1, :] if is_q else g_ref[1:2, :]
            x = x_ref[:, sl]
            r = lax.rsqrt(jnp.mean(x * x, axis=-1, keepdims=True) + NORM_EPS)
            xh = x * r
            dxn = dy * cs + _rot(dy * sn)
            dg = jnp.sum(dxn * xh, axis=0, keepdims=True)
            if is_q:
                dgq += dg
            else:
                dgk += dg
            dxh = dxn * gain
            o_ref[:, sl] = (r * (dxh - xh * jnp.mean(dxh * xh, axis=-1, keepdims=True))).astype(BF16)
        _acc_rows(st_ref, i, [dgq, dgk])

    return pl.pallas_call(
        body, name=name, grid=(t // tm,),
        in_specs=[pl.BlockSpec((tm, n_q * hd), lambda i: (i, 0)), pl.BlockSpec((tm, n_k * hd), lambda i: (i, 0)),
                  pl.BlockSpec((tm, n_k * hd), lambda i: (i, 0)), pl.BlockSpec((tm, c), lambda i: (i, 0)),
                  pl.BlockSpec((SUBLANES, hd), lambda i: (0, 0)), pl.BlockSpec((tm, hd), lambda i: (i, 0)), pl.BlockSpec((tm, hd), lambda i: (i, 0))],
        out_specs=[pl.BlockSpec((tm, c), lambda i: (i, 0)), pl.BlockSpec((SUBLANES, hd), lambda i: (0, 0))],
        out_shape=[jax.ShapeDtypeStruct((t, c), BF16), jax.ShapeDtypeStruct((SUBLANES, hd), F32)],
        compiler_params=_cp(dimension_semantics=("arbitrary",)),
    )(dq, dk, dv, qkv, gains, cos, sin)


def _attn_specs(n_ctx, nb, grp, n_qh):
    hd, blk = ATTN_HEAD_DIM, CHUNK
    kc, vc = n_qh, n_qh + ATTN_KV_HEADS
    specs = [pl.BlockSpec((blk, grp * hd), lambda h, b: (b, h))]
    for c0 in (kc, vc):
        specs += [pl.BlockSpec((n_ctx, hd), lambda h, b, c0=c0: (0, c0 + h)),
                  pl.BlockSpec((blk, hd), lambda h, b, c0=c0: (jnp.maximum(b - 1, 0), c0 + h)),
                  pl.BlockSpec((blk, hd), lambda h, b, c0=c0: (b, c0 + h)),
                  pl.BlockSpec((blk, hd), lambda h, b, c0=c0: (jnp.minimum(b + 1, nb - 1), c0 + h))]
    return specs


def _attn_masks(b, nctx_b, nb):
    row = lax.broadcasted_iota(jnp.int32, (CHUNK, CHUNK), 0)
    col = lax.broadcasted_iota(jnp.int32, (CHUNK, CHUNK), 1)
    lat = b >= nctx_b
    return [(col >= row) & lat & (b - 1 >= nctx_b), jnp.broadcast_to(lat, (CHUNK, CHUNK)), (col <= row) & lat & (b + 1 <= nb - 1)]


def attention(qkvr, sinks, n_ctx, n_qh, name):
    t = qkvr.shape[0]
    hd, blk = ATTN_HEAD_DIM, CHUNK
    grp = n_qh // ATTN_KV_HEADS
    nb, nctx_b = t // blk, n_ctx // blk
    scale = hd ** -0.5

    def body(q_ref, kc_ref, kp_ref, ko_ref, kn_ref, vc_ref, vp_ref, vo_ref, vn_ref, s_ref, o_ref, lse_ref):
        b = pl.program_id(1)
        masks = _attn_masks(b, nctx_b, nb)
        ks = [kc_ref[...], kp_ref[...], ko_ref[...], kn_ref[...]]
        vs = [vc_ref[...], vp_ref[...], vo_ref[...], vn_ref[...]]
        lane = lax.broadcasted_iota(jnp.int32, (blk, LANES), 1)
        lse_out = jnp.zeros((blk, LANES), F32)
        for g in range(grp):
            q = q_ref[:, g * hd:(g + 1) * hd]
            s = [_dot(q, k, _NT) * scale for k in ks]
            s = [s[0]] + [jnp.where(m, sx, -jnp.inf) for m, sx in zip(masks, s[1:])]
            sink = s_ref[0, 0:1, g:g + 1]
            m = sink
            for sx in s:
                m = jnp.maximum(m, jnp.max(sx, axis=-1, keepdims=True))
            p = [jnp.exp(sx - m) for sx in s]
            l = jnp.exp(sink - m)
            for px in p:
                l = l + jnp.sum(px, axis=-1, keepdims=True)
            inv = 1.0 / l
            o = jnp.zeros((blk, hd), F32)
            for px, v in zip(p, vs):
                o += _dot((px * inv).astype(BF16), v)
            o_ref[:, g * hd:(g + 1) * hd] = o.astype(BF16)
            lse_out = jnp.where(lane == g, m + jnp.log(l), lse_out)
        lse_ref[...] = lse_out

    return pl.pallas_call(
        body, name=name, grid=(ATTN_KV_HEADS, nb),
        in_specs=_attn_specs(n_ctx, nb, grp, n_qh) + [pl.BlockSpec((1, SUBLANES, LANES), lambda h, b: (h, 0, 0))],
        out_specs=[pl.BlockSpec((blk, grp * hd), lambda h, b: (b, h)), pl.BlockSpec((blk, LANES), lambda h, b: (b, h))],
        out_shape=[jax.ShapeDtypeStruct((t, n_qh * hd), BF16), jax.ShapeDtypeStruct((t, ATTN_KV_HEADS * LANES), F32)],
        compiler_params=_cp(dimension_semantics=("parallel", "parallel")),
    )(qkvr, qkvr, qkvr, qkvr, qkvr, qkvr, qkvr, qkvr, qkvr, sinks)


def attention_bwd(do, o, lse, qkvr, sinks, n_ctx, n_qh, name):
    t = qkvr.shape[0]
    hd, blk = ATTN_HEAD_DIM, CHUNK
    grp = n_qh // ATTN_KV_HEADS
    nb, nctx_b = t // blk, n_ctx // blk
    scale = hd ** -0.5
    kvw = ATTN_KV_HEADS * hd

    def body(do_ref, o_ref, lse_ref, q_ref, kc_ref, kp_ref, ko_ref, kn_ref, vc_ref, vp_ref, vo_ref, vn_ref, s_ref,
             dq_ref, dkc_ref, dvc_ref, dkp_ref, dvp_ref, dsk_ref):
        b = pl.program_id(1)
        masks = _attn_masks(b, nctx_b, nb)
        ks = [kc_ref[...], kp_ref[...], ko_ref[...], kn_ref[...]]
        vs = [vc_ref[...], vp_ref[...], vo_ref[...], vn_ref[...]]
        lane = lax.broadcasted_iota(jnp.int32, (1, LANES), 1)
        dks = [jnp.zeros(k.shape, F32) for k in ks]
        dvs = [jnp.zeros(v.shape, F32) for v in vs]
        dsk = jnp.zeros((1, LANES), F32)
        for g in range(grp):
            sl = slice(g * hd, (g + 1) * hd)
            q = q_ref[:, sl]
            dof = do_ref[:, sl]
            dob = dof.astype(BF16)
            lse = lse_ref[:, g:g + 1]
            delta = jnp.sum(dof * o_ref[:, sl].astype(F32), axis=-1, keepdims=True)
            s = [_dot(q, k, _NT) * scale for k in ks]
            s = [s[0]] + [jnp.where(m, sx, -jnp.inf) for m, sx in zip(masks, s[1:])]
            dq = jnp.zeros((blk, hd), F32)
            for x in range(4):
                p = jnp.exp(s[x] - lse)
                ds = (p * (_dot(dob, vs[x], _NT) - delta)).astype(BF16)
                dq += _dot(ds, ks[x])
                dks[x] += _dot(ds, q, _TN)
                dvs[x] += _dot(p.astype(BF16), dob, _TN)
            dq_ref[:, sl] = dq * scale
            p_sink = jnp.exp(s_ref[0, 0:1, g:g + 1] - lse)
            dsk = dsk + jnp.where(lane == g, -jnp.sum(p_sink * delta, axis=0, keepdims=True), 0.0)

        @pl.when(b == 0)
        def _():
            dkc_ref[...] = jnp.zeros_like(dkc_ref)
            dvc_ref[...] = jnp.zeros_like(dvc_ref)
            dsk_ref[...] = jnp.zeros_like(dsk_ref)

        dkc_ref[...] += dks[0] * scale
        dvc_ref[...] += dvs[0]
        dsk_ref[0, 0:1, :] += dsk
        for x in range(3):
            dkp_ref[0, x] = dks[x + 1] * scale
            dvp_ref[0, x] = dvs[x + 1]

    part = pl.BlockSpec((1, 3, blk, hd), lambda h, b: (b, 0, 0, h))
    ctxo = pl.BlockSpec((n_ctx, hd), lambda h, b: (0, h))
    return pl.pallas_call(
        body, name=name, grid=(ATTN_KV_HEADS, nb),
        in_specs=[pl.BlockSpec((blk, grp * hd), lambda h, b: (b, h)), pl.BlockSpec((blk, grp * hd), lambda h, b: (b, h)),
                  pl.BlockSpec((blk, LANES), lambda h, b: (b, h))] + _attn_specs(n_ctx, nb, grp, n_qh)
                 + [pl.BlockSpec((1, SUBLANES, LANES), lambda h, b: (h, 0, 0))],
        out_specs=[pl.BlockSpec((blk, grp * hd), lambda h, b: (b, h)), ctxo, ctxo, part, part,
                   pl.BlockSpec((1, SUBLANES, LANES), lambda h, b: (h, 0, 0))],
        out_shape=[jax.ShapeDtypeStruct((t, n_qh * hd), F32), jax.ShapeDtypeStruct((n_ctx, kvw), F32), jax.ShapeDtypeStruct((n_ctx, kvw), F32),
                   jax.ShapeDtypeStruct((nb, 3, blk, kvw), F32), jax.ShapeDtypeStruct((nb, 3, blk, kvw), F32),
                   jax.ShapeDtypeStruct((ATTN_KV_HEADS, SUBLANES, LANES), F32)],
        compiler_params=_cp(dimension_semantics=("parallel", "arbitrary")),
    )(do, o, lse, qkvr, qkvr, qkvr, qkvr, qkvr, qkvr, qkvr, qkvr, qkvr, sinks)


def band_reduce(ctx_part, band_part, n_ctx, name):
    nb, _, blk, w = band_part.shape
    nctx_b = n_ctx // blk

    def body(c_ref, p_ref, o_ref, n_ref, out_ref):
        b = pl.program_id(0)
        band = p_ref[0, 0] + o_ref[0, 0] + jnp.where(b + 1 <= nb - 1, n_ref[0, 0], 0.0)
        out_ref[...] = jnp.where(b < nctx_b, c_ref[...], band)

    return pl.pallas_call(
        body, name=name, grid=(nb,),
        in_specs=[pl.BlockSpec((blk, w), lambda b: (jnp.minimum(b, nctx_b - 1), 0)),
                  pl.BlockSpec((1, 1, blk, w), lambda b: (jnp.maximum(b - 1, 0), 2, 0, 0)),
                  pl.BlockSpec((1, 1, blk, w), lambda b: (b, 1, 0, 0)),
                  pl.BlockSpec((1, 1, blk, w), lambda b: (jnp.minimum(b + 1, nb - 1), 0, 0, 0))],
        out_specs=pl.BlockSpec((blk, w), lambda b: (b, 0)), out_shape=jax.ShapeDtypeStruct((nb * blk, w), F32),
        compiler_params=_cp(),
    )(ctx_part, band_part, band_part, band_part)


def loss_grad(xf, target, n_ctx, name):
    t, d = xf.shape
    tm = _pick(n_ctx, (ROW_TILE, 128))
    nct = n_ctx // tm

    def body(x_ref, t_ref, dy_ref, s_ref):
        i = pl.program_id(0)
        err = jnp.where(i < nct, 0.0, x_ref[...] - t_ref[...])
        dy_ref[...] = err * (1.0 / d)
        _acc_rows(s_ref, i, [jnp.sum(err * err, axis=0, keepdims=True)])

    return pl.pallas_call(
        body, name=name, grid=(t // tm,),
        in_specs=[pl.BlockSpec((tm, d), lambda i: (i, 0)), pl.BlockSpec((tm, d), lambda i: (jnp.maximum(i - nct, 0), 0))],
        out_specs=[pl.BlockSpec((tm, d), lambda i: (i, 0)), pl.BlockSpec((SUBLANES, d), lambda i: (0, 0))],
        out_shape=[jax.ShapeDtypeStruct((t, d), F32), jax.ShapeDtypeStruct((SUBLANES, d), F32)],
        compiler_params=_cp(dimension_semantics=("arbitrary",)),
    )(xf, target)


def adamw(w, g, m, v, name):
    r, c = w.shape
    tr = r
    while tr % 2 == 0 and tr * c * 4 > (1 << 20) and (tr // 2) % SUBLANES == 0:
        tr //= 2
    bc1, bc2 = 1.0 - ADAM_B1 ** ADAM_STEP, 1.0 - ADAM_B2 ** ADAM_STEP

    def body(w_ref, g_ref, m_ref, v_ref, d_ref, nm_ref, nv_ref):
        gv = g_ref[...]
        nm = ADAM_B1 * m_ref[...] + (1.0 - ADAM_B1) * gv
        nv = ADAM_B2 * v_ref[...] + (1.0 - ADAM_B2) * (gv * gv)
        nm_ref[...] = nm
        nv_ref[...] = nv
        d_ref[...] = -ADAM_LR * ((nm / bc1) / (jnp.sqrt(nv / bc2) + ADAM_EPS) + ADAM_WD * w_ref[...])

    blk = pl.BlockSpec((tr, c), lambda i: (i, 0))
    return pl.pallas_call(
        body, name=name, grid=(r // tr,), in_specs=[blk] * 4, out_specs=[blk] * 3,
        out_shape=[jax.ShapeDtypeStruct((r, c), F32)] * 3, compiler_params=_cp(dimension_semantics=("parallel",)),
    )(w, g, m, v)


ADA_ROWS = 16


def ada_fwd(cs, w, name):
    l, d, ns = w.shape
    tn = _pick(ns, (512, 256, 128))

    def body(c_ref, w_ref, o_ref):
        o_ref[0] = _dot(c_ref[...], w_ref[0].astype(BF16))

    return pl.pallas_call(
        body, name=name, grid=(l, ns // tn),
        in_specs=[pl.BlockSpec((ADA_ROWS, d), lambda i, j: (0, 0)), pl.BlockSpec((1, d, tn), lambda i, j: (i, 0, j))],
        out_specs=pl.BlockSpec((1, ADA_ROWS, tn), lambda i, j: (i, 0, j)),
        out_shape=jax.ShapeDtypeStruct((l, ADA_ROWS, ns), F32), compiler_params=_cp(),
    )(cs, w)


def ada_bwd(cs, gmod, w, name):
    l, d, ns = w.shape
    tn = _pick(ns, (512, 256, 128))

    def body(c_ref, g_ref, w_ref, dw_ref, dc_ref):
        first = (pl.program_id(0) == 0) & (pl.program_id(1) == 0)
        gb = g_ref[0].astype(BF16)
        dw_ref[0] = _dot(c_ref[...], gb, _TN)
        part = _dot(gb, w_ref[0].astype(BF16), _NT)

        @pl.when(first)
        def _():
            dc_ref[...] = part

        @pl.when(jnp.logical_not(first))
        def _():
            dc_ref[...] += part

    return pl.pallas_call(
        body, name=name, grid=(l, ns // tn),
        in_specs=[pl.BlockSpec((ADA_ROWS, d), lambda i, j: (0, 0)), pl.BlockSpec((1, ADA_ROWS, tn), lambda i, j: (i, 0, j)),
                  pl.BlockSpec((1, d, tn), lambda i, j: (i, 0, j))],
        out_specs=[pl.BlockSpec((1, d, tn), lambda i, j: (i, 0, j)), pl.BlockSpec((ADA_ROWS, d), lambda i, j: (0, 0))],
        out_shape=[jax.ShapeDtypeStruct((l, d, ns), F32), jax.ShapeDtypeStruct((ADA_ROWS, d), F32)],
        compiler_params=_cp(dimension_semantics=("arbitrary", "arbitrary")),
    )(cs, gmod, w)


def sum_leading(a, name):
    k, r, c = a.shape
    tr = _pick(r, (256, 128, 64, 32, 16, 8))

    def body(a_ref, o_ref):
        acc = a_ref[0]
        for q in range(1, k):
            acc = acc + a_ref[q]
        o_ref[...] = acc

    return pl.pallas_call(
        body, name=name, grid=(r // tr,), in_specs=[pl.BlockSpec((k, tr, c), lambda i: (0, i, 0))],
        out_specs=pl.BlockSpec((tr, c), lambda i: (i, 0)), out_shape=jax.ShapeDtypeStruct((r, c), F32), compiler_params=_cp(),
    )(a)


def _mesh_pos():
    return lax.axis_index("x"), lax.axis_index("y"), lax.axis_index("c")


def _other_chips(x, y):
    return [(1 - x, y), (x, 1 - y), (1 - x, 1 - y)]


def _rcopy(src, dst, send_sems, recv_sems, k, to):
    return pltpu.make_async_remote_copy(src_ref=src, dst_ref=dst, send_sem=send_sems.at[k], recv_sem=recv_sems.at[k],
                                        device_id=to, device_id_type=MESH)


def small_allgather(v, name):
    r, c_ = v.shape

    def body(v_ref, out_ref, send_sems, recv_sems, local_sem):
        x, y, c = _mesh_pos()
        sibling = (x, y, 1 - c)
        chips = _other_chips(x, y)

        def blk(px, py, pc):
            return out_ref.at[4 * px + 2 * py + pc]

        mine = pltpu.make_async_copy(v_ref, blk(x, y, c), local_sem)
        mine.start()
        first = [_rcopy(v_ref, blk(x, y, c), send_sems, recv_sems, 0, sibling)]
        first += [_rcopy(v_ref, blk(x, y, c), send_sems, recv_sems, 1 + j, (*chip, c)) for j, chip in enumerate(chips)]
        for cp in first:
            cp.start()
        passed = [_rcopy(blk(*chip, c), blk(*chip, c), send_sems, recv_sems, 4 + j, sibling) for j, chip in enumerate(chips)]
        for j, chip in enumerate(chips):
            _rcopy(blk(*chip, c), blk(*chip, c), send_sems, recv_sems, 1 + j, (x, y, c)).wait_recv()
            passed[j].start()
        _rcopy(blk(x, y, 1 - c), blk(x, y, 1 - c), send_sems, recv_sems, 0, (x, y, c)).wait_recv()
        for j, chip in enumerate(chips):
            _rcopy(blk(*chip, 1 - c), blk(*chip, 1 - c), send_sems, recv_sems, 4 + j, (x, y, c)).wait_recv()
        for cp in first + passed:
            cp.wait_send()
        mine.wait()

    return pl.pallas_call(
        body, name=name, out_shape=jax.ShapeDtypeStruct((N_DEV, r, c_), v.dtype),
        in_specs=[pl.BlockSpec(memory_space=pltpu.VMEM)], out_specs=pl.BlockSpec(memory_space=pltpu.VMEM),
        scratch_shapes=[pltpu.SemaphoreType.DMA((7,)), pltpu.SemaphoreType.DMA((7,)), pltpu.SemaphoreType.DMA],
        compiler_params=_cp(),
    )(v)


_HBM = pl.BlockSpec(memory_space=pltpu.HBM)


def weights_allgather(w, name):
    _, r, c_ = w.shape

    def body(w_ref, out_ref, send_sems, recv_sems, local_sem):
        x, y, c = _mesh_pos()
        sibling = (x, y, 1 - c)
        chips = _other_chips(x, y)

        def half(px, py, pc):
            return out_ref.at[2 * px + py, pc]

        mine = pltpu.make_async_copy(w_ref, out_ref.at[2 * x + y], local_sem)
        mine.start()
        first = [_rcopy(w_ref.at[c], half(x, y, c), send_sems, recv_sems, j, (*chip, c)) for j, chip in enumerate(chips)]
        for cp in first:
            cp.start()
        passed = [_rcopy(half(*chip, c), half(*chip, c), send_sems, recv_sems, 3 + j, sibling) for j, chip in enumerate(chips)]
        for j, chip in enumerate(chips):
            _rcopy(half(*chip, c), half(*chip, c), send_sems, recv_sems, j, (x, y, c)).wait_recv()
            passed[j].start()
        for j, chip in enumerate(chips):
            _rcopy(half(*chip, 1 - c), half(*chip, 1 - c), send_sems, recv_sems, 3 + j, (x, y, c)).wait_recv()
        for cp in first + passed:
            cp.wait_send()
        mine.wait()

    return pl.pallas_call(
        body, name=name, out_shape=jax.ShapeDtypeStruct((N_CHIP, 2, r, c_), w.dtype),
        in_specs=[_HBM], out_specs=_HBM,
        scratch_shapes=[pltpu.SemaphoreType.DMA((6,)), pltpu.SemaphoreType.DMA((6,)), pltpu.SemaphoreType.DMA],
        compiler_params=_cp(),
    )(w)


def grads_pair_exchange(g, name):
    n, _, r, c_ = g.shape

    def body(g_ref, out_ref, send_sems, recv_sems):
        x, y, c = _mesh_pos()
        cp = _rcopy(g_ref.at[:, 1 - c], out_ref, send_sems, recv_sems, 0, (x, y, 1 - c))
        cp.start()
        cp.wait()

    return pl.pallas_call(
        body, name=name, out_shape=jax.ShapeDtypeStruct((n, r, c_), g.dtype), in_specs=[_HBM], out_specs=_HBM,
        scratch_shapes=[pltpu.SemaphoreType.DMA((1,)), pltpu.SemaphoreType.DMA((1,))], compiler_params=_cp(),
    )(g)


def pair_add(g, got, name):
    n, _, r, c_ = g.shape
    tr = _pick(r, (512, 256, 128, 64, 32, 16))

    def body(c_ref, g_ref, o_ref, out_ref):
        out_ref[...] = (g_ref[:, 0].astype(F32) + o_ref[...].astype(F32)).astype(out_ref.dtype)

    grid_spec = pltpu.PrefetchScalarGridSpec(
        num_scalar_prefetch=1, grid=(r // tr,),
        in_specs=[pl.BlockSpec((n, 1, tr, c_), lambda i, c_ref: (0, c_ref[0], i, 0)), pl.BlockSpec((n, tr, c_), lambda i, c_ref: (0, i, 0))],
        out_specs=pl.BlockSpec((n, tr, c_), lambda i, c_ref: (0, i, 0)))
    return pl.pallas_call(body, name=name, grid_spec=grid_spec, out_shape=jax.ShapeDtypeStruct((n, r, c_), g.dtype),
                          compiler_params=_cp())(lax.axis_index("c").astype(jnp.int32).reshape(1), g, got)


def grads_chip_exchange(p, name):
    n, r, c_ = p.shape

    def body(p_ref, out_ref, send_sems, recv_sems, local_sem):
        x, y, c = _mesh_pos()
        chips = _other_chips(x, y)
        me = 2 * x + y
        mine = pltpu.make_async_copy(p_ref.at[me], out_ref.at[me], local_sem)
        mine.start()
        sends = [_rcopy(p_ref.at[2 * px + py], out_ref.at[me], send_sems, recv_sems, j, (px, py, c)) for j, (px, py) in enumerate(chips)]
        for cp in sends:
            cp.start()
        for j, (px, py) in enumerate(chips):
            _rcopy(p_ref.at[me], out_ref.at[2 * px + py], send_sems, recv_sems, j, (x, y, c)).wait_recv()
        for cp in sends:
            cp.wait_send()
        mine.wait()

    return pl.pallas_call(
        body, name=name, out_shape=jax.ShapeDtypeStruct((n, r, c_), p.dtype), in_specs=[_HBM], out_specs=_HBM,
        scratch_shapes=[pltpu.SemaphoreType.DMA((3,)), pltpu.SemaphoreType.DMA((3,)), pltpu.SemaphoreType.DMA],
        compiler_params=_cp(),
    )(p)


def sum_chips(a, name):
    k, r, c_ = a.shape
    tr = _pick(r, (512, 256, 128, 64, 32, 16))

    def body(a_ref, o_ref):
        acc = a_ref[0].astype(F32)
        for q in range(1, k):
            acc = acc + a_ref[q].astype(F32)
        o_ref[...] = acc

    return pl.pallas_call(
        body, name=name, grid=(r // tr,), in_specs=[pl.BlockSpec((k, tr, c_), lambda i: (0, i, 0))],
        out_specs=pl.BlockSpec((tr, c_), lambda i: (i, 0)), out_shape=jax.ShapeDtypeStruct((r, c_), F32), compiler_params=_cp(),
    )(a)


def halves_exchange(h, name):
    r, c_ = h.shape

    def body(h_ref, out_ref, send_sems, recv_sems, local_sem):
        x, y, c = _mesh_pos()
        mine = pltpu.make_async_copy(h_ref, out_ref.at[c], local_sem)
        mine.start()
        cp = _rcopy(h_ref, out_ref.at[c], send_sems, recv_sems, 0, (x, y, 1 - c))
        cp.start()
        _rcopy(h_ref, out_ref.at[1 - c], send_sems, recv_sems, 0, (x, y, c)).wait_recv()
        cp.wait_send()
        mine.wait()

    return pl.pallas_call(
        body, name=name, out_shape=jax.ShapeDtypeStruct((2, r, c_), h.dtype), in_specs=[_HBM], out_specs=_HBM,
        scratch_shapes=[pltpu.SemaphoreType.DMA((1,)), pltpu.SemaphoreType.DMA((1,)), pltpu.SemaphoreType.DMA],
        compiler_params=_cp(),
    )(h)


FLAT_COLS = 1024
FLAT_ROW_MULT = 512


def _pack_small(arrs):
    flat = jnp.concatenate([a.reshape(-1).astype(F32) for a in arrs])
    n = flat.shape[0]
    pad = -n % (SUBLANES * LANES)
    return jnp.pad(flat, (0, pad)).reshape(-1, LANES)


def _unpack_small(packed, shapes):
    flat = packed.reshape(-1)
    out, off = [], 0
    for s in shapes:
        n = math.prod(s)
        out.append(flat[off:off + n].reshape(s))
        off += n
    return out


def _flat_rows(n):
    per_half = -(-n // 2)
    rows = -(-per_half // FLAT_COLS)
    return -(-rows // FLAT_ROW_MULT) * FLAT_ROW_MULT


def _shard_major(layer, kind):
    k, n = layer.shape
    if kind == "col":
        return layer.reshape(k, N_CHIP, n // N_CHIP).transpose(1, 0, 2).reshape(N_CHIP, -1)
    return layer.reshape(N_CHIP, -1)


def _from_shard_major(flat4, shard_shape, kind):
    l, k, n = shard_shape
    s = flat4.reshape(N_CHIP, l, k, n)
    if kind == "col":
        return [s[:, i].transpose(1, 0, 2).reshape(k, N_CHIP * n) for i in range(l)]
    return [s[:, i].reshape(N_CHIP * k, n) for i in range(l)]


BIG = (("ssd_w_in", "col"), ("ssd_w_out", "row"), ("attn_w_qkv", "col"), ("attn_w_o", "row"), ("ffn_w_up", "col"), ("ffn_w_down", "row"))
WEIGHTS = ("c_ctx", "ada_w", "ada_b", "norm1_w", "norm2_w", "ssd_w_in", "ssd_conv_w", "ssd_conv_b", "ssd_dt_bias_f", "ssd_dt_bias_b",
           "ssd_a_log_f", "ssd_a_log_b", "ssd_d", "ssd_norm_w", "ssd_w_out", "attn_w_qkv", "attn_q_gain", "attn_k_gain", "attn_sinks",
           "attn_w_o", "ffn_w_up", "ffn_conv_w", "ffn_conv_b", "ffn_w_down")


def _taps_bias(w3, b):
    return jnp.concatenate([w3, b[None, :], jnp.zeros((SUBLANES - 4, w3.shape[1]), F32)], axis=0)


def kernel(x, c, ctx, c_ctx, ada_w, ada_b, norm1_w, norm2_w, ssd_w_in, ssd_conv_w, ssd_conv_b, ssd_dt_bias_f, ssd_dt_bias_b, ssd_a_log_f, ssd_a_log_b, ssd_d, ssd_norm_w, ssd_w_out, attn_w_qkv, attn_q_gain, attn_k_gain, attn_sinks, attn_w_o, ffn_w_up, ffn_conv_w, ffn_conv_b, ffn_w_down, loss_target, m_c_ctx, m_ada_w, m_ada_b, m_norm1_w, m_norm2_w, m_ssd_w_in, m_ssd_conv_w, m_ssd_conv_b, m_ssd_dt_bias_f, m_ssd_dt_bias_b, m_ssd_a_log_f, m_ssd_a_log_b, m_ssd_d, m_ssd_norm_w, m_ssd_w_out, m_attn_w_qkv, m_attn_q_gain, m_attn_k_gain, m_attn_sinks, m_attn_w_o, m_ffn_w_up, m_ffn_conv_w, m_ffn_conv_b, m_ffn_w_down, v_c_ctx, v_ada_w, v_ada_b, v_norm1_w, v_norm2_w, v_ssd_w_in, v_ssd_conv_w, v_ssd_conv_b, v_ssd_dt_bias_f, v_ssd_dt_bias_b, v_ssd_a_log_f, v_ssd_a_log_b, v_ssd_d, v_ssd_norm_w, v_ssd_w_out, v_attn_w_qkv, v_attn_q_gain, v_attn_k_gain, v_attn_sinks, v_attn_w_o, v_ffn_w_up, v_ffn_conv_w, v_ffn_conv_b, v_ffn_w_down):
    args = locals()
    w = {n: args[n] for n in WEIGHTS}
    mom = {n: args["m_" + n] for n in WEIGHTS}
    var = {n: args["v_" + n] for n in WEIGHTS}

    ix, iy, ic = _mesh_pos()
    chip = 2 * ix + iy
    dev = 2 * chip + ic
    depth, d = norm1_w.shape
    n_ctx, seq = ctx.shape[1], x.shape[1]
    t = n_ctx + seq
    d_inner = ssd_norm_w.shape[1]
    heads = ssd_d.shape[1]
    n_qh = attn_sinks.shape[1]
    grp = n_qh // ATTN_KV_HEADS
    d_ff = ffn_w_down.shape[1] * N_CHIP
    xbc_w = ssd_conv_b.shape[1]
    dt_col = d_inner + xbc_w
    n_ssd, n_att = ssd_w_in.shape[0], attn_w_qkv.shape[0]

    small_in = [c[0], ssd_conv_w, ffn_conv_w]
    g_in = small_allgather(_pack_small(small_in), "gather_cond")
    per_dev = [_unpack_small(g_in[k], [a.shape for a in small_in]) for k in range(N_DEV)]
    c_all = jnp.stack([p[0] for p in per_dev])
    ssd_conv_full = jnp.concatenate([per_dev[2 * j][1] for j in range(N_CHIP)], axis=-1)
    ffn_conv_full = jnp.concatenate([per_dev[2 * j][2] for j in range(N_CHIP)], axis=-1)

    cvec = jnp.concatenate([c_all, c_ctx[None, :], jnp.zeros((ADA_ROWS - N_DEV - 1, d), F32)], axis=0)
    cs16 = _silu(cvec).astype(BF16)
    mod_cols = ada_fwd(cs16, ada_w, "ada_fwd")
    ns_ada = mod_cols.shape[-1]
    g_mod = small_allgather(mod_cols.reshape(depth * ADA_ROWS, ns_ada), "gather_mod").reshape(N_DEV, depth, ADA_ROWS, ns_ada)
    mod_all = jnp.concatenate([g_mod[2 * j] for j in range(N_CHIP)], axis=-1) + ada_b[:, None, :]
    mod_lat = lax.dynamic_index_in_dim(mod_all, dev, axis=1, keepdims=False)
    mod_ctx = mod_all[:, N_DEV]
    mods = jnp.stack([mod_ctx, mod_lat], axis=1).reshape(depth, 2, 6, d)

    shards = [w[n] for n, _ in BIG]
    flat = jnp.concatenate([s.reshape(-1).astype(BF16) for s in shards])
    n_big = flat.shape[0]
    rows = _flat_rows(n_big)
    flat = jnp.pad(flat, (0, 2 * rows * FLAT_COLS - n_big)).reshape(2, rows, FLAT_COLS)
    gathered = weights_allgather(flat, "gather_weights").reshape(N_CHIP, -1)
    full, off = {}, 0
    for (n, kind), s in zip(BIG, shards):
        cnt = math.prod(s.shape)
        full[n] = _from_shard_major(gathered[:, off:off + cnt], s.shape, kind)
        off += cnt

    cos, sin = rope_tables(n_ctx, seq)
    sel = _group_select(heads, heads // SSD_GROUPS)
    bias128 = jnp.concatenate([ssd_dt_bias_f, ssd_dt_bias_b], axis=-1)[:, None, :]
    alog128 = jnp.concatenate([ssd_a_log_f, ssd_a_log_b], axis=-1)[:, None, :]
    dskip = jnp.repeat(ssd_d, SSD_HEAD_DIM, axis=-1)[:, None, :]
    gains = jnp.zeros((n_att, SUBLANES, ATTN_HEAD_DIM), F32).at[:, 0].set(attn_q_gain).at[:, 1].set(attn_k_gain)
    sinks3 = jnp.zeros((n_att, ATTN_KV_HEADS, SUBLANES, LANES), F32).at[:, :, 0, :grp].set(attn_sinks.reshape(n_att, ATTN_KV_HEADS, grp))
    wb_ssd = [_taps_bias(ssd_conv_full[j], ssd_conv_b[j]) for j in range(n_ssd)]
    wb_ffn = [_taps_bias(ffn_conv_full[i], ffn_conv_b[i]) for i in range(depth)]

    xs = jnp.concatenate([ctx[0], x[0]], axis=0)
    saved = []
    for i in range(depth):
        j = i // 2
        sh1, sc1, g1, sh2, sc2, g2 = [mods[i, :, q] for q in range(6)]
        s = {"x": xs}
        h1 = norm_mod(xs, norm1_w[i:i + 1], sh1, sc1, n_ctx, f"l{i}_norm1")
        s["h1"] = h1
        if i % 2 == 0:
            zx = matmul(h1, full["ssd_w_in"][j], "nn", f"l{i}_ssd_in")
            pre, xbc = dwconv_act(zx, d_inner, xbc_w, wb_ssd[j], n_ctx, "silu", F32, f"l{i}_ssd_conv")
            _, da, dtg, ag, agt = ssd_prep(zx, dt_col, bias128[j], alog128[j], sel, heads, f"l{i}_ssd_prep")
            yf, hf = ssd_scan(xbc, dtg, ag, agt, d_inner, n_ctx, False, f"l{i}_ssd_scan_f")
            yb, hb = ssd_scan(xbc, dtg, ag, agt, d_inner, n_ctx, True, f"l{i}_ssd_scan_b")
            ytot, yn = ssd_gate_norm(yf, yb, xbc, zx, dskip[j], ssd_norm_w[j:j + 1], f"l{i}_ssd_gate_norm")
            x1, mix = matmul_gate_res(yn, full["ssd_w_out"][j], xs, g1, n_ctx, f"l{i}_ssd_out")
            s.update(zx=zx, pre=pre, xbc=xbc, da=da, dtg=dtg, ag=ag, agt=agt, hf=hf, hb=hb, ytot=ytot, yn=yn)
        else:
            qkv = matmul(h1, full["attn_w_qkv"][j], "nn", f"l{i}_attn_qkv")
            qkvr = qk_prep(qkv, gains[j], cos, sin, n_qh, ATTN_KV_HEADS, f"l{i}_qk_prep")
            o, lse = attention(qkvr, sinks3[j], n_ctx, n_qh, f"l{i}_attn")
            x1, mix = matmul_gate_res(o, full["attn_w_o"][j], xs, g1, n_ctx, f"l{i}_attn_out")
            s.update(qkv=qkv, qkvr=qkvr, o=o, lse=lse)
        h2 = norm_mod(x1, norm2_w[i:i + 1], sh2, sc2, n_ctx, f"l{i}_norm2")
        u = matmul(h2, full["ffn_w_up"][i], "nn", f"l{i}_ffn_up")
        pv, pg, act = dwconv_act(u, 0, 2 * d_ff, wb_ffn[i], n_ctx, "glu", BF16, f"l{i}_ffn_conv")
        x2, f = matmul_gate_res(act, full["ffn_w_down"][i], x1, g2, n_ctx, f"l{i}_ffn_down")
        s.update(mix=mix, x1=x1, h2=h2, u=u, pv=pv, pg=pg, act=act, f=f)
        saved.append(s)
        xs = x2

    dxs, sq = loss_grad(xs, loss_target[0], n_ctx, "loss")
    loss = lax.psum(0.5 / d * jnp.sum(sq[0]), ("x", "y", "c"))

    big_g = {n: [None] * w[n].shape[0] for n, _ in BIG}
    gsm = {n: [None] * w[n].shape[0] for n in ("norm1_w", "norm2_w", "ssd_conv_w", "ssd_conv_b", "ssd_dt_bias_f", "ssd_dt_bias_b", "ssd_a_log_f",
                                               "ssd_a_log_b", "ssd_d", "ssd_norm_w", "attn_q_gain", "attn_k_gain", "attn_sinks", "ffn_conv_w", "ffn_conv_b")}
    dmod = [None] * depth
    for i in reversed(range(depth)):
        j = i // 2
        s = saved[i]
        sh1, sc1, g1, sh2, sc2, g2 = [mods[i, :, q] for q in range(6)]
        df, dg2 = gate_bwd(dxs, s["f"], g2, n_ctx, f"l{i}_ffn_gate_bwd")
        dact = matmul(df, full["ffn_w_down"][i], "nt", f"l{i}_ffn_down_dx")
        big_g["ffn_w_down"][i] = matmul(s["act"], df, "tn", f"l{i}_ffn_down_dw")
        (duv, dug), (dwv, dwg) = dwconv_act_bwd(dact, [s["pv"], s["pg"]], s["u"], 0, 2 * d_ff, wb_ffn[i], n_ctx, "glu", f"l{i}_ffn_conv_bwd")
        du = jnp.concatenate([duv, dug], axis=1)
        gsm["ffn_conv_w"][i] = jnp.concatenate([dwv[0:3], dwg[0:3]], axis=1)
        gsm["ffn_conv_b"][i] = jnp.concatenate([dwv[3], dwg[3]], axis=0)
        dh2 = matmul(du, full["ffn_w_up"][i], "nt", f"l{i}_ffn_up_dx")
        big_g["ffn_w_up"][i] = matmul(s["h2"], du, "tn", f"l{i}_ffn_up_dw")
        dx1, st2 = norm_mod_bwd(dh2, dxs, s["x1"], norm2_w[i:i + 1], sc2, n_ctx, f"l{i}_norm2_bwd")
        gsm["norm2_w"][i] = st2[4]
        dmix, dg1 = gate_bwd(dx1, s["mix"], g1, n_ctx, f"l{i}_mix_gate_bwd")
        if i % 2 == 0:
            dyn = matmul(dmix, full["ssd_w_out"][j], "nt", f"l{i}_ssd_out_dx")
            big_g["ssd_w_out"][j] = matmul(s["yn"], dmix, "tn", f"l{i}_ssd_out_dw")
            dy, dz, stn = ssd_gate_norm_bwd(dyn, s["ytot"], s["zx"], ssd_norm_w[j:j + 1], f"l{i}_ssd_gate_norm_bwd")
            gsm["ssd_norm_w"][j] = stn[0]
            o1 = ssd_scan_bwd(dy, s["xbc"], s["dtg"], s["ag"], s["agt"], s["hf"], dskip[j], None, d_inner, n_ctx, False, f"l{i}_ssd_scan_f_bwd")
            o2 = ssd_scan_bwd(dy, s["xbc"], s["dtg"], s["ag"], s["agt"], s["hb"], None, o1[:3], d_inner, n_ctx, True, f"l{i}_ssd_scan_b_bwd")
            gsm["ssd_d"][j] = jnp.sum(o1[6][0].reshape(heads, SSD_HEAD_DIM), axis=-1)
            dxbc = jnp.concatenate(o2[:3], axis=1)
            (dxbc_raw,), (dwb,) = dwconv_act_bwd(dxbc, [s["pre"]], s["zx"], d_inner, xbc_w, wb_ssd[j], n_ctx, "silu", f"l{i}_ssd_conv_bwd")
            gsm["ssd_conv_w"][j], gsm["ssd_conv_b"][j] = dwb[0:3], dwb[3]
            draw, stt = ssd_prep_bwd(s["zx"], dt_col, bias128[j], alog128[j], sel, o1[3] + o2[3], o1[4] + o2[4], o1[5] + o2[5], s["da"], heads,
                                     f"l{i}_ssd_prep_bwd")
            gsm["ssd_dt_bias_f"][j], gsm["ssd_dt_bias_b"][j] = stt[0, :heads], stt[0, heads:]
            gsm["ssd_a_log_f"][j], gsm["ssd_a_log_b"][j] = stt[1, :heads], stt[1, heads:]
            dzx = jnp.concatenate([dz, dxbc_raw, draw], axis=1)
            dh1 = matmul(dzx, full["ssd_w_in"][j], "nt", f"l{i}_ssd_in_dx")
            big_g["ssd_w_in"][j] = matmul(s["h1"], dzx, "tn", f"l{i}_ssd_in_dw")
        else:
            do = matmul(dmix, full["attn_w_o"][j], "nt", f"l{i}_attn_out_dx")
            big_g["attn_w_o"][j] = matmul(s["o"], dmix, "tn", f"l{i}_attn_out_dw")
            dq, dkc, dvc, dkp, dvp, dsk = attention_bwd(do, s["o"], s["lse"], s["qkvr"], sinks3[j], n_ctx, n_qh, f"l{i}_attn_bwd")
            dk = band_reduce(dkc, dkp, n_ctx, f"l{i}_dk_reduce")
            dv = band_reduce(dvc, dvp, n_ctx, f"l{i}_dv_reduce")
            dqkv, stg = qk_prep_bwd(dq, dk, dv, s["qkv"], gains[j], cos, sin, f"l{i}_qk_prep_bwd")
            gsm["attn_q_gain"][j], gsm["attn_k_gain"][j] = stg[0], stg[1]
            gsm["attn_sinks"][j] = dsk[:, 0, :grp].reshape(-1)
            dh1 = matmul(dqkv, full["attn_w_qkv"][j], "nt", f"l{i}_attn_qkv_dx")
            big_g["attn_w_qkv"][j] = matmul(s["h1"], dqkv, "tn", f"l{i}_attn_qkv_dw")
        dxs, st1 = norm_mod_bwd(dh1, dx1, s["x"], norm1_w[i:i + 1], sc1, n_ctx, f"l{i}_norm1_bwd")
        gsm["norm1_w"][i] = st1[4]
        dmod[i] = jnp.stack([st1[0:2], st1[2:4], dg1[0:2], st2[0:2], st2[2:4], dg2[0:2]], axis=1)
    grad_x = dxs[n_ctx:][None]

    dmod_all = jnp.stack(dmod)
    small_names = list(gsm)
    small_out = [jnp.stack(gsm[n]) for n in small_names] + [dmod_all[:, 0].reshape(depth, 6 * d), dmod_all[:, 1].reshape(depth, 6 * d)]
    packed = _pack_small(small_out)
    g_small = small_allgather(packed, "gather_small_grads")
    summed = _unpack_small(sum_leading(g_small, "sum_small_grads"), [a.shape for a in small_out])
    grads = dict(zip(small_names, summed[:len(small_names)]))
    dctx_sum, dlat_sum = summed[-2], summed[-1]
    grads["ada_b"] = dctx_sum + dlat_sum
    n_lat = depth * 6 * d
    lat_off = sum(a.size for a in small_out[:-1])
    dlat_rows = g_small.reshape(N_DEV, -1)[:, lat_off:lat_off + n_lat].reshape(N_DEV, depth, 6 * d)
    gmod = jnp.concatenate([dlat_rows, dctx_sum[None], jnp.zeros((ADA_ROWS - N_DEV - 1, depth, 6 * d), F32)], axis=0).transpose(1, 0, 2)
    gmod_cols = lax.dynamic_slice_in_dim(gmod, chip * ns_ada, ns_ada, axis=2)
    grads["ada_w"], dcs = ada_bwd(cs16, gmod_cols, ada_w, "ada_bwd")
    dcc = (dcs[N_DEV] * _dsilu(c_ctx))[None, :]
    g_dcc = small_allgather(jnp.pad(dcc, ((0, SUBLANES - 1), (0, 0))), "gather_dc_ctx")
    grads["c_ctx"] = sum_leading(g_dcc[0::2], "sum_dc_ctx")[0]
    grads["ssd_conv_w"] = lax.dynamic_slice_in_dim(grads["ssd_conv_w"], chip * ssd_conv_w.shape[-1], ssd_conv_w.shape[-1], axis=2)
    grads["ffn_conv_w"] = lax.dynamic_slice_in_dim(grads["ffn_conv_w"], chip * ffn_conv_w.shape[-1], ffn_conv_w.shape[-1], axis=2)

    g4 = jnp.concatenate([_shard_major(layer, kind) for n, kind in BIG for layer in big_g[n]], axis=1)
    g4 = jnp.pad(g4, ((0, 0), (0, 2 * rows * FLAT_COLS - n_big))).astype(BF16).reshape(N_CHIP, 2, rows, FLAT_COLS)
    got = grads_pair_exchange(g4, "rs_pair_exchange")
    pair = pair_add(g4, got, "rs_pair_add")
    landed = grads_chip_exchange(pair, "rs_chip_exchange")
    half = sum_chips(landed, "rs_chip_sum")
    red = halves_exchange(half, "rs_halves_exchange").reshape(-1)
    off = 0
    for (n, _), s in zip(BIG, shards):
        cnt = math.prod(s.shape)
        grads[n] = red[off:off + cnt].reshape(s.shape)
        off += cnt

    delta, new_m, new_v = {}, {}, {}
    for n in ("ada_w",) + tuple(b for b, _ in BIG):
        shp = w[n].shape
        two = lambda a: a.reshape(-1, shp[-1])
        dl, nm, nv = adamw(two(w[n]), two(grads[n]), two(mom[n]), two(var[n]), f"adamw_{n}")
        delta[n], new_m[n], new_v[n] = dl.reshape(shp), nm.reshape(shp), nv.reshape(shp)
    rest = [n for n in WEIGHTS if n not in delta]
    shapes = [w[n].shape for n in rest]
    outs = adamw(_pack_small([w[n] for n in rest]), _pack_small([grads[n] for n in rest]), _pack_small([mom[n] for n in rest]),
                 _pack_small([var[n] for n in rest]), "adamw_small")
    for res, tgt in zip(outs, (delta, new_m, new_v)):
        for n, a in zip(rest, _unpack_small(res, shapes)):
            tgt[n] = a
    grads = {n: grads[n].reshape(w[n].shape) for n in WEIGHTS}
    return (loss, grad_x, *[grads[n] for n in WEIGHTS], *[delta[n] for n in WEIGHTS], *[new_m[n] for n in WEIGHTS], *[new_v[n] for n in WEIGHTS])
```

```python
import functools
import math

import jax
import jax.numpy as jnp
from jax import lax
from jax.experimental import pallas as pl
from jax.experimental.pallas import tpu as pltpu

F32 = jnp.float32
BF16 = jnp.bfloat16

SSD_HEAD_DIM = 64
SSD_GROUPS = 8
SSD_STATE = 128
CHUNK = 128
ATTN_HEAD_DIM = 128
ATTN_KV_HEADS = 4
GRID_W = 64
ROPE_THETA = 10000.0
NORM_EPS = 1e-6
ADAM_LR, ADAM_B1, ADAM_B2, ADAM_EPS, ADAM_WD, ADAM_STEP = 0.001, 0.9, 0.999, 1e-08, 0.01, 10

LANES = 128
SUBLANES = 8
VMEM_LIMIT = 56 * 1024 * 1024
MESH = pl.DeviceIdType.MESH
N_DEV = 8
N_CHIP = 4


def _cp(**kw):
    return pltpu.CompilerParams(vmem_limit_bytes=VMEM_LIMIT, **kw)


def _pick(n, cands):
    for c in cands:
        if n % c == 0:
            return c
    return n


def _silu(x):
    return x * jax.nn.sigmoid(x)


def _dsilu(x):
    s = jax.nn.sigmoid(x)
    return s * (1.0 + x * (1.0 - s))


_DIMS = {"nn": (((1,), (0,)), ((), ())), "nt": (((1,), (1,)), ((), ())), "tn": (((0,), (0,)), ((), ()))}


TILES_M = (1408, 768, 512, 384, 256, 128)
TILES_N = (1408, 1024, 1152, 768, 512, 384, 256, 128)
TILES_K = (2048, 1408, 1152, 1024, 768, 512, 384, 256, 128)


class Mat:
    def __init__(self, arr, kind="plain", rows=None, row0=0, base=0, nparts=1):
        self.arr, self.kind, self.row0, self.base, self.nparts = arr, kind, row0, base, nparts
        if kind == "cols3":
            self.r, self.s = arr.shape[1], arr.shape[2] * nparts
        else:
            self.r, self.s = (rows if rows is not None else arr.shape[0]), arr.shape[1]

    def s_unit(self):
        return self.s // self.nparts

    def spec(self, tr, ts, r_of, s_of):
        if self.kind == "cols3":
            nps = self.s // self.nparts // ts
            return pl.BlockSpec((None, tr, ts), lambda *ids: (self.base + s_of(*ids) // nps, r_of(*ids), s_of(*ids) % nps))
        off = self.row0 // tr
        return pl.BlockSpec((tr, ts), lambda *ids: (off + r_of(*ids), s_of(*ids)))


MATMUL_VMEM_BUDGET = 36 * 1024 * 1024


def _fit_tiles(m, n, k, m_unit, n_unit, k_unit, result_bytes):
    cands = [[c for c in tiles if dim % c == 0 and unit % c == 0]
             for tiles, dim, unit in ((TILES_M, m, m_unit), (TILES_N, n, n_unit), (TILES_K, k, k_unit))]
    assert all(cands), (m, n, k, m_unit, n_unit, k_unit)
    idx = [0, 0, 0]
    while True:
        tm, tn, tk = (c[i] for c, i in zip(cands, idx))
        if 2 * 2 * (tm * tk + tk * tn) + tm * tn * (4 + 2 * result_bytes) <= MATMUL_VMEM_BUDGET:
            return tm, tn, tk
        shrinkable = [q for q in range(3) if idx[q] + 1 < len(cands[q])]
        assert shrinkable, (m, n, k)
        q = max(shrinkable, key=lambda q: cands[q][idx[q]])
        idx[q] += 1


def matmul(a, b, mode, name, out_dtype=F32, into=None):
    a = a if isinstance(a, Mat) else Mat(a)
    b = b if isinstance(b, Mat) else Mat(b)
    m, k = (a.s, a.r) if mode == "tn" else (a.r, a.s)
    n, kb = (b.r, b.s) if mode == "nt" else (b.s, b.r)
    assert k == kb, (name, a.r, a.s, b.r, b.s)
    m_unit = a.s_unit() if mode == "tn" else m
    k_unit = math.gcd(a.s_unit() if mode != "tn" else k, b.s_unit() if mode == "nt" else k)
    n_unit = n if mode == "nt" else b.s_unit()
    if into is not None:
        kind, buf, shape, row0 = into
        if kind == "col":
            n_unit = math.gcd(n_unit, shape[2])
        else:
            m_unit = math.gcd(m_unit, m // N_CHIP)
    tm, tn, tk = _fit_tiles(m, n, k, m_unit, n_unit, k_unit, jnp.dtype(out_dtype).itemsize)
    nk = k // tk
    ii, jj, kk_ = (lambda j, i, kk: i), (lambda j, i, kk: j), (lambda j, i, kk: kk)
    a_spec = a.spec(tk, tm, kk_, ii) if mode == "tn" else a.spec(tm, tk, ii, kk_)
    b_spec = b.spec(tn, tk, jj, kk_) if mode == "nt" else b.spec(tk, tn, kk_, jj)
    in_specs, args, aliases = [a_spec, b_spec], [a.arr, b.arr], {}
    if into is None:
        out_spec = pl.BlockSpec((tm, tn), lambda j, i, kk: (i, j))
        out_shape = jax.ShapeDtypeStruct((m, n), out_dtype)
    else:
        assert row0 % tm == 0
        r0 = row0 // tm
        if kind == "col":
            npn = shape[2] // tn
            out_spec = pl.BlockSpec((None, tm, tn), lambda j, i, kk: (j // npn, r0 + i, j % npn))
        else:
            npm = m // N_CHIP // tm
            out_spec = pl.BlockSpec((None, tm, tn), lambda j, i, kk: (i // npm, r0 + i % npm, j))
        out_shape = jax.ShapeDtypeStruct(shape, out_dtype)
        if buf is not None:
            in_specs.append(pl.BlockSpec(memory_space=pl.ANY))
            args.append(buf)
            aliases = {2: 0}

    def body(a_ref, b_ref, *rest):
        o_ref, acc_ref = rest[-2:]
        part = lax.dot_general(a_ref[...], b_ref[...], _DIMS[mode], preferred_element_type=F32)
        if nk == 1:
            o_ref[...] = part.astype(out_dtype)
        else:
            kk = pl.program_id(2)

            @pl.when(kk == 0)
            def _():
                acc_ref[...] = part

            @pl.when(kk > 0)
            def _():
                acc_ref[...] += part

            @pl.when(kk == nk - 1)
            def _():
                o_ref[...] = acc_ref[...].astype(out_dtype)

    return pl.pallas_call(
        body, name=name, grid=(n // tn, m // tm, nk), in_specs=in_specs, out_specs=out_spec, out_shape=out_shape,
        scratch_shapes=[pltpu.VMEM((tm, tn) if nk > 1 else (SUBLANES, LANES), F32)], input_output_aliases=aliases,
        compiler_params=_cp(dimension_semantics=("parallel", "parallel", "arbitrary")),
    )(*args)


def matmul_gate_res(a, w, res, gate, n_ctx, name):
    w = w if isinstance(w, Mat) else Mat(w)
    (m, k), n = a.shape, w.s
    assert k == w.r
    tm, tn, tk = _fit_tiles(m, n, k, m, w.s_unit(), k, 3 * 4)
    nk = k // tk

    def body(a_ref, b_ref, r_ref, g_ref, x_ref, y_ref, acc_ref):
        kk = pl.program_id(2)
        row0 = pl.program_id(1) * tm
        part = jnp.dot(a_ref[...], b_ref[...], preferred_element_type=F32)

        @pl.when(kk == 0)
        def _():
            acc_ref[...] = part

        @pl.when(kk > 0)
        def _():
            acc_ref[...] += part

        @pl.when(kk == nk - 1)
        def _():
            y = acc_ref[...]
            row = row0 + lax.broadcasted_iota(jnp.int32, (tm, 1), 0)
            g = jnp.where(row < n_ctx, g_ref[0:1, :], g_ref[1:2, :])
            y_ref[...] = y.astype(y_ref.dtype)
            x_ref[...] = r_ref[...] + g * y

    return pl.pallas_call(
        body, name=name, grid=(n // tn, m // tm, nk),
        in_specs=[pl.BlockSpec((tm, tk), lambda j, i, kk: (i, kk)), w.spec(tk, tn, lambda j, i, kk: kk, lambda j, i, kk: j),
                  pl.BlockSpec((tm, tn), lambda j, i, kk: (i, j)), pl.BlockSpec((2, tn), lambda j, i, kk: (0, j))],
        out_specs=[pl.BlockSpec((tm, tn), lambda j, i, kk: (i, j)), pl.BlockSpec((tm, tn), lambda j, i, kk: (i, j))],
        out_shape=[jax.ShapeDtypeStruct((m, n), F32), jax.ShapeDtypeStruct((m, n), F32)],
        scratch_shapes=[pltpu.VMEM((tm, tn), F32)],
        compiler_params=_cp(dimension_semantics=("parallel", "parallel", "arbitrary")),
    )(a, w.arr, res, gate)


ROW_TILE = 256


def _seg_row(ref2, i, n_ctx_tiles):
    return jnp.where(i < n_ctx_tiles, ref2[0:1, :], ref2[1:2, :])


def _acc_rows(ref, step, rows):
    @pl.when(step == 0)
    def _():
        ref[...] = jnp.zeros_like(ref)

    for r, v in enumerate(rows):
        ref[r:r + 1, :] += v


def norm_mod(x, w, shift, scale, n_ctx, name):
    t, d = x.shape
    tm = _pick(n_ctx, (ROW_TILE, 128))
    nct = n_ctx // tm

    def body(x_ref, w_ref, sh_ref, sc_ref, h_ref):
        i = pl.program_id(0)
        xv = x_ref[...]
        r = lax.rsqrt(jnp.mean(xv * xv, axis=-1, keepdims=True) + NORM_EPS)
        h_ref[...] = ((xv * r) * w_ref[...] * (1.0 + _seg_row(sc_ref, i, nct)) + _seg_row(sh_ref, i, nct)).astype(BF16)

    return pl.pallas_call(
        body, name=name, grid=(t // tm,),
        in_specs=[pl.BlockSpec((tm, d), lambda i: (i, 0)), pl.BlockSpec((1, d), lambda i: (0, 0)),
                  pl.BlockSpec((2, d), lambda i: (0, 0)), pl.BlockSpec((2, d), lambda i: (0, 0))],
        out_specs=pl.BlockSpec((tm, d), lambda i: (i, 0)),
        out_shape=jax.ShapeDtypeStruct((t, d), BF16), compiler_params=_cp(),
    )(x, w, shift, scale)


def norm_mod_bwd(dh, dres, x, w, scale, n_ctx, name):
    t, d = x.shape
    tm = _pick(n_ctx, (ROW_TILE, 128))
    nct = n_ctx // tm

    def body(dh_ref, dr_ref, x_ref, w_ref, sc_ref, dx_ref, st_ref):
        i = pl.program_id(0)
        xv, g = x_ref[...], dh_ref[...]
        r = lax.rsqrt(jnp.mean(xv * xv, axis=-1, keepdims=True) + NORM_EPS)
        xn = xv * r
        one_sc = 1.0 + _seg_row(sc_ref, i, nct)
        dxn = g * (w_ref[...] * one_sc)
        dx_ref[...] = dr_ref[...] + r * (dxn - xn * jnp.mean(dxn * xn, axis=-1, keepdims=True))
        gx = g * xn
        s_shift = jnp.sum(g, axis=0, keepdims=True)
        s_scale = jnp.sum(gx * w_ref[...], axis=0, keepdims=True)
        s_w = jnp.sum(gx * one_sc, axis=0, keepdims=True)
        _acc_rows(st_ref, i, [jnp.where(i < nct, s_shift, 0.0), jnp.where(i < nct, 0.0, s_shift),
                              jnp.where(i < nct, s_scale, 0.0), jnp.where(i < nct, 0.0, s_scale), s_w])

    return pl.pallas_call(
        body, name=name, grid=(t // tm,),
        in_specs=[pl.BlockSpec((tm, d), lambda i: (i, 0)), pl.BlockSpec((tm, d), lambda i: (i, 0)),
                  pl.BlockSpec((tm, d), lambda i: (i, 0)), pl.BlockSpec((1, d), lambda i: (0, 0)),
                  pl.BlockSpec((2, d), lambda i: (0, 0))],
        out_specs=[pl.BlockSpec((tm, d), lambda i: (i, 0)), pl.BlockSpec((SUBLANES, d), lambda i: (0, 0))],
        out_shape=[jax.ShapeDtypeStruct((t, d), F32), jax.ShapeDtypeStruct((SUBLANES, d), F32)],
        compiler_params=_cp(dimension_semantics=("arbitrary",)),
    )(dh, dres, x, w, scale)


def gate_bwd(dx, y, gate, n_ctx, name):
    t, d = dx.shape
    tm = _pick(n_ctx, (ROW_TILE, 128))
    nct = n_ctx // tm

    def body(dx_ref, y_ref, g_ref, dy_ref, dg_ref):
        i = pl.program_id(0)
        dxv = dx_ref[...]
        dy_ref[...] = (dxv * _seg_row(g_ref, i, nct)).astype(BF16)
        s = jnp.sum(dxv * y_ref[...], axis=0, keepdims=True)
        _acc_rows(dg_ref, i, [jnp.where(i < nct, s, 0.0), jnp.where(i < nct, 0.0, s)])

    return pl.pallas_call(
        body, name=name, grid=(t // tm,),
        in_specs=[pl.BlockSpec((tm, d), lambda i: (i, 0)), pl.BlockSpec((tm, d), lambda i: (i, 0)),
                  pl.BlockSpec((2, d), lambda i: (0, 0))],
        out_specs=[pl.BlockSpec((tm, d), lambda i: (i, 0)), pl.BlockSpec((SUBLANES, d), lambda i: (0, 0))],
        out_shape=[jax.ShapeDtypeStruct((t, d), BF16), jax.ShapeDtypeStruct((SUBLANES, d), F32)],
        compiler_params=_cp(dimension_semantics=("arbitrary",)),
    )(dx, y, gate)


def _seg_edges(i, tm, n_ctx, t):
    first = (i == 0) | (i == n_ctx // tm)
    last = (i == n_ctx // tm - 1) | (i == t // tm - 1)
    return first, last


def _shift_rows(x, prev8, next8, first, last):
    tm = x.shape[0]
    row = lax.broadcasted_iota(jnp.int32, (tm, 1), 0)
    prev_row = jnp.where(first, 0.0, prev8[SUBLANES - 1:SUBLANES, :])
    next_row = jnp.where(last, 0.0, next8[0:1, :])
    xp = jnp.where(row == 0, prev_row, pltpu.roll(x, 1, 0))
    xn = jnp.where(row == tm - 1, next_row, pltpu.roll(x, tm - 1, 0))
    return xp, xn


def _halo_specs(tm, tc, t, col):
    r, nblk8 = tm // SUBLANES, t // SUBLANES
    return [pl.BlockSpec((tm, tc), lambda j, i: (i, col(j))),
            pl.BlockSpec((SUBLANES, tc), lambda j, i: (jnp.maximum(i * r - 1, 0), col(j))),
            pl.BlockSpec((SUBLANES, tc), lambda j, i: (jnp.minimum((i + 1) * r, nblk8 - 1), col(j)))]


def dwconv_act(x, col0, c, wb, n_ctx, mode, act_dtype, name):
    t = x.shape[0]
    tm = _pick(n_ctx, (ROW_TILE, 128))
    nparts = 2 if mode == "glu" else 1
    cw = c // nparts
    tc = _pick(math.gcd(cw, col0) if col0 else cw, (512, 384, 256, 128))

    def conv(xm, xp8, xn8, w_ref, i):
        first, last = _seg_edges(i, tm, n_ctx, t)
        xp, xn = _shift_rows(xm[...], xp8[...], xn8[...], first, last)
        return w_ref[0:1, :] * xp + w_ref[1:2, :] * xm[...] + w_ref[2:3, :] * xn + w_ref[3:4, :]

    def body(*refs):
        i = pl.program_id(1)
        if mode == "silu":
            xm, xp8, xn8, w_ref, pre_ref, act_ref = refs
            pre = conv(xm, xp8, xn8, w_ref, i)
            pre_ref[...] = pre
            act_ref[...] = _silu(pre).astype(act_dtype)
        else:
            xm, xp8, xn8, w_ref, gm, gp8, gn8, gw_ref, pv_ref, pg_ref, act_ref = refs
            val = conv(xm, xp8, xn8, w_ref, i)
            gat = conv(gm, gp8, gn8, gw_ref, i)
            pv_ref[...] = val
            pg_ref[...] = gat
            act_ref[...] = (_silu(gat) * val).astype(act_dtype)

    in_specs, args = [], []
    for p in range(nparts):
        in_specs += _halo_specs(tm, tc, t, lambda j, p=p: (col0 + p * cw) // tc + j)
        in_specs.append(pl.BlockSpec((SUBLANES, tc), lambda j, i, p=p: (0, p * cw // tc + j)))
        args += [x, x, x, wb]
    out = pl.BlockSpec((tm, tc), lambda j, i: (i, j))
    out_shape = [jax.ShapeDtypeStruct((t, cw), F32)] * nparts + [jax.ShapeDtypeStruct((t, cw), act_dtype)]
    return pl.pallas_call(
        body, name=name, grid=(cw // tc, t // tm), in_specs=in_specs, out_specs=[out] * (nparts + 1), out_shape=out_shape,
        compiler_params=_cp(dimension_semantics=("parallel", "parallel")),
    )(*args)


def dwconv_act_bwd(dact, dcol0, pres, pcol0, x, xcol0, wb, wcol0, cw, n_ctx, mode, name, into=None, ocol0=0):
    t = x.shape[0]
    tm = _pick(n_ctx, (ROW_TILE, 128))
    nparts = 2 if mode == "glu" else 1
    tc = _pick(math.gcd(cw, dcol0, pcol0, xcol0, wcol0, ocol0), (512, 384, 256, 128))

    def dpre_of(dact_v, pres_v):
        if mode == "silu":
            return [dact_v * _dsilu(pres_v[0])]
        val, gat = pres_v
        return [dact_v * _silu(gat), dact_v * val * _dsilu(gat)]

    def body(*refs):
        i = pl.program_id(1)
        first, last = _seg_edges(i, tm, n_ctx, t)
        da = refs[0:3]
        pr = [refs[3 + 3 * p:6 + 3 * p] for p in range(nparts)]
        rest = refs[3 + 3 * nparts:]
        xs, ws = rest[:nparts], rest[nparts:2 * nparts]
        dx_ref, dw_ref = rest[-2:]
        dm = dpre_of(da[0][...], [p[0][...] for p in pr])
        dp8 = dpre_of(da[1][...], [p[1][...] for p in pr])
        dn8 = dpre_of(da[2][...], [p[2][...] for p in pr])
        for p in range(nparts):
            d_prev, d_next = _shift_rows(dm[p], dp8[p], dn8[p], first, last)
            w_ref, xv = ws[p], xs[p][...]
            dxv = (w_ref[0:1, :] * d_next + w_ref[1:2, :] * dm[p] + w_ref[2:3, :] * d_prev).astype(BF16)
            if mode == "glu":
                dx_ref[p] = dxv
            else:
                dx_ref[...] = dxv
            _acc_rows(dw_ref.at[p] if mode == "glu" else dw_ref, i,
                      [jnp.sum(d_next * xv, axis=0, keepdims=True), jnp.sum(dm[p] * xv, axis=0, keepdims=True),
                       jnp.sum(d_prev * xv, axis=0, keepdims=True), jnp.sum(dm[p], axis=0, keepdims=True)])

    in_specs = _halo_specs(tm, tc, t, lambda j: dcol0 // tc + j)
    args = [dact] * 3
    for p in range(nparts):
        in_specs += _halo_specs(tm, tc, t, lambda j: pcol0 // tc + j)
        args += [pres[p]] * 3
    for p in range(nparts):
        in_specs.append(pl.BlockSpec((tm, tc), lambda j, i, p=p: (i, (xcol0 + p * cw) // tc + j)))
        args.append(x)
    for p in range(nparts):
        in_specs.append(pl.BlockSpec((SUBLANES, tc), lambda j, i, p=p: (0, (wcol0 + p * cw) // tc + j)))
        args.append(wb)
    aliases = {}
    if mode == "glu":
        out_specs = [pl.BlockSpec((2, tm, tc), lambda j, i: (0, i, j)), pl.BlockSpec((2, SUBLANES, tc), lambda j, i: (0, 0, j))]
        out_shape = [jax.ShapeDtypeStruct((2, t, cw), BF16), jax.ShapeDtypeStruct((2, SUBLANES, cw), F32)]
    else:
        out_specs = [pl.BlockSpec((tm, tc), lambda j, i: (i, ocol0 // tc + j)), pl.BlockSpec((SUBLANES, tc), lambda j, i: (0, j))]
        out_shape = [jax.ShapeDtypeStruct((t, cw) if into is None else into.shape, BF16), jax.ShapeDtypeStruct((SUBLANES, cw), F32)]
        if into is not None:
            aliases = {len(args): 0}
            in_specs.append(pl.BlockSpec(memory_space=pl.ANY))
            args.append(into)
    return pl.pallas_call(
        body, name=name, grid=(cw // tc, t // tm), in_specs=in_specs, out_specs=out_specs, out_shape=out_shape,
        input_output_aliases=aliases, compiler_params=_cp(dimension_semantics=("parallel", "arbitrary")),
    )(*args)


HI = lax.Precision.HIGHEST
_NT = (((1,), (1,)), ((), ()))
_TN = (((0,), (0,)), ((), ()))


def _dot(a, b, dims=None, precision=None):
    if dims is None:
        return jnp.dot(a, b, preferred_element_type=F32, precision=precision)
    return lax.dot_general(a, b, dims, preferred_element_type=F32, precision=precision)


def _softplus(x):
    y = jnp.exp(-jnp.abs(x))
    u = 1.0 + y
    log1p = jnp.where(u == 1.0, y, y * jnp.log(u) / jnp.where(u == 1.0, 1.0, u - 1.0))
    return jnp.maximum(x, 0.0) + log1p


def _tri(n, upper):
    r = lax.broadcasted_iota(jnp.int32, (n, n), 0)
    c = lax.broadcasted_iota(jnp.int32, (n, n), 1)
    return ((r <= c) if upper else (r >= c)).astype(F32)


def _group_select(heads, e):
    g = jnp.arange(SSD_GROUPS)[:, None, None]
    src = jnp.arange(LANES)[None, :, None]
    dst = jnp.arange(LANES)[None, None, :]
    d, k = dst // e, dst % e
    return ((dst < 2 * e) & (src == d * heads + g * e + k)).astype(F32)


def ssd_prep(zx, col0, bias, alog, sel, heads, name):
    t = zx.shape[0]
    assert 2 * heads == LANES and col0 % LANES == 0
    nc = t // CHUNK

    def body(zx_ref, b_ref, al_ref, sel_ref, dt_ref, da_ref, dtg_ref, ag_ref, agt_ref):
        dt = _softplus(zx_ref[...] + b_ref[...])
        da = -jnp.exp(al_ref[...]) * dt
        lane = lax.broadcasted_iota(jnp.int32, (CHUNK, LANES), 1)
        a = jnp.where(lane < heads, _dot(_tri(CHUNK, False), da, precision=HI), _dot(_tri(CHUNK, True), da, precision=HI))
        dt_ref[...] = dt
        da_ref[...] = da
        for g in range(SSD_GROUPS):
            s = sel_ref[g]
            dtg_ref[g] = _dot(dt, s, precision=HI)
            a_g = _dot(a, s, precision=HI)
            ag_ref[g] = a_g
            agt_ref[g, 0] = a_g.T

    return pl.pallas_call(
        body, name=name, grid=(nc,),
        in_specs=[pl.BlockSpec((CHUNK, LANES), lambda c: (c, col0 // LANES)), pl.BlockSpec((1, LANES), lambda c: (0, 0)),
                  pl.BlockSpec((1, LANES), lambda c: (0, 0)), pl.BlockSpec((SSD_GROUPS, LANES, LANES), lambda c: (0, 0, 0))],
        out_specs=[pl.BlockSpec((CHUNK, LANES), lambda c: (c, 0)), pl.BlockSpec((CHUNK, LANES), lambda c: (c, 0)),
                   pl.BlockSpec((SSD_GROUPS, CHUNK, LANES), lambda c: (0, c, 0)), pl.BlockSpec((SSD_GROUPS, CHUNK, LANES), lambda c: (0, c, 0)),
                   pl.BlockSpec((SSD_GROUPS, 1, LANES, CHUNK), lambda c: (0, c, 0, 0))],
        out_shape=[jax.ShapeDtypeStruct((t, LANES), F32), jax.ShapeDtypeStruct((t, LANES), F32),
                   jax.ShapeDtypeStruct((SSD_GROUPS, t, LANES), F32), jax.ShapeDtypeStruct((SSD_GROUPS, t, LANES), F32),
                   jax.ShapeDtypeStruct((SSD_GROUPS, nc, LANES, CHUNK), F32)],
        compiler_params=_cp(),
    )(zx, bias, alog, sel)


def ssd_prep_bwd(zx, col0, bias, alog, sel, grads_f, grads_b, da_comp, heads, into, name):
    t = zx.shape[0]
    nc = t // CHUNK

    def body(zx_ref, b_ref, al_ref, sel_ref, ddtg_ref, dag_ref, dagt_ref, ddtg2_ref, dag2_ref, dagt2_ref, da_ref, _, draw_ref, st_ref):
        c = pl.program_id(0)
        ddt = jnp.zeros((CHUNK, LANES), F32)
        dacc = jnp.zeros((CHUNK, LANES), F32)
        for g in range(SSD_GROUPS):
            s = sel_ref[g]
            ddt += _dot(ddtg_ref[g] + ddtg2_ref[g], s, _NT, precision=HI)
            dacc += _dot(dag_ref[g] + dag2_ref[g] + (dagt_ref[g, 0] + dagt2_ref[g, 0]).T, s, _NT, precision=HI)
        lane = lax.broadcasted_iota(jnp.int32, (CHUNK, LANES), 1)
        dda = jnp.where(lane < heads, _dot(_tri(CHUNK, True), dacc, precision=HI), _dot(_tri(CHUNK, False), dacc, precision=HI))
        xin = zx_ref[...] + b_ref[...]
        ddt_tot = ddt - dda * jnp.exp(al_ref[...])
        draw = ddt_tot * jax.nn.sigmoid(xin)
        draw_ref[...] = draw.astype(BF16)
        _acc_rows(st_ref, c, [jnp.sum(draw, axis=0, keepdims=True), jnp.sum(dda * da_ref[...], axis=0, keepdims=True)])

    g3 = pl.BlockSpec((SSD_GROUPS, CHUNK, LANES), lambda c: (0, c, 0))
    g4 = pl.BlockSpec((SSD_GROUPS, 1, LANES, CHUNK), lambda c: (0, c, 0, 0))
    return pl.pallas_call(
        body, name=name, grid=(nc,),
        in_specs=[pl.BlockSpec((CHUNK, LANES), lambda c: (c, col0 // LANES)), pl.BlockSpec((1, LANES), lambda c: (0, 0)),
                  pl.BlockSpec((1, LANES), lambda c: (0, 0)), pl.BlockSpec((SSD_GROUPS, LANES, LANES), lambda c: (0, 0, 0)),
                  g3, g3, g4, g3, g3, g4, pl.BlockSpec((CHUNK, LANES), lambda c: (c, 0)), pl.BlockSpec(memory_space=pl.ANY)],
        out_specs=[pl.BlockSpec((CHUNK, LANES), lambda c: (c, col0 // LANES)), pl.BlockSpec((SUBLANES, LANES), lambda c: (0, 0))],
        out_shape=[jax.ShapeDtypeStruct(into.shape, BF16), jax.ShapeDtypeStruct((SUBLANES, LANES), F32)],
        input_output_aliases={11: 0}, compiler_params=_cp(dimension_semantics=("arbitrary",)),
    )(zx, bias, alog, sel, *grads_f, *grads_b, da_comp, into)


def _chunk_row(k, nctx_c, nc, rev):
    if not rev:
        return k
    return jnp.where(k < nctx_c, nctx_c - 1 - k, nc + nctx_c - 1 - k)


def _pair_consts(dtg, ag, agt, l0, end):
    lane = lax.broadcasted_iota(jnp.int32, (CHUNK, LANES), 1)
    lo = lane < SSD_HEAD_DIM
    a0, a1 = ag[:, l0:l0 + 1], ag[:, l0 + 1:l0 + 2]
    dtp = jnp.where(lo, dtg[:, l0:l0 + 1], dtg[:, l0 + 1:l0 + 2])
    acol = jnp.where(lo, a0, a1)
    aend = acol[end:end + 1, :]
    return lo, a0, a1, agt[l0:l0 + 1, :], agt[l0 + 1:l0 + 2, :], dtp, acol, aend


def ssd_scan(xbc, dtg, ag, agt, d_inner, n_ctx, rev, name):
    t = xbc.shape[0]
    e = d_inner // SSD_HEAD_DIM // SSD_GROUPS
    npair, gw = e // 2, e * SSD_HEAD_DIM
    assert e % 2 == 0 and gw % LANES == 0
    nc, nctx_c = t // CHUNK, n_ctx // CHUNK
    dirn = 1 if rev else 0
    end = 0 if rev else CHUNK - 1
    ridx = lambda k: _chunk_row(k, nctx_c, nc, rev)

    def body(x_ref, b_ref, c_ref, dtg_ref, ag_ref, agt_ref, y_ref, h_ref, state):
        k = pl.program_id(1)

        @pl.when(k == 0)
        def _():
            state[...] = jnp.zeros_like(state)

        bb, cbf = b_ref[...].astype(BF16), c_ref[...].astype(BF16)
        cb = _dot(cbf, bb, _NT)
        row = lax.broadcasted_iota(jnp.int32, (CHUNK, CHUNK), 0)
        col = lax.broadcasted_iota(jnp.int32, (CHUNK, CHUNK), 1)
        mask = (row <= col) if rev else (row >= col)
        rlo = lax.broadcasted_iota(jnp.int32, (LANES, 1), 0) < SSD_HEAD_DIM
        dtg_v, ag_v, agt_v = dtg_ref[0], ag_ref[0], agt_ref[0, 0]
        for p in range(npair):
            sl = slice(p * LANES, (p + 1) * LANES)
            lo, a0, a1, a0r, a1r, dtp, acol, aend = _pair_consts(dtg_v, ag_v, agt_v, dirn * e + 2 * p, end)
            xdt = x_ref[:, sl] * dtp
            xdtb = xdt.astype(BF16)
            w0 = (cb * jnp.exp(jnp.where(mask, a0 - a0r, -jnp.inf))).astype(BF16)
            w1 = (cb * jnp.exp(jnp.where(mask, a1 - a1r, -jnp.inf))).astype(BF16)
            yd = jnp.where(lo, _dot(w0, xdtb), _dot(w1, xdtb))
            xw = (xdt * jnp.exp(aend - acol)).astype(BF16)
            st = _dot(xw, bb, _TN)
            s_in = state[sl, :]
            h_ref[0, 0, sl, :] = s_in
            yo = _dot(cbf, s_in.astype(BF16), _NT)
            y_ref[:, sl] = yd + yo * jnp.exp(acol)
            cd = jnp.where(rlo, jnp.exp(aend[:, 0:1]), jnp.exp(aend[:, LANES - 1:LANES]))
            state[sl, :] = cd * s_in + st

    xcol = d_inner // LANES
    g3 = pl.BlockSpec((1, CHUNK, LANES), lambda g, k: (g, ridx(k), 0))
    return pl.pallas_call(
        body, name=name, grid=(SSD_GROUPS, nc),
        in_specs=[pl.BlockSpec((CHUNK, gw), lambda g, k: (ridx(k), g)),
                  pl.BlockSpec((CHUNK, SSD_STATE), lambda g, k: (ridx(k), xcol + g)),
                  pl.BlockSpec((CHUNK, SSD_STATE), lambda g, k: (ridx(k), xcol + SSD_GROUPS + g)),
                  g3, g3, pl.BlockSpec((1, 1, LANES, CHUNK), lambda g, k: (g, ridx(k), 0, 0))],
        out_specs=[pl.BlockSpec((CHUNK, gw), lambda g, k: (ridx(k), g)),
                   pl.BlockSpec((1, 1, npair * LANES, SSD_STATE), lambda g, k: (ridx(k), g, 0, 0))],
        out_shape=[jax.ShapeDtypeStruct((t, d_inner), F32), jax.ShapeDtypeStruct((nc, SSD_GROUPS, npair * LANES, SSD_STATE), F32)],
        scratch_shapes=[pltpu.VMEM((npair * LANES, SSD_STATE), F32)],
        compiler_params=_cp(dimension_semantics=("parallel", "arbitrary")),
    )(xbc, xbc, xbc, dtg, ag, agt)


def ssd_scan_bwd(dy, xbc, dtg, ag, agt, hst, dskip, prev, d_inner, n_ctx, rev, name):
    t = xbc.shape[0]
    e = d_inner // SSD_HEAD_DIM // SSD_GROUPS
    npair, gw = e // 2, e * SSD_HEAD_DIM
    nc, nctx_c = t // CHUNK, n_ctx // CHUNK
    dirn = 1 if rev else 0
    end = 0 if rev else CHUNK - 1
    ridx = lambda kk: _chunk_row(nc - 1 - kk, nctx_c, nc, rev)
    has_skip, has_prev = dskip is not None, prev is not None

    def body(*refs):
        dy_ref, x_ref, b_ref, c_ref, dtg_ref, ag_ref, agt_ref, h_ref = refs[:8]
        pos = 8
        if has_skip:
            ds_ref = refs[pos]
            pos += 1
        if has_prev:
            pdx_ref, pdb_ref, pdc_ref = refs[pos:pos + 3]
            pos += 3
        dx_ref, db_ref, dc_ref, ddtg_ref, dag_ref, dagt_ref, dd_ref, dstate = refs[pos:]
        kk = pl.program_id(1)

        @pl.when(kk == 0)
        def _():
            dstate[...] = jnp.zeros_like(dstate)

        bb, cbf = b_ref[...].astype(BF16), c_ref[...].astype(BF16)
        cb = _dot(cbf, bb, _NT)
        row = lax.broadcasted_iota(jnp.int32, (CHUNK, CHUNK), 0)
        col = lax.broadcasted_iota(jnp.int32, (CHUNK, CHUNK), 1)
        mask = (row <= col) if rev else (row >= col)
        rlo = lax.broadcasted_iota(jnp.int32, (LANES, 1), 0) < SSD_HEAD_DIM
        is_end = lax.broadcasted_iota(jnp.int32, (CHUNK, 1), 0) == end
        dtg_v, ag_v, agt_v = dtg_ref[0], ag_ref[0], agt_ref[0, 0]
        dcb = jnp.zeros((CHUNK, CHUNK), F32)
        d_c = jnp.zeros((CHUNK, SSD_STATE), F32)
        d_b = jnp.zeros((CHUNK, SSD_STATE), F32)
        ddt_out = jnp.zeros((CHUNK, LANES), F32)
        da_out = jnp.zeros((CHUNK, LANES), F32)
        dat_out = jnp.zeros((LANES, CHUNK), F32)
        for p in range(npair):
            sl = slice(p * LANES, (p + 1) * LANES)
            l0 = dirn * e + 2 * p
            lo, a0, a1, a0r, a1r, dtp, acol, aend = _pair_consts(dtg_v, ag_v, agt_v, l0, end)
            x = x_ref[:, sl]
            dyv = dy_ref[:, sl]
            xdt = x * dtp
            xdtb = xdt.astype(BF16)
            seg0 = jnp.exp(jnp.where(mask, a0 - a0r, -jnp.inf))
            seg1 = jnp.exp(jnp.where(mask, a1 - a1r, -jnp.inf))
            w0, w1 = cb * seg0, cb * seg1
            efs, dte = jnp.exp(acol), jnp.exp(aend - acol)
            xw = (xdt * dte).astype(BF16)
            s_in = h_ref[0, 0, sl, :]
            sb = s_in.astype(BF16)
            dyo = (dyv * efs).astype(BF16)
            da_exp = dyv * _dot(cbf, sb, _NT) * efs
            d_c += _dot(dyo, sb)
            ds_y = _dot(dyo, cbf, _TN)
            dyb = dyv.astype(BF16)
            dy0 = jnp.where(lo, dyv, 0.0).astype(BF16)
            dy1 = jnp.where(lo, 0.0, dyv).astype(BF16)
            dw0, dw1 = _dot(dy0, xdtb, _NT), _dot(dy1, xdtb, _NT)
            dxdt = jnp.where(lo, _dot(w0.astype(BF16), dyb, _TN), _dot(w1.astype(BF16), dyb, _TN))
            dcb += dw0 * seg0 + dw1 * seg1
            t0, t1 = dw0 * w0, dw1 * w1
            d_s = dstate[sl, :]
            dsb = d_s.astype(BF16)
            dxw = _dot(bb, dsb, _NT)
            d_b += _dot(xw, dsb)
            dxdt += dxw * dte
            tmp = dxw * xdt * dte
            da_exp -= tmp
            end_row = jnp.sum(tmp, axis=0, keepdims=True)
            prod = d_s * s_in
            e0, e1 = jnp.exp(aend[:, 0:1]), jnp.exp(aend[:, LANES - 1:LANES])
            sc0 = jnp.sum(jnp.where(rlo, prod, 0.0), keepdims=True) * e0
            sc1 = jnp.sum(jnp.where(rlo, 0.0, prod), keepdims=True) * e1
            dstate[sl, :] = jnp.where(rlo, e0, e1) * d_s + ds_y
            dxv = dxdt * dtp
            if has_skip:
                dxv += dyv * ds_ref[:, sl]
                _acc_rows(dd_ref.at[:, sl], kk, [jnp.sum(dyv * x, axis=0, keepdims=True)])
            if has_prev:
                dxv += pdx_ref[:, sl]
            dx_ref[:, sl] = dxv
            ddt_exp = dxdt * x
            lane = lax.broadcasted_iota(jnp.int32, (CHUNK, LANES), 1)
            sub = lax.broadcasted_iota(jnp.int32, (LANES, CHUNK), 0)
            for j, (sel, tj, scj) in enumerate(((lo, t0, sc0), (~lo, t1, sc1))):
                ddt_col = jnp.sum(jnp.where(sel, ddt_exp, 0.0), axis=1, keepdims=True)
                da_col = jnp.sum(jnp.where(sel, da_exp, 0.0), axis=1, keepdims=True) + jnp.sum(tj, axis=1, keepdims=True)
                da_end = jnp.sum(jnp.where(sel[0:1, :], end_row, 0.0), axis=1, keepdims=True) + scj
                da_col = da_col + jnp.where(is_end, da_end, 0.0)
                ddt_out += jnp.where(lane == l0 + j, ddt_col, 0.0)
                da_out += jnp.where(lane == l0 + j, da_col, 0.0)
                dat_out -= jnp.where(sub == l0 + j, jnp.sum(tj, axis=0, keepdims=True), 0.0)
        dcbb = dcb.astype(BF16)
        d_c += _dot(dcbb, bb)
        d_b += _dot(dcbb, cbf, _TN)
        if has_prev:
            d_b += pdb_ref[...]
            d_c += pdc_ref[...]
        db_ref[...] = d_b
        dc_ref[...] = d_c
        ddtg_ref[0] = ddt_out
        dag_ref[0] = da_out
        dagt_ref[0, 0] = dat_out
        if not has_skip:
            dd_ref[...] = jnp.zeros_like(dd_ref)

    xcol = d_inner // LANES
    xs_spec = pl.BlockSpec((CHUNK, gw), lambda g, kk: (ridx(kk), g))
    bc_spec = pl.BlockSpec((CHUNK, SSD_STATE), lambda g, kk: (ridx(kk), g))
    g3 = pl.BlockSpec((1, CHUNK, LANES), lambda g, kk: (g, ridx(kk), 0))
    g4 = pl.BlockSpec((1, 1, LANES, CHUNK), lambda g, kk: (g, ridx(kk), 0, 0))
    in_specs = [xs_spec, xs_spec,
                pl.BlockSpec((CHUNK, SSD_STATE), lambda g, kk: (ridx(kk), xcol + g)),
                pl.BlockSpec((CHUNK, SSD_STATE), lambda g, kk: (ridx(kk), xcol + SSD_GROUPS + g)),
                g3, g3, g4, pl.BlockSpec((1, 1, npair * LANES, SSD_STATE), lambda g, kk: (ridx(kk), g, 0, 0))]
    args = [dy, xbc, xbc, xbc, dtg, ag, agt, hst]
    if has_skip:
        in_specs.append(pl.BlockSpec((1, gw), lambda g, kk: (0, g)))
        args.append(dskip)
    if has_prev:
        in_specs += [xs_spec, bc_spec, bc_spec]
        args += list(prev)
    gn = SSD_GROUPS * SSD_STATE
    return pl.pallas_call(
        body, name=name, grid=(SSD_GROUPS, nc), in_specs=in_specs,
        out_specs=[xs_spec, bc_spec, bc_spec, g3, g3, g4, pl.BlockSpec((SUBLANES, gw), lambda g, kk: (0, g))],
        out_shape=[jax.ShapeDtypeStruct((t, d_inner), F32), jax.ShapeDtypeStruct((t, gn), F32), jax.ShapeDtypeStruct((t, gn), F32),
                   jax.ShapeDtypeStruct((SSD_GROUPS, t, LANES), F32), jax.ShapeDtypeStruct((SSD_GROUPS, t, LANES), F32),
                   jax.ShapeDtypeStruct((SSD_GROUPS, nc, LANES, CHUNK), F32), jax.ShapeDtypeStruct((SUBLANES, d_inner), F32)],
        scratch_shapes=[pltpu.VMEM((npair * LANES, SSD_STATE), F32)],
        compiler_params=_cp(dimension_semantics=("parallel", "arbitrary")),
    )(*args)


def ssd_gate_norm(yf, yb, xbc, zx, dskip, w, name):
    t, di = yf.shape
    tm = CHUNK

    def body(yf_ref, yb_ref, x_ref, z_ref, d_ref, w_ref, y_ref, o_ref):
        y = yf_ref[...] + yb_ref[...] + x_ref[...] * d_ref[...]
        y_ref[...] = y
        gz = y * _silu(z_ref[...])
        o_ref[...] = (gz * lax.rsqrt(jnp.mean(gz * gz, axis=-1, keepdims=True) + NORM_EPS) * w_ref[...]).astype(BF16)

    blk = pl.BlockSpec((tm, di), lambda i: (i, 0))
    vec = pl.BlockSpec((1, di), lambda i: (0, 0))
    return pl.pallas_call(
        body, name=name, grid=(t // tm,), in_specs=[blk, blk, blk, blk, vec, vec], out_specs=[blk, blk],
        out_shape=[jax.ShapeDtypeStruct((t, di), F32), jax.ShapeDtypeStruct((t, di), BF16)], compiler_params=_cp(),
    )(yf, yb, xbc, zx, dskip, w)


def ssd_gate_norm_bwd(dout, y, zx, w, name):
    t, di = y.shape
    tm = CHUNK

    def body(do_ref, y_ref, z_ref, w_ref, dy_ref, dz_ref, st_ref):
        i = pl.program_id(0)
        z, yv, g = z_ref[...], y_ref[...], do_ref[...]
        sz = _silu(z)
        gz = yv * sz
        r = lax.rsqrt(jnp.mean(gz * gz, axis=-1, keepdims=True) + NORM_EPS)
        n = gz * r
        dn = g * w_ref[...]
        dgz = r * (dn - n * jnp.mean(dn * n, axis=-1, keepdims=True))
        dy_ref[...] = dgz * sz
        dz_ref[...] = (dgz * yv * _dsilu(z)).astype(BF16)
        _acc_rows(st_ref, i, [jnp.sum(g * n, axis=0, keepdims=True)])

    blk = pl.BlockSpec((tm, di), lambda i: (i, 0))
    return pl.pallas_call(
        body, name=name, grid=(t // tm,), in_specs=[blk, blk, blk, pl.BlockSpec((1, di), lambda i: (0, 0))],
        out_specs=[blk, blk, pl.BlockSpec((SUBLANES, di), lambda i: (0, 0))],
        out_shape=[jax.ShapeDtypeStruct((t, di), F32), jax.ShapeDtypeStruct(zx.shape, BF16), jax.ShapeDtypeStruct((SUBLANES, di), F32)],
        compiler_params=_cp(dimension_semantics=("arbitrary",)),
    )(dout, y, zx, w)


def rope_tables(n_ctx, seq):
    pos = jnp.arange(seq)
    half = ATTN_HEAD_DIM // 4
    inv = ROPE_THETA ** (-jnp.arange(0, 2 * half, 2, dtype=F32) / (2 * half))
    ar = (pos // GRID_W).astype(F32)[:, None] * inv[None, :]
    ac = (pos % GRID_W).astype(F32)[:, None] * inv[None, :]
    cos = jnp.concatenate([jnp.cos(ar), jnp.cos(ar), jnp.cos(ac), jnp.cos(ac)], axis=-1)
    sin = jnp.concatenate([-jnp.sin(ar), jnp.sin(ar), -jnp.sin(ac), jnp.sin(ac)], axis=-1)
    cos = jnp.concatenate([jnp.ones((n_ctx, ATTN_HEAD_DIM), F32), cos], axis=0)
    sin = jnp.concatenate([jnp.zeros((n_ctx, ATTN_HEAD_DIM), F32), sin], axis=0)
    return cos, sin


def _rot(x):
    lane = lax.broadcasted_iota(jnp.int32, x.shape, 1)
    q = ATTN_HEAD_DIM // 4
    return jnp.where((lane % (2 * q)) < q, pltpu.roll(x, ATTN_HEAD_DIM - q, 1), pltpu.roll(x, q, 1))


def qk_prep(qkv, gains, cos, sin, n_q, n_k, name):
    t, c = qkv.shape
    tm = ROW_TILE
    hd = ATTN_HEAD_DIM

    def body(x_ref, g_ref, cos_ref, sin_ref, o_ref):
        cs, sn = cos_ref[...], sin_ref[...]
        for h in range(c // hd):
            sl = slice(h * hd, (h + 1) * hd)
            x = x_ref[:, sl]
            if h < n_q + n_k:
                gain = g_ref[0:1, :] if h < n_q else g_ref[1:2, :]
                xn = x * lax.rsqrt(jnp.mean(x * x, axis=-1, keepdims=True) + NORM_EPS) * gain
                x = xn * cs + _rot(xn) * sn
            o_ref[:, sl] = x.astype(BF16)

    return pl.pallas_call(
        body, name=name, grid=(t // tm,),
        in_specs=[pl.BlockSpec((tm, c), lambda i: (i, 0)), pl.BlockSpec((SUBLANES, hd), lambda i: (0, 0)),
                  pl.BlockSpec((tm, hd), lambda i: (i, 0)), pl.BlockSpec((tm, hd), lambda i: (i, 0))],
        out_specs=pl.BlockSpec((tm, c), lambda i: (i, 0)), out_shape=jax.ShapeDtypeStruct((t, c), BF16), compiler_params=_cp(),
    )(qkv, gains, cos, sin)


def qk_prep_bwd(dq, dk, dv, qkv, gains, cos, sin, name):
    t, c = qkv.shape
    tm = ROW_TILE
    hd = ATTN_HEAD_DIM
    n_q, n_k = dq.shape[1] // hd, dk.shape[1] // hd

    def body(dq_ref, dk_ref, dv_ref, x_ref, g_ref, cos_ref, sin_ref, o_ref, st_ref):
        i = pl.program_id(0)
        cs, sn = cos_ref[...], sin_ref[...]
        dgq = jnp.zeros((1, hd), F32)
        dgk = jnp.zeros((1, hd), F32)
        for h in range(c // hd):
            sl = slice(h * hd, (h + 1) * hd)
            if h >= n_q + n_k:
                hv = h - n_q - n_k
                o_ref[:, sl] = dv_ref[:, hv * hd:(hv + 1) * hd].astype(BF16)
                continue
            is_q = h < n_q
            dy = dq_ref[:, sl] if is_q else dk_ref[:, (h - n_q) * hd:(h - n_q + 1) * hd]
            gain = g_ref[0:1, :] if is_q else g_ref[1:2, :]
            x = x_ref[:, sl]
            r = lax.rsqrt(jnp.mean(x * x, axis=-1, keepdims=True) + NORM_EPS)
            xh = x * r
            dxn = dy * cs + _rot(dy * sn)
            dg = jnp.sum(dxn * xh, axis=0, keepdims=True)
            if is_q:
                dgq += dg
            else:
                dgk += dg
            dxh = dxn * gain
            o_ref[:, sl] = (r * (dxh - xh * jnp.mean(dxh * xh, axis=-1, keepdims=True))).astype(BF16)
        _acc_rows(st_ref, i, [dgq, dgk])

    return pl.pallas_call(
        body, name=name, grid=(t // tm,),
        in_specs=[pl.BlockSpec((tm, n_q * hd), lambda i: (i, 0)), pl.BlockSpec((tm, n_k * hd), lambda i: (i, 0)),
                  pl.BlockSpec((tm, n_k * hd), lambda i: (i, 0)), pl.BlockSpec((tm, c), lambda i: (i, 0)),
                  pl.BlockSpec((SUBLANES, hd), lambda i: (0, 0)), pl.BlockSpec((tm, hd), lambda i: (i, 0)), pl.BlockSpec((tm, hd), lambda i: (i, 0))],
        out_specs=[pl.BlockSpec((tm, c), lambda i: (i, 0)), pl.BlockSpec((SUBLANES, hd), lambda i: (0, 0))],
        out_shape=[jax.ShapeDtypeStruct((t, c), BF16), jax.ShapeDtypeStruct((SUBLANES, hd), F32)],
        compiler_params=_cp(dimension_semantics=("arbitrary",)),
    )(dq, dk, dv, qkv, gains, cos, sin)


def _attn_specs(n_ctx, nb, grp, n_qh):
    hd, blk = ATTN_HEAD_DIM, CHUNK
    kc, vc = n_qh, n_qh + ATTN_KV_HEADS
    specs = [pl.BlockSpec((blk, grp * hd), lambda h, b: (b, h))]
    for c0 in (kc, vc):
        specs += [pl.BlockSpec((n_ctx, hd), lambda h, b, c0=c0: (0, c0 + h)),
                  pl.BlockSpec((blk, hd), lambda h, b, c0=c0: (jnp.maximum(b - 1, 0), c0 + h)),
                  pl.BlockSpec((blk, hd), lambda h, b, c0=c0: (b, c0 + h)),
                  pl.BlockSpec((blk, hd), lambda h, b, c0=c0: (jnp.minimum(b + 1, nb - 1), c0 + h))]
    return specs


def _attn_masks(b, nctx_b, nb):
    row = lax.broadcasted_iota(jnp.int32, (CHUNK, CHUNK), 0)
    col = lax.broadcasted_iota(jnp.int32, (CHUNK, CHUNK), 1)
    lat = b >= nctx_b
    return [(col >= row) & lat & (b - 1 >= nctx_b), jnp.broadcast_to(lat, (CHUNK, CHUNK)), (col <= row) & lat & (b + 1 <= nb - 1)]


def attention(qkvr, sinks, n_ctx, n_qh, name):
    t = qkvr.shape[0]
    hd, blk = ATTN_HEAD_DIM, CHUNK
    grp = n_qh // ATTN_KV_HEADS
    nb, nctx_b = t // blk, n_ctx // blk
    scale = hd ** -0.5

    def body(q_ref, kc_ref, kp_ref, ko_ref, kn_ref, vc_ref, vp_ref, vo_ref, vn_ref, s_ref, o_ref, lse_ref):
        b = pl.program_id(1)
        masks = _attn_masks(b, nctx_b, nb)
        ks = [kc_ref[...], kp_ref[...], ko_ref[...], kn_ref[...]]
        vs = [vc_ref[...], vp_ref[...], vo_ref[...], vn_ref[...]]
        lane = lax.broadcasted_iota(jnp.int32, (blk, LANES), 1)
        lse_out = jnp.zeros((blk, LANES), F32)
        for g in range(grp):
            q = q_ref[:, g * hd:(g + 1) * hd]
            s = [_dot(q, k, _NT) * scale for k in ks]
            s = [s[0]] + [jnp.where(m, sx, -jnp.inf) for m, sx in zip(masks, s[1:])]
            sink = s_ref[0, 0:1, g:g + 1]
            m = sink
            for sx in s:
                m = jnp.maximum(m, jnp.max(sx, axis=-1, keepdims=True))
            p = [jnp.exp(sx - m) for sx in s]
            l = jnp.exp(sink - m)
            for px in p:
                l = l + jnp.sum(px, axis=-1, keepdims=True)
            inv = 1.0 / l
            o = jnp.zeros((blk, hd), F32)
            for px, v in zip(p, vs):
                o += _dot((px * inv).astype(BF16), v)
            o_ref[:, g * hd:(g + 1) * hd] = o.astype(BF16)
            lse_out = jnp.where(lane == g, m + jnp.log(l), lse_out)
        lse_ref[...] = lse_out

    return pl.pallas_call(
        body, name=name, grid=(ATTN_KV_HEADS, nb),
        in_specs=_attn_specs(n_ctx, nb, grp, n_qh) + [pl.BlockSpec((1, SUBLANES, LANES), lambda h, b: (h, 0, 0))],
        out_specs=[pl.BlockSpec((blk, grp * hd), lambda h, b: (b, h)), pl.BlockSpec((blk, LANES), lambda h, b: (b, h))],
        out_shape=[jax.ShapeDtypeStruct((t, n_qh * hd), BF16), jax.ShapeDtypeStruct((t, ATTN_KV_HEADS * LANES), F32)],
        compiler_params=_cp(dimension_semantics=("parallel", "parallel")),
    )(qkvr, qkvr, qkvr, qkvr, qkvr, qkvr, qkvr, qkvr, qkvr, sinks)


def attention_bwd(do, o, lse, qkvr, sinks, n_ctx, n_qh, name):
    t = qkvr.shape[0]
    hd, blk = ATTN_HEAD_DIM, CHUNK
    grp = n_qh // ATTN_KV_HEADS
    nb, nctx_b = t // blk, n_ctx // blk
    scale = hd ** -0.5
    kvw = ATTN_KV_HEADS * hd

    def body(do_ref, o_ref, lse_ref, q_ref, kc_ref, kp_ref, ko_ref, kn_ref, vc_ref, vp_ref, vo_ref, vn_ref, s_ref,
             dq_ref, dkc_ref, dvc_ref, dkp_ref, dvp_ref, dsk_ref):
        b = pl.program_id(1)
        masks = _attn_masks(b, nctx_b, nb)
        ks = [kc_ref[...], kp_ref[...], ko_ref[...], kn_ref[...]]
        vs = [vc_ref[...], vp_ref[...], vo_ref[...], vn_ref[...]]
        lane = lax.broadcasted_iota(jnp.int32, (1, LANES), 1)
        dks = [jnp.zeros(k.shape, F32) for k in ks]
        dvs = [jnp.zeros(v.shape, F32) for v in vs]
        dsk = jnp.zeros((1, LANES), F32)
        for g in range(grp):
            sl = slice(g * hd, (g + 1) * hd)
            q = q_ref[:, sl]
            dof = do_ref[:, sl]
            dob = dof.astype(BF16)
            lse = lse_ref[:, g:g + 1]
            delta = jnp.sum(dof * o_ref[:, sl].astype(F32), axis=-1, keepdims=True)
            s = [_dot(q, k, _NT) * scale for k in ks]
            s = [s[0]] + [jnp.where(m, sx, -jnp.inf) for m, sx in zip(masks, s[1:])]
            dq = jnp.zeros((blk, hd), F32)
            for x in range(4):
                p = jnp.exp(s[x] - lse)
                ds = (p * (_dot(dob, vs[x], _NT) - delta)).astype(BF16)
                dq += _dot(ds, ks[x])
                dks[x] += _dot(ds, q, _TN)
                dvs[x] += _dot(p.astype(BF16), dob, _TN)
            dq_ref[:, sl] = dq * scale
            p_sink = jnp.exp(s_ref[0, 0:1, g:g + 1] - lse)
            dsk = dsk + jnp.where(lane == g, -jnp.sum(p_sink * delta, axis=0, keepdims=True), 0.0)

        @pl.when(b == 0)
        def _():
            dkc_ref[...] = jnp.zeros_like(dkc_ref)
            dvc_ref[...] = jnp.zeros_like(dvc_ref)
            dsk_ref[...] = jnp.zeros_like(dsk_ref)

        dkc_ref[...] += dks[0] * scale
        dvc_ref[...] += dvs[0]
        dsk_ref[0, 0:1, :] += dsk
        for x in range(3):
            dkp_ref[0, x] = dks[x + 1] * scale
            dvp_ref[0, x] = dvs[x + 1]

    part = pl.BlockSpec((1, 3, blk, hd), lambda h, b: (b, 0, 0, h))
    ctxo = pl.BlockSpec((n_ctx, hd), lambda h, b: (0, h))
    return pl.pallas_call(
        body, name=name, grid=(ATTN_KV_HEADS, nb),
        in_specs=[pl.BlockSpec((blk, grp * hd), lambda h, b: (b, h)), pl.BlockSpec((blk, grp * hd), lambda h, b: (b, h)),
                  pl.BlockSpec((blk, LANES), lambda h, b: (b, h))] + _attn_specs(n_ctx, nb, grp, n_qh)
                 + [pl.BlockSpec((1, SUBLANES, LANES), lambda h, b: (h, 0, 0))],
        out_specs=[pl.BlockSpec((blk, grp * hd), lambda h, b: (b, h)), ctxo, ctxo, part, part,
                   pl.BlockSpec((1, SUBLANES, LANES), lambda h, b: (h, 0, 0))],
        out_shape=[jax.ShapeDtypeStruct((t, n_qh * hd), F32), jax.ShapeDtypeStruct((n_ctx, kvw), F32), jax.ShapeDtypeStruct((n_ctx, kvw), F32),
                   jax.ShapeDtypeStruct((nb, 3, blk, kvw), F32), jax.ShapeDtypeStruct((nb, 3, blk, kvw), F32),
                   jax.ShapeDtypeStruct((ATTN_KV_HEADS, SUBLANES, LANES), F32)],
        compiler_params=_cp(dimension_semantics=("parallel", "arbitrary")),
    )(do, o, lse, qkvr, qkvr, qkvr, qkvr, qkvr, qkvr, qkvr, qkvr, qkvr, sinks)


def band_reduce(ctx_part, band_part, n_ctx, name):
    nb, _, blk, w = band_part.shape
    nctx_b = n_ctx // blk

    def body(c_ref, p_ref, o_ref, n_ref, out_ref):
        b = pl.program_id(0)
        band = p_ref[0, 0] + o_ref[0, 0] + jnp.where(b + 1 <= nb - 1, n_ref[0, 0], 0.0)
        out_ref[...] = jnp.where(b < nctx_b, c_ref[...], band)

    return pl.pallas_call(
        body, name=name, grid=(nb,),
        in_specs=[pl.BlockSpec((blk, w), lambda b: (jnp.minimum(b, nctx_b - 1), 0)),
                  pl.BlockSpec((1, 1, blk, w), lambda b: (jnp.maximum(b - 1, 0), 2, 0, 0)),
                  pl.BlockSpec((1, 1, blk, w), lambda b: (b, 1, 0, 0)),
                  pl.BlockSpec((1, 1, blk, w), lambda b: (jnp.minimum(b + 1, nb - 1), 0, 0, 0))],
        out_specs=pl.BlockSpec((blk, w), lambda b: (b, 0)), out_shape=jax.ShapeDtypeStruct((nb * blk, w), F32),
        compiler_params=_cp(),
    )(ctx_part, band_part, band_part, band_part)


def loss_grad(xf, target, n_ctx, name):
    t, d = xf.shape
    tm = _pick(n_ctx, (ROW_TILE, 128))
    nct = n_ctx // tm

    def body(x_ref, t_ref, dy_ref, s_ref):
        i = pl.program_id(0)
        err = jnp.where(i < nct, 0.0, x_ref[...] - t_ref[...])
        dy_ref[...] = err * (1.0 / d)
        _acc_rows(s_ref, i, [jnp.sum(err * err, axis=0, keepdims=True)])

    return pl.pallas_call(
        body, name=name, grid=(t // tm,),
        in_specs=[pl.BlockSpec((tm, d), lambda i: (i, 0)), pl.BlockSpec((tm, d), lambda i: (jnp.maximum(i - nct, 0), 0))],
        out_specs=[pl.BlockSpec((tm, d), lambda i: (i, 0)), pl.BlockSpec((SUBLANES, d), lambda i: (0, 0))],
        out_shape=[jax.ShapeDtypeStruct((t, d), F32), jax.ShapeDtypeStruct((SUBLANES, d), F32)],
        compiler_params=_cp(dimension_semantics=("arbitrary",)),
    )(xf, target)


def adamw(w, g, m, v, name):
    r, c = w.shape
    tr = r
    while tr % 2 == 0 and tr * c * 4 > (1 << 20) and (tr // 2) % SUBLANES == 0:
        tr //= 2
    bc1, bc2 = 1.0 - ADAM_B1 ** ADAM_STEP, 1.0 - ADAM_B2 ** ADAM_STEP

    def body(w_ref, g_ref, m_ref, v_ref, d_ref, nm_ref, nv_ref):
        gv = g_ref[...]
        nm = ADAM_B1 * m_ref[...] + (1.0 - ADAM_B1) * gv
        nv = ADAM_B2 * v_ref[...] + (1.0 - ADAM_B2) * (gv * gv)
        nm_ref[...] = nm
        nv_ref[...] = nv
        d_ref[...] = -ADAM_LR * ((nm / bc1) / (jnp.sqrt(nv / bc2) + ADAM_EPS) + ADAM_WD * w_ref[...])

    blk = pl.BlockSpec((tr, c), lambda i: (i, 0))
    return pl.pallas_call(
        body, name=name, grid=(r // tr,), in_specs=[blk] * 4, out_specs=[blk] * 3,
        out_shape=[jax.ShapeDtypeStruct((r, c), F32)] * 3, compiler_params=_cp(dimension_semantics=("parallel",)),
    )(w, g, m, v)


ADA_ROWS = 16


def ada_fwd(cs, w, name):
    l, d, ns = w.shape
    tn = _pick(ns, (512, 256, 128))

    def body(c_ref, w_ref, o_ref):
        o_ref[0] = _dot(c_ref[...], w_ref[0].astype(BF16))

    return pl.pallas_call(
        body, name=name, grid=(l, ns // tn),
        in_specs=[pl.BlockSpec((ADA_ROWS, d), lambda i, j: (0, 0)), pl.BlockSpec((1, d, tn), lambda i, j: (i, 0, j))],
        out_specs=pl.BlockSpec((1, ADA_ROWS, tn), lambda i, j: (i, 0, j)),
        out_shape=jax.ShapeDtypeStruct((l, ADA_ROWS, ns), F32), compiler_params=_cp(),
    )(cs, w)


def ada_bwd(cs, gmod, w, name):
    l, d, ns = w.shape
    tn = _pick(ns, (512, 256, 128))

    def body(c_ref, g_ref, w_ref, dw_ref, dc_ref):
        first = (pl.program_id(0) == 0) & (pl.program_id(1) == 0)
        gb = g_ref[0].astype(BF16)
        dw_ref[0] = _dot(c_ref[...], gb, _TN)
        part = _dot(gb, w_ref[0].astype(BF16), _NT)

        @pl.when(first)
        def _():
            dc_ref[...] = part

        @pl.when(jnp.logical_not(first))
        def _():
            dc_ref[...] += part

    return pl.pallas_call(
        body, name=name, grid=(l, ns // tn),
        in_specs=[pl.BlockSpec((ADA_ROWS, d), lambda i, j: (0, 0)), pl.BlockSpec((1, ADA_ROWS, tn), lambda i, j: (i, 0, j)),
                  pl.BlockSpec((1, d, tn), lambda i, j: (i, 0, j))],
        out_specs=[pl.BlockSpec((1, d, tn), lambda i, j: (i, 0, j)), pl.BlockSpec((ADA_ROWS, d), lambda i, j: (0, 0))],
        out_shape=[jax.ShapeDtypeStruct((l, d, ns), F32), jax.ShapeDtypeStruct((ADA_ROWS, d), F32)],
        compiler_params=_cp(dimension_semantics=("arbitrary", "arbitrary")),
    )(cs, gmod, w)


def sum_leading(a, name):
    k, r, c = a.shape
    tr = _pick(r, (256, 128, 64, 32, 16, 8))

    def body(a_ref, o_ref):
        acc = a_ref[0]
        for q in range(1, k):
            acc = acc + a_ref[q]
        o_ref[...] = acc

    return pl.pallas_call(
        body, name=name, grid=(r // tr,), in_specs=[pl.BlockSpec((k, tr, c), lambda i: (0, i, 0))],
        out_specs=pl.BlockSpec((tr, c), lambda i: (i, 0)), out_shape=jax.ShapeDtypeStruct((r, c), F32), compiler_params=_cp(),
    )(a)


def _mesh_pos():
    return lax.axis_index("x"), lax.axis_index("y"), lax.axis_index("c")


def _other_chips(x, y):
    return [(1 - x, y), (x, 1 - y), (1 - x, 1 - y)]


def _rcopy(src, dst, send_sems, recv_sems, k, to):
    return pltpu.make_async_remote_copy(src_ref=src, dst_ref=dst, send_sem=send_sems.at[k], recv_sem=recv_sems.at[k],
                                        device_id=to, device_id_type=MESH)


def small_allgather(vs, name):
    nv = len(vs)

    def body(*refs):
        v_refs, out_refs = refs[:nv], refs[nv:2 * nv]
        send_sems, recv_sems, local_sems = refs[2 * nv:]
        x, y, c = _mesh_pos()
        sibling = (x, y, 1 - c)
        chips = _other_chips(x, y)

        def blk(q, px, py, pc):
            return out_refs[q].at[4 * px + 2 * py + pc]

        mine = [pltpu.make_async_copy(v_refs[q], blk(q, x, y, c), local_sems.at[q]) for q in range(nv)]
        first, passed = [], []
        for q in range(nv):
            mine[q].start()
            first.append(_rcopy(v_refs[q], blk(q, x, y, c), send_sems, recv_sems, 7 * q, sibling))
            first += [_rcopy(v_refs[q], blk(q, x, y, c), send_sems, recv_sems, 7 * q + 1 + j, (*chip, c)) for j, chip in enumerate(chips)]
        for cp in first:
            cp.start()
        for q in range(nv):
            for j, chip in enumerate(chips):
                _rcopy(blk(q, *chip, c), blk(q, *chip, c), send_sems, recv_sems, 7 * q + 1 + j, (x, y, c)).wait_recv()
                passed.append(_rcopy(blk(q, *chip, c), blk(q, *chip, c), send_sems, recv_sems, 7 * q + 4 + j, sibling))
                passed[-1].start()
        for q in range(nv):
            _rcopy(blk(q, x, y, 1 - c), blk(q, x, y, 1 - c), send_sems, recv_sems, 7 * q, (x, y, c)).wait_recv()
            for j, chip in enumerate(chips):
                _rcopy(blk(q, *chip, 1 - c), blk(q, *chip, 1 - c), send_sems, recv_sems, 7 * q + 4 + j, (x, y, c)).wait_recv()
        for cp in first + passed:
            cp.wait_send()
        for cp in mine:
            cp.wait()

    vm = pl.BlockSpec(memory_space=pltpu.VMEM)
    return pl.pallas_call(
        body, name=name, out_shape=[jax.ShapeDtypeStruct((N_DEV, *v.shape), v.dtype) for v in vs],
        in_specs=[vm] * nv, out_specs=[vm] * nv,
        scratch_shapes=[pltpu.SemaphoreType.DMA((7 * nv,)), pltpu.SemaphoreType.DMA((7 * nv,)), pltpu.SemaphoreType.DMA((nv,))],
        compiler_params=_cp(),
    )(*vs)


_HBM = pl.BlockSpec(memory_space=pltpu.HBM)


STREAM_TILE_BYTES = 2 * 1024 * 1024


def _stream_rows(rows, row_bytes):
    tr = 16
    while rows % (2 * tr) == 0 and 2 * tr * row_bytes <= STREAM_TILE_BYTES:
        tr *= 2
    assert rows % tr == 0
    return tr


def _scalars(*vals):
    return jnp.stack([jnp.asarray(v, jnp.int32) for v in vals])


def place_own(w, chip, name):
    l, k, ns = w.shape
    tk = _pick(k, (256, 128, 64))

    def body(s_ref, w_ref, o_ref):
        o_ref[...] = w_ref[...].astype(BF16)

    grid_spec = pltpu.PrefetchScalarGridSpec(
        num_scalar_prefetch=1, grid=(l, k // tk),
        in_specs=[pl.BlockSpec((None, tk, ns), lambda i, j, s: (i, j, 0))],
        out_specs=pl.BlockSpec((None, None, tk, ns), lambda i, j, s: (i, s[0], j, 0)))
    return pl.pallas_call(body, name=name, grid_spec=grid_spec, out_shape=jax.ShapeDtypeStruct((l, N_CHIP, k, ns), BF16),
                          compiler_params=_cp())(_scalars(chip), w)


def weights_allgather(bufs, name):
    nb = len(bufs)

    def body(*refs):
        out_refs = refs[nb:2 * nb]
        send_sems, recv_sems = refs[2 * nb:]
        x, y, c = _mesh_pos()
        sibling = (x, y, 1 - c)
        chips = _other_chips(x, y)

        def half(q, px, py, pc):
            hk = out_refs[q].shape[2] // 2
            return out_refs[q].at[:, 2 * px + py, pl.ds(pc * hk, hk)]

        first, passed = [], []
        for q in range(nb):
            first += [_rcopy(half(q, x, y, c), half(q, x, y, c), send_sems, recv_sems, 6 * q + j, (*chip, c)) for j, chip in enumerate(chips)]
        for cp in first:
            cp.start()
        for q in range(nb):
            for j, chip in enumerate(chips):
                _rcopy(half(q, *chip, c), half(q, *chip, c), send_sems, recv_sems, 6 * q + j, (x, y, c)).wait_recv()
                passed.append(_rcopy(half(q, *chip, c), half(q, *chip, c), send_sems, recv_sems, 6 * q + 3 + j, sibling))
                passed[-1].start()
        for q in range(nb):
            for j, chip in enumerate(chips):
                _rcopy(half(q, *chip, 1 - c), half(q, *chip, 1 - c), send_sems, recv_sems, 6 * q + 3 + j, (x, y, c)).wait_recv()
        for cp in first + passed:
            cp.wait_send()

    return pl.pallas_call(
        body, name=name, out_shape=[jax.ShapeDtypeStruct(b.shape, b.dtype) for b in bufs],
        in_specs=[_HBM] * nb, out_specs=[_HBM] * nb, input_output_aliases={q: q for q in range(nb)},
        scratch_shapes=[pltpu.SemaphoreType.DMA((6 * nb,)), pltpu.SemaphoreType.DMA((6 * nb,))],
        compiler_params=_cp(),
    )(*bufs)


def grads_pair_exchange(gs, name):
    ng = len(gs)

    def body(*refs):
        g_refs, out_refs = refs[:ng], refs[ng:2 * ng]
        send_sems, recv_sems = refs[2 * ng:]
        x, y, c = _mesh_pos()
        cps = []
        for q in range(ng):
            h = g_refs[q].shape[1] // 2
            cps.append(_rcopy(g_refs[q].at[:, pl.ds((1 - c) * h, h)], out_refs[q], send_sems, recv_sems, q, (x, y, 1 - c)))
            cps[-1].start()
        for cp in cps:
            cp.wait()

    return pl.pallas_call(
        body, name=name, out_shape=[jax.ShapeDtypeStruct((g.shape[0], g.shape[1] // 2, g.shape[2]), g.dtype) for g in gs],
        in_specs=[_HBM] * ng, out_specs=[_HBM] * ng,
        scratch_shapes=[pltpu.SemaphoreType.DMA((ng,)), pltpu.SemaphoreType.DMA((ng,))], compiler_params=_cp(),
    )(*gs)


def pair_add(g, got, c, chip, name):
    n, r, c_ = g.shape
    h = r // 2
    tr = _stream_rows(h, n * c_ * 2)
    nblk = h // tr

    def body(s_ref, g_ref, o_ref, pair_ref, land_ref):
        pair_ref[...] = (g_ref[...].astype(F32) + o_ref[...].astype(F32)).astype(BF16)
        me = s_ref[1]
        land_ref[...] = (g_ref[me].astype(F32) + o_ref[me].astype(F32)).astype(BF16)

    grid_spec = pltpu.PrefetchScalarGridSpec(
        num_scalar_prefetch=1, grid=(nblk,),
        in_specs=[pl.BlockSpec((n, tr, c_), lambda i, s: (0, s[0] * nblk + i, 0)), pl.BlockSpec((n, tr, c_), lambda i, s: (0, i, 0))],
        out_specs=[pl.BlockSpec((n, tr, c_), lambda i, s: (0, i, 0)), pl.BlockSpec((None, tr, c_), lambda i, s: (s[1], i, 0))])
    return pl.pallas_call(body, name=name, grid_spec=grid_spec, out_shape=[jax.ShapeDtypeStruct((n, h, c_), BF16)] * 2,
                          compiler_params=_cp())(_scalars(c, chip), g, got)


def grads_chip_exchange(pairs, lands, name):
    ng = len(pairs)

    def body(*refs):
        p_refs, out_refs = refs[:ng], refs[2 * ng:3 * ng]
        send_sems, recv_sems = refs[3 * ng:]
        x, y, c = _mesh_pos()
        chips = _other_chips(x, y)
        me = 2 * x + y
        sends = []
        for q in range(ng):
            sends += [_rcopy(p_refs[q].at[2 * px + py], out_refs[q].at[me], send_sems, recv_sems, 3 * q + j, (px, py, c))
                      for j, (px, py) in enumerate(chips)]
        for cp in sends:
            cp.start()
        for q in range(ng):
            for j, (px, py) in enumerate(chips):
                _rcopy(p_refs[q].at[me], out_refs[q].at[2 * px + py], send_sems, recv_sems, 3 * q + j, (x, y, c)).wait_recv()
        for cp in sends:
            cp.wait_send()

    return pl.pallas_call(
        body, name=name, out_shape=[jax.ShapeDtypeStruct(a.shape, a.dtype) for a in lands],
        in_specs=[_HBM] * (2 * ng), out_specs=[_HBM] * ng, input_output_aliases={ng + q: q for q in range(ng)},
        scratch_shapes=[pltpu.SemaphoreType.DMA((3 * ng,)), pltpu.SemaphoreType.DMA((3 * ng,))], compiler_params=_cp(),
    )(*pairs, *lands)


def sum_chips(a, c, name):
    k, h, c_ = a.shape
    tr = _stream_rows(h, k * c_ * 2)
    nblk = h // tr

    def body(s_ref, a_ref, o_ref):
        acc = a_ref[0].astype(F32)
        for q in range(1, k):
            acc = acc + a_ref[q].astype(F32)
        o_ref[...] = acc

    grid_spec = pltpu.PrefetchScalarGridSpec(
        num_scalar_prefetch=1, grid=(nblk,), in_specs=[pl.BlockSpec((k, tr, c_), lambda i, s: (0, i, 0))],
        out_specs=pl.BlockSpec((tr, c_), lambda i, s: (s[0] * nblk + i, 0)))
    return pl.pallas_call(body, name=name, grid_spec=grid_spec, out_shape=jax.ShapeDtypeStruct((2 * h, c_), F32),
                          compiler_params=_cp())(_scalars(c), a)


def halves_exchange(outs, name):
    ng = len(outs)

    def body(*refs):
        out_refs = refs[ng:2 * ng]
        send_sems, recv_sems = refs[2 * ng:]
        x, y, c = _mesh_pos()
        cps = []
        for q in range(ng):
            h = out_refs[q].shape[0] // 2
            cps.append(_rcopy(out_refs[q].at[pl.ds(c * h, h)], out_refs[q].at[pl.ds(c * h, h)], send_sems, recv_sems, q, (x, y, 1 - c)))
            cps[-1].start()
        for q in range(ng):
            h = out_refs[q].shape[0] // 2
            _rcopy(out_refs[q].at[pl.ds((1 - c) * h, h)], out_refs[q].at[pl.ds((1 - c) * h, h)], send_sems, recv_sems, q, (x, y, c)).wait_recv()
        for cp in cps:
            cp.wait_send()

    return pl.pallas_call(
        body, name=name, out_shape=[jax.ShapeDtypeStruct(a.shape, a.dtype) for a in outs],
        in_specs=[_HBM] * ng, out_specs=[_HBM] * ng, input_output_aliases={q: q for q in range(ng)},
        scratch_shapes=[pltpu.SemaphoreType.DMA((ng,)), pltpu.SemaphoreType.DMA((ng,))], compiler_params=_cp(),
    )(*outs)


def _rows8(a):
    return jnp.pad(a, ((0, -a.shape[0] % SUBLANES), (0, 0)))


def _chips_cols(g, rows):
    return jnp.concatenate([g[2 * j, :rows] for j in range(N_CHIP)], axis=-1)


BIG = (("ssd_w_in", "col"), ("ssd_w_out", "row"), ("attn_w_qkv", "col"), ("attn_w_o", "row"), ("ffn_w_up", "col"), ("ffn_w_down", "row"))
WEIGHTS = ("c_ctx", "ada_w", "ada_b", "norm1_w", "norm2_w", "ssd_w_in", "ssd_conv_w", "ssd_conv_b", "ssd_dt_bias_f", "ssd_dt_bias_b",
           "ssd_a_log_f", "ssd_a_log_b", "ssd_d", "ssd_norm_w", "ssd_w_out", "attn_w_qkv", "attn_q_gain", "attn_k_gain", "attn_sinks",
           "attn_w_o", "ffn_w_up", "ffn_conv_w", "ffn_conv_b", "ffn_w_down")


def _taps_bias(w3, b):
    return jnp.concatenate([w3, b[None, :], jnp.zeros((SUBLANES - 4, w3.shape[1]), F32)], axis=0)


def kernel(x, c, ctx, c_ctx, ada_w, ada_b, norm1_w, norm2_w, ssd_w_in, ssd_conv_w, ssd_conv_b, ssd_dt_bias_f, ssd_dt_bias_b, ssd_a_log_f, ssd_a_log_b, ssd_d, ssd_norm_w, ssd_w_out, attn_w_qkv, attn_q_gain, attn_k_gain, attn_sinks, attn_w_o, ffn_w_up, ffn_conv_w, ffn_conv_b, ffn_w_down, loss_target, m_c_ctx, m_ada_w, m_ada_b, m_norm1_w, m_norm2_w, m_ssd_w_in, m_ssd_conv_w, m_ssd_conv_b, m_ssd_dt_bias_f, m_ssd_dt_bias_b, m_ssd_a_log_f, m_ssd_a_log_b, m_ssd_d, m_ssd_norm_w, m_ssd_w_out, m_attn_w_qkv, m_attn_q_gain, m_attn_k_gain, m_attn_sinks, m_attn_w_o, m_ffn_w_up, m_ffn_conv_w, m_ffn_conv_b, m_ffn_w_down, v_c_ctx, v_ada_w, v_ada_b, v_norm1_w, v_norm2_w, v_ssd_w_in, v_ssd_conv_w, v_ssd_conv_b, v_ssd_dt_bias_f, v_ssd_dt_bias_b, v_ssd_a_log_f, v_ssd_a_log_b, v_ssd_d, v_ssd_norm_w, v_ssd_w_out, v_attn_w_qkv, v_attn_q_gain, v_attn_k_gain, v_attn_sinks, v_attn_w_o, v_ffn_w_up, v_ffn_conv_w, v_ffn_conv_b, v_ffn_w_down):
    args = locals()
    w = {n: args[n] for n in WEIGHTS}
    mom = {n: args["m_" + n] for n in WEIGHTS}
    var = {n: args["v_" + n] for n in WEIGHTS}

    ix, iy, ic = _mesh_pos()
    chip = 2 * ix + iy
    dev = 2 * chip + ic
    depth, d = norm1_w.shape
    n_ctx, seq = ctx.shape[1], x.shape[1]
    t = n_ctx + seq
    d_inner = ssd_norm_w.shape[1]
    heads = ssd_d.shape[1]
    n_qh = attn_sinks.shape[1]
    grp = n_qh // ATTN_KV_HEADS
    d_ff = ffn_w_down.shape[1] * N_CHIP
    xbc_w = ssd_conv_b.shape[1]
    dt_col = d_inner + xbc_w
    n_ssd, n_att = ssd_w_in.shape[0], attn_w_qkv.shape[0]

    sconv_rows, fconv_rows = n_ssd * 3, depth * 3
    g_c, g_sconv, g_fconv = small_allgather(
        [_rows8(c), _rows8(ssd_conv_w.reshape(sconv_rows, -1)), _rows8(ffn_conv_w.reshape(fconv_rows, -1))], "gather_cond")
    c_all = g_c[:, 0]
    ssd_conv_full = _chips_cols(g_sconv, sconv_rows).reshape(n_ssd, 3, -1)
    ffn_conv_full = _chips_cols(g_fconv, fconv_rows).reshape(depth, 3, -1)

    cvec = jnp.concatenate([c_all, c_ctx[None, :], jnp.zeros((ADA_ROWS - N_DEV - 1, d), F32)], axis=0)
    cs16 = _silu(cvec).astype(BF16)
    mod_cols = ada_fwd(cs16, ada_w, "ada_fwd")
    ns_ada = mod_cols.shape[-1]
    (g_mod,) = small_allgather([mod_cols.reshape(depth * ADA_ROWS, ns_ada)], "gather_mod")
    mod_all = _chips_cols(g_mod, depth * ADA_ROWS).reshape(depth, ADA_ROWS, -1) + ada_b[:, None, :]
    mod_lat = lax.dynamic_index_in_dim(mod_all, dev, axis=1, keepdims=False)
    mod_ctx = mod_all[:, N_DEV]
    mods = jnp.stack([mod_ctx, mod_lat], axis=1).reshape(depth, 2, 6, d)

    bufs = weights_allgather([place_own(w[n], chip, f"place_{n}") for n, _ in BIG], "gather_weights")
    wg = dict(zip([n for n, _ in BIG], bufs))
    ssd_in_full = [wg["ssd_w_in"][jj].transpose(1, 0, 2).reshape(d, -1) for jj in range(n_ssd)]

    def w_col(n, layer):
        b = wg[n]
        return Mat(b.reshape(-1, *b.shape[2:]), "cols3", base=layer * N_CHIP, nparts=N_CHIP)

    def w_row(n, layer):
        b = wg[n]
        rows = N_CHIP * b.shape[2]
        return Mat(b.reshape(-1, b.shape[3]), "rows", rows=rows, row0=layer * rows)

    cos, sin = rope_tables(n_ctx, seq)
    sel = _group_select(heads, heads // SSD_GROUPS)
    bias128 = jnp.concatenate([ssd_dt_bias_f, ssd_dt_bias_b], axis=-1)[:, None, :]
    alog128 = jnp.concatenate([ssd_a_log_f, ssd_a_log_b], axis=-1)[:, None, :]
    dskip = jnp.repeat(ssd_d, SSD_HEAD_DIM, axis=-1)[:, None, :]
    gains = jnp.zeros((n_att, SUBLANES, ATTN_HEAD_DIM), F32).at[:, 0].set(attn_q_gain).at[:, 1].set(attn_k_gain)
    sinks3 = jnp.zeros((n_att, ATTN_KV_HEADS, SUBLANES, LANES), F32).at[:, :, 0, :grp].set(attn_sinks.reshape(n_att, ATTN_KV_HEADS, grp))
    wb_ssd = [_taps_bias(ssd_conv_full[j], ssd_conv_b[j]) for j in range(n_ssd)]
    wb_ffn = [_taps_bias(ffn_conv_full[i], ffn_conv_b[i]) for i in range(depth)]

    xs = jnp.concatenate([ctx[0], x[0]], axis=0)
    saved = []
    for i in range(depth):
        j = i // 2
        sh1, sc1, g1, sh2, sc2, g2 = [mods[i, :, q] for q in range(6)]
        s = {"x": xs}
        h1 = norm_mod(xs, norm1_w[i:i + 1], sh1, sc1, n_ctx, f"l{i}_norm1")
        s["h1"] = h1
        if i % 2 == 0:
            zx = matmul(h1, ssd_in_full[j], "nn", f"l{i}_ssd_in")
            pre, xbc = dwconv_act(zx, d_inner, xbc_w, wb_ssd[j], n_ctx, "silu", F32, f"l{i}_ssd_conv")
            _, da, dtg, ag, agt = ssd_prep(zx, dt_col, bias128[j], alog128[j], sel, heads, f"l{i}_ssd_prep")
            yf, hf = ssd_scan(xbc, dtg, ag, agt, d_inner, n_ctx, False, f"l{i}_ssd_scan_f")
            yb, hb = ssd_scan(xbc, dtg, ag, agt, d_inner, n_ctx, True, f"l{i}_ssd_scan_b")
            ytot, yn = ssd_gate_norm(yf, yb, xbc, zx, dskip[j], ssd_norm_w[j:j + 1], f"l{i}_ssd_gate_norm")
            x1, mix = matmul_gate_res(yn, w_row("ssd_w_out", j), xs, g1, n_ctx, f"l{i}_ssd_out")
            s.update(zx=zx, pre=pre, xbc=xbc, da=da, dtg=dtg, ag=ag, agt=agt, hf=hf, hb=hb, ytot=ytot, yn=yn)
        else:
            qkv = matmul(h1, w_col("attn_w_qkv", j), "nn", f"l{i}_attn_qkv")
            qkvr = qk_prep(qkv, gains[j], cos, sin, n_qh, ATTN_KV_HEADS, f"l{i}_qk_prep")
            o, lse = attention(qkvr, sinks3[j], n_ctx, n_qh, f"l{i}_attn")
            x1, mix = matmul_gate_res(o, w_row("attn_w_o", j), xs, g1, n_ctx, f"l{i}_attn_out")
            s.update(qkv=qkv, qkvr=qkvr, o=o, lse=lse)
        h2 = norm_mod(x1, norm2_w[i:i + 1], sh2, sc2, n_ctx, f"l{i}_norm2")
        u = matmul(h2, w_col("ffn_w_up", i), "nn", f"l{i}_ffn_up")
        pv, pg, act = dwconv_act(u, 0, 2 * d_ff, wb_ffn[i], n_ctx, "glu", BF16, f"l{i}_ffn_conv")
        x2, f = matmul_gate_res(act, w_row("ffn_w_down", i), x1, g2, n_ctx, f"l{i}_ffn_down")
        s.update(mix=mix, x1=x1, h2=h2, u=u, pv=pv, pg=pg, act=act, f=f)
        saved.append(s)
        xs = x2

    dxs, sq = loss_grad(xs, loss_target[0], n_ctx, "loss")
    loss = lax.psum(0.5 / d * jnp.sum(sq[0]), ("x", "y", "c"))

    gbuf = {n: None for n, _ in BIG}
    gshape = {n: ((N_CHIP, wg[n].shape[0] * wg[n].shape[2], wg[n].shape[3]), wg[n].shape[2]) for n, _ in BIG}

    def dw_into(n, kind, layer, a, b, name):
        shape, rows_per_layer = gshape[n]
        gbuf[n] = matmul(a, b, "tn", name, out_dtype=BF16, into=(kind, gbuf[n], shape, layer * rows_per_layer))

    ssd_in_g = [None] * n_ssd
    st_norm1, st_norm2, st_gate1, st_gate2 = ([None] * depth for _ in range(4))
    st_sconv, st_snorm, st_sd, st_sdt = ([None] * n_ssd for _ in range(4))
    st_gain, st_sink = [None] * n_att, [None] * n_att
    st_fconv = [None] * depth
    for i in reversed(range(depth)):
        j = i // 2
        s = saved[i]
        sh1, sc1, g1, sh2, sc2, g2 = [mods[i, :, q] for q in range(6)]
        df, st_gate2[i] = gate_bwd(dxs, s["f"], g2, n_ctx, f"l{i}_ffn_gate_bwd")
        dact = matmul(df, w_row("ffn_w_down", i), "nt", f"l{i}_ffn_down_dx")
        dw_into("ffn_w_down", "row", i, s["act"], df, f"l{i}_ffn_down_dw")
        du3, st_fconv[i] = dwconv_act_bwd(dact, 0, [s["pv"], s["pg"]], 0, s["u"], 0, wb_ffn[i], 0, d_ff, n_ctx, "glu", f"l{i}_ffn_conv_bwd")
        du = Mat(du3, "cols3", nparts=2)
        dh2 = matmul(du, w_col("ffn_w_up", i), "nt", f"l{i}_ffn_up_dx")
        dw_into("ffn_w_up", "col", i, s["h2"], du, f"l{i}_ffn_up_dw")
        dx1, st_norm2[i] = norm_mod_bwd(dh2, dxs, s["x1"], norm2_w[i:i + 1], sc2, n_ctx, f"l{i}_norm2_bwd")
        dmix, st_gate1[i] = gate_bwd(dx1, s["mix"], g1, n_ctx, f"l{i}_mix_gate_bwd")
        if i % 2 == 0:
            zx = s["zx"]
            dyn = matmul(dmix, w_row("ssd_w_out", j), "nt", f"l{i}_ssd_out_dx")
            dw_into("ssd_w_out", "row", j, s["yn"], dmix, f"l{i}_ssd_out_dw")
            dy, dzx, st_snorm[j] = ssd_gate_norm_bwd(dyn, s["ytot"], zx, ssd_norm_w[j:j + 1], f"l{i}_ssd_gate_norm_bwd")
            o1 = ssd_scan_bwd(dy, s["xbc"], s["dtg"], s["ag"], s["agt"], s["hf"], dskip[j], None, d_inner, n_ctx, False, f"l{i}_ssd_scan_f_bwd")
            o2 = ssd_scan_bwd(dy, s["xbc"], s["dtg"], s["ag"], s["agt"], s["hb"], None, o1[:3], d_inner, n_ctx, True, f"l{i}_ssd_scan_b_bwd")
            st_sd[j] = o1[6]
            gn = SSD_GROUPS * SSD_STATE
            conv_st = []
            for src, width, col, tag in ((o2[0], d_inner, 0, "x"), (o2[1], gn, d_inner, "b"), (o2[2], gn, d_inner + gn, "c")):
                dzx, st = dwconv_act_bwd(src, 0, [s["pre"]], col, zx, d_inner + col, wb_ssd[j], col, width, n_ctx, "silu",
                                         f"l{i}_ssd_conv_bwd_{tag}", into=dzx, ocol0=d_inner + col)
                conv_st.append(st)
            st_sconv[j] = jnp.concatenate(conv_st, axis=1)
            dzx, st_sdt[j] = ssd_prep_bwd(zx, dt_col, bias128[j], alog128[j], sel, o1[3:6], o2[3:6], s["da"], heads, dzx, f"l{i}_ssd_prep_bwd")
            dh1 = matmul(dzx, ssd_in_full[j], "nt", f"l{i}_ssd_in_dx")
            ssd_in_g[j] = matmul(s["h1"], dzx, "tn", f"l{i}_ssd_in_dw")
        else:
            do = matmul(dmix, w_row("attn_w_o", j), "nt", f"l{i}_attn_out_dx")
            dw_into("attn_w_o", "row", j, s["o"], dmix, f"l{i}_attn_out_dw")
            dq, dkc, dvc, dkp, dvp, st_sink[j] = attention_bwd(do, s["o"], s["lse"], s["qkvr"], sinks3[j], n_ctx, n_qh, f"l{i}_attn_bwd")
            dk = band_reduce(dkc, dkp, n_ctx, f"l{i}_dk_reduce")
            dv = band_reduce(dvc, dvp, n_ctx, f"l{i}_dv_reduce")
            dqkv, st_gain[j] = qk_prep_bwd(dq, dk, dv, s["qkv"], gains[j], cos, sin, f"l{i}_qk_prep_bwd")
            dh1 = matmul(dqkv, w_col("attn_w_qkv", j), "nt", f"l{i}_attn_qkv_dx")
            dw_into("attn_w_qkv", "col", j, s["h1"], dqkv, f"l{i}_attn_qkv_dw")
        dxs, st_norm1[i] = norm_mod_bwd(dh1, dx1, s["x"], norm1_w[i:i + 1], sc1, n_ctx, f"l{i}_norm1_bwd")
    grad_x = dxs[n_ctx:][None]

    rows_d = ([st_norm1[i][4:5] for i in range(depth)] + [st_norm2[i][4:5] for i in range(depth)]
              + [st[seg:seg + 1] for seg in (0, 1) for i in range(depth)
                 for st in (st_norm1[i][0:2], st_norm1[i][2:4], st_gate1[i][0:2], st_norm2[i][0:2], st_norm2[i][2:4], st_gate2[i][0:2])])
    a_d = jnp.concatenate(rows_d, axis=0)
    a_sconv = _rows8(jnp.concatenate([st[0:4] for st in st_sconv], axis=0))
    a_fconv = _rows8(jnp.concatenate([jnp.concatenate([st[0, 0:4], st[1, 0:4]], axis=1) for st in st_fconv], axis=0))
    a_di = _rows8(jnp.concatenate([st[0:1] for st in st_snorm] + [st[0:1] for st in st_sd], axis=0))
    a_128 = _rows8(jnp.concatenate([st[0:2] for st in st_sdt] + [st[0:2] for st in st_gain] + [st[:, 0] for st in st_sink], axis=0))
    gathered = small_allgather([a_d, a_sconv, a_fconv, a_di, a_128], "gather_small_grads")
    s_d, s_sconv, s_fconv, s_di, s_128 = [sum_leading(g, f"sum_small_grads_{q}") for q, g in enumerate(gathered)]
    grads = {"norm1_w": s_d[0:depth], "norm2_w": s_d[depth:2 * depth]}
    dctx_sum = s_d[2 * depth:8 * depth].reshape(depth, 6 * d)
    grads["ada_b"] = dctx_sum + s_d[8 * depth:14 * depth].reshape(depth, 6 * d)
    sc = s_sconv[:4 * n_ssd].reshape(n_ssd, 4, -1)
    own_cols = lambda a, width: lax.dynamic_slice_in_dim(a, chip * width, width, axis=a.ndim - 1)
    grads["ssd_conv_w"], grads["ssd_conv_b"] = own_cols(sc[:, 0:3], ssd_conv_w.shape[-1]), sc[:, 3]
    fc = s_fconv[:4 * depth].reshape(depth, 4, -1)
    grads["ffn_conv_w"], grads["ffn_conv_b"] = own_cols(fc[:, 0:3], ffn_conv_w.shape[-1]), fc[:, 3]
    grads["ssd_norm_w"] = s_di[0:n_ssd]
    grads["ssd_d"] = jnp.sum(s_di[n_ssd:2 * n_ssd].reshape(n_ssd, heads, SSD_HEAD_DIM), axis=-1)
    dt_st = s_128[0:2 * n_ssd].reshape(n_ssd, 2, LANES)
    grads["ssd_dt_bias_f"], grads["ssd_dt_bias_b"] = dt_st[:, 0, :heads], dt_st[:, 0, heads:]
    grads["ssd_a_log_f"], grads["ssd_a_log_b"] = dt_st[:, 1, :heads], dt_st[:, 1, heads:]
    gain_st = s_128[2 * n_ssd:2 * n_ssd + 2 * n_att].reshape(n_att, 2, LANES)
    grads["attn_q_gain"], grads["attn_k_gain"] = gain_st[:, 0], gain_st[:, 1]
    sink_st = s_128[2 * n_ssd + 2 * n_att:2 * n_ssd + 2 * n_att + ATTN_KV_HEADS * n_att].reshape(n_att, ATTN_KV_HEADS, LANES)
    grads["attn_sinks"] = sink_st[:, :, :grp].reshape(n_att, n_qh)

    dlat_rows = gathered[0][:, 8 * depth:14 * depth].reshape(N_DEV, depth, 6 * d)
    gmod = jnp.concatenate([dlat_rows, dctx_sum[None], jnp.zeros((ADA_ROWS - N_DEV - 1, depth, 6 * d), F32)], axis=0).transpose(1, 0, 2)
    grads["ada_w"], dcs = ada_bwd(cs16, own_cols(gmod, ns_ada), ada_w, "ada_bwd")
    dcc = (dcs[N_DEV] * _dsilu(c_ctx))[None, :]
    (g_dcc,) = small_allgather([_rows8(dcc)], "gather_dc_ctx")
    grads["c_ctx"] = sum_leading(g_dcc[0::2], "sum_dc_ctx")[0]

    rs_names = [n for n, _ in BIG if n != "ssd_w_in"]
    rs_in = [gbuf[n] for n in rs_names]
    rs_in += [g.reshape(d, N_CHIP, -1).transpose(1, 0, 2).astype(BF16) for g in ssd_in_g]
    got = grads_pair_exchange(rs_in, "rs_pair_exchange")
    pairs, lands = zip(*[pair_add(g, o, ic, chip, f"rs_pair_add_{q}") for q, (g, o) in enumerate(zip(rs_in, got))])
    landed = grads_chip_exchange(list(pairs), list(lands), "rs_chip_exchange")
    red = halves_exchange([sum_chips(a, ic, f"rs_chip_sum_{q}") for q, a in enumerate(landed)], "rs_halves_exchange")
    for n, r in zip(rs_names, red):
        grads[n] = r.reshape(w[n].shape)
    grads["ssd_w_in"] = jnp.stack(red[len(rs_names):])

    delta, new_m, new_v = {}, {}, {}
    for n in WEIGHTS:
        shp = w[n].shape
        two = lambda a: a.reshape(-1, shp[-1])
        dl, nm, nv = adamw(two(w[n]), two(grads[n]), two(mom[n]), two(var[n]), f"adamw_{n}")
        delta[n], new_m[n], new_v[n] = dl.reshape(shp), nm.reshape(shp), nv.reshape(shp)
    grads = {n: grads[n].reshape(w[n].shape) for n in WEIGHTS}
    return (loss, grad_x, *[grads[n] for n in WEIGHTS], *[delta[n] for n in WEIGHTS], *[new_m[n] for n in WEIGHTS], *[new_v[n] for n in WEIGHTS])
```

```python
import functools
import math

import jax
import jax.numpy as jnp
from jax import lax
from jax.experimental import pallas as pl
from jax.experimental.pallas import tpu as pltpu

F32 = jnp.float32
BF16 = jnp.bfloat16

SSD_HEAD_DIM = 64
SSD_GROUPS = 8
SSD_STATE = 128
CHUNK = 128
ATTN_HEAD_DIM = 128
ATTN_KV_HEADS = 4
GRID_W = 64
ROPE_THETA = 10000.0
NORM_EPS = 1e-6
ADAM_LR, ADAM_B1, ADAM_B2, ADAM_EPS, ADAM_WD, ADAM_STEP = 0.001, 0.9, 0.999, 1e-08, 0.01, 10

LANES = 128
SUBLANES = 8
VMEM_LIMIT = 56 * 1024 * 1024
MESH = pl.DeviceIdType.MESH
N_DEV = 8
N_CHIP = 4


def _cp(**kw):
    return pltpu.CompilerParams(vmem_limit_bytes=VMEM_LIMIT, **kw)


def _pallas(body, name, grid, in_specs, out_specs, out_shape, scratch, args, dims, aliases=None, carry=None):
    in_specs, out_specs, out_shape, scratch, args = list(in_specs), list(out_specs), list(out_shape), list(scratch), list(args)
    aliases = dict(aliases or {})
    if carry is not None:
        bufs, items = carry
        n_in, n_out, nb = len(in_specs), len(out_specs), len(bufs)
        inner = body

        def body(*refs):
            ins, outs = refs[:n_in], refs[n_in + nb:n_in + nb + n_out]
            cbufs = refs[n_in + nb + n_out:n_in + 2 * nb + n_out]
            rest = refs[n_in + 2 * nb + n_out:]
            sems = rest[-2:]
            ids = [pl.program_id(q) for q in range(len(grid))]
            first = functools.reduce(jnp.logical_and, [i == 0 for i in ids])
            last = functools.reduce(jnp.logical_and, [i == g - 1 for i, g in zip(ids, grid)])

            @pl.when(first)
            def _():
                for cp in _chip_sends(cbufs, items, *sems):
                    cp.start()

            inner(*ins, *outs, *rest[:-2])

            @pl.when(last)
            def _():
                _chip_recv_waits(cbufs, items, *sems)
                for cp in _chip_sends(cbufs, items, *sems):
                    cp.wait_send()

        aliases.update({n_in + q: n_out + q for q in range(nb)})
        in_specs += [pl.BlockSpec(memory_space=pltpu.HBM)] * nb
        out_specs += [pl.BlockSpec(memory_space=pltpu.HBM)] * nb
        out_shape += [jax.ShapeDtypeStruct(b.shape, b.dtype) for b in bufs]
        scratch += [pltpu.SemaphoreType.DMA((3 * len(items),))] * 2
        args += list(bufs)
    return pl.pallas_call(
        body, name=name, grid=grid, in_specs=in_specs, out_specs=out_specs, out_shape=out_shape, scratch_shapes=scratch,
        input_output_aliases=aliases, compiler_params=_cp(dimension_semantics=dims),
    )(*args)


def _pick(n, cands):
    for c in cands:
        if n % c == 0:
            return c
    return n


def _silu(x):
    return x * jax.nn.sigmoid(x)


def _dsilu(x):
    s = jax.nn.sigmoid(x)
    return s * (1.0 + x * (1.0 - s))


_DIMS = {"nn": (((1,), (0,)), ((), ())), "nt": (((1,), (1,)), ((), ())), "tn": (((0,), (0,)), ((), ()))}


TILES_M = (1408, 768, 512, 384, 256, 128)
TILES_N = (1408, 1024, 1152, 768, 512, 384, 256, 128)
TILES_K = (2048, 1408, 1152, 1024, 768, 512, 384, 256, 128)


class Mat:
    def __init__(self, arr, kind="plain", rows=None, row0=0, base=0, nparts=1):
        self.arr, self.kind, self.row0, self.base, self.nparts = arr, kind, row0, base, nparts
        if kind == "cols3":
            self.r, self.s = arr.shape[1], arr.shape[2] * nparts
        else:
            self.r, self.s = (rows if rows is not None else arr.shape[0]), arr.shape[1]

    def s_unit(self):
        return self.s // self.nparts

    def spec(self, tr, ts, r_of, s_of):
        if self.kind == "cols3":
            nps = self.s // self.nparts // ts
            return pl.BlockSpec((None, tr, ts), lambda *ids: (self.base + s_of(*ids) // nps, r_of(*ids), s_of(*ids) % nps))
        off = self.row0 // tr
        return pl.BlockSpec((tr, ts), lambda *ids: (off + r_of(*ids), s_of(*ids)))


MATMUL_VMEM_BUDGET = 36 * 1024 * 1024


def _fit_tiles(m, n, k, m_unit, n_unit, k_unit, result_bytes):
    cands = [[c for c in tiles if dim % c == 0 and unit % c == 0]
             for tiles, dim, unit in ((TILES_M, m, m_unit), (TILES_N, n, n_unit), (TILES_K, k, k_unit))]
    assert all(cands), (m, n, k, m_unit, n_unit, k_unit)
    idx = [0, 0, 0]
    while True:
        tm, tn, tk = (c[i] for c, i in zip(cands, idx))
        if 2 * 2 * (tm * tk + tk * tn) + tm * tn * (4 + 2 * result_bytes) <= MATMUL_VMEM_BUDGET:
            return tm, tn, tk
        shrinkable = [q for q in range(3) if idx[q] + 1 < len(cands[q])]
        assert shrinkable, (m, n, k)
        q = max(shrinkable, key=lambda q: cands[q][idx[q]])
        idx[q] += 1


def matmul(a, b, mode, name, out_dtype=F32, into=None, carry=None):
    a = a if isinstance(a, Mat) else Mat(a)
    b = b if isinstance(b, Mat) else Mat(b)
    m, k = (a.s, a.r) if mode == "tn" else (a.r, a.s)
    n, kb = (b.r, b.s) if mode == "nt" else (b.s, b.r)
    assert k == kb, (name, a.r, a.s, b.r, b.s)
    m_unit = a.s_unit() if mode == "tn" else m
    k_unit = math.gcd(a.s_unit() if mode != "tn" else k, b.s_unit() if mode == "nt" else k)
    n_unit = n if mode == "nt" else b.s_unit()
    if into is not None:
        kind, buf, shape, row0 = into
        if kind == "col":
            n_unit = math.gcd(n_unit, shape[2])
        else:
            m_unit = math.gcd(m_unit, m // N_CHIP)
    tm, tn, tk = _fit_tiles(m, n, k, m_unit, n_unit, k_unit, jnp.dtype(out_dtype).itemsize)
    nk = k // tk
    ii, jj, kk_ = (lambda j, i, kk: i), (lambda j, i, kk: j), (lambda j, i, kk: kk)
    a_spec = a.spec(tk, tm, kk_, ii) if mode == "tn" else a.spec(tm, tk, ii, kk_)
    b_spec = b.spec(tn, tk, jj, kk_) if mode == "nt" else b.spec(tk, tn, kk_, jj)
    in_specs, args, aliases = [a_spec, b_spec], [a.arr, b.arr], {}
    if into is None:
        out_spec = pl.BlockSpec((tm, tn), lambda j, i, kk: (i, j))
        out_shape = jax.ShapeDtypeStruct((m, n), out_dtype)
    else:
        assert row0 % tm == 0
        r0 = row0 // tm
        if kind == "col":
            npn = shape[2] // tn
            out_spec = pl.BlockSpec((None, tm, tn), lambda j, i, kk: (j // npn, r0 + i, j % npn))
        else:
            npm = m // N_CHIP // tm
            out_spec = pl.BlockSpec((None, tm, tn), lambda j, i, kk: (i // npm, r0 + i % npm, j))
        out_shape = jax.ShapeDtypeStruct(shape, out_dtype)
        if buf is not None:
            in_specs.append(pl.BlockSpec(memory_space=pl.ANY))
            args.append(buf)
            aliases = {2: 0}

    def body(a_ref, b_ref, *rest):
        o_ref, acc_ref = rest[-2:]
        kk = pl.program_id(2)
        part = lax.dot_general(a_ref[...], b_ref[...], _DIMS[mode], preferred_element_type=F32)
        if nk == 1:
            o_ref[...] = part.astype(out_dtype)
        else:

            @pl.when(kk == 0)
            def _():
                acc_ref[...] = part

            @pl.when(kk > 0)
            def _():
                acc_ref[...] += part

            @pl.when(kk == nk - 1)
            def _():
                o_ref[...] = acc_ref[...].astype(out_dtype)

    res = _pallas(body, name, (n // tn, m // tm, nk), in_specs, [out_spec], [out_shape],
                  [pltpu.VMEM((tm, tn) if nk > 1 else (SUBLANES, LANES), F32)], args, ("parallel", "parallel", "arbitrary"),
                  aliases=aliases, carry=carry)
    return res[0] if carry is None else (res[0], res[1:])


def matmul_gate_res(a, w, res, gate, n_ctx, name):
    w = w if isinstance(w, Mat) else Mat(w)
    (m, k), n = a.shape, w.s
    assert k == w.r
    tm, tn, tk = _fit_tiles(m, n, k, m, w.s_unit(), k, 3 * 4)
    nk = k // tk

    def body(a_ref, b_ref, r_ref, g_ref, x_ref, y_ref, acc_ref):
        kk = pl.program_id(2)
        row0 = pl.program_id(1) * tm
        part = jnp.dot(a_ref[...], b_ref[...], preferred_element_type=F32)

        @pl.when(kk == 0)
        def _():
            acc_ref[...] = part

        @pl.when(kk > 0)
        def _():
            acc_ref[...] += part

        @pl.when(kk == nk - 1)
        def _():
            y = acc_ref[...]
            row = row0 + lax.broadcasted_iota(jnp.int32, (tm, 1), 0)
            g = jnp.where(row < n_ctx, g_ref[0:1, :], g_ref[1:2, :])
            y_ref[...] = y.astype(y_ref.dtype)
            x_ref[...] = r_ref[...] + g * y

    return pl.pallas_call(
        body, name=name, grid=(n // tn, m // tm, nk),
        in_specs=[pl.BlockSpec((tm, tk), lambda j, i, kk: (i, kk)), w.spec(tk, tn, lambda j, i, kk: kk, lambda j, i, kk: j),
                  pl.BlockSpec((tm, tn), lambda j, i, kk: (i, j)), pl.BlockSpec((2, tn), lambda j, i, kk: (0, j))],
        out_specs=[pl.BlockSpec((tm, tn), lambda j, i, kk: (i, j)), pl.BlockSpec((tm, tn), lambda j, i, kk: (i, j))],
        out_shape=[jax.ShapeDtypeStruct((m, n), F32), jax.ShapeDtypeStruct((m, n), F32)],
        scratch_shapes=[pltpu.VMEM((tm, tn), F32)],
        compiler_params=_cp(dimension_semantics=("parallel", "parallel", "arbitrary")),
    )(a, w.arr, res, gate)


ROW_TILE = 256


def _seg_row(ref2, i, n_ctx_tiles):
    return jnp.where(i < n_ctx_tiles, ref2[0:1, :], ref2[1:2, :])


def _acc_rows(ref, step, rows):
    @pl.when(step == 0)
    def _():
        ref[...] = jnp.zeros_like(ref)

    for r, v in enumerate(rows):
        ref[r:r + 1, :] += v


def norm_mod(x, w, shift, scale, n_ctx, name):
    t, d = x.shape
    tm = _pick(n_ctx, (ROW_TILE, 128))
    nct = n_ctx // tm

    def body(x_ref, w_ref, sh_ref, sc_ref, h_ref):
        i = pl.program_id(0)
        xv = x_ref[...]
        r = lax.rsqrt(jnp.mean(xv * xv, axis=-1, keepdims=True) + NORM_EPS)
        h_ref[...] = ((xv * r) * w_ref[...] * (1.0 + _seg_row(sc_ref, i, nct)) + _seg_row(sh_ref, i, nct)).astype(BF16)

    return pl.pallas_call(
        body, name=name, grid=(t // tm,),
        in_specs=[pl.BlockSpec((tm, d), lambda i: (i, 0)), pl.BlockSpec((1, d), lambda i: (0, 0)),
                  pl.BlockSpec((2, d), lambda i: (0, 0)), pl.BlockSpec((2, d), lambda i: (0, 0))],
        out_specs=pl.BlockSpec((tm, d), lambda i: (i, 0)),
        out_shape=jax.ShapeDtypeStruct((t, d), BF16), compiler_params=_cp(),
    )(x, w, shift, scale)


def norm_mod_bwd(dh, dres, x, w, scale, n_ctx, name):
    t, d = x.shape
    tm = _pick(n_ctx, (ROW_TILE, 128))
    nct = n_ctx // tm

    def body(dh_ref, dr_ref, x_ref, w_ref, sc_ref, dx_ref, st_ref):
        i = pl.program_id(0)
        xv, g = x_ref[...], dh_ref[...]
        r = lax.rsqrt(jnp.mean(xv * xv, axis=-1, keepdims=True) + NORM_EPS)
        xn = xv * r
        one_sc = 1.0 + _seg_row(sc_ref, i, nct)
        dxn = g * (w_ref[...] * one_sc)
        dx_ref[...] = dr_ref[...] + r * (dxn - xn * jnp.mean(dxn * xn, axis=-1, keepdims=True))
        gx = g * xn
        s_shift = jnp.sum(g, axis=0, keepdims=True)
        s_scale = jnp.sum(gx * w_ref[...], axis=0, keepdims=True)
        s_w = jnp.sum(gx * one_sc, axis=0, keepdims=True)
        _acc_rows(st_ref, i, [jnp.where(i < nct, s_shift, 0.0), jnp.where(i < nct, 0.0, s_shift),
                              jnp.where(i < nct, s_scale, 0.0), jnp.where(i < nct, 0.0, s_scale), s_w])

    return pl.pallas_call(
        body, name=name, grid=(t // tm,),
        in_specs=[pl.BlockSpec((tm, d), lambda i: (i, 0)), pl.BlockSpec((tm, d), lambda i: (i, 0)),
                  pl.BlockSpec((tm, d), lambda i: (i, 0)), pl.BlockSpec((1, d), lambda i: (0, 0)),
                  pl.BlockSpec((2, d), lambda i: (0, 0))],
        out_specs=[pl.BlockSpec((tm, d), lambda i: (i, 0)), pl.BlockSpec((SUBLANES, d), lambda i: (0, 0))],
        out_shape=[jax.ShapeDtypeStruct((t, d), F32), jax.ShapeDtypeStruct((SUBLANES, d), F32)],
        compiler_params=_cp(dimension_semantics=("arbitrary",)),
    )(dh, dres, x, w, scale)


def gate_bwd(dx, y, gate, n_ctx, name):
    t, d = dx.shape
    tm = _pick(n_ctx, (ROW_TILE, 128))
    nct = n_ctx // tm

    def body(dx_ref, y_ref, g_ref, dy_ref, dg_ref):
        i = pl.program_id(0)
        dxv = dx_ref[...]
        dy_ref[...] = (dxv * _seg_row(g_ref, i, nct)).astype(BF16)
        s = jnp.sum(dxv * y_ref[...], axis=0, keepdims=True)
        _acc_rows(dg_ref, i, [jnp.where(i < nct, s, 0.0), jnp.where(i < nct, 0.0, s)])

    return pl.pallas_call(
        body, name=name, grid=(t // tm,),
        in_specs=[pl.BlockSpec((tm, d), lambda i: (i, 0)), pl.BlockSpec((tm, d), lambda i: (i, 0)),
                  pl.BlockSpec((2, d), lambda i: (0, 0))],
        out_specs=[pl.BlockSpec((tm, d), lambda i: (i, 0)), pl.BlockSpec((SUBLANES, d), lambda i: (0, 0))],
        out_shape=[jax.ShapeDtypeStruct((t, d), BF16), jax.ShapeDtypeStruct((SUBLANES, d), F32)],
        compiler_params=_cp(dimension_semantics=("arbitrary",)),
    )(dx, y, gate)


def _seg_edges(i, tm, n_ctx, t):
    first = (i == 0) | (i == n_ctx // tm)
    last = (i == n_ctx // tm - 1) | (i == t // tm - 1)
    return first, last


def _shift_rows(x, prev_row, next_row, first, last):
    tm = x.shape[0]
    row = lax.broadcasted_iota(jnp.int32, (tm, 1), 0)
    xp = jnp.where(row == 0, jnp.where(first, 0.0, prev_row), pltpu.roll(x, 1, 0))
    xn = jnp.where(row == tm - 1, jnp.where(last, 0.0, next_row), pltpu.roll(x, tm - 1, 0))
    return xp, xn


def _halo_rows(dtype):
    return SUBLANES * 4 // jnp.dtype(dtype).itemsize


def _halo_specs(tm, tc, t, col, hb):
    r, nblk = tm // hb, t // hb
    return [pl.BlockSpec((tm, tc), lambda j, i: (i, col(j))),
            pl.BlockSpec((hb, tc), lambda j, i: (jnp.maximum(i * r - 1, 0), col(j))),
            pl.BlockSpec((hb, tc), lambda j, i: (jnp.minimum((i + 1) * r, nblk - 1), col(j)))]


def _conv_rows(w_ref, xm, prev, nxt, first, last):
    hb, tm = prev.shape[0], xm.shape[0]
    w0, w1, w2, b = w_ref[0:1, :], w_ref[1:2, :], w_ref[2:3, :], w_ref[3:4, :]
    xp, xn = _shift_rows(xm, prev[hb - 1:hb, :], nxt[0:1, :], first, last)
    pre = w0 * xp + w1 * xm + w2 * xn + b
    pre_before = w0 * prev[hb - 2:hb - 1, :] + w1 * prev[hb - 1:hb, :] + w2 * xm[0:1, :] + b
    pre_after = w0 * xm[tm - 1:tm, :] + w1 * nxt[0:1, :] + w2 * nxt[1:2, :] + b
    return pre, pre_before, pre_after


def dwconv_act(x, col0, c, wb, n_ctx, mode, act_dtype, name):
    t = x.shape[0]
    tm = _pick(n_ctx, (ROW_TILE, 128))
    nparts = 2 if mode == "glu" else 1
    cw = c // nparts
    tc = _pick(math.gcd(cw, col0), (512, 384, 256, 128))
    hb = _halo_rows(x.dtype)

    def body(*refs):
        first, last = _seg_edges(pl.program_id(1), tm, n_ctx, t)
        pres = [_conv_rows(refs[4 * p + 3], *[r[...].astype(F32) for r in refs[4 * p:4 * p + 3]], first, last)[0] for p in range(nparts)]
        refs[-1][...] = (_silu(pres[0]) if mode == "silu" else _silu(pres[1]) * pres[0]).astype(act_dtype)

    in_specs, args = [], []
    for p in range(nparts):
        in_specs += _halo_specs(tm, tc, t, lambda j, p=p: (col0 + p * cw) // tc + j, hb)
        in_specs.append(pl.BlockSpec((SUBLANES, tc), lambda j, i, p=p: (0, p * cw // tc + j)))
        args += [x, x, x, wb]
    return pl.pallas_call(
        body, name=name, grid=(cw // tc, t // tm), in_specs=in_specs, out_specs=pl.BlockSpec((tm, tc), lambda j, i: (i, j)),
        out_shape=jax.ShapeDtypeStruct((t, cw), act_dtype), compiler_params=_cp(dimension_semantics=("parallel", "parallel")),
    )(*args)


def dwconv_act_bwd(dact, dcol0, x, xcol0, wb, wcol0, cw, n_ctx, mode, name, into=None, ocol0=0):
    t = x.shape[0]
    tm = _pick(n_ctx, (ROW_TILE, 128))
    nparts = 2 if mode == "glu" else 1
    tc = _pick(math.gcd(cw, dcol0, xcol0, wcol0, ocol0), (512, 384, 256, 128))
    hb_d, hb_x = _halo_rows(dact.dtype), _halo_rows(x.dtype)

    def dpre_of(dact_v, pres_v):
        if mode == "silu":
            return [dact_v * _dsilu(pres_v[0])]
        val, gat = pres_v
        return [dact_v * _silu(gat), dact_v * val * _dsilu(gat)]

    def body(*refs):
        i = pl.program_id(1)
        first, last = _seg_edges(i, tm, n_ctx, t)
        da = [r[...].astype(F32) for r in refs[0:3]]
        xs = [[r[...].astype(F32) for r in refs[3 + 3 * p:6 + 3 * p]] for p in range(nparts)]
        ws = refs[3 + 3 * nparts:3 + 4 * nparts]
        dx_ref, dw_ref = refs[-2:]
        pres = [_conv_rows(ws[p], *xs[p], first, last) for p in range(nparts)]
        dm = dpre_of(da[0], [pr[0] for pr in pres])
        d_before = dpre_of(da[1][hb_d - 1:hb_d, :], [pr[1] for pr in pres])
        d_after = dpre_of(da[2][0:1, :], [pr[2] for pr in pres])
        for p in range(nparts):
            d_prev, d_next = _shift_rows(dm[p], d_before[p], d_after[p], first, last)
            w_ref, xv = ws[p], xs[p][0]
            dxv = (w_ref[0:1, :] * d_next + w_ref[1:2, :] * dm[p] + w_ref[2:3, :] * d_prev).astype(BF16)
            if mode == "glu":
                dx_ref[p] = dxv
            else:
                dx_ref[...] = dxv
            _acc_rows(dw_ref.at[p] if mode == "glu" else dw_ref, i,
                      [jnp.sum(d_next * xv, axis=0, keepdims=True), jnp.sum(dm[p] * xv, axis=0, keepdims=True),
                       jnp.sum(d_prev * xv, axis=0, keepdims=True), jnp.sum(dm[p], axis=0, keepdims=True)])

    in_specs = _halo_specs(tm, tc, t, lambda j: dcol0 // tc + j, hb_d)
    args = [dact] * 3
    for p in range(nparts):
        in_specs += _halo_specs(tm, tc, t, lambda j, p=p: (xcol0 + p * cw) // tc + j, hb_x)
        args += [x] * 3
    for p in range(nparts):
        in_specs.append(pl.BlockSpec((SUBLANES, tc), lambda j, i, p=p: (0, (wcol0 + p * cw) // tc + j)))
        args.append(wb)
    aliases = {}
    if mode == "glu":
        out_specs = [pl.BlockSpec((2, tm, tc), lambda j, i: (0, i, j)), pl.BlockSpec((2, SUBLANES, tc), lambda j, i: (0, 0, j))]
        out_shape = [jax.ShapeDtypeStruct((2, t, cw), BF16), jax.ShapeDtypeStruct((2, SUBLANES, cw), F32)]
    else:
        out_specs = [pl.BlockSpec((tm, tc), lambda j, i: (i, ocol0 // tc + j)), pl.BlockSpec((SUBLANES, tc), lambda j, i: (0, j))]
        out_shape = [jax.ShapeDtypeStruct((t, cw) if into is None else into.shape, BF16), jax.ShapeDtypeStruct((SUBLANES, cw), F32)]
        if into is not None:
            aliases = {len(args): 0}
            in_specs.append(pl.BlockSpec(memory_space=pl.ANY))
            args.append(into)
    return pl.pallas_call(
        body, name=name, grid=(cw // tc, t // tm), in_specs=in_specs, out_specs=out_specs, out_shape=out_shape,
        input_output_aliases=aliases, compiler_params=_cp(dimension_semantics=("parallel", "arbitrary")),
    )(*args)


HI = lax.Precision.HIGHEST
_NT = (((1,), (1,)), ((), ()))
_TN = (((0,), (0,)), ((), ()))


def _dot(a, b, dims=None, precision=None):
    if dims is None:
        return jnp.dot(a, b, preferred_element_type=F32, precision=precision)
    return lax.dot_general(a, b, dims, preferred_element_type=F32, precision=precision)


def _softplus(x):
    y = jnp.exp(-jnp.abs(x))
    u = 1.0 + y
    log1p = jnp.where(u == 1.0, y, y * jnp.log(u) / jnp.where(u == 1.0, 1.0, u - 1.0))
    return jnp.maximum(x, 0.0) + log1p


def _tri(n, upper):
    r = lax.broadcasted_iota(jnp.int32, (n, n), 0)
    c = lax.broadcasted_iota(jnp.int32, (n, n), 1)
    return ((r <= c) if upper else (r >= c)).astype(F32)


def _group_select(heads, e):
    g = jnp.arange(SSD_GROUPS)[:, None, None]
    src = jnp.arange(LANES)[None, :, None]
    dst = jnp.arange(LANES)[None, None, :]
    d, k = dst // e, dst % e
    return ((dst < 2 * e) & (src == d * heads + g * e + k)).astype(F32)


def ssd_prep(zx, col0, bias, alog, sel, heads, name):
    t = zx.shape[0]
    assert 2 * heads == LANES and col0 % LANES == 0
    nc = t // CHUNK

    def body(zx_ref, b_ref, al_ref, sel_ref, dt_ref, da_ref, dtg_ref, ag_ref, agt_ref):
        dt = _softplus(zx_ref[...] + b_ref[...])
        da = -jnp.exp(al_ref[...]) * dt
        lane = lax.broadcasted_iota(jnp.int32, (CHUNK, LANES), 1)
        a = jnp.where(lane < heads, _dot(_tri(CHUNK, False), da, precision=HI), _dot(_tri(CHUNK, True), da, precision=HI))
        dt_ref[...] = dt
        da_ref[...] = da
        for g in range(SSD_GROUPS):
            s = sel_ref[g]
            dtg_ref[g] = _dot(dt, s, precision=HI)
            a_g = _dot(a, s, precision=HI)
            ag_ref[g] = a_g
            agt_ref[g, 0] = a_g.T

    return pl.pallas_call(
        body, name=name, grid=(nc,),
        in_specs=[pl.BlockSpec((CHUNK, LANES), lambda c: (c, col0 // LANES)), pl.BlockSpec((1, LANES), lambda c: (0, 0)),
                  pl.BlockSpec((1, LANES), lambda c: (0, 0)), pl.BlockSpec((SSD_GROUPS, LANES, LANES), lambda c: (0, 0, 0))],
        out_specs=[pl.BlockSpec((CHUNK, LANES), lambda c: (c, 0)), pl.BlockSpec((CHUNK, LANES), lambda c: (c, 0)),
                   pl.BlockSpec((SSD_GROUPS, CHUNK, LANES), lambda c: (0, c, 0)), pl.BlockSpec((SSD_GROUPS, CHUNK, LANES), lambda c: (0, c, 0)),
                   pl.BlockSpec((SSD_GROUPS, 1, LANES, CHUNK), lambda c: (0, c, 0, 0))],
        out_shape=[jax.ShapeDtypeStruct((t, LANES), F32), jax.ShapeDtypeStruct((t, LANES), F32),
                   jax.ShapeDtypeStruct((SSD_GROUPS, t, LANES), F32), jax.ShapeDtypeStruct((SSD_GROUPS, t, LANES), F32),
                   jax.ShapeDtypeStruct((SSD_GROUPS, nc, LANES, CHUNK), F32)],
        compiler_params=_cp(),
    )(zx, bias, alog, sel)


def ssd_prep_bwd(zx, col0, bias, alog, sel, grads_f, grads_b, da_comp, heads, into, name):
    t = zx.shape[0]
    nc = t // CHUNK

    def body(zx_ref, b_ref, al_ref, sel_ref, ddtg_ref, dag_ref, dagt_ref, ddtg2_ref, dag2_ref, dagt2_ref, da_ref, _, draw_ref, st_ref):
        c = pl.program_id(0)
        ddt = jnp.zeros((CHUNK, LANES), F32)
        dacc = jnp.zeros((CHUNK, LANES), F32)
        for g in range(SSD_GROUPS):
            s = sel_ref[g]
            ddt += _dot(ddtg_ref[g] + ddtg2_ref[g], s, _NT, precision=HI)
            dacc += _dot(dag_ref[g] + dag2_ref[g] + (dagt_ref[g, 0] + dagt2_ref[g, 0]).T, s, _NT, precision=HI)
        lane = lax.broadcasted_iota(jnp.int32, (CHUNK, LANES), 1)
        dda = jnp.where(lane < heads, _dot(_tri(CHUNK, True), dacc, precision=HI), _dot(_tri(CHUNK, False), dacc, precision=HI))
        xin = zx_ref[...] + b_ref[...]
        ddt_tot = ddt - dda * jnp.exp(al_ref[...])
        draw = ddt_tot * jax.nn.sigmoid(xin)
        draw_ref[...] = draw.astype(BF16)
        _acc_rows(st_ref, c, [jnp.sum(draw, axis=0, keepdims=True), jnp.sum(dda * da_ref[...], axis=0, keepdims=True)])

    g3 = pl.BlockSpec((SSD_GROUPS, CHUNK, LANES), lambda c: (0, c, 0))
    g4 = pl.BlockSpec((SSD_GROUPS, 1, LANES, CHUNK), lambda c: (0, c, 0, 0))
    return pl.pallas_call(
        body, name=name, grid=(nc,),
        in_specs=[pl.BlockSpec((CHUNK, LANES), lambda c: (c, col0 // LANES)), pl.BlockSpec((1, LANES), lambda c: (0, 0)),
                  pl.BlockSpec((1, LANES), lambda c: (0, 0)), pl.BlockSpec((SSD_GROUPS, LANES, LANES), lambda c: (0, 0, 0)),
                  g3, g3, g4, g3, g3, g4, pl.BlockSpec((CHUNK, LANES), lambda c: (c, 0)), pl.BlockSpec(memory_space=pl.ANY)],
        out_specs=[pl.BlockSpec((CHUNK, LANES), lambda c: (c, col0 // LANES)), pl.BlockSpec((SUBLANES, LANES), lambda c: (0, 0))],
        out_shape=[jax.ShapeDtypeStruct(into.shape, BF16), jax.ShapeDtypeStruct((SUBLANES, LANES), F32)],
        input_output_aliases={11: 0}, compiler_params=_cp(dimension_semantics=("arbitrary",)),
    )(zx, bias, alog, sel, *grads_f, *grads_b, da_comp, into)


def _chunk_row(k, nctx_c, nc, rev):
    if not rev:
        return k
    return jnp.where(k < nctx_c, nctx_c - 1 - k, nc + nctx_c - 1 - k)


def _pair_consts(dtg, ag, agt, l0, end):
    lane = lax.broadcasted_iota(jnp.int32, (CHUNK, LANES), 1)
    lo = lane < SSD_HEAD_DIM
    a0, a1 = ag[:, l0:l0 + 1], ag[:, l0 + 1:l0 + 2]
    dtp = jnp.where(lo, dtg[:, l0:l0 + 1], dtg[:, l0 + 1:l0 + 2])
    acol = jnp.where(lo, a0, a1)
    aend = acol[end:end + 1, :]
    return lo, a0, a1, agt[l0:l0 + 1, :], agt[l0 + 1:l0 + 2, :], dtp, acol, aend


def ssd_scan(xbc, dtg, ag, agt, d_inner, n_ctx, rev, name, carry=None):
    t = xbc.shape[0]
    e = d_inner // SSD_HEAD_DIM // SSD_GROUPS
    npair, gw = e // 2, e * SSD_HEAD_DIM
    assert e % 2 == 0 and gw % LANES == 0
    nc, nctx_c = t // CHUNK, n_ctx // CHUNK
    dirn = 1 if rev else 0
    end = 0 if rev else CHUNK - 1
    ridx = lambda k: _chunk_row(k, nctx_c, nc, rev)

    def body(x_ref, b_ref, c_ref, dtg_ref, ag_ref, agt_ref, y_ref, h_ref, state):
        k = pl.program_id(1)

        @pl.when(k == 0)
        def _():
            state[...] = jnp.zeros_like(state)

        bb, cbf = b_ref[...].astype(BF16), c_ref[...].astype(BF16)
        cb = _dot(cbf, bb, _NT)
        row = lax.broadcasted_iota(jnp.int32, (CHUNK, CHUNK), 0)
        col = lax.broadcasted_iota(jnp.int32, (CHUNK, CHUNK), 1)
        mask = (row <= col) if rev else (row >= col)
        rlo = lax.broadcasted_iota(jnp.int32, (LANES, 1), 0) < SSD_HEAD_DIM
        dtg_v, ag_v, agt_v = dtg_ref[0], ag_ref[0], agt_ref[0, 0]
        for p in range(npair):
            sl = slice(p * LANES, (p + 1) * LANES)
            lo, a0, a1, a0r, a1r, dtp, acol, aend = _pair_consts(dtg_v, ag_v, agt_v, dirn * e + 2 * p, end)
            xdt = x_ref[:, sl] * dtp
            xdtb = xdt.astype(BF16)
            w0 = (cb * jnp.exp(jnp.where(mask, a0 - a0r, -jnp.inf))).astype(BF16)
            w1 = (cb * jnp.exp(jnp.where(mask, a1 - a1r, -jnp.inf))).astype(BF16)
            yd = jnp.where(lo, _dot(w0, xdtb), _dot(w1, xdtb))
            xw = (xdt * jnp.exp(aend - acol)).astype(BF16)
            st = _dot(xw, bb, _TN)
            s_in = state[sl, :]
            h_ref[0, 0, sl, :] = s_in
            yo = _dot(cbf, s_in.astype(BF16), _NT)
            y_ref[:, sl] = yd + yo * jnp.exp(acol)
            cd = jnp.where(rlo, jnp.exp(aend[:, 0:1]), jnp.exp(aend[:, LANES - 1:LANES]))
            state[sl, :] = cd * s_in + st

    xcol = d_inner // LANES
    g3 = pl.BlockSpec((1, CHUNK, LANES), lambda g, k: (g, ridx(k), 0))
    return _pallas(
        body, name, (SSD_GROUPS, nc),
        [pl.BlockSpec((CHUNK, gw), lambda g, k: (ridx(k), g)),
         pl.BlockSpec((CHUNK, SSD_STATE), lambda g, k: (ridx(k), xcol + g)),
         pl.BlockSpec((CHUNK, SSD_STATE), lambda g, k: (ridx(k), xcol + SSD_GROUPS + g)),
         g3, g3, pl.BlockSpec((1, 1, LANES, CHUNK), lambda g, k: (g, ridx(k), 0, 0))],
        [pl.BlockSpec((CHUNK, gw), lambda g, k: (ridx(k), g)),
         pl.BlockSpec((1, 1, npair * LANES, SSD_STATE), lambda g, k: (ridx(k), g, 0, 0))],
        [jax.ShapeDtypeStruct((t, d_inner), F32), jax.ShapeDtypeStruct((nc, SSD_GROUPS, npair * LANES, SSD_STATE), F32)],
        [pltpu.VMEM((npair * LANES, SSD_STATE), F32)], (xbc, xbc, xbc, dtg, ag, agt), ("parallel", "arbitrary"), carry=carry)


def ssd_scan_bwd(dy, xbc, dtg, ag, agt, hst, dskip, prev, d_inner, n_ctx, rev, name):
    t = xbc.shape[0]
    e = d_inner // SSD_HEAD_DIM // SSD_GROUPS
    npair, gw = e // 2, e * SSD_HEAD_DIM
    nc, nctx_c = t // CHUNK, n_ctx // CHUNK
    dirn = 1 if rev else 0
    end = 0 if rev else CHUNK - 1
    ridx = lambda kk: _chunk_row(nc - 1 - kk, nctx_c, nc, rev)
    has_skip, has_prev = dskip is not None, prev is not None

    def body(*refs):
        dy_ref, x_ref, b_ref, c_ref, dtg_ref, ag_ref, agt_ref, h_ref = refs[:8]
        pos = 8
        if has_skip:
            ds_ref = refs[pos]
            pos += 1
        if has_prev:
            pdx_ref, pdb_ref, pdc_ref = refs[pos:pos + 3]
            pos += 3
        dx_ref, db_ref, dc_ref, ddtg_ref, dag_ref, dagt_ref, dd_ref, dstate = refs[pos:]
        kk = pl.program_id(1)

        @pl.when(kk == 0)
        def _():
            dstate[...] = jnp.zeros_like(dstate)

        bb, cbf = b_ref[...].astype(BF16), c_ref[...].astype(BF16)
        cb = _dot(cbf, bb, _NT)
        row = lax.broadcasted_iota(jnp.int32, (CHUNK, CHUNK), 0)
        col = lax.broadcasted_iota(jnp.int32, (CHUNK, CHUNK), 1)
        mask = (row <= col) if rev else (row >= col)
        rlo = lax.broadcasted_iota(jnp.int32, (LANES, 1), 0) < SSD_HEAD_DIM
        is_end = lax.broadcasted_iota(jnp.int32, (CHUNK, 1), 0) == end
        dtg_v, ag_v, agt_v = dtg_ref[0], ag_ref[0], agt_ref[0, 0]
        dcb = jnp.zeros((CHUNK, CHUNK), F32)
        d_c = jnp.zeros((CHUNK, SSD_STATE), F32)
        d_b = jnp.zeros((CHUNK, SSD_STATE), F32)
        ddt_out = jnp.zeros((CHUNK, LANES), F32)
        da_out = jnp.zeros((CHUNK, LANES), F32)
        dat_out = jnp.zeros((LANES, CHUNK), F32)
        for p in range(npair):
            sl = slice(p * LANES, (p + 1) * LANES)
            l0 = dirn * e + 2 * p
            lo, a0, a1, a0r, a1r, dtp, acol, aend = _pair_consts(dtg_v, ag_v, agt_v, l0, end)
            x = x_ref[:, sl]
            dyv = dy_ref[:, sl]
            xdt = x * dtp
            xdtb = xdt.astype(BF16)
            seg0 = jnp.exp(jnp.where(mask, a0 - a0r, -jnp.inf))
            seg1 = jnp.exp(jnp.where(mask, a1 - a1r, -jnp.inf))
            w0, w1 = cb * seg0, cb * seg1
            efs, dte = jnp.exp(acol), jnp.exp(aend - acol)
            xw = (xdt * dte).astype(BF16)
            s_in = h_ref[0, 0, sl, :]
            sb = s_in.astype(BF16)
            dyo = (dyv * efs).astype(BF16)
            da_exp = dyv * _dot(cbf, sb, _NT) * efs
            d_c += _dot(dyo, sb)
            ds_y = _dot(dyo, cbf, _TN)
            dyb = dyv.astype(BF16)
            dy0 = jnp.where(lo, dyv, 0.0).astype(BF16)
            dy1 = jnp.where(lo, 0.0, dyv).astype(BF16)
            dw0, dw1 = _dot(dy0, xdtb, _NT), _dot(dy1, xdtb, _NT)
            dxdt = jnp.where(lo, _dot(w0.astype(BF16), dyb, _TN), _dot(w1.astype(BF16), dyb, _TN))
            dcb += dw0 * seg0 + dw1 * seg1
            t0, t1 = dw0 * w0, dw1 * w1
            d_s = dstate[sl, :]
            dsb = d_s.astype(BF16)
            dxw = _dot(bb, dsb, _NT)
            d_b += _dot(xw, dsb)
            dxdt += dxw * dte
            tmp = dxw * xdt * dte
            da_exp -= tmp
            end_row = jnp.sum(tmp, axis=0, keepdims=True)
            prod = d_s * s_in
            e0, e1 = jnp.exp(aend[:, 0:1]), jnp.exp(aend[:, LANES - 1:LANES])
            sc0 = jnp.sum(jnp.where(rlo, prod, 0.0), keepdims=True) * e0
            sc1 = jnp.sum(jnp.where(rlo, 0.0, prod), keepdims=True) * e1
            dstate[sl, :] = jnp.where(rlo, e0, e1) * d_s + ds_y
            dxv = dxdt * dtp
            if has_skip:
                dxv += dyv * ds_ref[:, sl]
                _acc_rows(dd_ref.at[:, sl], kk, [jnp.sum(dyv * x, axis=0, keepdims=True)])
            if has_prev:
                dxv += pdx_ref[:, sl]
            dx_ref[:, sl] = dxv
            ddt_exp = dxdt * x
            lane = lax.broadcasted_iota(jnp.int32, (CHUNK, LANES), 1)
            sub = lax.broadcasted_iota(jnp.int32, (LANES, CHUNK), 0)
            for j, (sel, tj, scj) in enumerate(((lo, t0, sc0), (~lo, t1, sc1))):
                ddt_col = jnp.sum(jnp.where(sel, ddt_exp, 0.0), axis=1, keepdims=True)
                da_col = jnp.sum(jnp.where(sel, da_exp, 0.0), axis=1, keepdims=True) + jnp.sum(tj, axis=1, keepdims=True)
                da_end = jnp.sum(jnp.where(sel[0:1, :], end_row, 0.0), axis=1, keepdims=True) + scj
                da_col = da_col + jnp.where(is_end, da_end, 0.0)
                ddt_out += jnp.where(lane == l0 + j, ddt_col, 0.0)
                da_out += jnp.where(lane == l0 + j, da_col, 0.0)
                dat_out -= jnp.where(sub == l0 + j, jnp.sum(tj, axis=0, keepdims=True), 0.0)
        dcbb = dcb.astype(BF16)
        d_c += _dot(dcbb, bb)
        d_b += _dot(dcbb, cbf, _TN)
        if has_prev:
            d_b += pdb_ref[...]
            d_c += pdc_ref[...]
        db_ref[...] = d_b
        dc_ref[...] = d_c
        ddtg_ref[0] = ddt_out
        dag_ref[0] = da_out
        dagt_ref[0, 0] = dat_out
        if not has_skip:
            dd_ref[...] = jnp.zeros_like(dd_ref)

    xcol = d_inner // LANES
    xs_spec = pl.BlockSpec((CHUNK, gw), lambda g, kk: (ridx(kk), g))
    bc_spec = pl.BlockSpec((CHUNK, SSD_STATE), lambda g, kk: (ridx(kk), g))
    g3 = pl.BlockSpec((1, CHUNK, LANES), lambda g, kk: (g, ridx(kk), 0))
    g4 = pl.BlockSpec((1, 1, LANES, CHUNK), lambda g, kk: (g, ridx(kk), 0, 0))
    in_specs = [xs_spec, xs_spec,
                pl.BlockSpec((CHUNK, SSD_STATE), lambda g, kk: (ridx(kk), xcol + g)),
                pl.BlockSpec((CHUNK, SSD_STATE), lambda g, kk: (ridx(kk), xcol + SSD_GROUPS + g)),
                g3, g3, g4, pl.BlockSpec((1, 1, npair * LANES, SSD_STATE), lambda g, kk: (ridx(kk), g, 0, 0))]
    args = [dy, xbc, xbc, xbc, dtg, ag, agt, hst]
    if has_skip:
        in_specs.append(pl.BlockSpec((1, gw), lambda g, kk: (0, g)))
        args.append(dskip)
    if has_prev:
        in_specs += [xs_spec, bc_spec, bc_spec]
        args += list(prev)
    gn = SSD_GROUPS * SSD_STATE
    return pl.pallas_call(
        body, name=name, grid=(SSD_GROUPS, nc), in_specs=in_specs,
        out_specs=[xs_spec, bc_spec, bc_spec, g3, g3, g4, pl.BlockSpec((SUBLANES, gw), lambda g, kk: (0, g))],
        out_shape=[jax.ShapeDtypeStruct((t, d_inner), F32), jax.ShapeDtypeStruct((t, gn), F32), jax.ShapeDtypeStruct((t, gn), F32),
                   jax.ShapeDtypeStruct((SSD_GROUPS, t, LANES), F32), jax.ShapeDtypeStruct((SSD_GROUPS, t, LANES), F32),
                   jax.ShapeDtypeStruct((SSD_GROUPS, nc, LANES, CHUNK), F32), jax.ShapeDtypeStruct((SUBLANES, d_inner), F32)],
        scratch_shapes=[pltpu.VMEM((npair * LANES, SSD_STATE), F32)],
        compiler_params=_cp(dimension_semantics=("parallel", "arbitrary")),
    )(*args)


def ssd_gate_norm(yf, yb, xbc, zx, dskip, w, name):
    t, di = yf.shape
    tm = CHUNK

    def body(yf_ref, yb_ref, x_ref, z_ref, d_ref, w_ref, y_ref, o_ref):
        y = yf_ref[...] + yb_ref[...] + x_ref[...] * d_ref[...]
        y_ref[...] = y
        gz = y * _silu(z_ref[...])
        o_ref[...] = (gz * lax.rsqrt(jnp.mean(gz * gz, axis=-1, keepdims=True) + NORM_EPS) * w_ref[...]).astype(BF16)

    blk = pl.BlockSpec((tm, di), lambda i: (i, 0))
    vec = pl.BlockSpec((1, di), lambda i: (0, 0))
    return pl.pallas_call(
        body, name=name, grid=(t // tm,), in_specs=[blk, blk, blk, blk, vec, vec], out_specs=[blk, blk],
        out_shape=[jax.ShapeDtypeStruct((t, di), F32), jax.ShapeDtypeStruct((t, di), BF16)], compiler_params=_cp(),
    )(yf, yb, xbc, zx, dskip, w)


def ssd_gate_norm_bwd(dout, y, zx, w, name):
    t, di = y.shape
    tm = CHUNK

    def body(do_ref, y_ref, z_ref, w_ref, dy_ref, dz_ref, st_ref):
        i = pl.program_id(0)
        z, yv, g = z_ref[...], y_ref[...], do_ref[...]
        sz = _silu(z)
        gz = yv * sz
        r = lax.rsqrt(jnp.mean(gz * gz, axis=-1, keepdims=True) + NORM_EPS)
        n = gz * r
        dn = g * w_ref[...]
        dgz = r * (dn - n * jnp.mean(dn * n, axis=-1, keepdims=True))
        dy_ref[...] = dgz * sz
        dz_ref[...] = (dgz * yv * _dsilu(z)).astype(BF16)
        _acc_rows(st_ref, i, [jnp.sum(g * n, axis=0, keepdims=True)])

    blk = pl.BlockSpec((tm, di), lambda i: (i, 0))
    return pl.pallas_call(
        body, name=name, grid=(t // tm,), in_specs=[blk, blk, blk, pl.BlockSpec((1, di), lambda i: (0, 0))],
        out_specs=[blk, blk, pl.BlockSpec((SUBLANES, di), lambda i: (0, 0))],
        out_shape=[jax.ShapeDtypeStruct((t, di), F32), jax.ShapeDtypeStruct(zx.shape, BF16), jax.ShapeDtypeStruct((SUBLANES, di), F32)],
        compiler_params=_cp(dimension_semantics=("arbitrary",)),
    )(dout, y, zx, w)


def rope_tables(n_ctx, seq):
    pos = jnp.arange(seq)
    half = ATTN_HEAD_DIM // 4
    inv = ROPE_THETA ** (-jnp.arange(0, 2 * half, 2, dtype=F32) / (2 * half))
    ar = (pos // GRID_W).astype(F32)[:, None] * inv[None, :]
    ac = (pos % GRID_W).astype(F32)[:, None] * inv[None, :]
    cos = jnp.concatenate([jnp.cos(ar), jnp.cos(ar), jnp.cos(ac), jnp.cos(ac)], axis=-1)
    sin = jnp.concatenate([-jnp.sin(ar), jnp.sin(ar), -jnp.sin(ac), jnp.sin(ac)], axis=-1)
    cos = jnp.concatenate([jnp.ones((n_ctx, ATTN_HEAD_DIM), F32), cos], axis=0)
    sin = jnp.concatenate([jnp.zeros((n_ctx, ATTN_HEAD_DIM), F32), sin], axis=0)
    return cos, sin


def _rot(x):
    lane = lax.broadcasted_iota(jnp.int32, x.shape, 1)
    q = ATTN_HEAD_DIM // 4
    return jnp.where((lane % (2 * q)) < q, pltpu.roll(x, ATTN_HEAD_DIM - q, 1), pltpu.roll(x, q, 1))


def qk_prep(qkv, gains, cos, sin, n_q, n_k, name):
    t, c = qkv.shape
    tm = ROW_TILE
    hd = ATTN_HEAD_DIM

    def body(x_ref, g_ref, cos_ref, sin_ref, o_ref):
        cs, sn = cos_ref[...], sin_ref[...]
        for h in range(c // hd):
            sl = slice(h * hd, (h + 1) * hd)
            x = x_ref[:, sl]
            if h < n_q + n_k:
                gain = g_ref[0:1, :] if h < n_q else g_ref[1:2, :]
                xn = x * lax.rsqrt(jnp.mean(x * x, axis=-1, keepdims=True) + NORM_EPS) * gain
                x = xn * cs + _rot(xn) * sn
            o_ref[:, sl] = x.astype(BF16)

    return pl.pallas_call(
        body, name=name, grid=(t // tm,),
        in_specs=[pl.BlockSpec((tm, c), lambda i: (i, 0)), pl.BlockSpec((SUBLANES, hd), lambda i: (0, 0)),
                  pl.BlockSpec((tm, hd), lambda i: (i, 0)), pl.BlockSpec((tm, hd), lambda i: (i, 0))],
        out_specs=pl.BlockSpec((tm, c), lambda i: (i, 0)), out_shape=jax.ShapeDtypeStruct((t, c), BF16), compiler_params=_cp(),
    )(qkv, gains, cos, sin)


def qk_prep_bwd(dq, dk, dv, qkv, gains, cos, sin, name):
    t, c = qkv.shape
    tm = ROW_TILE
    hd = ATTN_HEAD_DIM
    n_q, n_k = dq.shape[1] // hd, dk.shape[1] // hd

    def body(dq_ref, dk_ref, dv_ref, x_ref, g_ref, cos_ref, sin_ref, o_ref, st_ref):
        i = pl.program_id(0)
        cs, sn = cos_ref[...], sin_ref[...]
        dgq = jnp.zeros((1, hd), F32)
        dgk = jnp.zeros((1, hd), F32)
        for h in range(c // hd):
            sl = slice(h * hd, (h + 1) * hd)
            if h >= n_q + n_k:
                hv = h - n_q - n_k
                o_ref[:, sl] = dv_ref[:, hv * hd:(hv + 1) * hd].astype(BF16)
                continue
            is_q = h < n_q
            dy = dq_ref[:, sl] if is_q else dk_ref[:, (h - n_q) * hd:(h - n_q + 1) * hd]
            gain = g_ref[0:1, :] if is_q else g_ref[1:2, :]
            x = x_ref[:, sl]
            r = lax.rsqrt(jnp.mean(x * x, axis=-1, keepdims=True) + NORM_EPS)
            xh = x * r
            dxn = dy * cs + _rot(dy * sn)
            dg = jnp.sum(dxn * xh, axis=0, keepdims=True)
            if is_q:
                dgq += dg
            else:
                dgk += dg
            dxh = dxn * gain
            o_ref[:, sl] = (r * (dxh - xh * jnp.mean(dxh * xh, axis=-1, keepdims=True))).astype(BF16)
        _acc_rows(st_ref, i, [dgq, dgk])

    return pl.pallas_call(
        body, name=name, grid=(t // tm,),
        in_specs=[pl.BlockSpec((tm, n_q * hd), lambda i: (i, 0)), pl.BlockSpec((tm, n_k * hd), lambda i: (i, 0)),
                  pl.BlockSpec((tm, n_k * hd), lambda i: (i, 0)), pl.BlockSpec((tm, c), lambda i: (i, 0)),
                  pl.BlockSpec((SUBLANES, hd), lambda i: (0, 0)), pl.BlockSpec((tm, hd), lambda i: (i, 0)), pl.BlockSpec((tm, hd), lambda i: (i, 0))],
        out_specs=[pl.BlockSpec((tm, c), lambda i: (i, 0)), pl.BlockSpec((SUBLANES, hd), lambda i: (0, 0))],
        out_shape=[jax.ShapeDtypeStruct((t, c), BF16), jax.ShapeDtypeStruct((SUBLANES, hd), F32)],
        compiler_params=_cp(dimension_semantics=("arbitrary",)),
    )(dq, dk, dv, qkv, gains, cos, sin)


def _attn_specs(n_ctx, nb, grp, n_qh):
    hd, blk = ATTN_HEAD_DIM, CHUNK
    kc, vc = n_qh, n_qh + ATTN_KV_HEADS
    specs = [pl.BlockSpec((blk, grp * hd), lambda h, b: (b, h))]
    for c0 in (kc, vc):
        specs += [pl.BlockSpec((n_ctx, hd), lambda h, b, c0=c0: (0, c0 + h)),
                  pl.BlockSpec((blk, hd), lambda h, b, c0=c0: (jnp.maximum(b - 1, 0), c0 + h)),
                  pl.BlockSpec((blk, hd), lambda h, b, c0=c0: (b, c0 + h)),
                  pl.BlockSpec((blk, hd), lambda h, b, c0=c0: (jnp.minimum(b + 1, nb - 1), c0 + h))]
    return specs


def _attn_masks(b, nctx_b, nb):
    row = lax.broadcasted_iota(jnp.int32, (CHUNK, CHUNK), 0)
    col = lax.broadcasted_iota(jnp.int32, (CHUNK, CHUNK), 1)
    lat = b >= nctx_b
    return [(col >= row) & lat & (b - 1 >= nctx_b), jnp.broadcast_to(lat, (CHUNK, CHUNK)), (col <= row) & lat & (b + 1 <= nb - 1)]


def attention(qkvr, sinks, n_ctx, n_qh, name, carry=None):
    t = qkvr.shape[0]
    hd, blk = ATTN_HEAD_DIM, CHUNK
    grp = n_qh // ATTN_KV_HEADS
    nb, nctx_b = t // blk, n_ctx // blk
    scale = hd ** -0.5

    def body(q_ref, kc_ref, kp_ref, ko_ref, kn_ref, vc_ref, vp_ref, vo_ref, vn_ref, s_ref, o_ref, lse_ref):
        b = pl.program_id(1)
        masks = _attn_masks(b, nctx_b, nb)
        ks = [kc_ref[...], kp_ref[...], ko_ref[...], kn_ref[...]]
        vs = [vc_ref[...], vp_ref[...], vo_ref[...], vn_ref[...]]
        lane = lax.broadcasted_iota(jnp.int32, (blk, LANES), 1)
        lse_out = jnp.zeros((blk, LANES), F32)
        for g in range(grp):
            q = q_ref[:, g * hd:(g + 1) * hd]
            s = [_dot(q, k, _NT) * scale for k in ks]
            s = [s[0]] + [jnp.where(m, sx, -jnp.inf) for m, sx in zip(masks, s[1:])]
            sink = s_ref[0, 0:1, g:g + 1]
            m = sink
            for sx in s:
                m = jnp.maximum(m, jnp.max(sx, axis=-1, keepdims=True))
            p = [jnp.exp(sx - m) for sx in s]
            l = jnp.exp(sink - m)
            for px in p:
                l = l + jnp.sum(px, axis=-1, keepdims=True)
            inv = 1.0 / l
            o = jnp.zeros((blk, hd), F32)
            for px, v in zip(p, vs):
                o += _dot((px * inv).astype(BF16), v)
            o_ref[:, g * hd:(g + 1) * hd] = o.astype(BF16)
            lse_out = jnp.where(lane == g, m + jnp.log(l), lse_out)
        lse_ref[...] = lse_out

    return _pallas(
        body, name, (ATTN_KV_HEADS, nb),
        _attn_specs(n_ctx, nb, grp, n_qh) + [pl.BlockSpec((1, SUBLANES, LANES), lambda h, b: (h, 0, 0))],
        [pl.BlockSpec((blk, grp * hd), lambda h, b: (b, h)), pl.BlockSpec((blk, LANES), lambda h, b: (b, h))],
        [jax.ShapeDtypeStruct((t, n_qh * hd), BF16), jax.ShapeDtypeStruct((t, ATTN_KV_HEADS * LANES), F32)],
        [], (qkvr,) * 9 + (sinks,), ("parallel", "arbitrary"), carry=carry)


def attention_bwd(do, o, lse, qkvr, sinks, n_ctx, n_qh, name):
    t = qkvr.shape[0]
    hd, blk = ATTN_HEAD_DIM, CHUNK
    grp = n_qh // ATTN_KV_HEADS
    nb, nctx_b = t // blk, n_ctx // blk
    scale = hd ** -0.5
    kvw = ATTN_KV_HEADS * hd

    def body(do_ref, o_ref, lse_ref, q_ref, kc_ref, kp_ref, ko_ref, kn_ref, vc_ref, vp_ref, vo_ref, vn_ref, s_ref,
             dq_ref, dkc_ref, dvc_ref, dkp_ref, dvp_ref, dsk_ref):
        b = pl.program_id(1)
        masks = _attn_masks(b, nctx_b, nb)
        ks = [kc_ref[...], kp_ref[...], ko_ref[...], kn_ref[...]]
        vs = [vc_ref[...], vp_ref[...], vo_ref[...], vn_ref[...]]
        lane = lax.broadcasted_iota(jnp.int32, (1, LANES), 1)
        dks = [jnp.zeros(k.shape, F32) for k in ks]
        dvs = [jnp.zeros(v.shape, F32) for v in vs]
        dsk = jnp.zeros((1, LANES), F32)
        for g in range(grp):
            sl = slice(g * hd, (g + 1) * hd)
            q = q_ref[:, sl]
            dof = do_ref[:, sl]
            dob = dof.astype(BF16)
            lse = lse_ref[:, g:g + 1]
            delta = jnp.sum(dof * o_ref[:, sl].astype(F32), axis=-1, keepdims=True)
            s = [_dot(q, k, _NT) * scale for k in ks]
            s = [s[0]] + [jnp.where(m, sx, -jnp.inf) for m, sx in zip(masks, s[1:])]
            dq = jnp.zeros((blk, hd), F32)
            for x in range(4):
                p = jnp.exp(s[x] - lse)
                ds = (p * (_dot(dob, vs[x], _NT) - delta)).astype(BF16)
                dq += _dot(ds, ks[x])
                dks[x] += _dot(ds, q, _TN)
                dvs[x] += _dot(p.astype(BF16), dob, _TN)
            dq_ref[:, sl] = dq * scale
            p_sink = jnp.exp(s_ref[0, 0:1, g:g + 1] - lse)
            dsk = dsk + jnp.where(lane == g, -jnp.sum(p_sink * delta, axis=0, keepdims=True), 0.0)

        @pl.when(b == 0)
        def _():
            dkc_ref[...] = jnp.zeros_like(dkc_ref)
            dvc_ref[...] = jnp.zeros_like(dvc_ref)
            dsk_ref[...] = jnp.zeros_like(dsk_ref)

        dkc_ref[...] += dks[0] * scale
        dvc_ref[...] += dvs[0]
        dsk_ref[0, 0:1, :] += dsk
        for x in range(3):
            dkp_ref[0, x] = dks[x + 1] * scale
            dvp_ref[0, x] = dvs[x + 1]

    part = pl.BlockSpec((1, 3, blk, hd), lambda h, b: (b, 0, 0, h))
    ctxo = pl.BlockSpec((n_ctx, hd), lambda h, b: (0, h))
    return pl.pallas_call(
        body, name=name, grid=(ATTN_KV_HEADS, nb),
        in_specs=[pl.BlockSpec((blk, grp * hd), lambda h, b: (b, h)), pl.BlockSpec((blk, grp * hd), lambda h, b: (b, h)),
                  pl.BlockSpec((blk, LANES), lambda h, b: (b, h))] + _attn_specs(n_ctx, nb, grp, n_qh)
                 + [pl.BlockSpec((1, SUBLANES, LANES), lambda h, b: (h, 0, 0))],
        out_specs=[pl.BlockSpec((blk, grp * hd), lambda h, b: (b, h)), ctxo, ctxo, part, part,
                   pl.BlockSpec((1, SUBLANES, LANES), lambda h, b: (h, 0, 0))],
        out_shape=[jax.ShapeDtypeStruct((t, n_qh * hd), F32), jax.ShapeDtypeStruct((n_ctx, kvw), F32), jax.ShapeDtypeStruct((n_ctx, kvw), F32),
                   jax.ShapeDtypeStruct((nb, 3, blk, kvw), F32), jax.ShapeDtypeStruct((nb, 3, blk, kvw), F32),
                   jax.ShapeDtypeStruct((ATTN_KV_HEADS, SUBLANES, LANES), F32)],
        compiler_params=_cp(dimension_semantics=("parallel", "arbitrary")),
    )(do, o, lse, qkvr, qkvr, qkvr, qkvr, qkvr, qkvr, qkvr, qkvr, qkvr, sinks)


def band_reduce(ctx_part, band_part, n_ctx, name):
    nb, _, blk, w = band_part.shape
    nctx_b = n_ctx // blk

    def body(c_ref, p_ref, o_ref, n_ref, out_ref):
        b = pl.program_id(0)
        band = p_ref[0, 0] + o_ref[0, 0] + jnp.where(b + 1 <= nb - 1, n_ref[0, 0], 0.0)
        out_ref[...] = jnp.where(b < nctx_b, c_ref[...], band)

    return pl.pallas_call(
        body, name=name, grid=(nb,),
        in_specs=[pl.BlockSpec((blk, w), lambda b: (jnp.minimum(b, nctx_b - 1), 0)),
                  pl.BlockSpec((1, 1, blk, w), lambda b: (jnp.maximum(b - 1, 0), 2, 0, 0)),
                  pl.BlockSpec((1, 1, blk, w), lambda b: (b, 1, 0, 0)),
                  pl.BlockSpec((1, 1, blk, w), lambda b: (jnp.minimum(b + 1, nb - 1), 0, 0, 0))],
        out_specs=pl.BlockSpec((blk, w), lambda b: (b, 0)), out_shape=jax.ShapeDtypeStruct((nb * blk, w), F32),
        compiler_params=_cp(),
    )(ctx_part, band_part, band_part, band_part)


def loss_grad(xf, target, n_ctx, name):
    t, d = xf.shape
    tm = _pick(n_ctx, (ROW_TILE, 128))
    nct = n_ctx // tm

    def body(x_ref, t_ref, dy_ref, s_ref):
        i = pl.program_id(0)
        err = jnp.where(i < nct, 0.0, x_ref[...] - t_ref[...])
        dy_ref[...] = err * (1.0 / d)
        _acc_rows(s_ref, i, [jnp.sum(err * err, axis=0, keepdims=True)])

    return pl.pallas_call(
        body, name=name, grid=(t // tm,),
        in_specs=[pl.BlockSpec((tm, d), lambda i: (i, 0)), pl.BlockSpec((tm, d), lambda i: (jnp.maximum(i - nct, 0), 0))],
        out_specs=[pl.BlockSpec((tm, d), lambda i: (i, 0)), pl.BlockSpec((SUBLANES, d), lambda i: (0, 0))],
        out_shape=[jax.ShapeDtypeStruct((t, d), F32), jax.ShapeDtypeStruct((SUBLANES, d), F32)],
        compiler_params=_cp(dimension_semantics=("arbitrary",)),
    )(xf, target)


def adamw(w, g, m, v, name):
    r, c = w.shape
    tr = r
    while tr % 2 == 0 and tr * c * 4 > (1 << 20) and (tr // 2) % SUBLANES == 0:
        tr //= 2
    bc1, bc2 = 1.0 - ADAM_B1 ** ADAM_STEP, 1.0 - ADAM_B2 ** ADAM_STEP

    def body(w_ref, g_ref, m_ref, v_ref, d_ref, nm_ref, nv_ref):
        gv = g_ref[...]
        nm = ADAM_B1 * m_ref[...] + (1.0 - ADAM_B1) * gv
        nv = ADAM_B2 * v_ref[...] + (1.0 - ADAM_B2) * (gv * gv)
        nm_ref[...] = nm
        nv_ref[...] = nv
        d_ref[...] = -ADAM_LR * ((nm / bc1) / (jnp.sqrt(nv / bc2) + ADAM_EPS) + ADAM_WD * w_ref[...])

    blk = pl.BlockSpec((tr, c), lambda i: (i, 0))
    return pl.pallas_call(
        body, name=name, grid=(r // tr,), in_specs=[blk] * 4, out_specs=[blk] * 3,
        out_shape=[jax.ShapeDtypeStruct((r, c), F32)] * 3, compiler_params=_cp(dimension_semantics=("parallel",)),
    )(w, g, m, v)


ADA_ROWS = 16


def ada_fwd(cs, w, name):
    l, d, ns = w.shape
    tn = _pick(ns, (512, 256, 128))

    def body(c_ref, w_ref, o_ref):
        o_ref[0] = _dot(c_ref[...], w_ref[0].astype(BF16))

    return pl.pallas_call(
        body, name=name, grid=(l, ns // tn),
        in_specs=[pl.BlockSpec((ADA_ROWS, d), lambda i, j: (0, 0)), pl.BlockSpec((1, d, tn), lambda i, j: (i, 0, j))],
        out_specs=pl.BlockSpec((1, ADA_ROWS, tn), lambda i, j: (i, 0, j)),
        out_shape=jax.ShapeDtypeStruct((l, ADA_ROWS, ns), F32), compiler_params=_cp(),
    )(cs, w)


def ada_bwd(cs, gmod, w, name):
    l, d, ns = w.shape
    tn = _pick(ns, (512, 256, 128))

    def body(c_ref, g_ref, w_ref, dw_ref, dc_ref):
        first = (pl.program_id(0) == 0) & (pl.program_id(1) == 0)
        gb = g_ref[0].astype(BF16)
        dw_ref[0] = _dot(c_ref[...], gb, _TN)
        part = _dot(gb, w_ref[0].astype(BF16), _NT)

        @pl.when(first)
        def _():
            dc_ref[...] = part

        @pl.when(jnp.logical_not(first))
        def _():
            dc_ref[...] += part

    return pl.pallas_call(
        body, name=name, grid=(l, ns // tn),
        in_specs=[pl.BlockSpec((ADA_ROWS, d), lambda i, j: (0, 0)), pl.BlockSpec((1, ADA_ROWS, tn), lambda i, j: (i, 0, j)),
                  pl.BlockSpec((1, d, tn), lambda i, j: (i, 0, j))],
        out_specs=[pl.BlockSpec((1, d, tn), lambda i, j: (i, 0, j)), pl.BlockSpec((ADA_ROWS, d), lambda i, j: (0, 0))],
        out_shape=[jax.ShapeDtypeStruct((l, d, ns), F32), jax.ShapeDtypeStruct((ADA_ROWS, d), F32)],
        compiler_params=_cp(dimension_semantics=("arbitrary", "arbitrary")),
    )(cs, gmod, w)


def sum_leading(a, name):
    k, r, c = a.shape
    tr = _pick(r, (256, 128, 64, 32, 16, 8))

    def body(a_ref, o_ref):
        acc = a_ref[0]
        for q in range(1, k):
            acc = acc + a_ref[q]
        o_ref[...] = acc

    return pl.pallas_call(
        body, name=name, grid=(r // tr,), in_specs=[pl.BlockSpec((k, tr, c), lambda i: (0, i, 0))],
        out_specs=pl.BlockSpec((tr, c), lambda i: (i, 0)), out_shape=jax.ShapeDtypeStruct((r, c), F32), compiler_params=_cp(),
    )(a)


def _mesh_pos():
    return lax.axis_index("x"), lax.axis_index("y"), lax.axis_index("c")


def _other_chips(x, y):
    return [(1 - x, y), (x, 1 - y), (1 - x, 1 - y)]


def _rcopy(src, dst, send_sems, recv_sems, k, to):
    return pltpu.make_async_remote_copy(src_ref=src, dst_ref=dst, send_sem=send_sems.at[k], recv_sem=recv_sems.at[k],
                                        device_id=to, device_id_type=MESH)


def small_allgather(vs, name):
    nv = len(vs)

    def body(*refs):
        v_refs, out_refs = refs[:nv], refs[nv:2 * nv]
        send_sems, recv_sems, local_sems = refs[2 * nv:]
        x, y, c = _mesh_pos()
        sibling = (x, y, 1 - c)
        chips = _other_chips(x, y)

        def blk(q, px, py, pc):
            return out_refs[q].at[4 * px + 2 * py + pc]

        mine = [pltpu.make_async_copy(v_refs[q], blk(q, x, y, c), local_sems.at[q]) for q in range(nv)]
        first, passed = [], []
        for q in range(nv):
            mine[q].start()
            first.append(_rcopy(v_refs[q], blk(q, x, y, c), send_sems, recv_sems, 7 * q, sibling))
            first += [_rcopy(v_refs[q], blk(q, x, y, c), send_sems, recv_sems, 7 * q + 1 + j, (*chip, c)) for j, chip in enumerate(chips)]
        for cp in first:
            cp.start()
        for q in range(nv):
            for j, chip in enumerate(chips):
                _rcopy(blk(q, *chip, c), blk(q, *chip, c), send_sems, recv_sems, 7 * q + 1 + j, (x, y, c)).wait_recv()
                passed.append(_rcopy(blk(q, *chip, c), blk(q, *chip, c), send_sems, recv_sems, 7 * q + 4 + j, sibling))
                passed[-1].start()
        for q in range(nv):
            _rcopy(blk(q, x, y, 1 - c), blk(q, x, y, 1 - c), send_sems, recv_sems, 7 * q, (x, y, c)).wait_recv()
            for j, chip in enumerate(chips):
                _rcopy(blk(q, *chip, 1 - c), blk(q, *chip, 1 - c), send_sems, recv_sems, 7 * q + 4 + j, (x, y, c)).wait_recv()
        for cp in first + passed:
            cp.wait_send()
        for cp in mine:
            cp.wait()

    vm = pl.BlockSpec(memory_space=pltpu.VMEM)
    return pl.pallas_call(
        body, name=name, out_shape=[jax.ShapeDtypeStruct((N_DEV, *v.shape), v.dtype) for v in vs],
        in_specs=[vm] * nv, out_specs=[vm] * nv,
        scratch_shapes=[pltpu.SemaphoreType.DMA((7 * nv,)), pltpu.SemaphoreType.DMA((7 * nv,)), pltpu.SemaphoreType.DMA((nv,))],
        compiler_params=_cp(),
    )(*vs)


_HBM = pl.BlockSpec(memory_space=pltpu.HBM)


STREAM_TILE_BYTES = 2 * 1024 * 1024


def _stream_rows(rows, row_bytes):
    tr = 16
    while rows % (2 * tr) == 0 and 2 * tr * row_bytes <= STREAM_TILE_BYTES:
        tr *= 2
    assert rows % tr == 0
    return tr


def _scalars(*vals):
    return jnp.stack([jnp.asarray(v, jnp.int32) for v in vals])


def place_own(w, chip, name):
    l, k, ns = w.shape
    tk = _pick(k, (256, 128, 64))

    def body(s_ref, w_ref, o_ref):
        o_ref[...] = w_ref[...].astype(BF16)

    grid_spec = pltpu.PrefetchScalarGridSpec(
        num_scalar_prefetch=1, grid=(l, k // tk),
        in_specs=[pl.BlockSpec((None, tk, ns), lambda i, j, s: (i, j, 0))],
        out_specs=pl.BlockSpec((None, None, tk, ns), lambda i, j, s: (i, s[0], j, 0)))
    return pl.pallas_call(body, name=name, grid_spec=grid_spec, out_shape=jax.ShapeDtypeStruct((l, N_CHIP, k, ns), BF16),
                          compiler_params=_cp())(_scalars(chip), w)


def _half(ref, layer, px, py, pc):
    hk = ref.shape[2] // 2
    return ref.at[layer, 2 * px + py, pl.ds(pc * hk, hk)]


def _chip_sends(refs, items, send_sems, recv_sems):
    x, y, c = _mesh_pos()
    return [_rcopy(_half(refs[b], l, x, y, c), _half(refs[b], l, x, y, c), send_sems, recv_sems, 3 * q + j, (*chip, c))
            for q, (b, l) in enumerate(items) for j, chip in enumerate(_other_chips(x, y))]


def _chip_recv_waits(refs, items, send_sems, recv_sems):
    x, y, c = _mesh_pos()
    for q, (b, l) in enumerate(items):
        for j, chip in enumerate(_other_chips(x, y)):
            _rcopy(_half(refs[b], l, *chip, c), _half(refs[b], l, *chip, c), send_sems, recv_sems, 3 * q + j, (x, y, c)).wait_recv()


def _sibling_forward(refs, items, send_sems, recv_sems):
    x, y, c = _mesh_pos()
    passed = [_rcopy(_half(refs[b], l, *chip, c), _half(refs[b], l, *chip, c), send_sems, recv_sems, 3 * q + j, (x, y, 1 - c))
              for q, (b, l) in enumerate(items) for j, chip in enumerate(_other_chips(x, y))]
    for cp in passed:
        cp.start()
    for q, (b, l) in enumerate(items):
        for j, chip in enumerate(_other_chips(x, y)):
            _rcopy(_half(refs[b], l, *chip, 1 - c), _half(refs[b], l, *chip, 1 - c), send_sems, recv_sems, 3 * q + j, (x, y, c)).wait_recv()
    for cp in passed:
        cp.wait_send()


def _inplace_comm_call(body, bufs, n_sems, name):
    nb = len(bufs)
    return pl.pallas_call(
        body, name=name, out_shape=[jax.ShapeDtypeStruct(b.shape, b.dtype) for b in bufs],
        in_specs=[_HBM] * nb, out_specs=[_HBM] * nb, input_output_aliases={q: q for q in range(nb)},
        scratch_shapes=[pltpu.SemaphoreType.DMA((n_sems,))] * 4, compiler_params=_cp(),
    )(*bufs)


def weights_allgather(bufs, items, name):
    nb = len(bufs)

    def body(*refs):
        out_refs = refs[nb:2 * nb]
        s1, r1, s2, r2 = refs[2 * nb:]
        sends = _chip_sends(out_refs, items, s1, r1)
        for cp in sends:
            cp.start()
        _chip_recv_waits(out_refs, items, s1, r1)
        _sibling_forward(out_refs, items, s2, r2)
        for cp in sends:
            cp.wait_send()

    return _inplace_comm_call(body, bufs, 3 * len(items), name)


def weights_forward(bufs, items, name):
    nb = len(bufs)

    def body(*refs):
        s1, r1, _, _ = refs[2 * nb:]
        _sibling_forward(refs[nb:2 * nb], items, s1, r1)

    return _inplace_comm_call(body, bufs, 3 * len(items), name)


def grads_pair_exchange(gs, name):
    ng = len(gs)

    def body(*refs):
        g_refs, out_refs = refs[:ng], refs[ng:2 * ng]
        send_sems, recv_sems = refs[2 * ng:]
        x, y, c = _mesh_pos()
        cps = []
        for q in range(ng):
            h = g_refs[q].shape[1] // 2
            cps.append(_rcopy(g_refs[q].at[:, pl.ds((1 - c) * h, h)], out_refs[q], send_sems, recv_sems, q, (x, y, 1 - c)))
            cps[-1].start()
        for cp in cps:
            cp.wait()

    return pl.pallas_call(
        body, name=name, out_shape=[jax.ShapeDtypeStruct((g.shape[0], g.shape[1] // 2, g.shape[2]), g.dtype) for g in gs],
        in_specs=[_HBM] * ng, out_specs=[_HBM] * ng,
        scratch_shapes=[pltpu.SemaphoreType.DMA((ng,)), pltpu.SemaphoreType.DMA((ng,))], compiler_params=_cp(),
    )(*gs)


def pair_add(g, got, c, chip, name):
    n, r, c_ = g.shape
    h = r // 2
    tr = _stream_rows(h, n * c_ * 2)
    nblk = h // tr

    def body(s_ref, g_ref, o_ref, pair_ref, land_ref):
        pair_ref[...] = (g_ref[...].astype(F32) + o_ref[...].astype(F32)).astype(BF16)
        me = s_ref[1]
        land_ref[...] = (g_ref[me].astype(F32) + o_ref[me].astype(F32)).astype(BF16)

    grid_spec = pltpu.PrefetchScalarGridSpec(
        num_scalar_prefetch=1, grid=(nblk,),
        in_specs=[pl.BlockSpec((n, tr, c_), lambda i, s: (0, s[0] * nblk + i, 0)), pl.BlockSpec((n, tr, c_), lambda i, s: (0, i, 0))],
        out_specs=[pl.BlockSpec((n, tr, c_), lambda i, s: (0, i, 0)), pl.BlockSpec((None, tr, c_), lambda i, s: (s[1], i, 0))])
    return pl.pallas_call(body, name=name, grid_spec=grid_spec, out_shape=[jax.ShapeDtypeStruct((n, h, c_), BF16)] * 2,
                          compiler_params=_cp())(_scalars(c, chip), g, got)


def grads_chip_exchange(pairs, lands, name):
    ng = len(pairs)

    def body(*refs):
        p_refs, out_refs = refs[:ng], refs[2 * ng:3 * ng]
        send_sems, recv_sems = refs[3 * ng:]
        x, y, c = _mesh_pos()
        chips = _other_chips(x, y)
        me = 2 * x + y
        sends = []
        for q in range(ng):
            sends += [_rcopy(p_refs[q].at[2 * px + py], out_refs[q].at[me], send_sems, recv_sems, 3 * q + j, (px, py, c))
                      for j, (px, py) in enumerate(chips)]
        for cp in sends:
            cp.start()
        for q in range(ng):
            for j, (px, py) in enumerate(chips):
                _rcopy(p_refs[q].at[me], out_refs[q].at[2 * px + py], send_sems, recv_sems, 3 * q + j, (x, y, c)).wait_recv()
        for cp in sends:
            cp.wait_send()

    return pl.pallas_call(
        body, name=name, out_shape=[jax.ShapeDtypeStruct(a.shape, a.dtype) for a in lands],
        in_specs=[_HBM] * (2 * ng), out_specs=[_HBM] * ng, input_output_aliases={ng + q: q for q in range(ng)},
        scratch_shapes=[pltpu.SemaphoreType.DMA((3 * ng,)), pltpu.SemaphoreType.DMA((3 * ng,))], compiler_params=_cp(),
    )(*pairs, *lands)


def sum_chips(a, c, name):
    k, h, c_ = a.shape
    tr = _stream_rows(h, k * c_ * 2)
    nblk = h // tr

    def body(s_ref, a_ref, o_ref):
        acc = a_ref[0].astype(F32)
        for q in range(1, k):
            acc = acc + a_ref[q].astype(F32)
        o_ref[...] = acc

    grid_spec = pltpu.PrefetchScalarGridSpec(
        num_scalar_prefetch=1, grid=(nblk,), in_specs=[pl.BlockSpec((k, tr, c_), lambda i, s: (0, i, 0))],
        out_specs=pl.BlockSpec((tr, c_), lambda i, s: (s[0] * nblk + i, 0)))
    return pl.pallas_call(body, name=name, grid_spec=grid_spec, out_shape=jax.ShapeDtypeStruct((2 * h, c_), F32),
                          compiler_params=_cp())(_scalars(c), a)


def halves_exchange(outs, name):
    ng = len(outs)

    def body(*refs):
        out_refs = refs[ng:2 * ng]
        send_sems, recv_sems = refs[2 * ng:]
        x, y, c = _mesh_pos()
        cps = []
        for q in range(ng):
            h = out_refs[q].shape[0] // 2
            cps.append(_rcopy(out_refs[q].at[pl.ds(c * h, h)], out_refs[q].at[pl.ds(c * h, h)], send_sems, recv_sems, q, (x, y, 1 - c)))
            cps[-1].start()
        for q in range(ng):
            h = out_refs[q].shape[0] // 2
            _rcopy(out_refs[q].at[pl.ds((1 - c) * h, h)], out_refs[q].at[pl.ds((1 - c) * h, h)], send_sems, recv_sems, q, (x, y, c)).wait_recv()
        for cp in cps:
            cp.wait_send()

    return pl.pallas_call(
        body, name=name, out_shape=[jax.ShapeDtypeStruct(a.shape, a.dtype) for a in outs],
        in_specs=[_HBM] * ng, out_specs=[_HBM] * ng, input_output_aliases={q: q for q in range(ng)},
        scratch_shapes=[pltpu.SemaphoreType.DMA((ng,)), pltpu.SemaphoreType.DMA((ng,))], compiler_params=_cp(),
    )(*outs)


def _rows8(a):
    return jnp.pad(a, ((0, -a.shape[0] % SUBLANES), (0, 0)))


def _chips_cols(g, rows):
    return jnp.concatenate([g[2 * j, :rows] for j in range(N_CHIP)], axis=-1)


BIG = (("ssd_w_in", "col"), ("ssd_w_out", "row"), ("attn_w_qkv", "col"), ("attn_w_o", "row"), ("ffn_w_up", "col"), ("ffn_w_down", "row"))
WEIGHTS = ("c_ctx", "ada_w", "ada_b", "norm1_w", "norm2_w", "ssd_w_in", "ssd_conv_w", "ssd_conv_b", "ssd_dt_bias_f", "ssd_dt_bias_b",
           "ssd_a_log_f", "ssd_a_log_b", "ssd_d", "ssd_norm_w", "ssd_w_out", "attn_w_qkv", "attn_q_gain", "attn_k_gain", "attn_sinks",
           "attn_w_o", "ffn_w_up", "ffn_conv_w", "ffn_conv_b", "ffn_w_down")


def _taps_bias(w3, b):
    return jnp.concatenate([w3, b[None, :], jnp.zeros((SUBLANES - 4, w3.shape[1]), F32)], axis=0)


def kernel(x, c, ctx, c_ctx, ada_w, ada_b, norm1_w, norm2_w, ssd_w_in, ssd_conv_w, ssd_conv_b, ssd_dt_bias_f, ssd_dt_bias_b, ssd_a_log_f, ssd_a_log_b, ssd_d, ssd_norm_w, ssd_w_out, attn_w_qkv, attn_q_gain, attn_k_gain, attn_sinks, attn_w_o, ffn_w_up, ffn_conv_w, ffn_conv_b, ffn_w_down, loss_target, m_c_ctx, m_ada_w, m_ada_b, m_norm1_w, m_norm2_w, m_ssd_w_in, m_ssd_conv_w, m_ssd_conv_b, m_ssd_dt_bias_f, m_ssd_dt_bias_b, m_ssd_a_log_f, m_ssd_a_log_b, m_ssd_d, m_ssd_norm_w, m_ssd_w_out, m_attn_w_qkv, m_attn_q_gain, m_attn_k_gain, m_attn_sinks, m_attn_w_o, m_ffn_w_up, m_ffn_conv_w, m_ffn_conv_b, m_ffn_w_down, v_c_ctx, v_ada_w, v_ada_b, v_norm1_w, v_norm2_w, v_ssd_w_in, v_ssd_conv_w, v_ssd_conv_b, v_ssd_dt_bias_f, v_ssd_dt_bias_b, v_ssd_a_log_f, v_ssd_a_log_b, v_ssd_d, v_ssd_norm_w, v_ssd_w_out, v_attn_w_qkv, v_attn_q_gain, v_attn_k_gain, v_attn_sinks, v_attn_w_o, v_ffn_w_up, v_ffn_conv_w, v_ffn_conv_b, v_ffn_w_down):
    args = locals()
    w = {n: args[n] for n in WEIGHTS}
    mom = {n: args["m_" + n] for n in WEIGHTS}
    var = {n: args["v_" + n] for n in WEIGHTS}

    ix, iy, ic = _mesh_pos()
    chip = 2 * ix + iy
    dev = 2 * chip + ic
    depth, d = norm1_w.shape
    n_ctx, seq = ctx.shape[1], x.shape[1]
    t = n_ctx + seq
    d_inner = ssd_norm_w.shape[1]
    heads = ssd_d.shape[1]
    n_qh = attn_sinks.shape[1]
    grp = n_qh // ATTN_KV_HEADS
    d_ff = ffn_w_down.shape[1] * N_CHIP
    xbc_w = ssd_conv_b.shape[1]
    dt_col = d_inner + xbc_w
    n_ssd, n_att = ssd_w_in.shape[0], attn_w_qkv.shape[0]

    sconv_rows, fconv_rows = n_ssd * 3, depth * 3
    g_c, g_sconv, g_fconv = small_allgather(
        [_rows8(c), _rows8(ssd_conv_w.reshape(sconv_rows, -1)), _rows8(ffn_conv_w.reshape(fconv_rows, -1))], "gather_cond")
    c_all = g_c[:, 0]
    ssd_conv_full = _chips_cols(g_sconv, sconv_rows).reshape(n_ssd, 3, -1)
    ffn_conv_full = _chips_cols(g_fconv, fconv_rows).reshape(depth, 3, -1)

    cvec = jnp.concatenate([c_all, c_ctx[None, :], jnp.zeros((ADA_ROWS - N_DEV - 1, d), F32)], axis=0)
    cs16 = _silu(cvec).astype(BF16)
    mod_cols = ada_fwd(cs16, ada_w, "ada_fwd")
    ns_ada = mod_cols.shape[-1]
    (g_mod,) = small_allgather([mod_cols.reshape(depth * ADA_ROWS, ns_ada)], "gather_mod")
    mod_all = _chips_cols(g_mod, depth * ADA_ROWS).reshape(depth, ADA_ROWS, -1) + ada_b[:, None, :]
    mod_lat = lax.dynamic_index_in_dim(mod_all, dev, axis=1, keepdims=False)
    mod_ctx = mod_all[:, N_DEV]
    mods = jnp.stack([mod_ctx, mod_lat], axis=1).reshape(depth, 2, 6, d)

    bidx = {n: q for q, (n, _) in enumerate(BIG)}
    bufs = [place_own(w[n], chip, f"place_{n}") for n, _ in BIG]

    def layer_items(layer):
        mixer = ("ssd_w_in", "ssd_w_out") if layer % 2 == 0 else ("attn_w_qkv", "attn_w_o")
        return [(bidx[n], layer // 2) for n in mixer] + [(bidx["ffn_w_up"], layer), (bidx["ffn_w_down"], layer)]

    def ssd_in_full(layer):
        return bufs[bidx["ssd_w_in"]][layer].transpose(1, 0, 2).reshape(d, -1)

    def w_col(n, layer):
        b = bufs[bidx[n]]
        return Mat(b.reshape(-1, *b.shape[2:]), "cols3", base=layer * N_CHIP, nparts=N_CHIP)

    def w_row(n, layer):
        b = bufs[bidx[n]]
        rows = N_CHIP * b.shape[2]
        return Mat(b.reshape(-1, b.shape[3]), "rows", rows=rows, row0=layer * rows)

    def subset(items):
        used = sorted({b for b, _ in items})
        return [bufs[b] for b in used], [(used.index(b), l) for b, l in items], used

    def put_back(used, new):
        for b, a in zip(used, new):
            bufs[b] = a

    sub, its, used = subset(layer_items(0)[:1])
    put_back(used, weights_allgather(sub, its, "gather_weights_first"))

    cos, sin = rope_tables(n_ctx, seq)
    sel = _group_select(heads, heads // SSD_GROUPS)
    bias128 = jnp.concatenate([ssd_dt_bias_f, ssd_dt_bias_b], axis=-1)[:, None, :]
    alog128 = jnp.concatenate([ssd_a_log_f, ssd_a_log_b], axis=-1)[:, None, :]
    dskip = jnp.repeat(ssd_d, SSD_HEAD_DIM, axis=-1)[:, None, :]
    gains = jnp.zeros((n_att, SUBLANES, ATTN_HEAD_DIM), F32).at[:, 0].set(attn_q_gain).at[:, 1].set(attn_k_gain)
    sinks3 = jnp.zeros((n_att, ATTN_KV_HEADS, SUBLANES, LANES), F32).at[:, :, 0, :grp].set(attn_sinks.reshape(n_att, ATTN_KV_HEADS, grp))
    wb_ssd = [_taps_bias(ssd_conv_full[j], ssd_conv_b[j]) for j in range(n_ssd)]
    wb_ffn = [_taps_bias(ffn_conv_full[i], ffn_conv_b[i]) for i in range(depth)]

    xs = jnp.concatenate([ctx[0], x[0]], axis=0)
    saved = []
    for i in range(depth):
        j = i // 2
        sh1, sc1, g1, sh2, sc2, g2 = [mods[i, :, q] for q in range(6)]
        s = {"x": xs}
        h1 = norm_mod(xs, norm1_w[i:i + 1], sh1, sc1, n_ctx, f"l{i}_norm1")
        s["h1"] = h1
        if i % 2 == 0:
            w_in = ssd_in_full(j)
            if i == 0:
                sub, its, used = subset(layer_items(0)[1:])
                zx, carried = matmul(h1, w_in, "nn", f"l{i}_ssd_in", carry=(sub, its))
                put_back(used, weights_forward(carried, its, "forward_weights_l0"))
            else:
                zx = matmul(h1, w_in, "nn", f"l{i}_ssd_in")
        sub, its, used = subset(layer_items(i + 1)) if i + 1 < depth else (None, None, None)
        nxt = (sub, its) if sub is not None else None
        if i % 2 == 0:
            xbc = dwconv_act(zx, d_inner, xbc_w, wb_ssd[j], n_ctx, "silu", F32, f"l{i}_ssd_conv")
            _, da, dtg, ag, agt = ssd_prep(zx, dt_col, bias128[j], alog128[j], sel, heads, f"l{i}_ssd_prep")
            yf, hf, *carried = ssd_scan(xbc, dtg, ag, agt, d_inner, n_ctx, False, f"l{i}_ssd_scan_f", carry=nxt)
            if nxt is not None:
                put_back(used, weights_forward(carried, its, f"forward_weights_l{i + 1}"))
            yb, hb = ssd_scan(xbc, dtg, ag, agt, d_inner, n_ctx, True, f"l{i}_ssd_scan_b")
            ytot, yn = ssd_gate_norm(yf, yb, xbc, zx, dskip[j], ssd_norm_w[j:j + 1], f"l{i}_ssd_gate_norm")
            x1, mix = matmul_gate_res(yn, w_row("ssd_w_out", j), xs, g1, n_ctx, f"l{i}_ssd_out")
            s.update(w_in=w_in, zx=zx, xbc=xbc, da=da, dtg=dtg, ag=ag, agt=agt, hf=hf, hb=hb, ytot=ytot, yn=yn)
        else:
            qkv = matmul(h1, w_col("attn_w_qkv", j), "nn", f"l{i}_attn_qkv")
            qkvr = qk_prep(qkv, gains[j], cos, sin, n_qh, ATTN_KV_HEADS, f"l{i}_qk_prep")
            o, lse, *carried = attention(qkvr, sinks3[j], n_ctx, n_qh, f"l{i}_attn", carry=nxt)
            if nxt is not None:
                put_back(used, weights_forward(carried, its, f"forward_weights_l{i + 1}"))
            x1, mix = matmul_gate_res(o, w_row("attn_w_o", j), xs, g1, n_ctx, f"l{i}_attn_out")
            s.update(qkv=qkv, qkvr=qkvr, o=o, lse=lse)
        h2 = norm_mod(x1, norm2_w[i:i + 1], sh2, sc2, n_ctx, f"l{i}_norm2")
        u = matmul(h2, w_col("ffn_w_up", i), "nn", f"l{i}_ffn_up", out_dtype=BF16)
        act = dwconv_act(u, 0, 2 * d_ff, wb_ffn[i], n_ctx, "glu", BF16, f"l{i}_ffn_conv")
        x2, f = matmul_gate_res(act, w_row("ffn_w_down", i), x1, g2, n_ctx, f"l{i}_ffn_down")
        s.update(mix=mix, x1=x1, h2=h2, u=u, act=act, f=f)
        saved.append(s)
        xs = x2

    dxs, sq = loss_grad(xs, loss_target[0], n_ctx, "loss")
    loss = lax.psum(0.5 / d * jnp.sum(sq[0]), ("x", "y", "c"))

    gbuf = {n: None for n, _ in BIG}
    gshape = {n: ((N_CHIP, b.shape[0] * b.shape[2], b.shape[3]), b.shape[2]) for (n, _), b in zip(BIG, bufs)}

    def dw_into(n, kind, layer, a, b, name):
        shape, rows_per_layer = gshape[n]
        gbuf[n] = matmul(a, b, "tn", name, out_dtype=BF16, into=(kind, gbuf[n], shape, layer * rows_per_layer))

    ssd_in_g = [None] * n_ssd
    st_norm1, st_norm2, st_gate1, st_gate2 = ([None] * depth for _ in range(4))
    st_sconv, st_snorm, st_sd, st_sdt = ([None] * n_ssd for _ in range(4))
    st_gain, st_sink = [None] * n_att, [None] * n_att
    st_fconv = [None] * depth
    for i in reversed(range(depth)):
        j = i // 2
        s = saved[i]
        sh1, sc1, g1, sh2, sc2, g2 = [mods[i, :, q] for q in range(6)]
        df, st_gate2[i] = gate_bwd(dxs, s["f"], g2, n_ctx, f"l{i}_ffn_gate_bwd")
        dact = matmul(df, w_row("ffn_w_down", i), "nt", f"l{i}_ffn_down_dx")
        dw_into("ffn_w_down", "row", i, s["act"], df, f"l{i}_ffn_down_dw")
        du3, st_fconv[i] = dwconv_act_bwd(dact, 0, s["u"], 0, wb_ffn[i], 0, d_ff, n_ctx, "glu", f"l{i}_ffn_conv_bwd")
        du = Mat(du3, "cols3", nparts=2)
        dh2 = matmul(du, w_col("ffn_w_up", i), "nt", f"l{i}_ffn_up_dx")
        dw_into("ffn_w_up", "col", i, s["h2"], du, f"l{i}_ffn_up_dw")
        dx1, st_norm2[i] = norm_mod_bwd(dh2, dxs, s["x1"], norm2_w[i:i + 1], sc2, n_ctx, f"l{i}_norm2_bwd")
        dmix, st_gate1[i] = gate_bwd(dx1, s["mix"], g1, n_ctx, f"l{i}_mix_gate_bwd")
        if i % 2 == 0:
            zx = s["zx"]
            dyn = matmul(dmix, w_row("ssd_w_out", j), "nt", f"l{i}_ssd_out_dx")
            dw_into("ssd_w_out", "row", j, s["yn"], dmix, f"l{i}_ssd_out_dw")
            dy, dzx, st_snorm[j] = ssd_gate_norm_bwd(dyn, s["ytot"], zx, ssd_norm_w[j:j + 1], f"l{i}_ssd_gate_norm_bwd")
            o1 = ssd_scan_bwd(dy, s["xbc"], s["dtg"], s["ag"], s["agt"], s["hf"], dskip[j], None, d_inner, n_ctx, False, f"l{i}_ssd_scan_f_bwd")
            o2 = ssd_scan_bwd(dy, s["xbc"], s["dtg"], s["ag"], s["agt"], s["hb"], None, o1[:3], d_inner, n_ctx, True, f"l{i}_ssd_scan_b_bwd")
            st_sd[j] = o1[6]
            gn = SSD_GROUPS * SSD_STATE
            conv_st = []
            for src, width, col, tag in ((o2[0], d_inner, 0, "x"), (o2[1], gn, d_inner, "b"), (o2[2], gn, d_inner + gn, "c")):
                dzx, st = dwconv_act_bwd(src, 0, zx, d_inner + col, wb_ssd[j], col, width, n_ctx, "silu",
                                         f"l{i}_ssd_conv_bwd_{tag}", into=dzx, ocol0=d_inner + col)
                conv_st.append(st)
            st_sconv[j] = jnp.concatenate(conv_st, axis=1)
            dzx, st_sdt[j] = ssd_prep_bwd(zx, dt_col, bias128[j], alog128[j], sel, o1[3:6], o2[3:6], s["da"], heads, dzx, f"l{i}_ssd_prep_bwd")
            dh1 = matmul(dzx, s["w_in"], "nt", f"l{i}_ssd_in_dx")
            ssd_in_g[j] = matmul(s["h1"], dzx, "tn", f"l{i}_ssd_in_dw")
        else:
            do = matmul(dmix, w_row("attn_w_o", j), "nt", f"l{i}_attn_out_dx")
            dw_into("attn_w_o", "row", j, s["o"], dmix, f"l{i}_attn_out_dw")
            dq, dkc, dvc, dkp, dvp, st_sink[j] = attention_bwd(do, s["o"], s["lse"], s["qkvr"], sinks3[j], n_ctx, n_qh, f"l{i}_attn_bwd")
            dk = band_reduce(dkc, dkp, n_ctx, f"l{i}_dk_reduce")
            dv = band_reduce(dvc, dvp, n_ctx, f"l{i}_dv_reduce")
            dqkv, st_gain[j] = qk_prep_bwd(dq, dk, dv, s["qkv"], gains[j], cos, sin, f"l{i}_qk_prep_bwd")
            dh1 = matmul(dqkv, w_col("attn_w_qkv", j), "nt", f"l{i}_attn_qkv_dx")
            dw_into("attn_w_qkv", "col", j, s["h1"], dqkv, f"l{i}_attn_qkv_dw")
        dxs, st_norm1[i] = norm_mod_bwd(dh1, dx1, s["x"], norm1_w[i:i + 1], sc1, n_ctx, f"l{i}_norm1_bwd")
    grad_x = dxs[n_ctx:][None]

    rows_d = ([st_norm1[i][4:5] for i in range(depth)] + [st_norm2[i][4:5] for i in range(depth)]
              + [st[seg:seg + 1] for seg in (0, 1) for i in range(depth)
                 for st in (st_norm1[i][0:2], st_norm1[i][2:4], st_gate1[i][0:2], st_norm2[i][0:2], st_norm2[i][2:4], st_gate2[i][0:2])])
    a_d = jnp.concatenate(rows_d, axis=0)
    a_sconv = _rows8(jnp.concatenate([st[0:4] for st in st_sconv], axis=0))
    a_fconv = _rows8(jnp.concatenate([jnp.concatenate([st[0, 0:4], st[1, 0:4]], axis=1) for st in st_fconv], axis=0))
    a_di = _rows8(jnp.concatenate([st[0:1] for st in st_snorm] + [st[0:1] for st in st_sd], axis=0))
    a_128 = _rows8(jnp.concatenate([st[0:2] for st in st_sdt] + [st[0:2] for st in st_gain] + [st[:, 0] for st in st_sink], axis=0))
    gathered = small_allgather([a_d, a_sconv, a_fconv, a_di, a_128], "gather_small_grads")
    s_d, s_sconv, s_fconv, s_di, s_128 = [sum_leading(g, f"sum_small_grads_{q}") for q, g in enumerate(gathered)]
    grads = {"norm1_w": s_d[0:depth], "norm2_w": s_d[depth:2 * depth]}
    dctx_sum = s_d[2 * depth:8 * depth].reshape(depth, 6 * d)
    grads["ada_b"] = dctx_sum + s_d[8 * depth:14 * depth].reshape(depth, 6 * d)
    sc = s_sconv[:4 * n_ssd].reshape(n_ssd, 4, -1)
    own_cols = lambda a, width: lax.dynamic_slice_in_dim(a, chip * width, width, axis=a.ndim - 1)
    grads["ssd_conv_w"], grads["ssd_conv_b"] = own_cols(sc[:, 0:3], ssd_conv_w.shape[-1]), sc[:, 3]
    fc = s_fconv[:4 * depth].reshape(depth, 4, -1)
    grads["ffn_conv_w"], grads["ffn_conv_b"] = own_cols(fc[:, 0:3], ffn_conv_w.shape[-1]), fc[:, 3]
    grads["ssd_norm_w"] = s_di[0:n_ssd]
    grads["ssd_d"] = jnp.sum(s_di[n_ssd:2 * n_ssd].reshape(n_ssd, heads, SSD_HEAD_DIM), axis=-1)
    dt_st = s_128[0:2 * n_ssd].reshape(n_ssd, 2, LANES)
    grads["ssd_dt_bias_f"], grads["ssd_dt_bias_b"] = dt_st[:, 0, :heads], dt_st[:, 0, heads:]
    grads["ssd_a_log_f"], grads["ssd_a_log_b"] = dt_st[:, 1, :heads], dt_st[:, 1, heads:]
    gain_st = s_128[2 * n_ssd:2 * n_ssd + 2 * n_att].reshape(n_att, 2, LANES)
    grads["attn_q_gain"], grads["attn_k_gain"] = gain_st[:, 0], gain_st[:, 1]
    sink_st = s_128[2 * n_ssd + 2 * n_att:2 * n_ssd + 2 * n_att + ATTN_KV_HEADS * n_att].reshape(n_att, ATTN_KV_HEADS, LANES)
    grads["attn_sinks"] = sink_st[:, :, :grp].reshape(n_att, n_qh)

    dlat_rows = gathered[0][:, 8 * depth:14 * depth].reshape(N_DEV, depth, 6 * d)
    gmod = jnp.concatenate([dlat_rows, dctx_sum[None], jnp.zeros((ADA_ROWS - N_DEV - 1, depth, 6 * d), F32)], axis=0).transpose(1, 0, 2)
    grads["ada_w"], dcs = ada_bwd(cs16, own_cols(gmod, ns_ada), ada_w, "ada_bwd")
    dcc = (dcs[N_DEV] * _dsilu(c_ctx))[None, :]
    (g_dcc,) = small_allgather([_rows8(dcc)], "gather_dc_ctx")
    grads["c_ctx"] = sum_leading(g_dcc[0::2], "sum_dc_ctx")[0]

    rs_names = [n for n, _ in BIG if n != "ssd_w_in"]
    rs_in = [gbuf[n] for n in rs_names]
    rs_in += [g.reshape(d, N_CHIP, -1).transpose(1, 0, 2).astype(BF16) for g in ssd_in_g]
    got = grads_pair_exchange(rs_in, "rs_pair_exchange")
    pairs, lands = zip(*[pair_add(g, o, ic, chip, f"rs_pair_add_{q}") for q, (g, o) in enumerate(zip(rs_in, got))])
    landed = grads_chip_exchange(list(pairs), list(lands), "rs_chip_exchange")
    red = halves_exchange([sum_chips(a, ic, f"rs_chip_sum_{q}") for q, a in enumerate(landed)], "rs_halves_exchange")
    for n, r in zip(rs_names, red):
        grads[n] = r.reshape(w[n].shape)
    grads["ssd_w_in"] = jnp.stack(red[len(rs_names):])

    delta, new_m, new_v = {}, {}, {}
    for n in WEIGHTS:
        shp = w[n].shape
        two = lambda a: a.reshape(-1, shp[-1])
        dl, nm, nv = adamw(two(w[n]), two(grads[n]), two(mom[n]), two(var[n]), f"adamw_{n}")
        delta[n], new_m[n], new_v[n] = dl.reshape(shp), nm.reshape(shp), nv.reshape(shp)
    grads = {n: grads[n].reshape(w[n].shape) for n in WEIGHTS}
    return (loss, grad_x, *[grads[n] for n in WEIGHTS], *[delta[n] for n in WEIGHTS], *[new_m[n] for n in WEIGHTS], *[new_v[n] for n in WEIGHTS])
```

```python
import functools
import math

import jax
import jax.numpy as jnp
from jax import lax
from jax.experimental import pallas as pl
from jax.experimental.pallas import tpu as pltpu

F32 = jnp.float32
BF16 = jnp.bfloat16

SSD_HEAD_DIM = 64
SSD_GROUPS = 8
SSD_STATE = 128
CHUNK = 128
ATTN_HEAD_DIM = 128
ATTN_KV_HEADS = 4
GRID_W = 64
ROPE_THETA = 10000.0
NORM_EPS = 1e-6
ADAM_LR, ADAM_B1, ADAM_B2, ADAM_EPS, ADAM_WD, ADAM_STEP = 0.001, 0.9, 0.999, 1e-08, 0.01, 10

LANES = 128
SUBLANES = 8
VMEM_LIMIT = 56 * 1024 * 1024
MESH = pl.DeviceIdType.MESH
N_DEV = 8
N_CHIP = 4


def _cp(**kw):
    return pltpu.CompilerParams(vmem_limit_bytes=VMEM_LIMIT, **kw)


def _pallas(body, name, grid, in_specs, out_specs, out_shape, scratch, args, dims, aliases=None, carry=None):
    in_specs, out_specs, out_shape, scratch, args = list(in_specs), list(out_specs), list(out_shape), list(scratch), list(args)
    aliases = dict(aliases or {})
    if carry is not None:
        bufs, n_sems, sends, recv_waits = carry
        n_in, n_out, nb = len(in_specs), len(out_specs), len(bufs)
        inner = body

        def body(*refs):
            ins, outs = refs[:n_in], refs[n_in + nb:n_in + nb + n_out]
            cbufs = refs[n_in + nb + n_out:n_in + 2 * nb + n_out]
            rest = refs[n_in + 2 * nb + n_out:]
            sems = rest[-2:]
            ids = [pl.program_id(q) for q in range(len(grid))]
            first = functools.reduce(jnp.logical_and, [i == 0 for i in ids])
            last = functools.reduce(jnp.logical_and, [i == g - 1 for i, g in zip(ids, grid)])

            @pl.when(first)
            def _():
                for cp in sends(cbufs, *sems):
                    cp.start()

            inner(*ins, *outs, *rest[:-2])

            @pl.when(last)
            def _():
                recv_waits(cbufs, *sems)
                for cp in sends(cbufs, *sems):
                    cp.wait_send()

        aliases.update({n_in + q: n_out + q for q in range(nb)})
        in_specs += [pl.BlockSpec(memory_space=pltpu.HBM)] * nb
        out_specs += [pl.BlockSpec(memory_space=pltpu.HBM)] * nb
        out_shape += [jax.ShapeDtypeStruct(b.shape, b.dtype) for b in bufs]
        scratch += [pltpu.SemaphoreType.DMA((n_sems,))] * 2
        args += list(bufs)
    return pl.pallas_call(
        body, name=name, grid=grid, in_specs=in_specs, out_specs=out_specs, out_shape=out_shape, scratch_shapes=scratch,
        input_output_aliases=aliases, compiler_params=_cp(dimension_semantics=dims),
    )(*args)


def _pick(n, cands):
    for c in cands:
        if n % c == 0:
            return c
    return n


def _silu(x):
    return x * jax.nn.sigmoid(x)


def _dsilu(x):
    s = jax.nn.sigmoid(x)
    return s * (1.0 + x * (1.0 - s))


_DIMS = {"nn": (((1,), (0,)), ((), ())), "nt": (((1,), (1,)), ((), ())), "tn": (((0,), (0,)), ((), ()))}


TILES_M = (1408, 768, 512, 384, 256, 128)
TILES_N = (1408, 1024, 1152, 768, 512, 384, 256, 128)
TILES_K = (2048, 1408, 1152, 1024, 768, 512, 384, 256, 128)


class Mat:
    def __init__(self, arr, kind="plain", rows=None, row0=0, base=0, nparts=1):
        self.arr, self.kind, self.row0, self.base, self.nparts = arr, kind, row0, base, nparts
        if kind == "cols3":
            self.r, self.s = arr.shape[1], arr.shape[2] * nparts
        else:
            self.r, self.s = (rows if rows is not None else arr.shape[0]), arr.shape[1]

    def s_unit(self):
        return self.s // self.nparts

    def spec(self, tr, ts, r_of, s_of):
        if self.kind == "cols3":
            nps = self.s // self.nparts // ts
            return pl.BlockSpec((None, tr, ts), lambda *ids: (self.base + s_of(*ids) // nps, r_of(*ids), s_of(*ids) % nps))
        off = self.row0 // tr
        return pl.BlockSpec((tr, ts), lambda *ids: (off + r_of(*ids), s_of(*ids)))


MATMUL_VMEM_BUDGET = 36 * 1024 * 1024


def _fit_tiles(m, n, k, m_unit, n_unit, k_unit, result_bytes):
    cands = [[c for c in tiles if dim % c == 0 and unit % c == 0]
             for tiles, dim, unit in ((TILES_M, m, m_unit), (TILES_N, n, n_unit), (TILES_K, k, k_unit))]
    assert all(cands), (m, n, k, m_unit, n_unit, k_unit)
    idx = [0, 0, 0]
    while True:
        tm, tn, tk = (c[i] for c, i in zip(cands, idx))
        if 2 * 2 * (tm * tk + tk * tn) + tm * tn * (4 + 2 * result_bytes) <= MATMUL_VMEM_BUDGET:
            return tm, tn, tk
        shrinkable = [q for q in range(3) if idx[q] + 1 < len(cands[q])]
        assert shrinkable, (m, n, k)
        q = max(shrinkable, key=lambda q: cands[q][idx[q]])
        idx[q] += 1


def matmul(a, b, mode, name, out_dtype=F32, into=None, carry=None):
    a = a if isinstance(a, Mat) else Mat(a)
    b = b if isinstance(b, Mat) else Mat(b)
    m, k = (a.s, a.r) if mode == "tn" else (a.r, a.s)
    n, kb = (b.r, b.s) if mode == "nt" else (b.s, b.r)
    assert k == kb, (name, a.r, a.s, b.r, b.s)
    m_unit = a.s_unit() if mode == "tn" else m
    k_unit = math.gcd(a.s_unit() if mode != "tn" else k, b.s_unit() if mode == "nt" else k)
    n_unit = n if mode == "nt" else b.s_unit()
    if into is not None:
        kind, buf, shape, row0 = into
        if kind == "col":
            n_unit = math.gcd(n_unit, shape[2])
        else:
            m_unit = math.gcd(m_unit, m // N_CHIP)
    tm, tn, tk = _fit_tiles(m, n, k, m_unit, n_unit, k_unit, jnp.dtype(out_dtype).itemsize)
    nk = k // tk
    ii, jj, kk_ = (lambda j, i, kk: i), (lambda j, i, kk: j), (lambda j, i, kk: kk)
    a_spec = a.spec(tk, tm, kk_, ii) if mode == "tn" else a.spec(tm, tk, ii, kk_)
    b_spec = b.spec(tn, tk, jj, kk_) if mode == "nt" else b.spec(tk, tn, kk_, jj)
    in_specs, args, aliases = [a_spec, b_spec], [a.arr, b.arr], {}
    if into is None:
        out_spec = pl.BlockSpec((tm, tn), lambda j, i, kk: (i, j))
        out_shape = jax.ShapeDtypeStruct((m, n), out_dtype)
    else:
        assert row0 % tm == 0
        r0 = row0 // tm
        if kind == "col":
            npn = shape[2] // tn
            out_spec = pl.BlockSpec((None, tm, tn), lambda j, i, kk: (j // npn, r0 + i, j % npn))
        else:
            npm = m // N_CHIP // tm
            out_spec = pl.BlockSpec((None, tm, tn), lambda j, i, kk: (i // npm, r0 + i % npm, j))
        out_shape = jax.ShapeDtypeStruct(shape, out_dtype)
        if buf is not None:
            in_specs.append(pl.BlockSpec(memory_space=pl.ANY))
            args.append(buf)
            aliases = {2: 0}

    def body(a_ref, b_ref, *rest):
        o_ref, acc_ref = rest[-2:]
        kk = pl.program_id(2)
        part = lax.dot_general(a_ref[...], b_ref[...], _DIMS[mode], preferred_element_type=F32)
        if nk == 1:
            o_ref[...] = part.astype(out_dtype)
        else:

            @pl.when(kk == 0)
            def _():
                acc_ref[...] = part

            @pl.when(kk > 0)
            def _():
                acc_ref[...] += part

            @pl.when(kk == nk - 1)
            def _():
                o_ref[...] = acc_ref[...].astype(out_dtype)

    res = _pallas(body, name, (n // tn, m // tm, nk), in_specs, [out_spec], [out_shape],
                  [pltpu.VMEM((tm, tn) if nk > 1 else (SUBLANES, LANES), F32)], args, ("parallel", "parallel", "arbitrary"),
                  aliases=aliases, carry=carry)
    return res[0] if carry is None else (res[0], res[1:])


def matmul_gate_res(a, w, res, gate, n_ctx, name):
    w = w if isinstance(w, Mat) else Mat(w)
    (m, k), n = a.shape, w.s
    assert k == w.r
    tm, tn, tk = _fit_tiles(m, n, k, m, w.s_unit(), k, 3 * 4)
    nk = k // tk

    def body(a_ref, b_ref, r_ref, g_ref, x_ref, y_ref, acc_ref):
        kk = pl.program_id(2)
        row0 = pl.program_id(1) * tm
        part = jnp.dot(a_ref[...], b_ref[...], preferred_element_type=F32)

        @pl.when(kk == 0)
        def _():
            acc_ref[...] = part

        @pl.when(kk > 0)
        def _():
            acc_ref[...] += part

        @pl.when(kk == nk - 1)
        def _():
            y = acc_ref[...]
            row = row0 + lax.broadcasted_iota(jnp.int32, (tm, 1), 0)
            g = jnp.where(row < n_ctx, g_ref[0:1, :], g_ref[1:2, :])
            y_ref[...] = y.astype(y_ref.dtype)
            x_ref[...] = r_ref[...] + g * y

    return pl.pallas_call(
        body, name=name, grid=(n // tn, m // tm, nk),
        in_specs=[pl.BlockSpec((tm, tk), lambda j, i, kk: (i, kk)), w.spec(tk, tn, lambda j, i, kk: kk, lambda j, i, kk: j),
                  pl.BlockSpec((tm, tn), lambda j, i, kk: (i, j)), pl.BlockSpec((2, tn), lambda j, i, kk: (0, j))],
        out_specs=[pl.BlockSpec((tm, tn), lambda j, i, kk: (i, j)), pl.BlockSpec((tm, tn), lambda j, i, kk: (i, j))],
        out_shape=[jax.ShapeDtypeStruct((m, n), F32), jax.ShapeDtypeStruct((m, n), F32)],
        scratch_shapes=[pltpu.VMEM((tm, tn), F32)],
        compiler_params=_cp(dimension_semantics=("parallel", "parallel", "arbitrary")),
    )(a, w.arr, res, gate)


ROW_TILE = 256


def _seg_row(ref2, i, n_ctx_tiles):
    return jnp.where(i < n_ctx_tiles, ref2[0:1, :], ref2[1:2, :])


def _acc_rows(ref, step, rows):
    @pl.when(step == 0)
    def _():
        ref[...] = jnp.zeros_like(ref)

    for r, v in enumerate(rows):
        ref[r:r + 1, :] += v


def norm_mod(x, w, shift, scale, n_ctx, name):
    t, d = x.shape
    tm = _pick(n_ctx, (ROW_TILE, 128))
    nct = n_ctx // tm

    def body(x_ref, w_ref, sh_ref, sc_ref, h_ref):
        i = pl.program_id(0)
        xv = x_ref[...]
        r = lax.rsqrt(jnp.mean(xv * xv, axis=-1, keepdims=True) + NORM_EPS)
        h_ref[...] = ((xv * r) * w_ref[...] * (1.0 + _seg_row(sc_ref, i, nct)) + _seg_row(sh_ref, i, nct)).astype(BF16)

    return pl.pallas_call(
        body, name=name, grid=(t // tm,),
        in_specs=[pl.BlockSpec((tm, d), lambda i: (i, 0)), pl.BlockSpec((1, d), lambda i: (0, 0)),
                  pl.BlockSpec((2, d), lambda i: (0, 0)), pl.BlockSpec((2, d), lambda i: (0, 0))],
        out_specs=pl.BlockSpec((tm, d), lambda i: (i, 0)),
        out_shape=jax.ShapeDtypeStruct((t, d), BF16), compiler_params=_cp(),
    )(x, w, shift, scale)


def norm_mod_bwd(dh, dres, x, w, scale, n_ctx, name):
    t, d = x.shape
    tm = _pick(n_ctx, (ROW_TILE, 128))
    nct = n_ctx // tm

    def body(dh_ref, dr_ref, x_ref, w_ref, sc_ref, dx_ref, st_ref):
        i = pl.program_id(0)
        xv, g = x_ref[...], dh_ref[...]
        r = lax.rsqrt(jnp.mean(xv * xv, axis=-1, keepdims=True) + NORM_EPS)
        xn = xv * r
        one_sc = 1.0 + _seg_row(sc_ref, i, nct)
        dxn = g * (w_ref[...] * one_sc)
        dx_ref[...] = dr_ref[...] + r * (dxn - xn * jnp.mean(dxn * xn, axis=-1, keepdims=True))
        gx = g * xn
        s_shift = jnp.sum(g, axis=0, keepdims=True)
        s_scale = jnp.sum(gx * w_ref[...], axis=0, keepdims=True)
        s_w = jnp.sum(gx * one_sc, axis=0, keepdims=True)
        _acc_rows(st_ref, i, [jnp.where(i < nct, s_shift, 0.0), jnp.where(i < nct, 0.0, s_shift),
                              jnp.where(i < nct, s_scale, 0.0), jnp.where(i < nct, 0.0, s_scale), s_w])

    return pl.pallas_call(
        body, name=name, grid=(t // tm,),
        in_specs=[pl.BlockSpec((tm, d), lambda i: (i, 0)), pl.BlockSpec((tm, d), lambda i: (i, 0)),
                  pl.BlockSpec((tm, d), lambda i: (i, 0)), pl.BlockSpec((1, d), lambda i: (0, 0)),
                  pl.BlockSpec((2, d), lambda i: (0, 0))],
        out_specs=[pl.BlockSpec((tm, d), lambda i: (i, 0)), pl.BlockSpec((SUBLANES, d), lambda i: (0, 0))],
        out_shape=[jax.ShapeDtypeStruct((t, d), F32), jax.ShapeDtypeStruct((SUBLANES, d), F32)],
        compiler_params=_cp(dimension_semantics=("arbitrary",)),
    )(dh, dres, x, w, scale)


def gate_bwd(dx, y, gate, n_ctx, name):
    t, d = dx.shape
    tm = _pick(n_ctx, (ROW_TILE, 128))
    nct = n_ctx // tm

    def body(dx_ref, y_ref, g_ref, dy_ref, dg_ref):
        i = pl.program_id(0)
        dxv = dx_ref[...]
        dy_ref[...] = (dxv * _seg_row(g_ref, i, nct)).astype(BF16)
        s = jnp.sum(dxv * y_ref[...], axis=0, keepdims=True)
        _acc_rows(dg_ref, i, [jnp.where(i < nct, s, 0.0), jnp.where(i < nct, 0.0, s)])

    return pl.pallas_call(
        body, name=name, grid=(t // tm,),
        in_specs=[pl.BlockSpec((tm, d), lambda i: (i, 0)), pl.BlockSpec((tm, d), lambda i: (i, 0)),
                  pl.BlockSpec((2, d), lambda i: (0, 0))],
        out_specs=[pl.BlockSpec((tm, d), lambda i: (i, 0)), pl.BlockSpec((SUBLANES, d), lambda i: (0, 0))],
        out_shape=[jax.ShapeDtypeStruct((t, d), BF16), jax.ShapeDtypeStruct((SUBLANES, d), F32)],
        compiler_params=_cp(dimension_semantics=("arbitrary",)),
    )(dx, y, gate)


def _seg_edges(i, tm, n_ctx, t):
    first = (i == 0) | (i == n_ctx // tm)
    last = (i == n_ctx // tm - 1) | (i == t // tm - 1)
    return first, last


def _shift_rows(x, prev_row, next_row, first, last):
    tm = x.shape[0]
    row = lax.broadcasted_iota(jnp.int32, (tm, 1), 0)
    xp = jnp.where(row == 0, jnp.where(first, 0.0, prev_row), pltpu.roll(x, 1, 0))
    xn = jnp.where(row == tm - 1, jnp.where(last, 0.0, next_row), pltpu.roll(x, tm - 1, 0))
    return xp, xn


def _halo_rows(dtype):
    return SUBLANES * 4 // jnp.dtype(dtype).itemsize


def _halo_specs(tm, tc, t, col, hb):
    r, nblk = tm // hb, t // hb
    return [pl.BlockSpec((tm, tc), lambda j, i: (i, col(j))),
            pl.BlockSpec((hb, tc), lambda j, i: (jnp.maximum(i * r - 1, 0), col(j))),
            pl.BlockSpec((hb, tc), lambda j, i: (jnp.minimum((i + 1) * r, nblk - 1), col(j)))]


def _conv_rows(w_ref, xm, prev, nxt, first, last):
    hb, tm = prev.shape[0], xm.shape[0]
    w0, w1, w2, b = w_ref[0:1, :], w_ref[1:2, :], w_ref[2:3, :], w_ref[3:4, :]
    xp, xn = _shift_rows(xm, prev[hb - 1:hb, :], nxt[0:1, :], first, last)
    pre = w0 * xp + w1 * xm + w2 * xn + b
    pre_before = w0 * prev[hb - 2:hb - 1, :] + w1 * prev[hb - 1:hb, :] + w2 * xm[0:1, :] + b
    pre_after = w0 * xm[tm - 1:tm, :] + w1 * nxt[0:1, :] + w2 * nxt[1:2, :] + b
    return pre, pre_before, pre_after


def dwconv_act(x, col0, c, wb, n_ctx, mode, act_dtype, name):
    t = x.shape[0]
    tm = _pick(n_ctx, (ROW_TILE, 128))
    nparts = 2 if mode == "glu" else 1
    cw = c // nparts
    tc = _pick(math.gcd(cw, col0), (512, 384, 256, 128))
    hb = _halo_rows(x.dtype)

    def body(*refs):
        first, last = _seg_edges(pl.program_id(1), tm, n_ctx, t)
        pres = [_conv_rows(refs[4 * p + 3], *[r[...].astype(F32) for r in refs[4 * p:4 * p + 3]], first, last)[0] for p in range(nparts)]
        refs[-1][...] = (_silu(pres[0]) if mode == "silu" else _silu(pres[1]) * pres[0]).astype(act_dtype)

    in_specs, args = [], []
    for p in range(nparts):
        in_specs += _halo_specs(tm, tc, t, lambda j, p=p: (col0 + p * cw) // tc + j, hb)
        in_specs.append(pl.BlockSpec((SUBLANES, tc), lambda j, i, p=p: (0, p * cw // tc + j)))
        args += [x, x, x, wb]
    return pl.pallas_call(
        body, name=name, grid=(cw // tc, t // tm), in_specs=in_specs, out_specs=pl.BlockSpec((tm, tc), lambda j, i: (i, j)),
        out_shape=jax.ShapeDtypeStruct((t, cw), act_dtype), compiler_params=_cp(dimension_semantics=("parallel", "parallel")),
    )(*args)


def dwconv_act_bwd(dact, dcol0, x, xcol0, wb, wcol0, cw, n_ctx, mode, name, into=None, ocol0=0):
    t = x.shape[0]
    tm = _pick(n_ctx, (ROW_TILE, 128))
    nparts = 2 if mode == "glu" else 1
    tc = _pick(math.gcd(cw, dcol0, xcol0, wcol0, ocol0), (512, 384, 256, 128))
    hb_d, hb_x = _halo_rows(dact.dtype), _halo_rows(x.dtype)

    def dpre_of(dact_v, pres_v):
        if mode == "silu":
            return [dact_v * _dsilu(pres_v[0])]
        val, gat = pres_v
        sg = jax.nn.sigmoid(gat)
        return [dact_v * (gat * sg), dact_v * val * (sg * (1.0 + gat * (1.0 - sg)))]

    def body(*refs):
        i = pl.program_id(1)
        first, last = _seg_edges(i, tm, n_ctx, t)
        da = [r[...].astype(F32) for r in refs[0:3]]
        xs = [[r[...].astype(F32) for r in refs[3 + 3 * p:6 + 3 * p]] for p in range(nparts)]
        ws = refs[3 + 3 * nparts:3 + 4 * nparts]
        dx_ref, dw_ref = refs[-2:]
        pres = [_conv_rows(ws[p], *xs[p], first, last) for p in range(nparts)]
        dm = dpre_of(da[0], [pr[0] for pr in pres])
        d_before = dpre_of(da[1][hb_d - 1:hb_d, :], [pr[1] for pr in pres])
        d_after = dpre_of(da[2][0:1, :], [pr[2] for pr in pres])
        for p in range(nparts):
            d_prev, d_next = _shift_rows(dm[p], d_before[p], d_after[p], first, last)
            w_ref, xv = ws[p], xs[p][0]
            dxv = (w_ref[0:1, :] * d_next + w_ref[1:2, :] * dm[p] + w_ref[2:3, :] * d_prev).astype(BF16)
            if mode == "glu":
                dx_ref[p] = dxv
            else:
                dx_ref[...] = dxv
            _acc_rows(dw_ref.at[p] if mode == "glu" else dw_ref, i,
                      [jnp.sum(d_next * xv, axis=0, keepdims=True), jnp.sum(dm[p] * xv, axis=0, keepdims=True),
                       jnp.sum(d_prev * xv, axis=0, keepdims=True), jnp.sum(dm[p], axis=0, keepdims=True)])

    in_specs = _halo_specs(tm, tc, t, lambda j: dcol0 // tc + j, hb_d)
    args = [dact] * 3
    for p in range(nparts):
        in_specs += _halo_specs(tm, tc, t, lambda j, p=p: (xcol0 + p * cw) // tc + j, hb_x)
        args += [x] * 3
    for p in range(nparts):
        in_specs.append(pl.BlockSpec((SUBLANES, tc), lambda j, i, p=p: (0, (wcol0 + p * cw) // tc + j)))
        args.append(wb)
    aliases = {}
    if mode == "glu":
        out_specs = [pl.BlockSpec((2, tm, tc), lambda j, i: (0, i, j)), pl.BlockSpec((2, SUBLANES, tc), lambda j, i: (0, 0, j))]
        out_shape = [jax.ShapeDtypeStruct((2, t, cw), BF16), jax.ShapeDtypeStruct((2, SUBLANES, cw), F32)]
    else:
        out_specs = [pl.BlockSpec((tm, tc), lambda j, i: (i, ocol0 // tc + j)), pl.BlockSpec((SUBLANES, tc), lambda j, i: (0, j))]
        out_shape = [jax.ShapeDtypeStruct((t, cw) if into is None else into.shape, BF16), jax.ShapeDtypeStruct((SUBLANES, cw), F32)]
        if into is not None:
            aliases = {len(args): 0}
            in_specs.append(pl.BlockSpec(memory_space=pl.ANY))
            args.append(into)
    return pl.pallas_call(
        body, name=name, grid=(cw // tc, t // tm), in_specs=in_specs, out_specs=out_specs, out_shape=out_shape,
        input_output_aliases=aliases, compiler_params=_cp(dimension_semantics=("parallel", "arbitrary")),
    )(*args)


HI = lax.Precision.HIGHEST
_NT = (((1,), (1,)), ((), ()))
_TN = (((0,), (0,)), ((), ()))


def _dot(a, b, dims=None, precision=None):
    if dims is None:
        return jnp.dot(a, b, preferred_element_type=F32, precision=precision)
    return lax.dot_general(a, b, dims, preferred_element_type=F32, precision=precision)


def _softplus(x):
    y = jnp.exp(-jnp.abs(x))
    u = 1.0 + y
    log1p = jnp.where(u == 1.0, y, y * jnp.log(u) / jnp.where(u == 1.0, 1.0, u - 1.0))
    return jnp.maximum(x, 0.0) + log1p


def _tri(n, upper):
    r = lax.broadcasted_iota(jnp.int32, (n, n), 0)
    c = lax.broadcasted_iota(jnp.int32, (n, n), 1)
    return ((r <= c) if upper else (r >= c)).astype(F32)


def _group_select(heads, e):
    g = jnp.arange(SSD_GROUPS)[:, None, None]
    src = jnp.arange(LANES)[None, :, None]
    dst = jnp.arange(LANES)[None, None, :]
    d, k = dst // e, dst % e
    return ((dst < 2 * e) & (src == d * heads + g * e + k)).astype(F32)


def ssd_prep(zx, col0, bias, alog, sel, heads, name):
    t = zx.shape[0]
    assert 2 * heads == LANES and col0 % LANES == 0
    nc = t // CHUNK

    def body(zx_ref, b_ref, al_ref, sel_ref, dt_ref, da_ref, dtg_ref, ag_ref, agt_ref):
        dt = _softplus(zx_ref[...] + b_ref[...])
        da = -jnp.exp(al_ref[...]) * dt
        lane = lax.broadcasted_iota(jnp.int32, (CHUNK, LANES), 1)
        a = jnp.where(lane < heads, _dot(_tri(CHUNK, False), da, precision=HI), _dot(_tri(CHUNK, True), da, precision=HI))
        dt_ref[...] = dt
        da_ref[...] = da
        for g in range(SSD_GROUPS):
            s = sel_ref[g]
            dtg_ref[g] = _dot(dt, s, precision=HI)
            a_g = _dot(a, s, precision=HI)
            ag_ref[g] = a_g
            agt_ref[g, 0] = a_g.T

    return pl.pallas_call(
        body, name=name, grid=(nc,),
        in_specs=[pl.BlockSpec((CHUNK, LANES), lambda c: (c, col0 // LANES)), pl.BlockSpec((1, LANES), lambda c: (0, 0)),
                  pl.BlockSpec((1, LANES), lambda c: (0, 0)), pl.BlockSpec((SSD_GROUPS, LANES, LANES), lambda c: (0, 0, 0))],
        out_specs=[pl.BlockSpec((CHUNK, LANES), lambda c: (c, 0)), pl.BlockSpec((CHUNK, LANES), lambda c: (c, 0)),
                   pl.BlockSpec((SSD_GROUPS, CHUNK, LANES), lambda c: (0, c, 0)), pl.BlockSpec((SSD_GROUPS, CHUNK, LANES), lambda c: (0, c, 0)),
                   pl.BlockSpec((SSD_GROUPS, 1, LANES, CHUNK), lambda c: (0, c, 0, 0))],
        out_shape=[jax.ShapeDtypeStruct((t, LANES), F32), jax.ShapeDtypeStruct((t, LANES), F32),
                   jax.ShapeDtypeStruct((SSD_GROUPS, t, LANES), F32), jax.ShapeDtypeStruct((SSD_GROUPS, t, LANES), F32),
                   jax.ShapeDtypeStruct((SSD_GROUPS, nc, LANES, CHUNK), F32)],
        compiler_params=_cp(),
    )(zx, bias, alog, sel)


def ssd_prep_bwd(zx, col0, bias, alog, sel, grads_f, grads_b, da_comp, heads, into, name):
    t = zx.shape[0]
    nc = t // CHUNK

    def body(zx_ref, b_ref, al_ref, sel_ref, ddtg_ref, dag_ref, dagt_ref, ddtg2_ref, dag2_ref, dagt2_ref, da_ref, _, draw_ref, st_ref):
        c = pl.program_id(0)
        ddt = jnp.zeros((CHUNK, LANES), F32)
        dacc = jnp.zeros((CHUNK, LANES), F32)
        for g in range(SSD_GROUPS):
            s = sel_ref[g]
            ddt += _dot(ddtg_ref[g] + ddtg2_ref[g], s, _NT, precision=HI)
            dacc += _dot(dag_ref[g] + dag2_ref[g] + (dagt_ref[g, 0] + dagt2_ref[g, 0]).T, s, _NT, precision=HI)
        lane = lax.broadcasted_iota(jnp.int32, (CHUNK, LANES), 1)
        dda = jnp.where(lane < heads, _dot(_tri(CHUNK, True), dacc, precision=HI), _dot(_tri(CHUNK, False), dacc, precision=HI))
        xin = zx_ref[...] + b_ref[...]
        ddt_tot = ddt - dda * jnp.exp(al_ref[...])
        draw = ddt_tot * jax.nn.sigmoid(xin)
        draw_ref[...] = draw.astype(BF16)
        _acc_rows(st_ref, c, [jnp.sum(draw, axis=0, keepdims=True), jnp.sum(dda * da_ref[...], axis=0, keepdims=True)])

    g3 = pl.BlockSpec((SSD_GROUPS, CHUNK, LANES), lambda c: (0, c, 0))
    g4 = pl.BlockSpec((SSD_GROUPS, 1, LANES, CHUNK), lambda c: (0, c, 0, 0))
    return pl.pallas_call(
        body, name=name, grid=(nc,),
        in_specs=[pl.BlockSpec((CHUNK, LANES), lambda c: (c, col0 // LANES)), pl.BlockSpec((1, LANES), lambda c: (0, 0)),
                  pl.BlockSpec((1, LANES), lambda c: (0, 0)), pl.BlockSpec((SSD_GROUPS, LANES, LANES), lambda c: (0, 0, 0)),
                  g3, g3, g4, g3, g3, g4, pl.BlockSpec((CHUNK, LANES), lambda c: (c, 0)), pl.BlockSpec(memory_space=pl.ANY)],
        out_specs=[pl.BlockSpec((CHUNK, LANES), lambda c: (c, col0 // LANES)), pl.BlockSpec((SUBLANES, LANES), lambda c: (0, 0))],
        out_shape=[jax.ShapeDtypeStruct(into.shape, BF16), jax.ShapeDtypeStruct((SUBLANES, LANES), F32)],
        input_output_aliases={11: 0}, compiler_params=_cp(dimension_semantics=("arbitrary",)),
    )(zx, bias, alog, sel, *grads_f, *grads_b, da_comp, into)


def _chunk_row(k, nctx_c, nc, rev):
    if not rev:
        return k
    return jnp.where(k < nctx_c, nctx_c - 1 - k, nc + nctx_c - 1 - k)


def _pair_consts(dtg, ag, agt, l0, end):
    lane = lax.broadcasted_iota(jnp.int32, (CHUNK, LANES), 1)
    lo = lane < SSD_HEAD_DIM
    a0, a1 = ag[:, l0:l0 + 1], ag[:, l0 + 1:l0 + 2]
    dtp = jnp.where(lo, dtg[:, l0:l0 + 1], dtg[:, l0 + 1:l0 + 2])
    acol = jnp.where(lo, a0, a1)
    aend = acol[end:end + 1, :]
    return lo, a0, a1, agt[l0:l0 + 1, :], agt[l0 + 1:l0 + 2, :], dtp, acol, aend


def ssd_scan(xbc, dtg, ag, agt, d_inner, n_ctx, rev, name, carry=None):
    t = xbc.shape[0]
    e = d_inner // SSD_HEAD_DIM // SSD_GROUPS
    npair, gw = e // 2, e * SSD_HEAD_DIM
    assert e % 2 == 0 and gw % LANES == 0
    nc, nctx_c = t // CHUNK, n_ctx // CHUNK
    dirn = 1 if rev else 0
    end = 0 if rev else CHUNK - 1
    ridx = lambda k: _chunk_row(k, nctx_c, nc, rev)

    def body(x_ref, b_ref, c_ref, dtg_ref, ag_ref, agt_ref, y_ref, h_ref, state):
        k = pl.program_id(1)

        @pl.when(k == 0)
        def _():
            state[...] = jnp.zeros_like(state)

        bb, cbf = b_ref[...].astype(BF16), c_ref[...].astype(BF16)
        cb = _dot(cbf, bb, _NT)
        row = lax.broadcasted_iota(jnp.int32, (CHUNK, CHUNK), 0)
        col = lax.broadcasted_iota(jnp.int32, (CHUNK, CHUNK), 1)
        mask = (row <= col) if rev else (row >= col)
        rlo = lax.broadcasted_iota(jnp.int32, (LANES, 1), 0) < SSD_HEAD_DIM
        dtg_v, ag_v, agt_v = dtg_ref[0], ag_ref[0], agt_ref[0, 0]
        for p in range(npair):
            sl = slice(p * LANES, (p + 1) * LANES)
            lo, a0, a1, a0r, a1r, dtp, acol, aend = _pair_consts(dtg_v, ag_v, agt_v, dirn * e + 2 * p, end)
            xdt = x_ref[:, sl] * dtp
            xdtb = xdt.astype(BF16)
            w0 = (cb * jnp.exp(jnp.where(mask, a0 - a0r, -jnp.inf))).astype(BF16)
            w1 = (cb * jnp.exp(jnp.where(mask, a1 - a1r, -jnp.inf))).astype(BF16)
            yd = jnp.where(lo, _dot(w0, xdtb), _dot(w1, xdtb))
            xw = (xdt * jnp.exp(aend - acol)).astype(BF16)
            st = _dot(xw, bb, _TN)
            s_in = state[sl, :]
            h_ref[0, 0, sl, :] = s_in
            yo = _dot(cbf, s_in.astype(BF16), _NT)
            y_ref[:, sl] = yd + yo * jnp.exp(acol)
            cd = jnp.where(rlo, jnp.exp(aend[:, 0:1]), jnp.exp(aend[:, LANES - 1:LANES]))
            state[sl, :] = cd * s_in + st

    xcol = d_inner // LANES
    g3 = pl.BlockSpec((1, CHUNK, LANES), lambda g, k: (g, ridx(k), 0))
    return _pallas(
        body, name, (SSD_GROUPS, nc),
        [pl.BlockSpec((CHUNK, gw), lambda g, k: (ridx(k), g)),
         pl.BlockSpec((CHUNK, SSD_STATE), lambda g, k: (ridx(k), xcol + g)),
         pl.BlockSpec((CHUNK, SSD_STATE), lambda g, k: (ridx(k), xcol + SSD_GROUPS + g)),
         g3, g3, pl.BlockSpec((1, 1, LANES, CHUNK), lambda g, k: (g, ridx(k), 0, 0))],
        [pl.BlockSpec((CHUNK, gw), lambda g, k: (ridx(k), g)),
         pl.BlockSpec((1, 1, npair * LANES, SSD_STATE), lambda g, k: (ridx(k), g, 0, 0))],
        [jax.ShapeDtypeStruct((t, d_inner), F32), jax.ShapeDtypeStruct((nc, SSD_GROUPS, npair * LANES, SSD_STATE), F32)],
        [pltpu.VMEM((npair * LANES, SSD_STATE), F32)], (xbc, xbc, xbc, dtg, ag, agt), ("parallel", "arbitrary"), carry=carry)


def ssd_scan_bwd(dy, xbc, dtg, ag, agt, hst, dskip, prev, d_inner, n_ctx, rev, name, carry=None):
    t = xbc.shape[0]
    e = d_inner // SSD_HEAD_DIM // SSD_GROUPS
    npair, gw = e // 2, e * SSD_HEAD_DIM
    nc, nctx_c = t // CHUNK, n_ctx // CHUNK
    dirn = 1 if rev else 0
    end = 0 if rev else CHUNK - 1
    ridx = lambda kk: _chunk_row(nc - 1 - kk, nctx_c, nc, rev)
    has_skip, has_prev = dskip is not None, prev is not None

    def body(*refs):
        dy_ref, x_ref, b_ref, c_ref, dtg_ref, ag_ref, agt_ref, h_ref = refs[:8]
        pos = 8
        if has_skip:
            ds_ref = refs[pos]
            pos += 1
        if has_prev:
            pdx_ref, pdb_ref, pdc_ref = refs[pos:pos + 3]
            pos += 3
        dx_ref, db_ref, dc_ref, ddtg_ref, dag_ref, dagt_ref, dd_ref, dstate = refs[pos:]
        kk = pl.program_id(1)

        @pl.when(kk == 0)
        def _():
            dstate[...] = jnp.zeros_like(dstate)

        bb, cbf = b_ref[...].astype(BF16), c_ref[...].astype(BF16)
        cb = _dot(cbf, bb, _NT)
        row = lax.broadcasted_iota(jnp.int32, (CHUNK, CHUNK), 0)
        col = lax.broadcasted_iota(jnp.int32, (CHUNK, CHUNK), 1)
        mask = (row <= col) if rev else (row >= col)
        rlo = lax.broadcasted_iota(jnp.int32, (LANES, 1), 0) < SSD_HEAD_DIM
        is_end = lax.broadcasted_iota(jnp.int32, (CHUNK, 1), 0) == end
        dtg_v, ag_v, agt_v = dtg_ref[0], ag_ref[0], agt_ref[0, 0]
        dcb = jnp.zeros((CHUNK, CHUNK), F32)
        d_c = jnp.zeros((CHUNK, SSD_STATE), F32)
        d_b = jnp.zeros((CHUNK, SSD_STATE), F32)
        ddt_out = jnp.zeros((CHUNK, LANES), F32)
        da_out = jnp.zeros((CHUNK, LANES), F32)
        dat_out = jnp.zeros((LANES, CHUNK), F32)
        for p in range(npair):
            sl = slice(p * LANES, (p + 1) * LANES)
            l0 = dirn * e + 2 * p
            lo, a0, a1, a0r, a1r, dtp, acol, aend = _pair_consts(dtg_v, ag_v, agt_v, l0, end)
            x = x_ref[:, sl]
            dyv = dy_ref[:, sl]
            xdt = x * dtp
            xdtb = xdt.astype(BF16)
            seg0 = jnp.exp(jnp.where(mask, a0 - a0r, -jnp.inf))
            seg1 = jnp.exp(jnp.where(mask, a1 - a1r, -jnp.inf))
            w0, w1 = cb * seg0, cb * seg1
            efs, dte = jnp.exp(acol), jnp.exp(aend - acol)
            xw = (xdt * dte).astype(BF16)
            s_in = h_ref[0, 0, sl, :]
            sb = s_in.astype(BF16)
            dyo = (dyv * efs).astype(BF16)
            da_exp = dyv * _dot(cbf, sb, _NT) * efs
            d_c += _dot(dyo, sb)
            ds_y = _dot(dyo, cbf, _TN)
            dyb = dyv.astype(BF16)
            dy0 = jnp.where(lo, dyv, 0.0).astype(BF16)
            dy1 = jnp.where(lo, 0.0, dyv).astype(BF16)
            dw0, dw1 = _dot(dy0, xdtb, _NT), _dot(dy1, xdtb, _NT)
            dxdt = jnp.where(lo, _dot(w0.astype(BF16), dyb, _TN), _dot(w1.astype(BF16), dyb, _TN))
            dcb += dw0 * seg0 + dw1 * seg1
            t0, t1 = dw0 * w0, dw1 * w1
            d_s = dstate[sl, :]
            dsb = d_s.astype(BF16)
            dxw = _dot(bb, dsb, _NT)
            d_b += _dot(xw, dsb)
            dxdt += dxw * dte
            tmp = dxw * xdt * dte
            da_exp -= tmp
            end_row = jnp.sum(tmp, axis=0, keepdims=True)
            prod = d_s * s_in
            e0, e1 = jnp.exp(aend[:, 0:1]), jnp.exp(aend[:, LANES - 1:LANES])
            sc0 = jnp.sum(jnp.where(rlo, prod, 0.0), keepdims=True) * e0
            sc1 = jnp.sum(jnp.where(rlo, 0.0, prod), keepdims=True) * e1
            dstate[sl, :] = jnp.where(rlo, e0, e1) * d_s + ds_y
            dxv = dxdt * dtp
            if has_skip:
                dxv += dyv * ds_ref[:, sl]
                _acc_rows(dd_ref.at[:, sl], kk, [jnp.sum(dyv * x, axis=0, keepdims=True)])
            if has_prev:
                dxv += pdx_ref[:, sl]
            dx_ref[:, sl] = dxv
            ddt_exp = dxdt * x
            lane = lax.broadcasted_iota(jnp.int32, (CHUNK, LANES), 1)
            sub = lax.broadcasted_iota(jnp.int32, (LANES, CHUNK), 0)
            for j, (sel, tj, scj) in enumerate(((lo, t0, sc0), (~lo, t1, sc1))):
                ddt_col = jnp.sum(jnp.where(sel, ddt_exp, 0.0), axis=1, keepdims=True)
                da_col = jnp.sum(jnp.where(sel, da_exp, 0.0), axis=1, keepdims=True) + jnp.sum(tj, axis=1, keepdims=True)
                da_end = jnp.sum(jnp.where(sel[0:1, :], end_row, 0.0), axis=1, keepdims=True) + scj
                da_col = da_col + jnp.where(is_end, da_end, 0.0)
                ddt_out += jnp.where(lane == l0 + j, ddt_col, 0.0)
                da_out += jnp.where(lane == l0 + j, da_col, 0.0)
                dat_out -= jnp.where(sub == l0 + j, jnp.sum(tj, axis=0, keepdims=True), 0.0)
        dcbb = dcb.astype(BF16)
        d_c += _dot(dcbb, bb)
        d_b += _dot(dcbb, cbf, _TN)
        if has_prev:
            d_b += pdb_ref[...]
            d_c += pdc_ref[...]
        db_ref[...] = d_b
        dc_ref[...] = d_c
        ddtg_ref[0] = ddt_out
        dag_ref[0] = da_out
        dagt_ref[0, 0] = dat_out
        if not has_skip:
            dd_ref[...] = jnp.zeros_like(dd_ref)

    xcol = d_inner // LANES
    xs_spec = pl.BlockSpec((CHUNK, gw), lambda g, kk: (ridx(kk), g))
    bc_spec = pl.BlockSpec((CHUNK, SSD_STATE), lambda g, kk: (ridx(kk), g))
    g3 = pl.BlockSpec((1, CHUNK, LANES), lambda g, kk: (g, ridx(kk), 0))
    g4 = pl.BlockSpec((1, 1, LANES, CHUNK), lambda g, kk: (g, ridx(kk), 0, 0))
    in_specs = [xs_spec, xs_spec,
                pl.BlockSpec((CHUNK, SSD_STATE), lambda g, kk: (ridx(kk), xcol + g)),
                pl.BlockSpec((CHUNK, SSD_STATE), lambda g, kk: (ridx(kk), xcol + SSD_GROUPS + g)),
                g3, g3, g4, pl.BlockSpec((1, 1, npair * LANES, SSD_STATE), lambda g, kk: (ridx(kk), g, 0, 0))]
    args = [dy, xbc, xbc, xbc, dtg, ag, agt, hst]
    if has_skip:
        in_specs.append(pl.BlockSpec((1, gw), lambda g, kk: (0, g)))
        args.append(dskip)
    if has_prev:
        in_specs += [xs_spec, bc_spec, bc_spec]
        args += list(prev)
    gn = SSD_GROUPS * SSD_STATE
    return _pallas(
        body, name, (SSD_GROUPS, nc), in_specs,
        [xs_spec, bc_spec, bc_spec, g3, g3, g4, pl.BlockSpec((SUBLANES, gw), lambda g, kk: (0, g))],
        [jax.ShapeDtypeStruct((t, d_inner), F32), jax.ShapeDtypeStruct((t, gn), F32), jax.ShapeDtypeStruct((t, gn), F32),
         jax.ShapeDtypeStruct((SSD_GROUPS, t, LANES), F32), jax.ShapeDtypeStruct((SSD_GROUPS, t, LANES), F32),
         jax.ShapeDtypeStruct((SSD_GROUPS, nc, LANES, CHUNK), F32), jax.ShapeDtypeStruct((SUBLANES, d_inner), F32)],
        [pltpu.VMEM((npair * LANES, SSD_STATE), F32)], args, ("parallel", "arbitrary"), carry=carry)


def ssd_gate_norm(yf, yb, xbc, zx, dskip, w, name):
    t, di = yf.shape
    tm = CHUNK

    def body(yf_ref, yb_ref, x_ref, z_ref, d_ref, w_ref, y_ref, o_ref):
        y = yf_ref[...] + yb_ref[...] + x_ref[...] * d_ref[...]
        y_ref[...] = y
        gz = y * _silu(z_ref[...])
        o_ref[...] = (gz * lax.rsqrt(jnp.mean(gz * gz, axis=-1, keepdims=True) + NORM_EPS) * w_ref[...]).astype(BF16)

    blk = pl.BlockSpec((tm, di), lambda i: (i, 0))
    vec = pl.BlockSpec((1, di), lambda i: (0, 0))
    return pl.pallas_call(
        body, name=name, grid=(t // tm,), in_specs=[blk, blk, blk, blk, vec, vec], out_specs=[blk, blk],
        out_shape=[jax.ShapeDtypeStruct((t, di), F32), jax.ShapeDtypeStruct((t, di), BF16)], compiler_params=_cp(),
    )(yf, yb, xbc, zx, dskip, w)


def ssd_gate_norm_bwd(dout, y, zx, w, name):
    t, di = y.shape
    tm = CHUNK

    def body(do_ref, y_ref, z_ref, w_ref, dy_ref, dz_ref, st_ref):
        i = pl.program_id(0)
        z, yv, g = z_ref[...], y_ref[...], do_ref[...]
        sz = _silu(z)
        gz = yv * sz
        r = lax.rsqrt(jnp.mean(gz * gz, axis=-1, keepdims=True) + NORM_EPS)
        n = gz * r
        dn = g * w_ref[...]
        dgz = r * (dn - n * jnp.mean(dn * n, axis=-1, keepdims=True))
        dy_ref[...] = dgz * sz
        dz_ref[...] = (dgz * yv * _dsilu(z)).astype(BF16)
        _acc_rows(st_ref, i, [jnp.sum(g * n, axis=0, keepdims=True)])

    blk = pl.BlockSpec((tm, di), lambda i: (i, 0))
    return pl.pallas_call(
        body, name=name, grid=(t // tm,), in_specs=[blk, blk, blk, pl.BlockSpec((1, di), lambda i: (0, 0))],
        out_specs=[blk, blk, pl.BlockSpec((SUBLANES, di), lambda i: (0, 0))],
        out_shape=[jax.ShapeDtypeStruct((t, di), F32), jax.ShapeDtypeStruct(zx.shape, BF16), jax.ShapeDtypeStruct((SUBLANES, di), F32)],
        compiler_params=_cp(dimension_semantics=("arbitrary",)),
    )(dout, y, zx, w)


def rope_tables(n_ctx, seq):
    pos = jnp.arange(seq)
    half = ATTN_HEAD_DIM // 4
    inv = ROPE_THETA ** (-jnp.arange(0, 2 * half, 2, dtype=F32) / (2 * half))
    ar = (pos // GRID_W).astype(F32)[:, None] * inv[None, :]
    ac = (pos % GRID_W).astype(F32)[:, None] * inv[None, :]
    cos = jnp.concatenate([jnp.cos(ar), jnp.cos(ar), jnp.cos(ac), jnp.cos(ac)], axis=-1)
    sin = jnp.concatenate([-jnp.sin(ar), jnp.sin(ar), -jnp.sin(ac), jnp.sin(ac)], axis=-1)
    cos = jnp.concatenate([jnp.ones((n_ctx, ATTN_HEAD_DIM), F32), cos], axis=0)
    sin = jnp.concatenate([jnp.zeros((n_ctx, ATTN_HEAD_DIM), F32), sin], axis=0)
    return cos, sin


def _rot(x):
    lane = lax.broadcasted_iota(jnp.int32, x.shape, 1)
    q = ATTN_HEAD_DIM // 4
    return jnp.where((lane % (2 * q)) < q, pltpu.roll(x, ATTN_HEAD_DIM - q, 1), pltpu.roll(x, q, 1))


def qk_prep(qkv, gains, cos, sin, n_q, n_k, name):
    t, c = qkv.shape
    tm = ROW_TILE
    hd = ATTN_HEAD_DIM

    def body(x_ref, g_ref, cos_ref, sin_ref, o_ref):
        cs, sn = cos_ref[...], sin_ref[...]
        for h in range(c // hd):
            sl = slice(h * hd, (h + 1) * hd)
            x = x_ref[:, sl]
            if h < n_q + n_k:
                gain = g_ref[0:1, :] if h < n_q else g_ref[1:2, :]
                xn = x * lax.rsqrt(jnp.mean(x * x, axis=-1, keepdims=True) + NORM_EPS) * gain
                x = xn * cs + _rot(xn) * sn
            o_ref[:, sl] = x.astype(BF16)

    return pl.pallas_call(
        body, name=name, grid=(t // tm,),
        in_specs=[pl.BlockSpec((tm, c), lambda i: (i, 0)), pl.BlockSpec((SUBLANES, hd), lambda i: (0, 0)),
                  pl.BlockSpec((tm, hd), lambda i: (i, 0)), pl.BlockSpec((tm, hd), lambda i: (i, 0))],
        out_specs=pl.BlockSpec((tm, c), lambda i: (i, 0)), out_shape=jax.ShapeDtypeStruct((t, c), BF16), compiler_params=_cp(),
    )(qkv, gains, cos, sin)


def qk_prep_bwd(dq, dk, dv, qkv, gains, cos, sin, name):
    t, c = qkv.shape
    tm = ROW_TILE
    hd = ATTN_HEAD_DIM
    n_q, n_k = dq.shape[1] // hd, dk.shape[1] // hd

    def body(dq_ref, dk_ref, dv_ref, x_ref, g_ref, cos_ref, sin_ref, o_ref, st_ref):
        i = pl.program_id(0)
        cs, sn = cos_ref[...], sin_ref[...]
        dgq = jnp.zeros((1, hd), F32)
        dgk = jnp.zeros((1, hd), F32)
        for h in range(c // hd):
            sl = slice(h * hd, (h + 1) * hd)
            if h >= n_q + n_k:
                hv = h - n_q - n_k
                o_ref[:, sl] = dv_ref[:, hv * hd:(hv + 1) * hd].astype(BF16)
                continue
            is_q = h < n_q
            dy = dq_ref[:, sl] if is_q else dk_ref[:, (h - n_q) * hd:(h - n_q + 1) * hd]
            gain = g_ref[0:1, :] if is_q else g_ref[1:2, :]
            x = x_ref[:, sl]
            r = lax.rsqrt(jnp.mean(x * x, axis=-1, keepdims=True) + NORM_EPS)
            xh = x * r
            dxn = dy * cs + _rot(dy * sn)
            dg = jnp.sum(dxn * xh, axis=0, keepdims=True)
            if is_q:
                dgq += dg
            else:
                dgk += dg
            dxh = dxn * gain
            o_ref[:, sl] = (r * (dxh - xh * jnp.mean(dxh * xh, axis=-1, keepdims=True))).astype(BF16)
        _acc_rows(st_ref, i, [dgq, dgk])

    return pl.pallas_call(
        body, name=name, grid=(t // tm,),
        in_specs=[pl.BlockSpec((tm, n_q * hd), lambda i: (i, 0)), pl.BlockSpec((tm, n_k * hd), lambda i: (i, 0)),
                  pl.BlockSpec((tm, n_k * hd), lambda i: (i, 0)), pl.BlockSpec((tm, c), lambda i: (i, 0)),
                  pl.BlockSpec((SUBLANES, hd), lambda i: (0, 0)), pl.BlockSpec((tm, hd), lambda i: (i, 0)), pl.BlockSpec((tm, hd), lambda i: (i, 0))],
        out_specs=[pl.BlockSpec((tm, c), lambda i: (i, 0)), pl.BlockSpec((SUBLANES, hd), lambda i: (0, 0))],
        out_shape=[jax.ShapeDtypeStruct((t, c), BF16), jax.ShapeDtypeStruct((SUBLANES, hd), F32)],
        compiler_params=_cp(dimension_semantics=("arbitrary",)),
    )(dq, dk, dv, qkv, gains, cos, sin)


def _attn_specs(n_ctx, nb, grp, n_qh):
    hd, blk = ATTN_HEAD_DIM, CHUNK
    kc, vc = n_qh, n_qh + ATTN_KV_HEADS
    specs = [pl.BlockSpec((blk, grp * hd), lambda h, b: (b, h))]
    for c0 in (kc, vc):
        specs += [pl.BlockSpec((n_ctx, hd), lambda h, b, c0=c0: (0, c0 + h)),
                  pl.BlockSpec((blk, hd), lambda h, b, c0=c0: (jnp.maximum(b - 1, 0), c0 + h)),
                  pl.BlockSpec((blk, hd), lambda h, b, c0=c0: (b, c0 + h)),
                  pl.BlockSpec((blk, hd), lambda h, b, c0=c0: (jnp.minimum(b + 1, nb - 1), c0 + h))]
    return specs


def _attn_masks(b, nctx_b, nb):
    row = lax.broadcasted_iota(jnp.int32, (CHUNK, CHUNK), 0)
    col = lax.broadcasted_iota(jnp.int32, (CHUNK, CHUNK), 1)
    lat = b >= nctx_b
    return [(col >= row) & lat & (b - 1 >= nctx_b), jnp.broadcast_to(lat, (CHUNK, CHUNK)), (col <= row) & lat & (b + 1 <= nb - 1)]


def attention(qkvr, sinks, n_ctx, n_qh, name, carry=None):
    t = qkvr.shape[0]
    hd, blk = ATTN_HEAD_DIM, CHUNK
    grp = n_qh // ATTN_KV_HEADS
    nb, nctx_b = t // blk, n_ctx // blk
    scale = hd ** -0.5

    def body(q_ref, kc_ref, kp_ref, ko_ref, kn_ref, vc_ref, vp_ref, vo_ref, vn_ref, s_ref, o_ref, lse_ref):
        b = pl.program_id(1)
        masks = _attn_masks(b, nctx_b, nb)
        ks = [kc_ref[...], kp_ref[...], ko_ref[...], kn_ref[...]]
        vs = [vc_ref[...], vp_ref[...], vo_ref[...], vn_ref[...]]
        lane = lax.broadcasted_iota(jnp.int32, (blk, LANES), 1)
        lse_out = jnp.zeros((blk, LANES), F32)
        for g in range(grp):
            q = q_ref[:, g * hd:(g + 1) * hd]
            s = [_dot(q, k, _NT) * scale for k in ks]
            s = [s[0]] + [jnp.where(m, sx, -jnp.inf) for m, sx in zip(masks, s[1:])]
            sink = s_ref[0, 0:1, g:g + 1]
            m = sink
            for sx in s:
                m = jnp.maximum(m, jnp.max(sx, axis=-1, keepdims=True))
            p = [jnp.exp(sx - m) for sx in s]
            l = jnp.exp(sink - m)
            for px in p:
                l = l + jnp.sum(px, axis=-1, keepdims=True)
            inv = 1.0 / l
            o = jnp.zeros((blk, hd), F32)
            for px, v in zip(p, vs):
                o += _dot((px * inv).astype(BF16), v)
            o_ref[:, g * hd:(g + 1) * hd] = o.astype(BF16)
            lse_out = jnp.where(lane == g, m + jnp.log(l), lse_out)
        lse_ref[...] = lse_out

    return _pallas(
        body, name, (ATTN_KV_HEADS, nb),
        _attn_specs(n_ctx, nb, grp, n_qh) + [pl.BlockSpec((1, SUBLANES, LANES), lambda h, b: (h, 0, 0))],
        [pl.BlockSpec((blk, grp * hd), lambda h, b: (b, h)), pl.BlockSpec((blk, LANES), lambda h, b: (b, h))],
        [jax.ShapeDtypeStruct((t, n_qh * hd), BF16), jax.ShapeDtypeStruct((t, ATTN_KV_HEADS * LANES), F32)],
        [], (qkvr,) * 9 + (sinks,), ("parallel", "arbitrary"), carry=carry)


def attention_bwd(do, o, lse, qkvr, sinks, n_ctx, n_qh, name, carry=None):
    t = qkvr.shape[0]
    hd, blk = ATTN_HEAD_DIM, CHUNK
    grp = n_qh // ATTN_KV_HEADS
    nb, nctx_b = t // blk, n_ctx // blk
    scale = hd ** -0.5
    kvw = ATTN_KV_HEADS * hd

    def body(do_ref, o_ref, lse_ref, q_ref, kc_ref, kp_ref, ko_ref, kn_ref, vc_ref, vp_ref, vo_ref, vn_ref, s_ref,
             dq_ref, dkc_ref, dvc_ref, dkp_ref, dvp_ref, dsk_ref):
        b = pl.program_id(1)
        masks = _attn_masks(b, nctx_b, nb)
        ks = [kc_ref[...], kp_ref[...], ko_ref[...], kn_ref[...]]
        vs = [vc_ref[...], vp_ref[...], vo_ref[...], vn_ref[...]]
        lane = lax.broadcasted_iota(jnp.int32, (1, LANES), 1)
        dks = [jnp.zeros(k.shape, F32) for k in ks]
        dvs = [jnp.zeros(v.shape, F32) for v in vs]
        dsk = jnp.zeros((1, LANES), F32)
        for g in range(grp):
            sl = slice(g * hd, (g + 1) * hd)
            q = q_ref[:, sl]
            dof = do_ref[:, sl]
            dob = dof.astype(BF16)
            lse = lse_ref[:, g:g + 1]
            delta = jnp.sum(dof * o_ref[:, sl].astype(F32), axis=-1, keepdims=True)
            s = [_dot(q, k, _NT) * scale for k in ks]
            s = [s[0]] + [jnp.where(m, sx, -jnp.inf) for m, sx in zip(masks, s[1:])]
            dq = jnp.zeros((blk, hd), F32)
            for x in range(4):
                p = jnp.exp(s[x] - lse)
                ds = (p * (_dot(dob, vs[x], _NT) - delta)).astype(BF16)
                dq += _dot(ds, ks[x])
                dks[x] += _dot(ds, q, _TN)
                dvs[x] += _dot(p.astype(BF16), dob, _TN)
            dq_ref[:, sl] = dq * scale
            p_sink = jnp.exp(s_ref[0, 0:1, g:g + 1] - lse)
            dsk = dsk + jnp.where(lane == g, -jnp.sum(p_sink * delta, axis=0, keepdims=True), 0.0)

        @pl.when(b == 0)
        def _():
            dkc_ref[...] = jnp.zeros_like(dkc_ref)
            dvc_ref[...] = jnp.zeros_like(dvc_ref)
            dsk_ref[...] = jnp.zeros_like(dsk_ref)

        dkc_ref[...] += dks[0] * scale
        dvc_ref[...] += dvs[0]
        dsk_ref[0, 0:1, :] += dsk
        for x in range(3):
            dkp_ref[0, x] = dks[x + 1] * scale
            dvp_ref[0, x] = dvs[x + 1]

    part = pl.BlockSpec((1, 3, blk, hd), lambda h, b: (b, 0, 0, h))
    ctxo = pl.BlockSpec((n_ctx, hd), lambda h, b: (0, h))
    return _pallas(
        body, name, (ATTN_KV_HEADS, nb),
        [pl.BlockSpec((blk, grp * hd), lambda h, b: (b, h)), pl.BlockSpec((blk, grp * hd), lambda h, b: (b, h)),
         pl.BlockSpec((blk, LANES), lambda h, b: (b, h))] + _attn_specs(n_ctx, nb, grp, n_qh)
        + [pl.BlockSpec((1, SUBLANES, LANES), lambda h, b: (h, 0, 0))],
        [pl.BlockSpec((blk, grp * hd), lambda h, b: (b, h)), ctxo, ctxo, part, part,
         pl.BlockSpec((1, SUBLANES, LANES), lambda h, b: (h, 0, 0))],
        [jax.ShapeDtypeStruct((t, n_qh * hd), F32), jax.ShapeDtypeStruct((n_ctx, kvw), F32), jax.ShapeDtypeStruct((n_ctx, kvw), F32),
         jax.ShapeDtypeStruct((nb, 3, blk, kvw), F32), jax.ShapeDtypeStruct((nb, 3, blk, kvw), F32),
         jax.ShapeDtypeStruct((ATTN_KV_HEADS, SUBLANES, LANES), F32)],
        [], (do, o, lse) + (qkvr,) * 9 + (sinks,), ("parallel", "arbitrary"), carry=carry)


def band_reduce(ctx_part, band_part, n_ctx, name):
    nb, _, blk, w = band_part.shape
    nctx_b = n_ctx // blk

    def body(c_ref, p_ref, o_ref, n_ref, out_ref):
        b = pl.program_id(0)
        band = p_ref[0, 0] + o_ref[0, 0] + jnp.where(b + 1 <= nb - 1, n_ref[0, 0], 0.0)
        out_ref[...] = jnp.where(b < nctx_b, c_ref[...], band)

    return pl.pallas_call(
        body, name=name, grid=(nb,),
        in_specs=[pl.BlockSpec((blk, w), lambda b: (jnp.minimum(b, nctx_b - 1), 0)),
                  pl.BlockSpec((1, 1, blk, w), lambda b: (jnp.maximum(b - 1, 0), 2, 0, 0)),
                  pl.BlockSpec((1, 1, blk, w), lambda b: (b, 1, 0, 0)),
                  pl.BlockSpec((1, 1, blk, w), lambda b: (jnp.minimum(b + 1, nb - 1), 0, 0, 0))],
        out_specs=pl.BlockSpec((blk, w), lambda b: (b, 0)), out_shape=jax.ShapeDtypeStruct((nb * blk, w), F32),
        compiler_params=_cp(),
    )(ctx_part, band_part, band_part, band_part)


def loss_grad(xf, target, n_ctx, name):
    t, d = xf.shape
    tm = _pick(n_ctx, (ROW_TILE, 128))
    nct = n_ctx // tm

    def body(x_ref, t_ref, dy_ref, s_ref):
        i = pl.program_id(0)
        err = jnp.where(i < nct, 0.0, x_ref[...] - t_ref[...])
        dy_ref[...] = err * (1.0 / d)
        _acc_rows(s_ref, i, [jnp.sum(err * err, axis=0, keepdims=True)])

    return pl.pallas_call(
        body, name=name, grid=(t // tm,),
        in_specs=[pl.BlockSpec((tm, d), lambda i: (i, 0)), pl.BlockSpec((tm, d), lambda i: (jnp.maximum(i - nct, 0), 0))],
        out_specs=[pl.BlockSpec((tm, d), lambda i: (i, 0)), pl.BlockSpec((SUBLANES, d), lambda i: (0, 0))],
        out_shape=[jax.ShapeDtypeStruct((t, d), F32), jax.ShapeDtypeStruct((SUBLANES, d), F32)],
        compiler_params=_cp(dimension_semantics=("arbitrary",)),
    )(xf, target)


def adamw(w, g, m, v, name):
    r, c = w.shape
    tr = r
    while tr % 2 == 0 and tr * c * 4 > (1 << 20) and (tr // 2) % SUBLANES == 0:
        tr //= 2
    bc1, bc2 = 1.0 - ADAM_B1 ** ADAM_STEP, 1.0 - ADAM_B2 ** ADAM_STEP

    def body(w_ref, g_ref, m_ref, v_ref, d_ref, nm_ref, nv_ref):
        gv = g_ref[...]
        nm = ADAM_B1 * m_ref[...] + (1.0 - ADAM_B1) * gv
        nv = ADAM_B2 * v_ref[...] + (1.0 - ADAM_B2) * (gv * gv)
        nm_ref[...] = nm
        nv_ref[...] = nv
        d_ref[...] = -ADAM_LR * ((nm / bc1) / (jnp.sqrt(nv / bc2) + ADAM_EPS) + ADAM_WD * w_ref[...])

    blk = pl.BlockSpec((tr, c), lambda i: (i, 0))
    return pl.pallas_call(
        body, name=name, grid=(r // tr,), in_specs=[blk] * 4, out_specs=[blk] * 3,
        out_shape=[jax.ShapeDtypeStruct((r, c), F32)] * 3, compiler_params=_cp(dimension_semantics=("parallel",)),
    )(w, g, m, v)


ADA_ROWS = 16


def ada_fwd(cs, w, name):
    l, d, ns = w.shape
    tn = _pick(ns, (512, 256, 128))

    def body(c_ref, w_ref, o_ref):
        o_ref[0] = _dot(c_ref[...], w_ref[0].astype(BF16))

    return pl.pallas_call(
        body, name=name, grid=(l, ns // tn),
        in_specs=[pl.BlockSpec((ADA_ROWS, d), lambda i, j: (0, 0)), pl.BlockSpec((1, d, tn), lambda i, j: (i, 0, j))],
        out_specs=pl.BlockSpec((1, ADA_ROWS, tn), lambda i, j: (i, 0, j)),
        out_shape=jax.ShapeDtypeStruct((l, ADA_ROWS, ns), F32), compiler_params=_cp(),
    )(cs, w)


def ada_bwd(cs, gmod, w, name):
    l, d, ns = w.shape
    tn = _pick(ns, (512, 256, 128))

    def body(c_ref, g_ref, w_ref, dw_ref, dc_ref):
        first = (pl.program_id(0) == 0) & (pl.program_id(1) == 0)
        gb = g_ref[0].astype(BF16)
        dw_ref[0] = _dot(c_ref[...], gb, _TN)
        part = _dot(gb, w_ref[0].astype(BF16), _NT)

        @pl.when(first)
        def _():
            dc_ref[...] = part

        @pl.when(jnp.logical_not(first))
        def _():
            dc_ref[...] += part

    return pl.pallas_call(
        body, name=name, grid=(l, ns // tn),
        in_specs=[pl.BlockSpec((ADA_ROWS, d), lambda i, j: (0, 0)), pl.BlockSpec((1, ADA_ROWS, tn), lambda i, j: (i, 0, j)),
                  pl.BlockSpec((1, d, tn), lambda i, j: (i, 0, j))],
        out_specs=[pl.BlockSpec((1, d, tn), lambda i, j: (i, 0, j)), pl.BlockSpec((ADA_ROWS, d), lambda i, j: (0, 0))],
        out_shape=[jax.ShapeDtypeStruct((l, d, ns), F32), jax.ShapeDtypeStruct((ADA_ROWS, d), F32)],
        compiler_params=_cp(dimension_semantics=("arbitrary", "arbitrary")),
    )(cs, gmod, w)


def sum_leading(a, name):
    k, r, c = a.shape
    tr = _pick(r, (256, 128, 64, 32, 16, 8))

    def body(a_ref, o_ref):
        acc = a_ref[0]
        for q in range(1, k):
            acc = acc + a_ref[q]
        o_ref[...] = acc

    return pl.pallas_call(
        body, name=name, grid=(r // tr,), in_specs=[pl.BlockSpec((k, tr, c), lambda i: (0, i, 0))],
        out_specs=pl.BlockSpec((tr, c), lambda i: (i, 0)), out_shape=jax.ShapeDtypeStruct((r, c), F32), compiler_params=_cp(),
    )(a)


def _mesh_pos():
    return lax.axis_index("x"), lax.axis_index("y"), lax.axis_index("c")


def _other_chips(x, y):
    return [(1 - x, y), (x, 1 - y), (1 - x, 1 - y)]


def _rcopy(src, dst, send_sems, recv_sems, k, to):
    return pltpu.make_async_remote_copy(src_ref=src, dst_ref=dst, send_sem=send_sems.at[k], recv_sem=recv_sems.at[k],
                                        device_id=to, device_id_type=MESH)


def small_allgather(vs, name):
    nv = len(vs)

    def body(*refs):
        v_refs, out_refs = refs[:nv], refs[nv:2 * nv]
        send_sems, recv_sems, local_sems = refs[2 * nv:]
        x, y, c = _mesh_pos()
        sibling = (x, y, 1 - c)
        chips = _other_chips(x, y)

        def blk(q, px, py, pc):
            return out_refs[q].at[4 * px + 2 * py + pc]

        mine = [pltpu.make_async_copy(v_refs[q], blk(q, x, y, c), local_sems.at[q]) for q in range(nv)]
        first, passed = [], []
        for q in range(nv):
            mine[q].start()
            first.append(_rcopy(v_refs[q], blk(q, x, y, c), send_sems, recv_sems, 7 * q, sibling))
            first += [_rcopy(v_refs[q], blk(q, x, y, c), send_sems, recv_sems, 7 * q + 1 + j, (*chip, c)) for j, chip in enumerate(chips)]
        for cp in first:
            cp.start()
        for q in range(nv):
            for j, chip in enumerate(chips):
                _rcopy(blk(q, *chip, c), blk(q, *chip, c), send_sems, recv_sems, 7 * q + 1 + j, (x, y, c)).wait_recv()
                passed.append(_rcopy(blk(q, *chip, c), blk(q, *chip, c), send_sems, recv_sems, 7 * q + 4 + j, sibling))
                passed[-1].start()
        for q in range(nv):
            _rcopy(blk(q, x, y, 1 - c), blk(q, x, y, 1 - c), send_sems, recv_sems, 7 * q, (x, y, c)).wait_recv()
            for j, chip in enumerate(chips):
                _rcopy(blk(q, *chip, 1 - c), blk(q, *chip, 1 - c), send_sems, recv_sems, 7 * q + 4 + j, (x, y, c)).wait_recv()
        for cp in first + passed:
            cp.wait_send()
        for cp in mine:
            cp.wait()

    vm = pl.BlockSpec(memory_space=pltpu.VMEM)
    return pl.pallas_call(
        body, name=name, out_shape=[jax.ShapeDtypeStruct((N_DEV, *v.shape), v.dtype) for v in vs],
        in_specs=[vm] * nv, out_specs=[vm] * nv,
        scratch_shapes=[pltpu.SemaphoreType.DMA((7 * nv,)), pltpu.SemaphoreType.DMA((7 * nv,)), pltpu.SemaphoreType.DMA((nv,))],
        compiler_params=_cp(),
    )(*vs)


_HBM = pl.BlockSpec(memory_space=pltpu.HBM)


STREAM_TILE_BYTES = 2 * 1024 * 1024


def _stream_rows(rows, row_bytes):
    tr = 16
    while rows % (2 * tr) == 0 and 2 * tr * row_bytes <= STREAM_TILE_BYTES:
        tr *= 2
    assert rows % tr == 0
    return tr


def _scalars(*vals):
    return jnp.stack([jnp.asarray(v, jnp.int32) for v in vals])


def place_own(w, chip, name):
    l, k, ns = w.shape
    tk = _pick(k, (256, 128, 64))

    def body(s_ref, w_ref, o_ref):
        o_ref[...] = w_ref[...].astype(BF16)

    grid_spec = pltpu.PrefetchScalarGridSpec(
        num_scalar_prefetch=1, grid=(l, k // tk),
        in_specs=[pl.BlockSpec((None, tk, ns), lambda i, j, s: (i, j, 0))],
        out_specs=pl.BlockSpec((None, None, tk, ns), lambda i, j, s: (i, s[0], j, 0)))
    return pl.pallas_call(body, name=name, grid_spec=grid_spec, out_shape=jax.ShapeDtypeStruct((l, N_CHIP, k, ns), BF16),
                          compiler_params=_cp())(_scalars(chip), w)


def _half(ref, layer, px, py, pc):
    hk = ref.shape[2] // 2
    return ref.at[layer, 2 * px + py, pl.ds(pc * hk, hk)]


def _chip_sends(refs, items, send_sems, recv_sems):
    x, y, c = _mesh_pos()
    return [_rcopy(_half(refs[b], l, x, y, c), _half(refs[b], l, x, y, c), send_sems, recv_sems, 3 * q + j, (*chip, c))
            for q, (b, l) in enumerate(items) for j, chip in enumerate(_other_chips(x, y))]


def _chip_recv_waits(refs, items, send_sems, recv_sems):
    x, y, c = _mesh_pos()
    for q, (b, l) in enumerate(items):
        for j, chip in enumerate(_other_chips(x, y)):
            _rcopy(_half(refs[b], l, *chip, c), _half(refs[b], l, *chip, c), send_sems, recv_sems, 3 * q + j, (x, y, c)).wait_recv()


def _sibling_forward(refs, items, send_sems, recv_sems):
    x, y, c = _mesh_pos()
    passed = [_rcopy(_half(refs[b], l, *chip, c), _half(refs[b], l, *chip, c), send_sems, recv_sems, 3 * q + j, (x, y, 1 - c))
              for q, (b, l) in enumerate(items) for j, chip in enumerate(_other_chips(x, y))]
    for cp in passed:
        cp.start()
    for q, (b, l) in enumerate(items):
        for j, chip in enumerate(_other_chips(x, y)):
            _rcopy(_half(refs[b], l, *chip, 1 - c), _half(refs[b], l, *chip, 1 - c), send_sems, recv_sems, 3 * q + j, (x, y, c)).wait_recv()
    for cp in passed:
        cp.wait_send()


def _inplace_comm_call(body, bufs, n_sems, name):
    nb = len(bufs)
    return pl.pallas_call(
        body, name=name, out_shape=[jax.ShapeDtypeStruct(b.shape, b.dtype) for b in bufs],
        in_specs=[_HBM] * nb, out_specs=[_HBM] * nb, input_output_aliases={q: q for q in range(nb)},
        scratch_shapes=[pltpu.SemaphoreType.DMA((n_sems,))] * 4, compiler_params=_cp(),
    )(*bufs)


def weights_allgather(bufs, items, name):
    nb = len(bufs)

    def body(*refs):
        out_refs = refs[nb:2 * nb]
        s1, r1, s2, r2 = refs[2 * nb:]
        sends = _chip_sends(out_refs, items, s1, r1)
        for cp in sends:
            cp.start()
        _chip_recv_waits(out_refs, items, s1, r1)
        _sibling_forward(out_refs, items, s2, r2)
        for cp in sends:
            cp.wait_send()

    return _inplace_comm_call(body, bufs, 3 * len(items), name)


def weights_forward(bufs, items, name):
    nb = len(bufs)

    def body(*refs):
        s1, r1, _, _ = refs[2 * nb:]
        _sibling_forward(refs[nb:2 * nb], items, s1, r1)

    return _inplace_comm_call(body, bufs, 3 * len(items), name)


def grads_pair_exchange(gs, items, name):
    ng, ni = len(gs), len(items)

    def body(*refs):
        g_refs, out_refs = refs[:ng], refs[ng:ng + ni]
        send_sems, recv_sems = refs[ng + ni:]
        x, y, c = _mesh_pos()
        cps = []
        for q, (a, row0, nrows) in enumerate(items):
            h = nrows // 2
            cps.append(_rcopy(g_refs[a].at[:, pl.ds(row0 + (1 - c) * h, h)], out_refs[q], send_sems, recv_sems, q, (x, y, 1 - c)))
            cps[-1].start()
        for cp in cps:
            cp.wait()

    return pl.pallas_call(
        body, name=name, out_shape=[jax.ShapeDtypeStruct((N_CHIP, nrows // 2, gs[a].shape[2]), BF16) for a, _, nrows in items],
        in_specs=[_HBM] * ng, out_specs=[_HBM] * ni,
        scratch_shapes=[pltpu.SemaphoreType.DMA((ni,)), pltpu.SemaphoreType.DMA((ni,))], compiler_params=_cp(),
    )(*gs)


def pair_add(g, got, c, chip, row0, name):
    n, h, c_ = got.shape
    tr = _stream_rows(math.gcd(h, row0) if row0 else h, n * c_ * 2)
    nblk = h // tr

    def body(s_ref, g_ref, o_ref, pair_ref, land_ref):
        pair_ref[...] = (g_ref[...].astype(F32) + o_ref[...].astype(F32)).astype(BF16)
        me = s_ref[1]
        land_ref[...] = (g_ref[me].astype(F32) + o_ref[me].astype(F32)).astype(BF16)

    grid_spec = pltpu.PrefetchScalarGridSpec(
        num_scalar_prefetch=1, grid=(nblk,),
        in_specs=[pl.BlockSpec((n, tr, c_), lambda i, s: (0, row0 // tr + s[0] * nblk + i, 0)), pl.BlockSpec((n, tr, c_), lambda i, s: (0, i, 0))],
        out_specs=[pl.BlockSpec((n, tr, c_), lambda i, s: (0, i, 0)), pl.BlockSpec((None, tr, c_), lambda i, s: (s[1], i, 0))])
    return pl.pallas_call(body, name=name, grid_spec=grid_spec, out_shape=[jax.ShapeDtypeStruct((n, h, c_), BF16)] * 2,
                          compiler_params=_cp())(_scalars(c, chip), g, got)


def _rs_sends(refs, send_sems, recv_sems):
    ng = len(refs) // 2
    x, y, c = _mesh_pos()
    return [_rcopy(refs[q].at[2 * px + py], refs[ng + q].at[2 * x + y], send_sems, recv_sems, 3 * q + j, (px, py, c))
            for q in range(ng) for j, (px, py) in enumerate(_other_chips(x, y))]


def _rs_recv_waits(refs, send_sems, recv_sems):
    ng = len(refs) // 2
    x, y, c = _mesh_pos()
    for q in range(ng):
        for j, (px, py) in enumerate(_other_chips(x, y)):
            _rcopy(refs[q].at[2 * x + y], refs[ng + q].at[2 * px + py], send_sems, recv_sems, 3 * q + j, (x, y, c)).wait_recv()


def rs_carry(pairs, lands):
    return (list(pairs) + list(lands), 3 * len(pairs), _rs_sends, _rs_recv_waits)


def gather_carry(bufs, items):
    return (list(bufs), 3 * len(items), lambda refs, ss, rs: _chip_sends(refs, items, ss, rs),
            lambda refs, ss, rs: _chip_recv_waits(refs, items, ss, rs))


def grads_chip_exchange(pairs, lands, name):
    ng = len(pairs)

    def body(*refs):
        cbufs, sems = refs[2 * ng:4 * ng], refs[4 * ng:]
        sends = _rs_sends(cbufs, *sems)
        for cp in sends:
            cp.start()
        _rs_recv_waits(cbufs, *sems)
        for cp in sends:
            cp.wait_send()

    res = pl.pallas_call(
        body, name=name, out_shape=[jax.ShapeDtypeStruct(a.shape, a.dtype) for a in list(pairs) + list(lands)],
        in_specs=[_HBM] * (2 * ng), out_specs=[_HBM] * (2 * ng), input_output_aliases={q: q for q in range(2 * ng)},
        scratch_shapes=[pltpu.SemaphoreType.DMA((3 * ng,)), pltpu.SemaphoreType.DMA((3 * ng,))], compiler_params=_cp(),
    )(*pairs, *lands)
    return res[ng:]


def sum_chips(a, c, into, shape, row0, name):
    k, h, c_ = a.shape
    tr = _stream_rows(math.gcd(h, row0) if row0 else h, k * c_ * 2)
    nblk = h // tr

    def body(s_ref, a_ref, *rest):
        acc = a_ref[0].astype(F32)
        for q in range(1, k):
            acc = acc + a_ref[q].astype(F32)
        rest[-1][...] = acc

    in_specs, args, aliases = [pl.BlockSpec((k, tr, c_), lambda i, s: (0, i, 0))], [_scalars(c), a], {}
    if into is not None:
        in_specs.append(pl.BlockSpec(memory_space=pl.ANY))
        args.append(into)
        aliases = {2: 0}
    grid_spec = pltpu.PrefetchScalarGridSpec(
        num_scalar_prefetch=1, grid=(nblk,), in_specs=in_specs,
        out_specs=pl.BlockSpec((tr, c_), lambda i, s: (row0 // tr + s[0] * nblk + i, 0)))
    return pl.pallas_call(body, name=name, grid_spec=grid_spec, out_shape=jax.ShapeDtypeStruct(shape, F32),
                          input_output_aliases=aliases, compiler_params=_cp())(*args)


def halves_exchange(outs, items, name):
    ng = len(outs)

    def body(*refs):
        out_refs = refs[ng:2 * ng]
        send_sems, recv_sems = refs[2 * ng:]
        x, y, c = _mesh_pos()
        rows = lambda a, row0, nrows, half: out_refs[a].at[pl.ds(row0 + half * (nrows // 2), nrows // 2)]
        cps = [_rcopy(rows(*it, c), rows(*it, c), send_sems, recv_sems, q, (x, y, 1 - c)) for q, it in enumerate(items)]
        for cp in cps:
            cp.start()
        for q, it in enumerate(items):
            _rcopy(rows(*it, 1 - c), rows(*it, 1 - c), send_sems, recv_sems, q, (x, y, c)).wait_recv()
        for cp in cps:
            cp.wait_send()

    return pl.pallas_call(
        body, name=name, out_shape=[jax.ShapeDtypeStruct(a.shape, a.dtype) for a in outs],
        in_specs=[_HBM] * ng, out_specs=[_HBM] * ng, input_output_aliases={q: q for q in range(ng)},
        scratch_shapes=[pltpu.SemaphoreType.DMA((len(items),)), pltpu.SemaphoreType.DMA((len(items),))], compiler_params=_cp(),
    )(*outs)


def _rows8(a):
    return jnp.pad(a, ((0, -a.shape[0] % SUBLANES), (0, 0)))


def _chips_cols(g, rows):
    return jnp.concatenate([g[2 * j, :rows] for j in range(N_CHIP)], axis=-1)


BIG = (("ssd_w_in", "col"), ("ssd_w_out", "row"), ("attn_w_qkv", "col"), ("attn_w_o", "row"), ("ffn_w_up", "col"), ("ffn_w_down", "row"))
WEIGHTS = ("c_ctx", "ada_w", "ada_b", "norm1_w", "norm2_w", "ssd_w_in", "ssd_conv_w", "ssd_conv_b", "ssd_dt_bias_f", "ssd_dt_bias_b",
           "ssd_a_log_f", "ssd_a_log_b", "ssd_d", "ssd_norm_w", "ssd_w_out", "attn_w_qkv", "attn_q_gain", "attn_k_gain", "attn_sinks",
           "attn_w_o", "ffn_w_up", "ffn_conv_w", "ffn_conv_b", "ffn_w_down")


def _taps_bias(w3, b):
    return jnp.concatenate([w3, b[None, :], jnp.zeros((SUBLANES - 4, w3.shape[1]), F32)], axis=0)


def kernel(x, c, ctx, c_ctx, ada_w, ada_b, norm1_w, norm2_w, ssd_w_in, ssd_conv_w, ssd_conv_b, ssd_dt_bias_f, ssd_dt_bias_b, ssd_a_log_f, ssd_a_log_b, ssd_d, ssd_norm_w, ssd_w_out, attn_w_qkv, attn_q_gain, attn_k_gain, attn_sinks, attn_w_o, ffn_w_up, ffn_conv_w, ffn_conv_b, ffn_w_down, loss_target, m_c_ctx, m_ada_w, m_ada_b, m_norm1_w, m_norm2_w, m_ssd_w_in, m_ssd_conv_w, m_ssd_conv_b, m_ssd_dt_bias_f, m_ssd_dt_bias_b, m_ssd_a_log_f, m_ssd_a_log_b, m_ssd_d, m_ssd_norm_w, m_ssd_w_out, m_attn_w_qkv, m_attn_q_gain, m_attn_k_gain, m_attn_sinks, m_attn_w_o, m_ffn_w_up, m_ffn_conv_w, m_ffn_conv_b, m_ffn_w_down, v_c_ctx, v_ada_w, v_ada_b, v_norm1_w, v_norm2_w, v_ssd_w_in, v_ssd_conv_w, v_ssd_conv_b, v_ssd_dt_bias_f, v_ssd_dt_bias_b, v_ssd_a_log_f, v_ssd_a_log_b, v_ssd_d, v_ssd_norm_w, v_ssd_w_out, v_attn_w_qkv, v_attn_q_gain, v_attn_k_gain, v_attn_sinks, v_attn_w_o, v_ffn_w_up, v_ffn_conv_w, v_ffn_conv_b, v_ffn_w_down):
    args = locals()
    w = {n: args[n] for n in WEIGHTS}
    mom = {n: args["m_" + n] for n in WEIGHTS}
    var = {n: args["v_" + n] for n in WEIGHTS}

    ix, iy, ic = _mesh_pos()
    chip = 2 * ix + iy
    dev = 2 * chip + ic
    depth, d = norm1_w.shape
    n_ctx, seq = ctx.shape[1], x.shape[1]
    t = n_ctx + seq
    d_inner = ssd_norm_w.shape[1]
    heads = ssd_d.shape[1]
    n_qh = attn_sinks.shape[1]
    grp = n_qh // ATTN_KV_HEADS
    d_ff = ffn_w_down.shape[1] * N_CHIP
    xbc_w = ssd_conv_b.shape[1]
    dt_col = d_inner + xbc_w
    n_ssd, n_att = ssd_w_in.shape[0], attn_w_qkv.shape[0]

    sconv_rows, fconv_rows = n_ssd * 3, depth * 3
    g_c, g_sconv, g_fconv = small_allgather(
        [_rows8(c), _rows8(ssd_conv_w.reshape(sconv_rows, -1)), _rows8(ffn_conv_w.reshape(fconv_rows, -1))], "gather_cond")
    c_all = g_c[:, 0]
    ssd_conv_full = _chips_cols(g_sconv, sconv_rows).reshape(n_ssd, 3, -1)
    ffn_conv_full = _chips_cols(g_fconv, fconv_rows).reshape(depth, 3, -1)

    cvec = jnp.concatenate([c_all, c_ctx[None, :], jnp.zeros((ADA_ROWS - N_DEV - 1, d), F32)], axis=0)
    cs16 = _silu(cvec).astype(BF16)
    mod_cols = ada_fwd(cs16, ada_w, "ada_fwd")
    ns_ada = mod_cols.shape[-1]
    (g_mod,) = small_allgather([mod_cols.reshape(depth * ADA_ROWS, ns_ada)], "gather_mod")
    mod_all = _chips_cols(g_mod, depth * ADA_ROWS).reshape(depth, ADA_ROWS, -1) + ada_b[:, None, :]
    mod_lat = lax.dynamic_index_in_dim(mod_all, dev, axis=1, keepdims=False)
    mod_ctx = mod_all[:, N_DEV]
    mods = jnp.stack([mod_ctx, mod_lat], axis=1).reshape(depth, 2, 6, d)

    bidx = {n: q for q, (n, _) in enumerate(BIG)}
    bufs = [place_own(w[n], chip, f"place_{n}") for n, _ in BIG]

    def layer_items(layer):
        mixer = ("ssd_w_in", "ssd_w_out") if layer % 2 == 0 else ("attn_w_qkv", "attn_w_o")
        return [(bidx[n], layer // 2) for n in mixer] + [(bidx["ffn_w_up"], layer), (bidx["ffn_w_down"], layer)]

    def ssd_in_full(layer):
        return bufs[bidx["ssd_w_in"]][layer].transpose(1, 0, 2).reshape(d, -1)

    def w_col(n, layer):
        b = bufs[bidx[n]]
        return Mat(b.reshape(-1, *b.shape[2:]), "cols3", base=layer * N_CHIP, nparts=N_CHIP)

    def w_row(n, layer):
        b = bufs[bidx[n]]
        rows = N_CHIP * b.shape[2]
        return Mat(b.reshape(-1, b.shape[3]), "rows", rows=rows, row0=layer * rows)

    def subset(items):
        used = sorted({b for b, _ in items})
        return [bufs[b] for b in used], [(used.index(b), l) for b, l in items], used

    def put_back(used, new):
        for b, a in zip(used, new):
            bufs[b] = a

    sub, its, used = subset(layer_items(0)[:1])
    put_back(used, weights_allgather(sub, its, "gather_weights_first"))

    cos, sin = rope_tables(n_ctx, seq)
    sel = _group_select(heads, heads // SSD_GROUPS)
    bias128 = jnp.concatenate([ssd_dt_bias_f, ssd_dt_bias_b], axis=-1)[:, None, :]
    alog128 = jnp.concatenate([ssd_a_log_f, ssd_a_log_b], axis=-1)[:, None, :]
    dskip = jnp.repeat(ssd_d, SSD_HEAD_DIM, axis=-1)[:, None, :]
    gains = jnp.zeros((n_att, SUBLANES, ATTN_HEAD_DIM), F32).at[:, 0].set(attn_q_gain).at[:, 1].set(attn_k_gain)
    sinks3 = jnp.zeros((n_att, ATTN_KV_HEADS, SUBLANES, LANES), F32).at[:, :, 0, :grp].set(attn_sinks.reshape(n_att, ATTN_KV_HEADS, grp))
    wb_ssd = [_taps_bias(ssd_conv_full[j], ssd_conv_b[j]) for j in range(n_ssd)]
    wb_ffn = [_taps_bias(ffn_conv_full[i], ffn_conv_b[i]) for i in range(depth)]

    xs = jnp.concatenate([ctx[0], x[0]], axis=0)
    saved = []
    for i in range(depth):
        j = i // 2
        sh1, sc1, g1, sh2, sc2, g2 = [mods[i, :, q] for q in range(6)]
        s = {"x": xs}
        h1 = norm_mod(xs, norm1_w[i:i + 1], sh1, sc1, n_ctx, f"l{i}_norm1")
        s["h1"] = h1
        if i % 2 == 0:
            w_in = ssd_in_full(j)
            if i == 0:
                sub, its, used = subset(layer_items(0)[1:])
                zx, carried = matmul(h1, w_in, "nn", f"l{i}_ssd_in", carry=gather_carry(sub, its))
                put_back(used, weights_forward(carried, its, "forward_weights_l0"))
            else:
                zx = matmul(h1, w_in, "nn", f"l{i}_ssd_in")
        sub, its, used = subset(layer_items(i + 1)) if i + 1 < depth else (None, None, None)
        nxt = gather_carry(sub, its) if sub is not None else None
        if i % 2 == 0:
            xbc = dwconv_act(zx, d_inner, xbc_w, wb_ssd[j], n_ctx, "silu", F32, f"l{i}_ssd_conv")
            _, da, dtg, ag, agt = ssd_prep(zx, dt_col, bias128[j], alog128[j], sel, heads, f"l{i}_ssd_prep")
            yf, hf, *carried = ssd_scan(xbc, dtg, ag, agt, d_inner, n_ctx, False, f"l{i}_ssd_scan_f", carry=nxt)
            if nxt is not None:
                put_back(used, weights_forward(carried, its, f"forward_weights_l{i + 1}"))
            yb, hb = ssd_scan(xbc, dtg, ag, agt, d_inner, n_ctx, True, f"l{i}_ssd_scan_b")
            ytot, yn = ssd_gate_norm(yf, yb, xbc, zx, dskip[j], ssd_norm_w[j:j + 1], f"l{i}_ssd_gate_norm")
            x1, mix = matmul_gate_res(yn, w_row("ssd_w_out", j), xs, g1, n_ctx, f"l{i}_ssd_out")
            s.update(w_in=w_in, zx=zx, xbc=xbc, da=da, dtg=dtg, ag=ag, agt=agt, hf=hf, hb=hb, ytot=ytot, yn=yn)
        else:
            qkv = matmul(h1, w_col("attn_w_qkv", j), "nn", f"l{i}_attn_qkv")
            qkvr = qk_prep(qkv, gains[j], cos, sin, n_qh, ATTN_KV_HEADS, f"l{i}_qk_prep")
            o, lse, *carried = attention(qkvr, sinks3[j], n_ctx, n_qh, f"l{i}_attn", carry=nxt)
            if nxt is not None:
                put_back(used, weights_forward(carried, its, f"forward_weights_l{i + 1}"))
            x1, mix = matmul_gate_res(o, w_row("attn_w_o", j), xs, g1, n_ctx, f"l{i}_attn_out")
            s.update(qkv=qkv, qkvr=qkvr, o=o, lse=lse)
        h2 = norm_mod(x1, norm2_w[i:i + 1], sh2, sc2, n_ctx, f"l{i}_norm2")
        u = matmul(h2, w_col("ffn_w_up", i), "nn", f"l{i}_ffn_up", out_dtype=BF16)
        act = dwconv_act(u, 0, 2 * d_ff, wb_ffn[i], n_ctx, "glu", BF16, f"l{i}_ffn_conv")
        x2, f = matmul_gate_res(act, w_row("ffn_w_down", i), x1, g2, n_ctx, f"l{i}_ffn_down")
        s.update(mix=mix, x1=x1, h2=h2, u=u, act=act, f=f)
        saved.append(s)
        xs = x2

    dxs, sq = loss_grad(xs, loss_target[0], n_ctx, "loss")
    loss = lax.psum(0.5 / d * jnp.sum(sq[0]), ("x", "y", "c"))

    gbuf = {n: None for n, _ in BIG}
    gshape = {n: ((N_CHIP, b.shape[0] * b.shape[2], b.shape[3]), b.shape[2]) for (n, _), b in zip(BIG, bufs)}

    def dw_into(n, kind, layer, a, b, name):
        shape, rows_per_layer = gshape[n]
        gbuf[n] = matmul(a, b, "tn", name, out_dtype=BF16, into=(kind, gbuf[n], shape, layer * rows_per_layer))

    ssd_in_g = [None] * n_ssd
    reduced, reduced_items = {}, []

    def start_rs(layer):
        gs, items, dest = [], [], []
        for b, l in layer_items(layer):
            n = BIG[b][0]
            if n == "ssd_w_in":
                gs.append(ssd_in_g[l].reshape(d, N_CHIP, -1).transpose(1, 0, 2).astype(BF16))
                items.append((len(gs) - 1, 0, d))
                dest.append(((n, l), (d, gs[-1].shape[2]), 0))
            else:
                shape, rows_per_layer = gshape[n]
                gs.append(gbuf[n])
                items.append((len(gs) - 1, l * rows_per_layer, rows_per_layer))
                dest.append((n, shape[1:], l * rows_per_layer))
        got = grads_pair_exchange(gs, items, f"rs_pair_exchange_l{layer}")
        both = [pair_add(gs[a], o, ic, chip, row0, f"rs_pair_add_l{layer}_{q}") for q, ((a, row0, _), o) in enumerate(zip(items, got))]
        return [p for p, _ in both], [ld for _, ld in both], dest

    def finish_rs(pend, carried):
        if not pend:
            return None
        landed = carried[len(pend[0]):]
        for q, (a, (key, shape, row0)) in enumerate(zip(landed, pend[2])):
            reduced[key] = sum_chips(a, ic, reduced.get(key), shape, row0, f"rs_chip_sum_{q}_r{row0}_{key if isinstance(key, str) else key[0] + str(key[1])}")
            reduced_items.append((key, row0, 2 * a.shape[1]))
        return None

    pending = None
    st_norm1, st_norm2, st_gate1, st_gate2 = ([None] * depth for _ in range(4))
    st_sconv, st_snorm, st_sd, st_sdt = ([None] * n_ssd for _ in range(4))
    st_gain, st_sink = [None] * n_att, [None] * n_att
    st_fconv = [None] * depth
    for i in reversed(range(depth)):
        j = i // 2
        s = saved[i]
        sh1, sc1, g1, sh2, sc2, g2 = [mods[i, :, q] for q in range(6)]
        df, st_gate2[i] = gate_bwd(dxs, s["f"], g2, n_ctx, f"l{i}_ffn_gate_bwd")
        dact = matmul(df, w_row("ffn_w_down", i), "nt", f"l{i}_ffn_down_dx")
        dw_into("ffn_w_down", "row", i, s["act"], df, f"l{i}_ffn_down_dw")
        du3, st_fconv[i] = dwconv_act_bwd(dact, 0, s["u"], 0, wb_ffn[i], 0, d_ff, n_ctx, "glu", f"l{i}_ffn_conv_bwd")
        du = Mat(du3, "cols3", nparts=2)
        dh2 = matmul(du, w_col("ffn_w_up", i), "nt", f"l{i}_ffn_up_dx")
        dw_into("ffn_w_up", "col", i, s["h2"], du, f"l{i}_ffn_up_dw")
        dx1, st_norm2[i] = norm_mod_bwd(dh2, dxs, s["x1"], norm2_w[i:i + 1], sc2, n_ctx, f"l{i}_norm2_bwd")
        dmix, st_gate1[i] = gate_bwd(dx1, s["mix"], g1, n_ctx, f"l{i}_mix_gate_bwd")
        if i % 2 == 0:
            zx = s["zx"]
            dyn = matmul(dmix, w_row("ssd_w_out", j), "nt", f"l{i}_ssd_out_dx")
            dw_into("ssd_w_out", "row", j, s["yn"], dmix, f"l{i}_ssd_out_dw")
            dy, dzx, st_snorm[j] = ssd_gate_norm_bwd(dyn, s["ytot"], zx, ssd_norm_w[j:j + 1], f"l{i}_ssd_gate_norm_bwd")
            o1 = ssd_scan_bwd(dy, s["xbc"], s["dtg"], s["ag"], s["agt"], s["hf"], dskip[j], None, d_inner, n_ctx, False, f"l{i}_ssd_scan_f_bwd",
                              carry=rs_carry(*pending[:2]) if pending else None)
            o1, pending = o1[:7], finish_rs(pending, o1[7:])
            o2 = ssd_scan_bwd(dy, s["xbc"], s["dtg"], s["ag"], s["agt"], s["hb"], None, o1[:3], d_inner, n_ctx, True, f"l{i}_ssd_scan_b_bwd")
            st_sd[j] = o1[6]
            gn = SSD_GROUPS * SSD_STATE
            conv_st = []
            for src, width, col, tag in ((o2[0], d_inner, 0, "x"), (o2[1], gn, d_inner, "b"), (o2[2], gn, d_inner + gn, "c")):
                dzx, st = dwconv_act_bwd(src, 0, zx, d_inner + col, wb_ssd[j], col, width, n_ctx, "silu",
                                         f"l{i}_ssd_conv_bwd_{tag}", into=dzx, ocol0=d_inner + col)
                conv_st.append(st)
            st_sconv[j] = jnp.concatenate(conv_st, axis=1)
            dzx, st_sdt[j] = ssd_prep_bwd(zx, dt_col, bias128[j], alog128[j], sel, o1[3:6], o2[3:6], s["da"], heads, dzx, f"l{i}_ssd_prep_bwd")
            dh1 = matmul(dzx, s["w_in"], "nt", f"l{i}_ssd_in_dx")
            ssd_in_g[j] = matmul(s["h1"], dzx, "tn", f"l{i}_ssd_in_dw")
        else:
            do = matmul(dmix, w_row("attn_w_o", j), "nt", f"l{i}_attn_out_dx")
            dw_into("attn_w_o", "row", j, s["o"], dmix, f"l{i}_attn_out_dw")
            ab = attention_bwd(do, s["o"], s["lse"], s["qkvr"], sinks3[j], n_ctx, n_qh, f"l{i}_attn_bwd",
                               carry=rs_carry(*pending[:2]) if pending else None)
            (dq, dkc, dvc, dkp, dvp, st_sink[j]), pending = ab[:6], finish_rs(pending, ab[6:])
            dk = band_reduce(dkc, dkp, n_ctx, f"l{i}_dk_reduce")
            dv = band_reduce(dvc, dvp, n_ctx, f"l{i}_dv_reduce")
            dqkv, st_gain[j] = qk_prep_bwd(dq, dk, dv, s["qkv"], gains[j], cos, sin, f"l{i}_qk_prep_bwd")
            dh1 = matmul(dqkv, w_col("attn_w_qkv", j), "nt", f"l{i}_attn_qkv_dx")
            dw_into("attn_w_qkv", "col", j, s["h1"], dqkv, f"l{i}_attn_qkv_dw")
        dxs, st_norm1[i] = norm_mod_bwd(dh1, dx1, s["x"], norm1_w[i:i + 1], sc1, n_ctx, f"l{i}_norm1_bwd")
        pending = start_rs(i)
    pairs, lands, _ = pending
    finish_rs(pending, list(pairs) + list(grads_chip_exchange(pairs, lands, "rs_chip_exchange_l0")))
    grad_x = dxs[n_ctx:][None]

    rows_d = ([st_norm1[i][4:5] for i in range(depth)] + [st_norm2[i][4:5] for i in range(depth)]
              + [st[seg:seg + 1] for seg in (0, 1) for i in range(depth)
                 for st in (st_norm1[i][0:2], st_norm1[i][2:4], st_gate1[i][0:2], st_norm2[i][0:2], st_norm2[i][2:4], st_gate2[i][0:2])])
    a_d = jnp.concatenate(rows_d, axis=0)
    a_sconv = _rows8(jnp.concatenate([st[0:4] for st in st_sconv], axis=0))
    a_fconv = _rows8(jnp.concatenate([jnp.concatenate([st[0, 0:4], st[1, 0:4]], axis=1) for st in st_fconv], axis=0))
    a_di = _rows8(jnp.concatenate([st[0:1] for st in st_snorm] + [st[0:1] for st in st_sd], axis=0))
    a_128 = _rows8(jnp.concatenate([st[0:2] for st in st_sdt] + [st[0:2] for st in st_gain] + [st[:, 0] for st in st_sink], axis=0))
    gathered = small_allgather([a_d, a_sconv, a_fconv, a_di, a_128], "gather_small_grads")
    s_d, s_sconv, s_fconv, s_di, s_128 = [sum_leading(g, f"sum_small_grads_{q}") for q, g in enumerate(gathered)]
    grads = {"norm1_w": s_d[0:depth], "norm2_w": s_d[depth:2 * depth]}
    dctx_sum = s_d[2 * depth:8 * depth].reshape(depth, 6 * d)
    grads["ada_b"] = dctx_sum + s_d[8 * depth:14 * depth].reshape(depth, 6 * d)
    sc = s_sconv[:4 * n_ssd].reshape(n_ssd, 4, -1)
    own_cols = lambda a, width: lax.dynamic_slice_in_dim(a, chip * width, width, axis=a.ndim - 1)
    grads["ssd_conv_w"], grads["ssd_conv_b"] = own_cols(sc[:, 0:3], ssd_conv_w.shape[-1]), sc[:, 3]
    fc = s_fconv[:4 * depth].reshape(depth, 4, -1)
    grads["ffn_conv_w"], grads["ffn_conv_b"] = own_cols(fc[:, 0:3], ffn_conv_w.shape[-1]), fc[:, 3]
    grads["ssd_norm_w"] = s_di[0:n_ssd]
    grads["ssd_d"] = jnp.sum(s_di[n_ssd:2 * n_ssd].reshape(n_ssd, heads, SSD_HEAD_DIM), axis=-1)
    dt_st = s_128[0:2 * n_ssd].reshape(n_ssd, 2, LANES)
    grads["ssd_dt_bias_f"], grads["ssd_dt_bias_b"] = dt_st[:, 0, :heads], dt_st[:, 0, heads:]
    grads["ssd_a_log_f"], grads["ssd_a_log_b"] = dt_st[:, 1, :heads], dt_st[:, 1, heads:]
    gain_st = s_128[2 * n_ssd:2 * n_ssd + 2 * n_att].reshape(n_att, 2, LANES)
    grads["attn_q_gain"], grads["attn_k_gain"] = gain_st[:, 0], gain_st[:, 1]
    sink_st = s_128[2 * n_ssd + 2 * n_att:2 * n_ssd + 2 * n_att + ATTN_KV_HEADS * n_att].reshape(n_att, ATTN_KV_HEADS, LANES)
    grads["attn_sinks"] = sink_st[:, :, :grp].reshape(n_att, n_qh)

    dlat_rows = gathered[0][:, 8 * depth:14 * depth].reshape(N_DEV, depth, 6 * d)
    gmod = jnp.concatenate([dlat_rows, dctx_sum[None], jnp.zeros((ADA_ROWS - N_DEV - 1, depth, 6 * d), F32)], axis=0).transpose(1, 0, 2)
    grads["ada_w"], dcs = ada_bwd(cs16, own_cols(gmod, ns_ada), ada_w, "ada_bwd")
    dcc = (dcs[N_DEV] * _dsilu(c_ctx))[None, :]
    (g_dcc,) = small_allgather([_rows8(dcc)], "gather_dc_ctx")
    grads["c_ctx"] = sum_leading(g_dcc[0::2], "sum_dc_ctx")[0]

    keys = list(reduced)
    red = dict(zip(keys, halves_exchange([reduced[k] for k in keys], [(keys.index(k), r0, nr) for k, r0, nr in reduced_items],
                                         "rs_halves_exchange")))
    for n, _ in BIG:
        grads[n] = jnp.stack([red[(n, l)] for l in range(n_ssd)]) if n == "ssd_w_in" else red[n].reshape(w[n].shape)

    delta, new_m, new_v = {}, {}, {}
    for n in WEIGHTS:
        shp = w[n].shape
        two = lambda a: a.reshape(-1, shp[-1])
        dl, nm, nv = adamw(two(w[n]), two(grads[n]), two(mom[n]), two(var[n]), f"adamw_{n}")
        delta[n], new_m[n], new_v[n] = dl.reshape(shp), nm.reshape(shp), nv.reshape(shp)
    grads = {n: grads[n].reshape(w[n].shape) for n in WEIGHTS}
    return (loss, grad_x, *[grads[n] for n in WEIGHTS], *[delta[n] for n in WEIGHTS], *[new_m[n] for n in WEIGHTS], *[new_v[n] for n in WEIGHTS])
```

```python
import functools
import math

import jax
import jax.numpy as jnp
from jax import lax
from jax.experimental import pallas as pl
from jax.experimental.pallas import tpu as pltpu

F32 = jnp.float32
BF16 = jnp.bfloat16

SSD_HEAD_DIM = 64
SSD_GROUPS = 8
SSD_STATE = 128
CHUNK = 128
ATTN_HEAD_DIM = 128
ATTN_KV_HEADS = 4
GRID_W = 64
ROPE_THETA = 10000.0
NORM_EPS = 1e-6
ADAM_LR, ADAM_B1, ADAM_B2, ADAM_EPS, ADAM_WD, ADAM_STEP = 0.001, 0.9, 0.999, 1e-08, 0.01, 10

LANES = 128
SUBLANES = 8
VMEM_LIMIT = 56 * 1024 * 1024
MESH = pl.DeviceIdType.MESH
N_DEV = 8
N_CHIP = 4


def _cp(**kw):
    return pltpu.CompilerParams(vmem_limit_bytes=VMEM_LIMIT, **kw)


def _pallas(body, name, grid, in_specs, out_specs, out_shape, scratch, args, dims, aliases=None, carry=None):
    in_specs, out_specs, out_shape, scratch, args = list(in_specs), list(out_specs), list(out_shape), list(scratch), list(args)
    aliases = dict(aliases or {})
    if carry is not None:
        bufs, n_sems, sends, recv_waits = carry
        n_in, n_out, nb = len(in_specs), len(out_specs), len(bufs)
        inner = body

        def body(*refs):
            ins, outs = refs[:n_in], refs[n_in + nb:n_in + nb + n_out]
            cbufs = refs[n_in + nb + n_out:n_in + 2 * nb + n_out]
            rest = refs[n_in + 2 * nb + n_out:]
            sems = rest[-2:]
            ids = [pl.program_id(q) for q in range(len(grid))]
            first = functools.reduce(jnp.logical_and, [i == 0 for i in ids])
            last = functools.reduce(jnp.logical_and, [i == g - 1 for i, g in zip(ids, grid)])

            @pl.when(first)
            def _():
                for cp in sends(cbufs, *sems):
                    cp.start()

            inner(*ins, *outs, *rest[:-2])

            @pl.when(last)
            def _():
                recv_waits(cbufs, *sems)
                for cp in sends(cbufs, *sems):
                    cp.wait_send()

        aliases.update({n_in + q: n_out + q for q in range(nb)})
        in_specs += [pl.BlockSpec(memory_space=pltpu.HBM)] * nb
        out_specs += [pl.BlockSpec(memory_space=pltpu.HBM)] * nb
        out_shape += [jax.ShapeDtypeStruct(b.shape, b.dtype) for b in bufs]
        scratch += [pltpu.SemaphoreType.DMA((n_sems,))] * 2
        args += list(bufs)
    return pl.pallas_call(
        body, name=name, grid=grid, in_specs=in_specs, out_specs=out_specs, out_shape=out_shape, scratch_shapes=scratch,
        input_output_aliases=aliases, compiler_params=_cp(dimension_semantics=dims),
    )(*args)


def _pick(n, cands):
    for c in cands:
        if n % c == 0:
            return c
    return n


def _silu(x):
    return x * jax.nn.sigmoid(x)


def _dsilu(x):
    s = jax.nn.sigmoid(x)
    return s * (1.0 + x * (1.0 - s))


_DIMS = {"nn": (((1,), (0,)), ((), ())), "nt": (((1,), (1,)), ((), ())), "tn": (((0,), (0,)), ((), ()))}


TILES_M = (1408, 768, 512, 384, 256, 128)
TILES_N = (1408, 1024, 1152, 768, 512, 384, 256, 128)
TILES_K = (2048, 1408, 1152, 1024, 768, 512, 384, 256, 128)


class Mat:
    def __init__(self, arr, kind="plain", rows=None, row0=0, base=0, nparts=1):
        self.arr, self.kind, self.row0, self.base, self.nparts = arr, kind, row0, base, nparts
        if kind == "cols3":
            self.r, self.s = arr.shape[1], arr.shape[2] * nparts
        else:
            self.r, self.s = (rows if rows is not None else arr.shape[0]), arr.shape[1]

    def s_unit(self):
        return self.s // self.nparts

    def spec(self, tr, ts, r_of, s_of):
        if self.kind == "cols3":
            nps = self.s // self.nparts // ts
            return pl.BlockSpec((None, tr, ts), lambda *ids: (self.base + s_of(*ids) // nps, r_of(*ids), s_of(*ids) % nps))
        off = self.row0 // tr
        return pl.BlockSpec((tr, ts), lambda *ids: (off + r_of(*ids), s_of(*ids)))


MATMUL_VMEM_BUDGET = 36 * 1024 * 1024


def _fit_tiles(m, n, k, m_unit, n_unit, k_unit, result_bytes):
    cands = [[c for c in tiles if dim % c == 0 and unit % c == 0]
             for tiles, dim, unit in ((TILES_M, m, m_unit), (TILES_N, n, n_unit), (TILES_K, k, k_unit))]
    assert all(cands), (m, n, k, m_unit, n_unit, k_unit)
    idx = [0, 0, 0]
    while True:
        tm, tn, tk = (c[i] for c, i in zip(cands, idx))
        if 2 * 2 * (tm * tk + tk * tn) + tm * tn * (4 + 2 * result_bytes) <= MATMUL_VMEM_BUDGET:
            return tm, tn, tk
        shrinkable = [q for q in range(3) if idx[q] + 1 < len(cands[q])]
        assert shrinkable, (m, n, k)
        q = max(shrinkable, key=lambda q: cands[q][idx[q]])
        idx[q] += 1


def matmul(a, b, mode, name, out_dtype=F32, into=None, carry=None):
    a = a if isinstance(a, Mat) else Mat(a)
    b = b if isinstance(b, Mat) else Mat(b)
    m, k = (a.s, a.r) if mode == "tn" else (a.r, a.s)
    n, kb = (b.r, b.s) if mode == "nt" else (b.s, b.r)
    assert k == kb, (name, a.r, a.s, b.r, b.s)
    m_unit = a.s_unit() if mode == "tn" else m
    k_unit = math.gcd(a.s_unit() if mode != "tn" else k, b.s_unit() if mode == "nt" else k)
    n_unit = n if mode == "nt" else b.s_unit()
    if into is not None:
        kind, buf, shape, row0 = into
        if kind == "col":
            n_unit = math.gcd(n_unit, shape[2])
        else:
            m_unit = math.gcd(m_unit, m // N_CHIP)
    tm, tn, tk = _fit_tiles(m, n, k, m_unit, n_unit, k_unit, jnp.dtype(out_dtype).itemsize)
    nk = k // tk
    ii, jj, kk_ = (lambda j, i, kk: i), (lambda j, i, kk: j), (lambda j, i, kk: kk)
    a_spec = a.spec(tk, tm, kk_, ii) if mode == "tn" else a.spec(tm, tk, ii, kk_)
    b_spec = b.spec(tn, tk, jj, kk_) if mode == "nt" else b.spec(tk, tn, kk_, jj)
    in_specs, args, aliases = [a_spec, b_spec], [a.arr, b.arr], {}
    if into is None:
        out_spec = pl.BlockSpec((tm, tn), lambda j, i, kk: (i, j))
        out_shape = jax.ShapeDtypeStruct((m, n), out_dtype)
    else:
        assert row0 % tm == 0
        r0 = row0 // tm
        if kind == "col":
            npn = shape[2] // tn
            out_spec = pl.BlockSpec((None, tm, tn), lambda j, i, kk: (j // npn, r0 + i, j % npn))
        else:
            npm = m // N_CHIP // tm
            out_spec = pl.BlockSpec((None, tm, tn), lambda j, i, kk: (i // npm, r0 + i % npm, j))
        out_shape = jax.ShapeDtypeStruct(shape, out_dtype)
        if buf is not None:
            in_specs.append(pl.BlockSpec(memory_space=pl.ANY))
            args.append(buf)
            aliases = {2: 0}

    def body(a_ref, b_ref, *rest):
        o_ref, acc_ref = rest[-2:]
        kk = pl.program_id(2)
        part = lax.dot_general(a_ref[...], b_ref[...], _DIMS[mode], preferred_element_type=F32)
        if nk == 1:
            o_ref[...] = part.astype(out_dtype)
        else:

            @pl.when(kk == 0)
            def _():
                acc_ref[...] = part

            @pl.when(kk > 0)
            def _():
                acc_ref[...] += part

            @pl.when(kk == nk - 1)
            def _():
                o_ref[...] = acc_ref[...].astype(out_dtype)

    res = _pallas(body, name, (n // tn, m // tm, nk), in_specs, [out_spec], [out_shape],
                  [pltpu.VMEM((tm, tn) if nk > 1 else (SUBLANES, LANES), F32)], args, ("parallel", "parallel", "arbitrary"),
                  aliases=aliases, carry=carry)
    return res[0] if carry is None else (res[0], res[1:])


def matmul_gate_res(a, w, res, gate, n_ctx, name):
    w = w if isinstance(w, Mat) else Mat(w)
    (m, k), n = a.shape, w.s
    assert k == w.r
    tm, tn, tk = _fit_tiles(m, n, k, m, w.s_unit(), k, 3 * 4)
    nk = k // tk

    def body(a_ref, b_ref, r_ref, g_ref, x_ref, y_ref, acc_ref):
        kk = pl.program_id(2)
        row0 = pl.program_id(1) * tm
        part = jnp.dot(a_ref[...], b_ref[...], preferred_element_type=F32)

        @pl.when(kk == 0)
        def _():
            acc_ref[...] = part

        @pl.when(kk > 0)
        def _():
            acc_ref[...] += part

        @pl.when(kk == nk - 1)
        def _():
            y = acc_ref[...]
            row = row0 + lax.broadcasted_iota(jnp.int32, (tm, 1), 0)
            g = jnp.where(row < n_ctx, g_ref[0:1, :], g_ref[1:2, :])
            y_ref[...] = y.astype(y_ref.dtype)
            x_ref[...] = r_ref[...] + g * y

    return pl.pallas_call(
        body, name=name, grid=(n // tn, m // tm, nk),
        in_specs=[pl.BlockSpec((tm, tk), lambda j, i, kk: (i, kk)), w.spec(tk, tn, lambda j, i, kk: kk, lambda j, i, kk: j),
                  pl.BlockSpec((tm, tn), lambda j, i, kk: (i, j)), pl.BlockSpec((2, tn), lambda j, i, kk: (0, j))],
        out_specs=[pl.BlockSpec((tm, tn), lambda j, i, kk: (i, j)), pl.BlockSpec((tm, tn), lambda j, i, kk: (i, j))],
        out_shape=[jax.ShapeDtypeStruct((m, n), F32), jax.ShapeDtypeStruct((m, n), F32)],
        scratch_shapes=[pltpu.VMEM((tm, tn), F32)],
        compiler_params=_cp(dimension_semantics=("parallel", "parallel", "arbitrary")),
    )(a, w.arr, res, gate)


ROW_TILE = 256


def _seg_row(ref2, i, n_ctx_tiles):
    return jnp.where(i < n_ctx_tiles, ref2[0:1, :], ref2[1:2, :])


def _acc_rows(ref, step, rows):
    @pl.when(step == 0)
    def _():
        ref[...] = jnp.zeros_like(ref)

    for r, v in enumerate(rows):
        ref[r:r + 1, :] += v


def norm_mod(x, w, shift, scale, n_ctx, name):
    t, d = x.shape
    tm = _pick(n_ctx, (ROW_TILE, 128))
    nct = n_ctx // tm

    def body(x_ref, w_ref, sh_ref, sc_ref, h_ref):
        i = pl.program_id(0)
        xv = x_ref[...]
        r = lax.rsqrt(jnp.mean(xv * xv, axis=-1, keepdims=True) + NORM_EPS)
        h_ref[...] = ((xv * r) * w_ref[...] * (1.0 + _seg_row(sc_ref, i, nct)) + _seg_row(sh_ref, i, nct)).astype(BF16)

    return pl.pallas_call(
        body, name=name, grid=(t // tm,),
        in_specs=[pl.BlockSpec((tm, d), lambda i: (i, 0)), pl.BlockSpec((1, d), lambda i: (0, 0)),
                  pl.BlockSpec((2, d), lambda i: (0, 0)), pl.BlockSpec((2, d), lambda i: (0, 0))],
        out_specs=pl.BlockSpec((tm, d), lambda i: (i, 0)),
        out_shape=jax.ShapeDtypeStruct((t, d), BF16), compiler_params=_cp(),
    )(x, w, shift, scale)


def norm_mod_bwd(dh, dres, x, w, scale, n_ctx, name):
    t, d = x.shape
    tm = _pick(n_ctx, (ROW_TILE, 128))
    nct = n_ctx // tm

    def body(dh_ref, dr_ref, x_ref, w_ref, sc_ref, dx_ref, st_ref):
        i = pl.program_id(0)
        xv, g = x_ref[...], dh_ref[...]
        r = lax.rsqrt(jnp.mean(xv * xv, axis=-1, keepdims=True) + NORM_EPS)
        xn = xv * r
        one_sc = 1.0 + _seg_row(sc_ref, i, nct)
        dxn = g * (w_ref[...] * one_sc)
        dx_ref[...] = dr_ref[...] + r * (dxn - xn * jnp.mean(dxn * xn, axis=-1, keepdims=True))
        gx = g * xn
        s_shift = jnp.sum(g, axis=0, keepdims=True)
        s_scale = jnp.sum(gx * w_ref[...], axis=0, keepdims=True)
        s_w = jnp.sum(gx * one_sc, axis=0, keepdims=True)
        _acc_rows(st_ref, i, [jnp.where(i < nct, s_shift, 0.0), jnp.where(i < nct, 0.0, s_shift),
                              jnp.where(i < nct, s_scale, 0.0), jnp.where(i < nct, 0.0, s_scale), s_w])

    return pl.pallas_call(
        body, name=name, grid=(t // tm,),
        in_specs=[pl.BlockSpec((tm, d), lambda i: (i, 0)), pl.BlockSpec((tm, d), lambda i: (i, 0)),
                  pl.BlockSpec((tm, d), lambda i: (i, 0)), pl.BlockSpec((1, d), lambda i: (0, 0)),
                  pl.BlockSpec((2, d), lambda i: (0, 0))],
        out_specs=[pl.BlockSpec((tm, d), lambda i: (i, 0)), pl.BlockSpec((SUBLANES, d), lambda i: (0, 0))],
        out_shape=[jax.ShapeDtypeStruct((t, d), F32), jax.ShapeDtypeStruct((SUBLANES, d), F32)],
        compiler_params=_cp(dimension_semantics=("arbitrary",)),
    )(dh, dres, x, w, scale)


def gate_bwd(dx, y, gate, n_ctx, name):
    t, d = dx.shape
    tm = _pick(n_ctx, (ROW_TILE, 128))
    nct = n_ctx // tm

    def body(dx_ref, y_ref, g_ref, dy_ref, dg_ref):
        i = pl.program_id(0)
        dxv = dx_ref[...]
        dy_ref[...] = (dxv * _seg_row(g_ref, i, nct)).astype(BF16)
        s = jnp.sum(dxv * y_ref[...], axis=0, keepdims=True)
        _acc_rows(dg_ref, i, [jnp.where(i < nct, s, 0.0), jnp.where(i < nct, 0.0, s)])

    return pl.pallas_call(
        body, name=name, grid=(t // tm,),
        in_specs=[pl.BlockSpec((tm, d), lambda i: (i, 0)), pl.BlockSpec((tm, d), lambda i: (i, 0)),
                  pl.BlockSpec((2, d), lambda i: (0, 0))],
        out_specs=[pl.BlockSpec((tm, d), lambda i: (i, 0)), pl.BlockSpec((SUBLANES, d), lambda i: (0, 0))],
        out_shape=[jax.ShapeDtypeStruct((t, d), BF16), jax.ShapeDtypeStruct((SUBLANES, d), F32)],
        compiler_params=_cp(dimension_semantics=("arbitrary",)),
    )(dx, y, gate)


def _seg_edges(i, tm, n_ctx, t):
    first = (i == 0) | (i == n_ctx // tm)
    last = (i == n_ctx // tm - 1) | (i == t // tm - 1)
    return first, last


def _shift_rows(x, prev_row, next_row, first, last):
    tm = x.shape[0]
    row = lax.broadcasted_iota(jnp.int32, (tm, 1), 0)
    xp = jnp.where(row == 0, jnp.where(first, 0.0, prev_row), pltpu.roll(x, 1, 0))
    xn = jnp.where(row == tm - 1, jnp.where(last, 0.0, next_row), pltpu.roll(x, tm - 1, 0))
    return xp, xn


def _halo_rows(dtype):
    return SUBLANES * 4 // jnp.dtype(dtype).itemsize


def _halo_specs(tm, tc, t, col, hb):
    r, nblk = tm // hb, t // hb
    return [pl.BlockSpec((tm, tc), lambda j, i: (i, col(j))),
            pl.BlockSpec((hb, tc), lambda j, i: (jnp.maximum(i * r - 1, 0), col(j))),
            pl.BlockSpec((hb, tc), lambda j, i: (jnp.minimum((i + 1) * r, nblk - 1), col(j)))]


def _conv_rows(w_ref, xm, prev, nxt, first, last):
    hb, tm = prev.shape[0], xm.shape[0]
    w0, w1, w2, b = w_ref[0:1, :], w_ref[1:2, :], w_ref[2:3, :], w_ref[3:4, :]
    xp, xn = _shift_rows(xm, prev[hb - 1:hb, :], nxt[0:1, :], first, last)
    pre = w0 * xp + w1 * xm + w2 * xn + b
    pre_before = w0 * prev[hb - 2:hb - 1, :] + w1 * prev[hb - 1:hb, :] + w2 * xm[0:1, :] + b
    pre_after = w0 * xm[tm - 1:tm, :] + w1 * nxt[0:1, :] + w2 * nxt[1:2, :] + b
    return pre, pre_before, pre_after


def dwconv_act(x, col0, c, wb, n_ctx, mode, act_dtype, name):
    t = x.shape[0]
    tm = _pick(n_ctx, (ROW_TILE, 128))
    nparts = 2 if mode == "glu" else 1
    cw = c // nparts
    tc = _pick(math.gcd(cw, col0), (512, 384, 256, 128))
    hb = _halo_rows(x.dtype)

    def body(*refs):
        first, last = _seg_edges(pl.program_id(1), tm, n_ctx, t)
        pres = [_conv_rows(refs[4 * p + 3], *[r[...].astype(F32) for r in refs[4 * p:4 * p + 3]], first, last)[0] for p in range(nparts)]
        refs[-1][...] = (_silu(pres[0]) if mode == "silu" else _silu(pres[1]) * pres[0]).astype(act_dtype)

    in_specs, args = [], []
    for p in range(nparts):
        in_specs += _halo_specs(tm, tc, t, lambda j, p=p: (col0 + p * cw) // tc + j, hb)
        in_specs.append(pl.BlockSpec((SUBLANES, tc), lambda j, i, p=p: (0, p * cw // tc + j)))
        args += [x, x, x, wb]
    return pl.pallas_call(
        body, name=name, grid=(cw // tc, t // tm), in_specs=in_specs, out_specs=pl.BlockSpec((tm, tc), lambda j, i: (i, j)),
        out_shape=jax.ShapeDtypeStruct((t, cw), act_dtype), compiler_params=_cp(dimension_semantics=("parallel", "parallel")),
    )(*args)


def dwconv_act_bwd(dact, dcol0, x, xcol0, wb, wcol0, cw, n_ctx, mode, name, into=None, ocol0=0):
    t = x.shape[0]
    tm = _pick(n_ctx, (ROW_TILE, 128))
    nparts = 2 if mode == "glu" else 1
    tc = _pick(math.gcd(cw, dcol0, xcol0, wcol0, ocol0), (512, 384, 256, 128))
    hb_d, hb_x = _halo_rows(dact.dtype), _halo_rows(x.dtype)

    def dpre_of(dact_v, pres_v):
        if mode == "silu":
            return [dact_v * _dsilu(pres_v[0])]
        val, gat = pres_v
        sg = jax.nn.sigmoid(gat)
        return [dact_v * (gat * sg), dact_v * val * (sg * (1.0 + gat * (1.0 - sg)))]

    def body(*refs):
        i = pl.program_id(1)
        first, last = _seg_edges(i, tm, n_ctx, t)
        da = [r[...].astype(F32) for r in refs[0:3]]
        xs = [[r[...].astype(F32) for r in refs[3 + 3 * p:6 + 3 * p]] for p in range(nparts)]
        ws = refs[3 + 3 * nparts:3 + 4 * nparts]
        dx_ref, dw_ref = refs[-2:]
        pres = [_conv_rows(ws[p], *xs[p], first, last) for p in range(nparts)]
        dm = dpre_of(da[0], [pr[0] for pr in pres])
        d_before = dpre_of(da[1][hb_d - 1:hb_d, :], [pr[1] for pr in pres])
        d_after = dpre_of(da[2][0:1, :], [pr[2] for pr in pres])
        for p in range(nparts):
            d_prev, d_next = _shift_rows(dm[p], d_before[p], d_after[p], first, last)
            w_ref, xv = ws[p], xs[p][0]
            dxv = (w_ref[0:1, :] * d_next + w_ref[1:2, :] * dm[p] + w_ref[2:3, :] * d_prev).astype(BF16)
            if mode == "glu":
                dx_ref[p] = dxv
            else:
                dx_ref[...] = dxv
            _acc_rows(dw_ref.at[p] if mode == "glu" else dw_ref, i,
                      [jnp.sum(d_next * xv, axis=0, keepdims=True), jnp.sum(dm[p] * xv, axis=0, keepdims=True),
                       jnp.sum(d_prev * xv, axis=0, keepdims=True), jnp.sum(dm[p], axis=0, keepdims=True)])

    in_specs = _halo_specs(tm, tc, t, lambda j: dcol0 // tc + j, hb_d)
    args = [dact] * 3
    for p in range(nparts):
        in_specs += _halo_specs(tm, tc, t, lambda j, p=p: (xcol0 + p * cw) // tc + j, hb_x)
        args += [x] * 3
    for p in range(nparts):
        in_specs.append(pl.BlockSpec((SUBLANES, tc), lambda j, i, p=p: (0, (wcol0 + p * cw) // tc + j)))
        args.append(wb)
    aliases = {}
    if mode == "glu":
        out_specs = [pl.BlockSpec((2, tm, tc), lambda j, i: (0, i, j)), pl.BlockSpec((2, SUBLANES, tc), lambda j, i: (0, 0, j))]
        out_shape = [jax.ShapeDtypeStruct((2, t, cw), BF16), jax.ShapeDtypeStruct((2, SUBLANES, cw), F32)]
    else:
        out_specs = [pl.BlockSpec((tm, tc), lambda j, i: (i, ocol0 // tc + j)), pl.BlockSpec((SUBLANES, tc), lambda j, i: (0, j))]
        out_shape = [jax.ShapeDtypeStruct((t, cw) if into is None else into.shape, BF16), jax.ShapeDtypeStruct((SUBLANES, cw), F32)]
        if into is not None:
            aliases = {len(args): 0}
            in_specs.append(pl.BlockSpec(memory_space=pl.ANY))
            args.append(into)
    return pl.pallas_call(
        body, name=name, grid=(cw // tc, t // tm), in_specs=in_specs, out_specs=out_specs, out_shape=out_shape,
        input_output_aliases=aliases, compiler_params=_cp(dimension_semantics=("parallel", "arbitrary")),
    )(*args)


HI = lax.Precision.HIGHEST
_NT = (((1,), (1,)), ((), ()))
_TN = (((0,), (0,)), ((), ()))


def _dot(a, b, dims=None, precision=None):
    if dims is None:
        return jnp.dot(a, b, preferred_element_type=F32, precision=precision)
    return lax.dot_general(a, b, dims, preferred_element_type=F32, precision=precision)


def _softplus(x):
    y = jnp.exp(-jnp.abs(x))
    u = 1.0 + y
    log1p = jnp.where(u == 1.0, y, y * jnp.log(u) / jnp.where(u == 1.0, 1.0, u - 1.0))
    return jnp.maximum(x, 0.0) + log1p


def _tri(n, upper):
    r = lax.broadcasted_iota(jnp.int32, (n, n), 0)
    c = lax.broadcasted_iota(jnp.int32, (n, n), 1)
    return ((r <= c) if upper else (r >= c)).astype(F32)


def _group_select(heads, e):
    g = jnp.arange(SSD_GROUPS)[:, None, None]
    src = jnp.arange(LANES)[None, :, None]
    dst = jnp.arange(LANES)[None, None, :]
    d, k = dst // e, dst % e
    return ((dst < 2 * e) & (src == d * heads + g * e + k)).astype(F32)


def ssd_prep(zx, col0, bias, alog, sel, heads, name):
    t = zx.shape[0]
    assert 2 * heads == LANES and col0 % LANES == 0
    nc = t // CHUNK

    def body(zx_ref, b_ref, al_ref, sel_ref, dt_ref, da_ref, dtg_ref, ag_ref, agt_ref):
        dt = _softplus(zx_ref[...] + b_ref[...])
        da = -jnp.exp(al_ref[...]) * dt
        lane = lax.broadcasted_iota(jnp.int32, (CHUNK, LANES), 1)
        a = jnp.where(lane < heads, _dot(_tri(CHUNK, False), da, precision=HI), _dot(_tri(CHUNK, True), da, precision=HI))
        dt_ref[...] = dt
        da_ref[...] = da
        for g in range(SSD_GROUPS):
            s = sel_ref[g]
            dtg_ref[g] = _dot(dt, s, precision=HI)
            a_g = _dot(a, s, precision=HI)
            ag_ref[g] = a_g
            agt_ref[g, 0] = a_g.T

    return pl.pallas_call(
        body, name=name, grid=(nc,),
        in_specs=[pl.BlockSpec((CHUNK, LANES), lambda c: (c, col0 // LANES)), pl.BlockSpec((1, LANES), lambda c: (0, 0)),
                  pl.BlockSpec((1, LANES), lambda c: (0, 0)), pl.BlockSpec((SSD_GROUPS, LANES, LANES), lambda c: (0, 0, 0))],
        out_specs=[pl.BlockSpec((CHUNK, LANES), lambda c: (c, 0)), pl.BlockSpec((CHUNK, LANES), lambda c: (c, 0)),
                   pl.BlockSpec((SSD_GROUPS, CHUNK, LANES), lambda c: (0, c, 0)), pl.BlockSpec((SSD_GROUPS, CHUNK, LANES), lambda c: (0, c, 0)),
                   pl.BlockSpec((SSD_GROUPS, 1, LANES, CHUNK), lambda c: (0, c, 0, 0))],
        out_shape=[jax.ShapeDtypeStruct((t, LANES), F32), jax.ShapeDtypeStruct((t, LANES), F32),
                   jax.ShapeDtypeStruct((SSD_GROUPS, t, LANES), F32), jax.ShapeDtypeStruct((SSD_GROUPS, t, LANES), F32),
                   jax.ShapeDtypeStruct((SSD_GROUPS, nc, LANES, CHUNK), F32)],
        compiler_params=_cp(),
    )(zx, bias, alog, sel)


def ssd_prep_bwd(zx, col0, bias, alog, sel, grads_f, grads_b, da_comp, heads, into, name):
    t = zx.shape[0]
    nc = t // CHUNK

    def body(zx_ref, b_ref, al_ref, sel_ref, ddtg_ref, dag_ref, dagt_ref, ddtg2_ref, dag2_ref, dagt2_ref, da_ref, _, draw_ref, st_ref):
        c = pl.program_id(0)
        ddt = jnp.zeros((CHUNK, LANES), F32)
        dacc = jnp.zeros((CHUNK, LANES), F32)
        for g in range(SSD_GROUPS):
            s = sel_ref[g]
            ddt += _dot(ddtg_ref[g] + ddtg2_ref[g], s, _NT, precision=HI)
            dacc += _dot(dag_ref[g] + dag2_ref[g] + (dagt_ref[g, 0] + dagt2_ref[g, 0]).T, s, _NT, precision=HI)
        lane = lax.broadcasted_iota(jnp.int32, (CHUNK, LANES), 1)
        dda = jnp.where(lane < heads, _dot(_tri(CHUNK, True), dacc, precision=HI), _dot(_tri(CHUNK, False), dacc, precision=HI))
        xin = zx_ref[...] + b_ref[...]
        ddt_tot = ddt - dda * jnp.exp(al_ref[...])
        draw = ddt_tot * jax.nn.sigmoid(xin)
        draw_ref[...] = draw.astype(BF16)
        _acc_rows(st_ref, c, [jnp.sum(draw, axis=0, keepdims=True), jnp.sum(dda * da_ref[...], axis=0, keepdims=True)])

    g3 = pl.BlockSpec((SSD_GROUPS, CHUNK, LANES), lambda c: (0, c, 0))
    g4 = pl.BlockSpec((SSD_GROUPS, 1, LANES, CHUNK), lambda c: (0, c, 0, 0))
    return pl.pallas_call(
        body, name=name, grid=(nc,),
        in_specs=[pl.BlockSpec((CHUNK, LANES), lambda c: (c, col0 // LANES)), pl.BlockSpec((1, LANES), lambda c: (0, 0)),
                  pl.BlockSpec((1, LANES), lambda c: (0, 0)), pl.BlockSpec((SSD_GROUPS, LANES, LANES), lambda c: (0, 0, 0)),
                  g3, g3, g4, g3, g3, g4, pl.BlockSpec((CHUNK, LANES), lambda c: (c, 0)), pl.BlockSpec(memory_space=pl.ANY)],
        out_specs=[pl.BlockSpec((CHUNK, LANES), lambda c: (c, col0 // LANES)), pl.BlockSpec((SUBLANES, LANES), lambda c: (0, 0))],
        out_shape=[jax.ShapeDtypeStruct(into.shape, BF16), jax.ShapeDtypeStruct((SUBLANES, LANES), F32)],
        input_output_aliases={11: 0}, compiler_params=_cp(dimension_semantics=("arbitrary",)),
    )(zx, bias, alog, sel, *grads_f, *grads_b, da_comp, into)


def _chunk_row(k, nctx_c, nc, rev):
    if not rev:
        return k
    return jnp.where(k < nctx_c, nctx_c - 1 - k, nc + nctx_c - 1 - k)


def _pair_consts(dtg, ag, agt, l0, end):
    lane = lax.broadcasted_iota(jnp.int32, (CHUNK, LANES), 1)
    lo = lane < SSD_HEAD_DIM
    a0, a1 = ag[:, l0:l0 + 1], ag[:, l0 + 1:l0 + 2]
    dtp = jnp.where(lo, dtg[:, l0:l0 + 1], dtg[:, l0 + 1:l0 + 2])
    acol = jnp.where(lo, a0, a1)
    aend = acol[end:end + 1, :]
    return lo, a0, a1, agt[l0:l0 + 1, :], agt[l0 + 1:l0 + 2, :], dtp, acol, aend


def ssd_scan(xbc, dtg, ag, agt, d_inner, n_ctx, rev, name, carry=None):
    t = xbc.shape[0]
    e = d_inner // SSD_HEAD_DIM // SSD_GROUPS
    npair, gw = e // 2, e * SSD_HEAD_DIM
    assert e % 2 == 0 and gw % LANES == 0
    nc, nctx_c = t // CHUNK, n_ctx // CHUNK
    dirn = 1 if rev else 0
    end = 0 if rev else CHUNK - 1
    ridx = lambda k: _chunk_row(k, nctx_c, nc, rev)

    def body(x_ref, b_ref, c_ref, dtg_ref, ag_ref, agt_ref, y_ref, h_ref, state):
        k = pl.program_id(1)

        @pl.when(k == 0)
        def _():
            state[...] = jnp.zeros_like(state)

        bb, cbf = b_ref[...].astype(BF16), c_ref[...].astype(BF16)
        cb = _dot(cbf, bb, _NT)
        row = lax.broadcasted_iota(jnp.int32, (CHUNK, CHUNK), 0)
        col = lax.broadcasted_iota(jnp.int32, (CHUNK, CHUNK), 1)
        mask = (row <= col) if rev else (row >= col)
        rlo = lax.broadcasted_iota(jnp.int32, (LANES, 1), 0) < SSD_HEAD_DIM
        dtg_v, ag_v, agt_v = dtg_ref[0], ag_ref[0], agt_ref[0, 0]
        consts = [_pair_consts(dtg_v, ag_v, agt_v, dirn * e + 2 * p, end) for p in range(npair)]
        dtp = jnp.concatenate([c[5] for c in consts], axis=1)
        acol = jnp.concatenate([c[6] for c in consts], axis=1)
        aend = jnp.concatenate([c[7] for c in consts], axis=1)
        xdt = x_ref[...] * dtp
        xdtb = xdt.astype(BF16)
        s_in = state[...]
        h_ref[0, 0] = s_in
        yo = _dot(cbf, s_in.astype(BF16), _NT)
        st = _dot((xdt * jnp.exp(aend - acol)).astype(BF16), bb, _TN)
        yd = []
        for p, (lo, a0, a1, a0r, a1r, _, _, _) in enumerate(consts):
            xp = xdtb[:, p * LANES:(p + 1) * LANES]
            w01 = jnp.concatenate([(cb * jnp.exp(jnp.where(mask, a0 - a0r, -jnp.inf))).astype(BF16),
                                   (cb * jnp.exp(jnp.where(mask, a1 - a1r, -jnp.inf))).astype(BF16)], axis=1)
            yd.append(_dot(w01, jnp.concatenate([jnp.where(lo, xp, 0), jnp.where(lo, 0, xp)], axis=0)))
        y_ref[...] = jnp.concatenate(yd, axis=1) + yo * jnp.exp(acol)
        cd = jnp.concatenate([jnp.where(rlo, jnp.exp(c[7][:, 0:1]), jnp.exp(c[7][:, LANES - 1:LANES])) for c in consts], axis=0)
        state[...] = cd * s_in + st

    xcol = d_inner // LANES
    g3 = pl.BlockSpec((1, CHUNK, LANES), lambda g, k: (g, ridx(k), 0))
    return _pallas(
        body, name, (SSD_GROUPS, nc),
        [pl.BlockSpec((CHUNK, gw), lambda g, k: (ridx(k), g)),
         pl.BlockSpec((CHUNK, SSD_STATE), lambda g, k: (ridx(k), xcol + g)),
         pl.BlockSpec((CHUNK, SSD_STATE), lambda g, k: (ridx(k), xcol + SSD_GROUPS + g)),
         g3, g3, pl.BlockSpec((1, 1, LANES, CHUNK), lambda g, k: (g, ridx(k), 0, 0))],
        [pl.BlockSpec((CHUNK, gw), lambda g, k: (ridx(k), g)),
         pl.BlockSpec((1, 1, npair * LANES, SSD_STATE), lambda g, k: (ridx(k), g, 0, 0))],
        [jax.ShapeDtypeStruct((t, d_inner), F32), jax.ShapeDtypeStruct((nc, SSD_GROUPS, npair * LANES, SSD_STATE), F32)],
        [pltpu.VMEM((npair * LANES, SSD_STATE), F32)], (xbc, xbc, xbc, dtg, ag, agt), ("parallel", "arbitrary"), carry=carry)


def ssd_scan_bwd(dy, xbc, dtg, ag, agt, hst, dskip, prev, d_inner, n_ctx, rev, name, carry=None):
    t = xbc.shape[0]
    e = d_inner // SSD_HEAD_DIM // SSD_GROUPS
    npair, gw = e // 2, e * SSD_HEAD_DIM
    nc, nctx_c = t // CHUNK, n_ctx // CHUNK
    dirn = 1 if rev else 0
    end = 0 if rev else CHUNK - 1
    ridx = lambda kk: _chunk_row(nc - 1 - kk, nctx_c, nc, rev)
    has_skip, has_prev = dskip is not None, prev is not None

    def body(*refs):
        dy_ref, x_ref, b_ref, c_ref, dtg_ref, ag_ref, agt_ref, h_ref = refs[:8]
        pos = 8
        if has_skip:
            ds_ref = refs[pos]
            pos += 1
        if has_prev:
            pdx_ref, pdb_ref, pdc_ref = refs[pos:pos + 3]
            pos += 3
        dx_ref, db_ref, dc_ref, ddtg_ref, dag_ref, dagt_ref, dd_ref, dstate = refs[pos:]
        kk = pl.program_id(1)

        @pl.when(kk == 0)
        def _():
            dstate[...] = jnp.zeros_like(dstate)

        bb, cbf = b_ref[...].astype(BF16), c_ref[...].astype(BF16)
        cb = _dot(cbf, bb, _NT)
        row = lax.broadcasted_iota(jnp.int32, (CHUNK, CHUNK), 0)
        col = lax.broadcasted_iota(jnp.int32, (CHUNK, CHUNK), 1)
        mask = (row <= col) if rev else (row >= col)
        rlo = lax.broadcasted_iota(jnp.int32, (LANES, 1), 0) < SSD_HEAD_DIM
        is_end = lax.broadcasted_iota(jnp.int32, (CHUNK, 1), 0) == end
        dtg_v, ag_v, agt_v = dtg_ref[0], ag_ref[0], agt_ref[0, 0]
        consts = [_pair_consts(dtg_v, ag_v, agt_v, dirn * e + 2 * p, end) for p in range(npair)]
        dtp = jnp.concatenate([c[5] for c in consts], axis=1)
        acol = jnp.concatenate([c[6] for c in consts], axis=1)
        aend = jnp.concatenate([c[7] for c in consts], axis=1)
        x, dyv = x_ref[...], dy_ref[...]
        xdt = x * dtp
        xdtb = xdt.astype(BF16)
        efs, dte = jnp.exp(acol), jnp.exp(aend - acol)
        s_in, d_s = h_ref[0, 0], dstate[...]
        sb, dsb = s_in.astype(BF16), d_s.astype(BF16)
        dyo = (dyv * efs).astype(BF16)
        da_exp = dyv * _dot(cbf, sb, _NT) * efs
        d_c = _dot(dyo, sb)
        ds_y = _dot(dyo, cbf, _TN)
        dxw = _dot(bb, dsb, _NT)
        d_b = _dot((xdt * dte).astype(BF16), dsb)
        tmp = dxw * xdt * dte
        da_exp = da_exp - tmp
        end_row = jnp.sum(tmp, axis=0, keepdims=True)
        prod = d_s * s_in
        cd = jnp.concatenate([jnp.where(rlo, jnp.exp(c[7][:, 0:1]), jnp.exp(c[7][:, LANES - 1:LANES])) for c in consts], axis=0)
        dstate[...] = cd * d_s + ds_y
        dcb = jnp.zeros((CHUNK, CHUNK), F32)
        ddt_out = jnp.zeros((CHUNK, LANES), F32)
        da_out = jnp.zeros((CHUNK, LANES), F32)
        dat_out = jnp.zeros((LANES, CHUNK), F32)
        lane = lax.broadcasted_iota(jnp.int32, (CHUNK, LANES), 1)
        sub = lax.broadcasted_iota(jnp.int32, (LANES, CHUNK), 0)
        dxdt_parts = []
        for p, (lo, a0, a1, a0r, a1r, _, _, aend_p) in enumerate(consts):
            sl = slice(p * LANES, (p + 1) * LANES)
            l0 = dirn * e + 2 * p
            seg0 = jnp.exp(jnp.where(mask, a0 - a0r, -jnp.inf))
            seg1 = jnp.exp(jnp.where(mask, a1 - a1r, -jnp.inf))
            w0, w1 = cb * seg0, cb * seg1
            dyp = dyv[:, sl]
            dy01 = jnp.concatenate([jnp.where(lo, dyp, 0.0), jnp.where(lo, 0.0, dyp)], axis=0).astype(BF16)
            dw01 = _dot(dy01, xdtb[:, sl], _NT)
            dw0, dw1 = dw01[:CHUNK], dw01[CHUNK:]
            dxdt_p = _dot(jnp.concatenate([w0.astype(BF16), w1.astype(BF16)], axis=0), dy01, _TN) + dxw[:, sl] * dte[:, sl]
            dxdt_parts.append(dxdt_p)
            dcb += dw0 * seg0 + dw1 * seg1
            t0, t1 = dw0 * w0, dw1 * w1
            prod_p = prod[sl, :]
            sc0 = jnp.sum(jnp.where(rlo, prod_p, 0.0), keepdims=True) * jnp.exp(aend_p[:, 0:1])
            sc1 = jnp.sum(jnp.where(rlo, 0.0, prod_p), keepdims=True) * jnp.exp(aend_p[:, LANES - 1:LANES])
            ddt_exp, da_exp_p, end_row_p = dxdt_p * x[:, sl], da_exp[:, sl], end_row[:, sl]
            for j, (sel, tj, scj) in enumerate(((lo, t0, sc0), (~lo, t1, sc1))):
                ddt_col = jnp.sum(jnp.where(sel, ddt_exp, 0.0), axis=1, keepdims=True)
                da_col = jnp.sum(jnp.where(sel, da_exp_p, 0.0), axis=1, keepdims=True) + jnp.sum(tj, axis=1, keepdims=True)
                da_end = jnp.sum(jnp.where(sel[0:1, :], end_row_p, 0.0), axis=1, keepdims=True) + scj
                da_col = da_col + jnp.where(is_end, da_end, 0.0)
                ddt_out += jnp.where(lane == l0 + j, ddt_col, 0.0)
                da_out += jnp.where(lane == l0 + j, da_col, 0.0)
                dat_out -= jnp.where(sub == l0 + j, jnp.sum(tj, axis=0, keepdims=True), 0.0)
        dxv = jnp.concatenate(dxdt_parts, axis=1) * dtp
        if has_skip:
            dxv += dyv * ds_ref[...]
            _acc_rows(dd_ref, kk, [jnp.sum(dyv * x, axis=0, keepdims=True)])
        if has_prev:
            dxv += pdx_ref[...]
        dx_ref[...] = dxv
        dcbb = dcb.astype(BF16)
        d_c += _dot(dcbb, bb)
        d_b += _dot(dcbb, cbf, _TN)
        if has_prev:
            d_b += pdb_ref[...]
            d_c += pdc_ref[...]
        db_ref[...] = d_b
        dc_ref[...] = d_c
        ddtg_ref[0] = ddt_out
        dag_ref[0] = da_out
        dagt_ref[0, 0] = dat_out
        if not has_skip:
            dd_ref[...] = jnp.zeros_like(dd_ref)

    xcol = d_inner // LANES
    xs_spec = pl.BlockSpec((CHUNK, gw), lambda g, kk: (ridx(kk), g))
    bc_spec = pl.BlockSpec((CHUNK, SSD_STATE), lambda g, kk: (ridx(kk), g))
    g3 = pl.BlockSpec((1, CHUNK, LANES), lambda g, kk: (g, ridx(kk), 0))
    g4 = pl.BlockSpec((1, 1, LANES, CHUNK), lambda g, kk: (g, ridx(kk), 0, 0))
    in_specs = [xs_spec, xs_spec,
                pl.BlockSpec((CHUNK, SSD_STATE), lambda g, kk: (ridx(kk), xcol + g)),
                pl.BlockSpec((CHUNK, SSD_STATE), lambda g, kk: (ridx(kk), xcol + SSD_GROUPS + g)),
                g3, g3, g4, pl.BlockSpec((1, 1, npair * LANES, SSD_STATE), lambda g, kk: (ridx(kk), g, 0, 0))]
    args = [dy, xbc, xbc, xbc, dtg, ag, agt, hst]
    if has_skip:
        in_specs.append(pl.BlockSpec((1, gw), lambda g, kk: (0, g)))
        args.append(dskip)
    if has_prev:
        in_specs += [xs_spec, bc_spec, bc_spec]
        args += list(prev)
    gn = SSD_GROUPS * SSD_STATE
    return _pallas(
        body, name, (SSD_GROUPS, nc), in_specs,
        [xs_spec, bc_spec, bc_spec, g3, g3, g4, pl.BlockSpec((SUBLANES, gw), lambda g, kk: (0, g))],
        [jax.ShapeDtypeStruct((t, d_inner), F32), jax.ShapeDtypeStruct((t, gn), F32), jax.ShapeDtypeStruct((t, gn), F32),
         jax.ShapeDtypeStruct((SSD_GROUPS, t, LANES), F32), jax.ShapeDtypeStruct((SSD_GROUPS, t, LANES), F32),
         jax.ShapeDtypeStruct((SSD_GROUPS, nc, LANES, CHUNK), F32), jax.ShapeDtypeStruct((SUBLANES, d_inner), F32)],
        [pltpu.VMEM((npair * LANES, SSD_STATE), F32)], args, ("parallel", "arbitrary"), carry=carry)


def ssd_gate_norm(yf, yb, xbc, zx, dskip, w, name):
    t, di = yf.shape
    tm = CHUNK

    def body(yf_ref, yb_ref, x_ref, z_ref, d_ref, w_ref, y_ref, o_ref):
        y = yf_ref[...] + yb_ref[...] + x_ref[...] * d_ref[...]
        y_ref[...] = y
        gz = y * _silu(z_ref[...])
        o_ref[...] = (gz * lax.rsqrt(jnp.mean(gz * gz, axis=-1, keepdims=True) + NORM_EPS) * w_ref[...]).astype(BF16)

    blk = pl.BlockSpec((tm, di), lambda i: (i, 0))
    vec = pl.BlockSpec((1, di), lambda i: (0, 0))
    return pl.pallas_call(
        body, name=name, grid=(t // tm,), in_specs=[blk, blk, blk, blk, vec, vec], out_specs=[blk, blk],
        out_shape=[jax.ShapeDtypeStruct((t, di), F32), jax.ShapeDtypeStruct((t, di), BF16)], compiler_params=_cp(),
    )(yf, yb, xbc, zx, dskip, w)


def ssd_gate_norm_bwd(dout, y, zx, w, name):
    t, di = y.shape
    tm = CHUNK

    def body(do_ref, y_ref, z_ref, w_ref, dy_ref, dz_ref, st_ref):
        i = pl.program_id(0)
        z, yv, g = z_ref[...], y_ref[...], do_ref[...]
        sz = _silu(z)
        gz = yv * sz
        r = lax.rsqrt(jnp.mean(gz * gz, axis=-1, keepdims=True) + NORM_EPS)
        n = gz * r
        dn = g * w_ref[...]
        dgz = r * (dn - n * jnp.mean(dn * n, axis=-1, keepdims=True))
        dy_ref[...] = dgz * sz
        dz_ref[...] = (dgz * yv * _dsilu(z)).astype(BF16)
        _acc_rows(st_ref, i, [jnp.sum(g * n, axis=0, keepdims=True)])

    blk = pl.BlockSpec((tm, di), lambda i: (i, 0))
    return pl.pallas_call(
        body, name=name, grid=(t // tm,), in_specs=[blk, blk, blk, pl.BlockSpec((1, di), lambda i: (0, 0))],
        out_specs=[blk, blk, pl.BlockSpec((SUBLANES, di), lambda i: (0, 0))],
        out_shape=[jax.ShapeDtypeStruct((t, di), F32), jax.ShapeDtypeStruct(zx.shape, BF16), jax.ShapeDtypeStruct((SUBLANES, di), F32)],
        compiler_params=_cp(dimension_semantics=("arbitrary",)),
    )(dout, y, zx, w)


def rope_tables(n_ctx, seq):
    pos = jnp.arange(seq)
    half = ATTN_HEAD_DIM // 4
    inv = ROPE_THETA ** (-jnp.arange(0, 2 * half, 2, dtype=F32) / (2 * half))
    ar = (pos // GRID_W).astype(F32)[:, None] * inv[None, :]
    ac = (pos % GRID_W).astype(F32)[:, None] * inv[None, :]
    cos = jnp.concatenate([jnp.cos(ar), jnp.cos(ar), jnp.cos(ac), jnp.cos(ac)], axis=-1)
    sin = jnp.concatenate([-jnp.sin(ar), jnp.sin(ar), -jnp.sin(ac), jnp.sin(ac)], axis=-1)
    cos = jnp.concatenate([jnp.ones((n_ctx, ATTN_HEAD_DIM), F32), cos], axis=0)
    sin = jnp.concatenate([jnp.zeros((n_ctx, ATTN_HEAD_DIM), F32), sin], axis=0)
    return cos, sin


def _rot(x):
    lane = lax.broadcasted_iota(jnp.int32, x.shape, 1)
    q = ATTN_HEAD_DIM // 4
    return jnp.where((lane % (2 * q)) < q, pltpu.roll(x, ATTN_HEAD_DIM - q, 1), pltpu.roll(x, q, 1))


def qk_prep(qkv, gains, cos, sin, n_q, n_k, name):
    t, c = qkv.shape
    tm = ROW_TILE
    hd = ATTN_HEAD_DIM

    def body(x_ref, g_ref, cos_ref, sin_ref, o_ref):
        cs, sn = cos_ref[...], sin_ref[...]
        for h in range(c // hd):
            sl = slice(h * hd, (h + 1) * hd)
            x = x_ref[:, sl]
            if h < n_q + n_k:
                gain = g_ref[0:1, :] if h < n_q else g_ref[1:2, :]
                xn = x * lax.rsqrt(jnp.mean(x * x, axis=-1, keepdims=True) + NORM_EPS) * gain
                x = xn * cs + _rot(xn) * sn
            o_ref[:, sl] = x.astype(BF16)

    return pl.pallas_call(
        body, name=name, grid=(t // tm,),
        in_specs=[pl.BlockSpec((tm, c), lambda i: (i, 0)), pl.BlockSpec((SUBLANES, hd), lambda i: (0, 0)),
                  pl.BlockSpec((tm, hd), lambda i: (i, 0)), pl.BlockSpec((tm, hd), lambda i: (i, 0))],
        out_specs=pl.BlockSpec((tm, c), lambda i: (i, 0)), out_shape=jax.ShapeDtypeStruct((t, c), BF16), compiler_params=_cp(),
    )(qkv, gains, cos, sin)


def qk_prep_bwd(dq, dk, dv, qkv, gains, cos, sin, name):
    t, c = qkv.shape
    tm = ROW_TILE
    hd = ATTN_HEAD_DIM
    n_q, n_k = dq.shape[1] // hd, dk.shape[1] // hd

    def body(dq_ref, dk_ref, dv_ref, x_ref, g_ref, cos_ref, sin_ref, o_ref, st_ref):
        i = pl.program_id(0)
        cs, sn = cos_ref[...], sin_ref[...]
        dgq = jnp.zeros((1, hd), F32)
        dgk = jnp.zeros((1, hd), F32)
        for h in range(c // hd):
            sl = slice(h * hd, (h + 1) * hd)
            if h >= n_q + n_k:
                hv = h - n_q - n_k
                o_ref[:, sl] = dv_ref[:, hv * hd:(hv + 1) * hd].astype(BF16)
                continue
            is_q = h < n_q
            dy = dq_ref[:, sl] if is_q else dk_ref[:, (h - n_q) * hd:(h - n_q + 1) * hd]
            gain = g_ref[0:1, :] if is_q else g_ref[1:2, :]
            x = x_ref[:, sl]
            r = lax.rsqrt(jnp.mean(x * x, axis=-1, keepdims=True) + NORM_EPS)
            xh = x * r
            dxn = dy * cs + _rot(dy * sn)
            dg = jnp.sum(dxn * xh, axis=0, keepdims=True)
            if is_q:
                dgq += dg
            else:
                dgk += dg
            dxh = dxn * gain
            o_ref[:, sl] = (r * (dxh - xh * jnp.mean(dxh * xh, axis=-1, keepdims=True))).astype(BF16)
        _acc_rows(st_ref, i, [dgq, dgk])

    return pl.pallas_call(
        body, name=name, grid=(t // tm,),
        in_specs=[pl.BlockSpec((tm, n_q * hd), lambda i: (i, 0)), pl.BlockSpec((tm, n_k * hd), lambda i: (i, 0)),
                  pl.BlockSpec((tm, n_k * hd), lambda i: (i, 0)), pl.BlockSpec((tm, c), lambda i: (i, 0)),
                  pl.BlockSpec((SUBLANES, hd), lambda i: (0, 0)), pl.BlockSpec((tm, hd), lambda i: (i, 0)), pl.BlockSpec((tm, hd), lambda i: (i, 0))],
        out_specs=[pl.BlockSpec((tm, c), lambda i: (i, 0)), pl.BlockSpec((SUBLANES, hd), lambda i: (0, 0))],
        out_shape=[jax.ShapeDtypeStruct((t, c), BF16), jax.ShapeDtypeStruct((SUBLANES, hd), F32)],
        compiler_params=_cp(dimension_semantics=("arbitrary",)),
    )(dq, dk, dv, qkv, gains, cos, sin)


def _attn_specs(n_ctx, nb, grp, n_qh):
    hd, blk = ATTN_HEAD_DIM, CHUNK
    kc, vc = n_qh, n_qh + ATTN_KV_HEADS
    specs = [pl.BlockSpec((blk, grp * hd), lambda h, b: (b, h))]
    for c0 in (kc, vc):
        specs += [pl.BlockSpec((n_ctx, hd), lambda h, b, c0=c0: (0, c0 + h)),
                  pl.BlockSpec((blk, hd), lambda h, b, c0=c0: (jnp.maximum(b - 1, 0), c0 + h)),
                  pl.BlockSpec((blk, hd), lambda h, b, c0=c0: (b, c0 + h)),
                  pl.BlockSpec((blk, hd), lambda h, b, c0=c0: (jnp.minimum(b + 1, nb - 1), c0 + h))]
    return specs


def _attn_masks(b, nctx_b, nb):
    row = lax.broadcasted_iota(jnp.int32, (CHUNK, CHUNK), 0)
    col = lax.broadcasted_iota(jnp.int32, (CHUNK, CHUNK), 1)
    lat = b >= nctx_b
    return [(col >= row) & lat & (b - 1 >= nctx_b), jnp.broadcast_to(lat, (CHUNK, CHUNK)), (col <= row) & lat & (b + 1 <= nb - 1)]


def _attn_bias(b, n_ctx, nctx_b, nb):
    band = [jnp.where(m, 0.0, -jnp.inf) for m in _attn_masks(b, nctx_b, nb)]
    return jnp.concatenate([jnp.zeros((CHUNK, n_ctx), F32)] + band, axis=1)


def attention(qkvr, sinks, n_ctx, n_qh, name, carry=None):
    t = qkvr.shape[0]
    hd, blk = ATTN_HEAD_DIM, CHUNK
    grp = n_qh // ATTN_KV_HEADS
    nb, nctx_b = t // blk, n_ctx // blk
    scale = hd ** -0.5

    def body(q_ref, kc_ref, kp_ref, ko_ref, kn_ref, vc_ref, vp_ref, vo_ref, vn_ref, s_ref, o_ref, lse_ref):
        b = pl.program_id(1)
        k_all = jnp.concatenate([kc_ref[...], kp_ref[...], ko_ref[...], kn_ref[...]], axis=0)
        v_all = jnp.concatenate([vc_ref[...], vp_ref[...], vo_ref[...], vn_ref[...]], axis=0)
        bias = jnp.concatenate([_attn_bias(b, n_ctx, nctx_b, nb)] * grp, axis=0)
        q = jnp.concatenate([q_ref[:, g * hd:(g + 1) * hd] for g in range(grp)], axis=0)
        sink = jnp.concatenate([jnp.broadcast_to(s_ref[0, 0:1, g:g + 1], (blk, 1)) for g in range(grp)], axis=0)
        s = _dot(q, k_all, _NT) * scale + bias
        m = jnp.maximum(sink, jnp.max(s, axis=-1, keepdims=True))
        p = jnp.exp(s - m)
        l = jnp.exp(sink - m) + jnp.sum(p, axis=-1, keepdims=True)
        o = _dot((p * (1.0 / l)).astype(BF16), v_all).astype(BF16)
        lse = m + jnp.log(l)
        lane = lax.broadcasted_iota(jnp.int32, (blk, LANES), 1)
        lse_out = jnp.zeros((blk, LANES), F32)
        for g in range(grp):
            o_ref[:, g * hd:(g + 1) * hd] = o[g * blk:(g + 1) * blk]
            lse_out = jnp.where(lane == g, lse[g * blk:(g + 1) * blk], lse_out)
        lse_ref[...] = lse_out

    return _pallas(
        body, name, (ATTN_KV_HEADS, nb),
        _attn_specs(n_ctx, nb, grp, n_qh) + [pl.BlockSpec((1, SUBLANES, LANES), lambda h, b: (h, 0, 0))],
        [pl.BlockSpec((blk, grp * hd), lambda h, b: (b, h)), pl.BlockSpec((blk, LANES), lambda h, b: (b, h))],
        [jax.ShapeDtypeStruct((t, n_qh * hd), BF16), jax.ShapeDtypeStruct((t, ATTN_KV_HEADS * LANES), F32)],
        [], (qkvr,) * 9 + (sinks,), ("parallel", "arbitrary"), carry=carry)


def attention_bwd(do, o, lse, qkvr, sinks, n_ctx, n_qh, name, carry=None):
    t = qkvr.shape[0]
    hd, blk = ATTN_HEAD_DIM, CHUNK
    grp = n_qh // ATTN_KV_HEADS
    nb, nctx_b = t // blk, n_ctx // blk
    scale = hd ** -0.5
    kvw = ATTN_KV_HEADS * hd

    def body(do_ref, o_ref, lse_ref, q_ref, kc_ref, kp_ref, ko_ref, kn_ref, vc_ref, vp_ref, vo_ref, vn_ref, s_ref,
             dq_ref, dkc_ref, dvc_ref, dkp_ref, dvp_ref, dsk_ref):
        b = pl.program_id(1)
        k_all = jnp.concatenate([kc_ref[...], kp_ref[...], ko_ref[...], kn_ref[...]], axis=0)
        v_all = jnp.concatenate([vc_ref[...], vp_ref[...], vo_ref[...], vn_ref[...]], axis=0)
        bias = _attn_bias(b, n_ctx, nctx_b, nb)
        lane = lax.broadcasted_iota(jnp.int32, (1, LANES), 1)
        dk_all = jnp.zeros(k_all.shape, F32)
        dv_all = jnp.zeros(v_all.shape, F32)
        dsk = jnp.zeros((1, LANES), F32)
        for g in range(grp):
            sl = slice(g * hd, (g + 1) * hd)
            q = q_ref[:, sl]
            dof = do_ref[:, sl]
            dob = dof.astype(BF16)
            lse = lse_ref[:, g:g + 1]
            delta = jnp.sum(dof * o_ref[:, sl].astype(F32), axis=-1, keepdims=True)
            p = jnp.exp(_dot(q, k_all, _NT) * scale + bias - lse)
            ds = (p * (_dot(dob, v_all, _NT) - delta)).astype(BF16)
            dq_ref[:, sl] = _dot(ds, k_all) * scale
            dk_all += _dot(ds, q, _TN)
            dv_all += _dot(p.astype(BF16), dob, _TN)
            p_sink = jnp.exp(s_ref[0, 0:1, g:g + 1] - lse)
            dsk = dsk + jnp.where(lane == g, -jnp.sum(p_sink * delta, axis=0, keepdims=True), 0.0)

        @pl.when(b == 0)
        def _():
            dkc_ref[...] = jnp.zeros_like(dkc_ref)
            dvc_ref[...] = jnp.zeros_like(dvc_ref)
            dsk_ref[...] = jnp.zeros_like(dsk_ref)

        dkc_ref[...] += dk_all[:n_ctx] * scale
        dvc_ref[...] += dv_all[:n_ctx]
        dsk_ref[0, 0:1, :] += dsk
        for x in range(3):
            rows = slice(n_ctx + x * blk, n_ctx + (x + 1) * blk)
            dkp_ref[0, x] = dk_all[rows] * scale
            dvp_ref[0, x] = dv_all[rows]

    part = pl.BlockSpec((1, 3, blk, hd), lambda h, b: (b, 0, 0, h))
    ctxo = pl.BlockSpec((n_ctx, hd), lambda h, b: (0, h))
    return _pallas(
        body, name, (ATTN_KV_HEADS, nb),
        [pl.BlockSpec((blk, grp * hd), lambda h, b: (b, h)), pl.BlockSpec((blk, grp * hd), lambda h, b: (b, h)),
         pl.BlockSpec((blk, LANES), lambda h, b: (b, h))] + _attn_specs(n_ctx, nb, grp, n_qh)
        + [pl.BlockSpec((1, SUBLANES, LANES), lambda h, b: (h, 0, 0))],
        [pl.BlockSpec((blk, grp * hd), lambda h, b: (b, h)), ctxo, ctxo, part, part,
         pl.BlockSpec((1, SUBLANES, LANES), lambda h, b: (h, 0, 0))],
        [jax.ShapeDtypeStruct((t, n_qh * hd), F32), jax.ShapeDtypeStruct((n_ctx, kvw), F32), jax.ShapeDtypeStruct((n_ctx, kvw), F32),
         jax.ShapeDtypeStruct((nb, 3, blk, kvw), F32), jax.ShapeDtypeStruct((nb, 3, blk, kvw), F32),
         jax.ShapeDtypeStruct((ATTN_KV_HEADS, SUBLANES, LANES), F32)],
        [], (do, o, lse) + (qkvr,) * 9 + (sinks,), ("parallel", "arbitrary"), carry=carry)


def band_reduce(ctx_part, band_part, n_ctx, name):
    nb, _, blk, w = band_part.shape
    nctx_b = n_ctx // blk

    def body(c_ref, p_ref, o_ref, n_ref, out_ref):
        b = pl.program_id(0)
        band = p_ref[0, 0] + o_ref[0, 0] + jnp.where(b + 1 <= nb - 1, n_ref[0, 0], 0.0)
        out_ref[...] = jnp.where(b < nctx_b, c_ref[...], band)

    return pl.pallas_call(
        body, name=name, grid=(nb,),
        in_specs=[pl.BlockSpec((blk, w), lambda b: (jnp.minimum(b, nctx_b - 1), 0)),
                  pl.BlockSpec((1, 1, blk, w), lambda b: (jnp.maximum(b - 1, 0), 2, 0, 0)),
                  pl.BlockSpec((1, 1, blk, w), lambda b: (b, 1, 0, 0)),
                  pl.BlockSpec((1, 1, blk, w), lambda b: (jnp.minimum(b + 1, nb - 1), 0, 0, 0))],
        out_specs=pl.BlockSpec((blk, w), lambda b: (b, 0)), out_shape=jax.ShapeDtypeStruct((nb * blk, w), F32),
        compiler_params=_cp(),
    )(ctx_part, band_part, band_part, band_part)


def loss_grad(xf, target, n_ctx, name):
    t, d = xf.shape
    tm = _pick(n_ctx, (ROW_TILE, 128))
    nct = n_ctx // tm

    def body(x_ref, t_ref, dy_ref, s_ref):
        i = pl.program_id(0)
        err = jnp.where(i < nct, 0.0, x_ref[...] - t_ref[...])
        dy_ref[...] = err * (1.0 / d)
        _acc_rows(s_ref, i, [jnp.sum(err * err, axis=0, keepdims=True)])

    return pl.pallas_call(
        body, name=name, grid=(t // tm,),
        in_specs=[pl.BlockSpec((tm, d), lambda i: (i, 0)), pl.BlockSpec((tm, d), lambda i: (jnp.maximum(i - nct, 0), 0))],
        out_specs=[pl.BlockSpec((tm, d), lambda i: (i, 0)), pl.BlockSpec((SUBLANES, d), lambda i: (0, 0))],
        out_shape=[jax.ShapeDtypeStruct((t, d), F32), jax.ShapeDtypeStruct((SUBLANES, d), F32)],
        compiler_params=_cp(dimension_semantics=("arbitrary",)),
    )(xf, target)


def adamw(w, g, m, v, name):
    r, c = w.shape
    tr = r
    while tr % 2 == 0 and tr * c * 4 > (1 << 20) and (tr // 2) % SUBLANES == 0:
        tr //= 2
    bc1, bc2 = 1.0 - ADAM_B1 ** ADAM_STEP, 1.0 - ADAM_B2 ** ADAM_STEP

    def body(w_ref, g_ref, m_ref, v_ref, d_ref, nm_ref, nv_ref):
        gv = g_ref[...]
        nm = ADAM_B1 * m_ref[...] + (1.0 - ADAM_B1) * gv
        nv = ADAM_B2 * v_ref[...] + (1.0 - ADAM_B2) * (gv * gv)
        nm_ref[...] = nm
        nv_ref[...] = nv
        d_ref[...] = -ADAM_LR * ((nm / bc1) / (jnp.sqrt(nv / bc2) + ADAM_EPS) + ADAM_WD * w_ref[...])

    blk = pl.BlockSpec((tr, c), lambda i: (i, 0))
    return pl.pallas_call(
        body, name=name, grid=(r // tr,), in_specs=[blk] * 4, out_specs=[blk] * 3,
        out_shape=[jax.ShapeDtypeStruct((r, c), F32)] * 3, compiler_params=_cp(dimension_semantics=("parallel",)),
    )(w, g, m, v)


ADA_ROWS = 16


def ada_fwd(cs, w, name):
    l, d, ns = w.shape
    tn = _pick(ns, (512, 256, 128))

    def body(c_ref, w_ref, o_ref):
        o_ref[0] = _dot(c_ref[...], w_ref[0].astype(BF16))

    return pl.pallas_call(
        body, name=name, grid=(l, ns // tn),
        in_specs=[pl.BlockSpec((ADA_ROWS, d), lambda i, j: (0, 0)), pl.BlockSpec((1, d, tn), lambda i, j: (i, 0, j))],
        out_specs=pl.BlockSpec((1, ADA_ROWS, tn), lambda i, j: (i, 0, j)),
        out_shape=jax.ShapeDtypeStruct((l, ADA_ROWS, ns), F32), compiler_params=_cp(),
    )(cs, w)


def ada_bwd(cs, gmod, w, name):
    l, d, ns = w.shape
    tn = _pick(ns, (512, 256, 128))

    def body(c_ref, g_ref, w_ref, dw_ref, dc_ref):
        first = (pl.program_id(0) == 0) & (pl.program_id(1) == 0)
        gb = g_ref[0].astype(BF16)
        dw_ref[0] = _dot(c_ref[...], gb, _TN)
        part = _dot(gb, w_ref[0].astype(BF16), _NT)

        @pl.when(first)
        def _():
            dc_ref[...] = part

        @pl.when(jnp.logical_not(first))
        def _():
            dc_ref[...] += part

    return pl.pallas_call(
        body, name=name, grid=(l, ns // tn),
        in_specs=[pl.BlockSpec((ADA_ROWS, d), lambda i, j: (0, 0)), pl.BlockSpec((1, ADA_ROWS, tn), lambda i, j: (i, 0, j)),
                  pl.BlockSpec((1, d, tn), lambda i, j: (i, 0, j))],
        out_specs=[pl.BlockSpec((1, d, tn), lambda i, j: (i, 0, j)), pl.BlockSpec((ADA_ROWS, d), lambda i, j: (0, 0))],
        out_shape=[jax.ShapeDtypeStruct((l, d, ns), F32), jax.ShapeDtypeStruct((ADA_ROWS, d), F32)],
        compiler_params=_cp(dimension_semantics=("arbitrary", "arbitrary")),
    )(cs, gmod, w)


def sum_leading(a, name):
    k, r, c = a.shape
    tr = _pick(r, (256, 128, 64, 32, 16, 8))

    def body(a_ref, o_ref):
        acc = a_ref[0]
        for q in range(1, k):
            acc = acc + a_ref[q]
        o_ref[...] = acc

    return pl.pallas_call(
        body, name=name, grid=(r // tr,), in_specs=[pl.BlockSpec((k, tr, c), lambda i: (0, i, 0))],
        out_specs=pl.BlockSpec((tr, c), lambda i: (i, 0)), out_shape=jax.ShapeDtypeStruct((r, c), F32), compiler_params=_cp(),
    )(a)


def _mesh_pos():
    return lax.axis_index("x"), lax.axis_index("y"), lax.axis_index("c")


def _other_chips(x, y):
    return [(1 - x, y), (x, 1 - y), (1 - x, 1 - y)]


def _rcopy(src, dst, send_sems, recv_sems, k, to):
    return pltpu.make_async_remote_copy(src_ref=src, dst_ref=dst, send_sem=send_sems.at[k], recv_sem=recv_sems.at[k],
                                        device_id=to, device_id_type=MESH)


def small_allgather(vs, name):
    nv = len(vs)

    def body(*refs):
        v_refs, out_refs = refs[:nv], refs[nv:2 * nv]
        send_sems, recv_sems, local_sems = refs[2 * nv:]
        x, y, c = _mesh_pos()
        sibling = (x, y, 1 - c)
        chips = _other_chips(x, y)

        def blk(q, px, py, pc):
            return out_refs[q].at[4 * px + 2 * py + pc]

        mine = [pltpu.make_async_copy(v_refs[q], blk(q, x, y, c), local_sems.at[q]) for q in range(nv)]
        first, passed = [], []
        for q in range(nv):
            mine[q].start()
            first.append(_rcopy(v_refs[q], blk(q, x, y, c), send_sems, recv_sems, 7 * q, sibling))
            first += [_rcopy(v_refs[q], blk(q, x, y, c), send_sems, recv_sems, 7 * q + 1 + j, (*chip, c)) for j, chip in enumerate(chips)]
        for cp in first:
            cp.start()
        for q in range(nv):
            for j, chip in enumerate(chips):
                _rcopy(blk(q, *chip, c), blk(q, *chip, c), send_sems, recv_sems, 7 * q + 1 + j, (x, y, c)).wait_recv()
                passed.append(_rcopy(blk(q, *chip, c), blk(q, *chip, c), send_sems, recv_sems, 7 * q + 4 + j, sibling))
                passed[-1].start()
        for q in range(nv):
            _rcopy(blk(q, x, y, 1 - c), blk(q, x, y, 1 - c), send_sems, recv_sems, 7 * q, (x, y, c)).wait_recv()
            for j, chip in enumerate(chips):
                _rcopy(blk(q, *chip, 1 - c), blk(q, *chip, 1 - c), send_sems, recv_sems, 7 * q + 4 + j, (x, y, c)).wait_recv()
        for cp in first + passed:
            cp.wait_send()
        for cp in mine:
            cp.wait()

    vm = pl.BlockSpec(memory_space=pltpu.VMEM)
    return pl.pallas_call(
        body, name=name, out_shape=[jax.ShapeDtypeStruct((N_DEV, *v.shape), v.dtype) for v in vs],
        in_specs=[vm] * nv, out_specs=[vm] * nv,
        scratch_shapes=[pltpu.SemaphoreType.DMA((7 * nv,)), pltpu.SemaphoreType.DMA((7 * nv,)), pltpu.SemaphoreType.DMA((nv,))],
        compiler_params=_cp(),
    )(*vs)


_HBM = pl.BlockSpec(memory_space=pltpu.HBM)


STREAM_TILE_BYTES = 2 * 1024 * 1024


def _stream_rows(rows, row_bytes):
    tr = 16
    while rows % (2 * tr) == 0 and 2 * tr * row_bytes <= STREAM_TILE_BYTES:
        tr *= 2
    assert rows % tr == 0
    return tr


def _scalars(*vals):
    return jnp.stack([jnp.asarray(v, jnp.int32) for v in vals])


def place_own(w, chip, name):
    l, k, ns = w.shape
    tk = _pick(k, (256, 128, 64))

    def body(s_ref, w_ref, o_ref):
        o_ref[...] = w_ref[...].astype(BF16)

    grid_spec = pltpu.PrefetchScalarGridSpec(
        num_scalar_prefetch=1, grid=(l, k // tk),
        in_specs=[pl.BlockSpec((None, tk, ns), lambda i, j, s: (i, j, 0))],
        out_specs=pl.BlockSpec((None, None, tk, ns), lambda i, j, s: (i, s[0], j, 0)))
    return pl.pallas_call(body, name=name, grid_spec=grid_spec, out_shape=jax.ShapeDtypeStruct((l, N_CHIP, k, ns), BF16),
                          compiler_params=_cp())(_scalars(chip), w)


def _half(ref, layer, px, py, pc):
    hk = ref.shape[2] // 2
    return ref.at[layer, 2 * px + py, pl.ds(pc * hk, hk)]


def _chip_sends(refs, items, send_sems, recv_sems):
    x, y, c = _mesh_pos()
    return [_rcopy(_half(refs[b], l, x, y, c), _half(refs[b], l, x, y, c), send_sems, recv_sems, 3 * q + j, (*chip, c))
            for q, (b, l) in enumerate(items) for j, chip in enumerate(_other_chips(x, y))]


def _chip_recv_waits(refs, items, send_sems, recv_sems):
    x, y, c = _mesh_pos()
    for q, (b, l) in enumerate(items):
        for j, chip in enumerate(_other_chips(x, y)):
            _rcopy(_half(refs[b], l, *chip, c), _half(refs[b], l, *chip, c), send_sems, recv_sems, 3 * q + j, (x, y, c)).wait_recv()


def _sibling_forward(refs, items, send_sems, recv_sems):
    x, y, c = _mesh_pos()
    passed = [_rcopy(_half(refs[b], l, *chip, c), _half(refs[b], l, *chip, c), send_sems, recv_sems, 3 * q + j, (x, y, 1 - c))
              for q, (b, l) in enumerate(items) for j, chip in enumerate(_other_chips(x, y))]
    for cp in passed:
        cp.start()
    for q, (b, l) in enumerate(items):
        for j, chip in enumerate(_other_chips(x, y)):
            _rcopy(_half(refs[b], l, *chip, 1 - c), _half(refs[b], l, *chip, 1 - c), send_sems, recv_sems, 3 * q + j, (x, y, c)).wait_recv()
    for cp in passed:
        cp.wait_send()


def _inplace_comm_call(body, bufs, n_sems, name):
    nb = len(bufs)
    return pl.pallas_call(
        body, name=name, out_shape=[jax.ShapeDtypeStruct(b.shape, b.dtype) for b in bufs],
        in_specs=[_HBM] * nb, out_specs=[_HBM] * nb, input_output_aliases={q: q for q in range(nb)},
        scratch_shapes=[pltpu.SemaphoreType.DMA((n_sems,))] * 4, compiler_params=_cp(),
    )(*bufs)


def weights_allgather(bufs, items, name):
    nb = len(bufs)

    def body(*refs):
        out_refs = refs[nb:2 * nb]
        s1, r1, s2, r2 = refs[2 * nb:]
        sends = _chip_sends(out_refs, items, s1, r1)
        for cp in sends:
            cp.start()
        _chip_recv_waits(out_refs, items, s1, r1)
        _sibling_forward(out_refs, items, s2, r2)
        for cp in sends:
            cp.wait_send()

    return _inplace_comm_call(body, bufs, 3 * len(items), name)


def weights_forward(bufs, items, name):
    nb = len(bufs)

    def body(*refs):
        s1, r1, _, _ = refs[2 * nb:]
        _sibling_forward(refs[nb:2 * nb], items, s1, r1)

    return _inplace_comm_call(body, bufs, 3 * len(items), name)


def grads_pair_exchange(gs, items, name):
    ng, ni = len(gs), len(items)

    def body(*refs):
        g_refs, out_refs = refs[:ng], refs[ng:ng + ni]
        send_sems, recv_sems = refs[ng + ni:]
        x, y, c = _mesh_pos()
        cps = []
        for q, (a, row0, nrows) in enumerate(items):
            h = nrows // 2
            cps.append(_rcopy(g_refs[a].at[:, pl.ds(row0 + (1 - c) * h, h)], out_refs[q], send_sems, recv_sems, q, (x, y, 1 - c)))
            cps[-1].start()
        for cp in cps:
            cp.wait()

    return pl.pallas_call(
        body, name=name, out_shape=[jax.ShapeDtypeStruct((N_CHIP, nrows // 2, gs[a].shape[2]), BF16) for a, _, nrows in items],
        in_specs=[_HBM] * ng, out_specs=[_HBM] * ni,
        scratch_shapes=[pltpu.SemaphoreType.DMA((ni,)), pltpu.SemaphoreType.DMA((ni,))], compiler_params=_cp(),
    )(*gs)


def pair_add(g, got, c, chip, row0, name):
    n, h, c_ = got.shape
    tr = _stream_rows(math.gcd(h, row0) if row0 else h, n * c_ * 2)
    nblk = h // tr

    def body(s_ref, g_ref, o_ref, pair_ref, land_ref):
        pair_ref[...] = (g_ref[...].astype(F32) + o_ref[...].astype(F32)).astype(BF16)
        me = s_ref[1]
        land_ref[...] = (g_ref[me].astype(F32) + o_ref[me].astype(F32)).astype(BF16)

    grid_spec = pltpu.PrefetchScalarGridSpec(
        num_scalar_prefetch=1, grid=(nblk,),
        in_specs=[pl.BlockSpec((n, tr, c_), lambda i, s: (0, row0 // tr + s[0] * nblk + i, 0)), pl.BlockSpec((n, tr, c_), lambda i, s: (0, i, 0))],
        out_specs=[pl.BlockSpec((n, tr, c_), lambda i, s: (0, i, 0)), pl.BlockSpec((None, tr, c_), lambda i, s: (s[1], i, 0))])
    return pl.pallas_call(body, name=name, grid_spec=grid_spec, out_shape=[jax.ShapeDtypeStruct((n, h, c_), BF16)] * 2,
                          compiler_params=_cp())(_scalars(c, chip), g, got)


def _rs_sends(refs, send_sems, recv_sems):
    ng = len(refs) // 2
    x, y, c = _mesh_pos()
    return [_rcopy(refs[q].at[2 * px + py], refs[ng + q].at[2 * x + y], send_sems, recv_sems, 3 * q + j, (px, py, c))
            for q in range(ng) for j, (px, py) in enumerate(_other_chips(x, y))]


def _rs_recv_waits(refs, send_sems, recv_sems):
    ng = len(refs) // 2
    x, y, c = _mesh_pos()
    for q in range(ng):
        for j, (px, py) in enumerate(_other_chips(x, y)):
            _rcopy(refs[q].at[2 * x + y], refs[ng + q].at[2 * px + py], send_sems, recv_sems, 3 * q + j, (x, y, c)).wait_recv()


def rs_carry(pairs, lands):
    return (list(pairs) + list(lands), 3 * len(pairs), _rs_sends, _rs_recv_waits)


def gather_carry(bufs, items):
    return (list(bufs), 3 * len(items), lambda refs, ss, rs: _chip_sends(refs, items, ss, rs),
            lambda refs, ss, rs: _chip_recv_waits(refs, items, ss, rs))


def grads_chip_exchange(pairs, lands, name):
    ng = len(pairs)

    def body(*refs):
        cbufs, sems = refs[2 * ng:4 * ng], refs[4 * ng:]
        sends = _rs_sends(cbufs, *sems)
        for cp in sends:
            cp.start()
        _rs_recv_waits(cbufs, *sems)
        for cp in sends:
            cp.wait_send()

    res = pl.pallas_call(
        body, name=name, out_shape=[jax.ShapeDtypeStruct(a.shape, a.dtype) for a in list(pairs) + list(lands)],
        in_specs=[_HBM] * (2 * ng), out_specs=[_HBM] * (2 * ng), input_output_aliases={q: q for q in range(2 * ng)},
        scratch_shapes=[pltpu.SemaphoreType.DMA((3 * ng,)), pltpu.SemaphoreType.DMA((3 * ng,))], compiler_params=_cp(),
    )(*pairs, *lands)
    return res[ng:]


def sum_chips(a, c, into, shape, row0, name):
    k, h, c_ = a.shape
    tr = _stream_rows(math.gcd(h, row0) if row0 else h, k * c_ * 2)
    nblk = h // tr

    def body(s_ref, a_ref, *rest):
        acc = a_ref[0].astype(F32)
        for q in range(1, k):
            acc = acc + a_ref[q].astype(F32)
        rest[-1][...] = acc

    in_specs, args, aliases = [pl.BlockSpec((k, tr, c_), lambda i, s: (0, i, 0))], [_scalars(c), a], {}
    if into is not None:
        in_specs.append(pl.BlockSpec(memory_space=pl.ANY))
        args.append(into)
        aliases = {2: 0}
    grid_spec = pltpu.PrefetchScalarGridSpec(
        num_scalar_prefetch=1, grid=(nblk,), in_specs=in_specs,
        out_specs=pl.BlockSpec((tr, c_), lambda i, s: (row0 // tr + s[0] * nblk + i, 0)))
    return pl.pallas_call(body, name=name, grid_spec=grid_spec, out_shape=jax.ShapeDtypeStruct(shape, F32),
                          input_output_aliases=aliases, compiler_params=_cp())(*args)


def halves_exchange(outs, items, name):
    ng = len(outs)

    def body(*refs):
        out_refs = refs[ng:2 * ng]
        send_sems, recv_sems = refs[2 * ng:]
        x, y, c = _mesh_pos()
        rows = lambda a, row0, nrows, half: out_refs[a].at[pl.ds(row0 + half * (nrows // 2), nrows // 2)]
        cps = [_rcopy(rows(*it, c), rows(*it, c), send_sems, recv_sems, q, (x, y, 1 - c)) for q, it in enumerate(items)]
        for cp in cps:
            cp.start()
        for q, it in enumerate(items):
            _rcopy(rows(*it, 1 - c), rows(*it, 1 - c), send_sems, recv_sems, q, (x, y, c)).wait_recv()
        for cp in cps:
            cp.wait_send()

    return pl.pallas_call(
        body, name=name, out_shape=[jax.ShapeDtypeStruct(a.shape, a.dtype) for a in outs],
        in_specs=[_HBM] * ng, out_specs=[_HBM] * ng, input_output_aliases={q: q for q in range(ng)},
        scratch_shapes=[pltpu.SemaphoreType.DMA((len(items),)), pltpu.SemaphoreType.DMA((len(items),))], compiler_params=_cp(),
    )(*outs)


def _rows8(a):
    return jnp.pad(a, ((0, -a.shape[0] % SUBLANES), (0, 0)))


def _chips_cols(g, rows):
    return jnp.concatenate([g[2 * j, :rows] for j in range(N_CHIP)], axis=-1)


BIG = (("ssd_w_in", "col"), ("ssd_w_out", "row"), ("attn_w_qkv", "col"), ("attn_w_o", "row"), ("ffn_w_up", "col"), ("ffn_w_down", "row"))
WEIGHTS = ("c_ctx", "ada_w", "ada_b", "norm1_w", "norm2_w", "ssd_w_in", "ssd_conv_w", "ssd_conv_b", "ssd_dt_bias_f", "ssd_dt_bias_b",
           "ssd_a_log_f", "ssd_a_log_b", "ssd_d", "ssd_norm_w", "ssd_w_out", "attn_w_qkv", "attn_q_gain", "attn_k_gain", "attn_sinks",
           "attn_w_o", "ffn_w_up", "ffn_conv_w", "ffn_conv_b", "ffn_w_down")


def _taps_bias(w3, b):
    return jnp.concatenate([w3, b[None, :], jnp.zeros((SUBLANES - 4, w3.shape[1]), F32)], axis=0)


def kernel(x, c, ctx, c_ctx, ada_w, ada_b, norm1_w, norm2_w, ssd_w_in, ssd_conv_w, ssd_conv_b, ssd_dt_bias_f, ssd_dt_bias_b, ssd_a_log_f, ssd_a_log_b, ssd_d, ssd_norm_w, ssd_w_out, attn_w_qkv, attn_q_gain, attn_k_gain, attn_sinks, attn_w_o, ffn_w_up, ffn_conv_w, ffn_conv_b, ffn_w_down, loss_target, m_c_ctx, m_ada_w, m_ada_b, m_norm1_w, m_norm2_w, m_ssd_w_in, m_ssd_conv_w, m_ssd_conv_b, m_ssd_dt_bias_f, m_ssd_dt_bias_b, m_ssd_a_log_f, m_ssd_a_log_b, m_ssd_d, m_ssd_norm_w, m_ssd_w_out, m_attn_w_qkv, m_attn_q_gain, m_attn_k_gain, m_attn_sinks, m_attn_w_o, m_ffn_w_up, m_ffn_conv_w, m_ffn_conv_b, m_ffn_w_down, v_c_ctx, v_ada_w, v_ada_b, v_norm1_w, v_norm2_w, v_ssd_w_in, v_ssd_conv_w, v_ssd_conv_b, v_ssd_dt_bias_f, v_ssd_dt_bias_b, v_ssd_a_log_f, v_ssd_a_log_b, v_ssd_d, v_ssd_norm_w, v_ssd_w_out, v_attn_w_qkv, v_attn_q_gain, v_attn_k_gain, v_attn_sinks, v_attn_w_o, v_ffn_w_up, v_ffn_conv_w, v_ffn_conv_b, v_ffn_w_down):
    args = locals()
    w = {n: args[n] for n in WEIGHTS}
    mom = {n: args["m_" + n] for n in WEIGHTS}
    var = {n: args["v_" + n] for n in WEIGHTS}

    ix, iy, ic = _mesh_pos()
    chip = 2 * ix + iy
    dev = 2 * chip + ic
    depth, d = norm1_w.shape
    n_ctx, seq = ctx.shape[1], x.shape[1]
    t = n_ctx + seq
    d_inner = ssd_norm_w.shape[1]
    heads = ssd_d.shape[1]
    n_qh = attn_sinks.shape[1]
    grp = n_qh // ATTN_KV_HEADS
    d_ff = ffn_w_down.shape[1] * N_CHIP
    xbc_w = ssd_conv_b.shape[1]
    dt_col = d_inner + xbc_w
    n_ssd, n_att = ssd_w_in.shape[0], attn_w_qkv.shape[0]

    sconv_rows, fconv_rows = n_ssd * 3, depth * 3
    g_c, g_sconv, g_fconv = small_allgather(
        [_rows8(c), _rows8(ssd_conv_w.reshape(sconv_rows, -1)), _rows8(ffn_conv_w.reshape(fconv_rows, -1))], "gather_cond")
    c_all = g_c[:, 0]
    ssd_conv_full = _chips_cols(g_sconv, sconv_rows).reshape(n_ssd, 3, -1)
    ffn_conv_full = _chips_cols(g_fconv, fconv_rows).reshape(depth, 3, -1)

    cvec = jnp.concatenate([c_all, c_ctx[None, :], jnp.zeros((ADA_ROWS - N_DEV - 1, d), F32)], axis=0)
    cs16 = _silu(cvec).astype(BF16)
    mod_cols = ada_fwd(cs16, ada_w, "ada_fwd")
    ns_ada = mod_cols.shape[-1]
    (g_mod,) = small_allgather([mod_cols.reshape(depth * ADA_ROWS, ns_ada)], "gather_mod")
    mod_all = _chips_cols(g_mod, depth * ADA_ROWS).reshape(depth, ADA_ROWS, -1) + ada_b[:, None, :]
    mod_lat = lax.dynamic_index_in_dim(mod_all, dev, axis=1, keepdims=False)
    mod_ctx = mod_all[:, N_DEV]
    mods = jnp.stack([mod_ctx, mod_lat], axis=1).reshape(depth, 2, 6, d)

    bidx = {n: q for q, (n, _) in enumerate(BIG)}
    bufs = [place_own(w[n], chip, f"place_{n}") for n, _ in BIG]

    def layer_items(layer):
        mixer = ("ssd_w_in", "ssd_w_out") if layer % 2 == 0 else ("attn_w_qkv", "attn_w_o")
        return [(bidx[n], layer // 2) for n in mixer] + [(bidx["ffn_w_up"], layer), (bidx["ffn_w_down"], layer)]

    def ssd_in_full(layer):
        return bufs[bidx["ssd_w_in"]][layer].transpose(1, 0, 2).reshape(d, -1)

    def w_col(n, layer):
        b = bufs[bidx[n]]
        return Mat(b.reshape(-1, *b.shape[2:]), "cols3", base=layer * N_CHIP, nparts=N_CHIP)

    def w_row(n, layer):
        b = bufs[bidx[n]]
        rows = N_CHIP * b.shape[2]
        return Mat(b.reshape(-1, b.shape[3]), "rows", rows=rows, row0=layer * rows)

    def subset(items):
        used = sorted({b for b, _ in items})
        return [bufs[b] for b in used], [(used.index(b), l) for b, l in items], used

    def put_back(used, new):
        for b, a in zip(used, new):
            bufs[b] = a

    sub, its, used = subset(layer_items(0)[:1])
    put_back(used, weights_allgather(sub, its, "gather_weights_first"))

    cos, sin = rope_tables(n_ctx, seq)
    sel = _group_select(heads, heads // SSD_GROUPS)
    bias128 = jnp.concatenate([ssd_dt_bias_f, ssd_dt_bias_b], axis=-1)[:, None, :]
    alog128 = jnp.concatenate([ssd_a_log_f, ssd_a_log_b], axis=-1)[:, None, :]
    dskip = jnp.repeat(ssd_d, SSD_HEAD_DIM, axis=-1)[:, None, :]
    gains = jnp.zeros((n_att, SUBLANES, ATTN_HEAD_DIM), F32).at[:, 0].set(attn_q_gain).at[:, 1].set(attn_k_gain)
    sinks3 = jnp.zeros((n_att, ATTN_KV_HEADS, SUBLANES, LANES), F32).at[:, :, 0, :grp].set(attn_sinks.reshape(n_att, ATTN_KV_HEADS, grp))
    wb_ssd = [_taps_bias(ssd_conv_full[j], ssd_conv_b[j]) for j in range(n_ssd)]
    wb_ffn = [_taps_bias(ffn_conv_full[i], ffn_conv_b[i]) for i in range(depth)]

    xs = jnp.concatenate([ctx[0], x[0]], axis=0)
    saved = []
    for i in range(depth):
        j = i // 2
        sh1, sc1, g1, sh2, sc2, g2 = [mods[i, :, q] for q in range(6)]
        s = {"x": xs}
        h1 = norm_mod(xs, norm1_w[i:i + 1], sh1, sc1, n_ctx, f"l{i}_norm1")
        s["h1"] = h1
        if i % 2 == 0:
            w_in = ssd_in_full(j)
            if i == 0:
                sub, its, used = subset(layer_items(0)[1:])
                zx, carried = matmul(h1, w_in, "nn", f"l{i}_ssd_in", carry=gather_carry(sub, its))
                put_back(used, weights_forward(carried, its, "forward_weights_l0"))
            else:
                zx = matmul(h1, w_in, "nn", f"l{i}_ssd_in")
        sub, its, used = subset(layer_items(i + 1)) if i + 1 < depth else (None, None, None)
        nxt = gather_carry(sub, its) if sub is not None else None
        if i % 2 == 0:
            xbc = dwconv_act(zx, d_inner, xbc_w, wb_ssd[j], n_ctx, "silu", F32, f"l{i}_ssd_conv")
            _, da, dtg, ag, agt = ssd_prep(zx, dt_col, bias128[j], alog128[j], sel, heads, f"l{i}_ssd_prep")
            yf, hf, *carried = ssd_scan(xbc, dtg, ag, agt, d_inner, n_ctx, False, f"l{i}_ssd_scan_f", carry=nxt)
            if nxt is not None:
                put_back(used, weights_forward(carried, its, f"forward_weights_l{i + 1}"))
            yb, hb = ssd_scan(xbc, dtg, ag, agt, d_inner, n_ctx, True, f"l{i}_ssd_scan_b")
            ytot, yn = ssd_gate_norm(yf, yb, xbc, zx, dskip[j], ssd_norm_w[j:j + 1], f"l{i}_ssd_gate_norm")
            x1, mix = matmul_gate_res(yn, w_row("ssd_w_out", j), xs, g1, n_ctx, f"l{i}_ssd_out")
            s.update(w_in=w_in, zx=zx, xbc=xbc, da=da, dtg=dtg, ag=ag, agt=agt, hf=hf, hb=hb, ytot=ytot, yn=yn)
        else:
            qkv = matmul(h1, w_col("attn_w_qkv", j), "nn", f"l{i}_attn_qkv")
            qkvr = qk_prep(qkv, gains[j], cos, sin, n_qh, ATTN_KV_HEADS, f"l{i}_qk_prep")
            o, lse, *carried = attention(qkvr, sinks3[j], n_ctx, n_qh, f"l{i}_attn", carry=nxt)
            if nxt is not None:
                put_back(used, weights_forward(carried, its, f"forward_weights_l{i + 1}"))
            x1, mix = matmul_gate_res(o, w_row("attn_w_o", j), xs, g1, n_ctx, f"l{i}_attn_out")
            s.update(qkv=qkv, qkvr=qkvr, o=o, lse=lse)
        h2 = norm_mod(x1, norm2_w[i:i + 1], sh2, sc2, n_ctx, f"l{i}_norm2")
        u = matmul(h2, w_col("ffn_w_up", i), "nn", f"l{i}_ffn_up", out_dtype=BF16)
        act = dwconv_act(u, 0, 2 * d_ff, wb_ffn[i], n_ctx, "glu", BF16, f"l{i}_ffn_conv")
        x2, f = matmul_gate_res(act, w_row("ffn_w_down", i), x1, g2, n_ctx, f"l{i}_ffn_down")
        s.update(mix=mix, x1=x1, h2=h2, u=u, act=act, f=f)
        saved.append(s)
        xs = x2

    dxs, sq = loss_grad(xs, loss_target[0], n_ctx, "loss")
    loss = lax.psum(0.5 / d * jnp.sum(sq[0]), ("x", "y", "c"))

    gbuf = {n: None for n, _ in BIG}
    gshape = {n: ((N_CHIP, b.shape[0] * b.shape[2], b.shape[3]), b.shape[2]) for (n, _), b in zip(BIG, bufs)}

    def dw_into(n, kind, layer, a, b, name):
        shape, rows_per_layer = gshape[n]
        gbuf[n] = matmul(a, b, "tn", name, out_dtype=BF16, into=(kind, gbuf[n], shape, layer * rows_per_layer))

    ssd_in_g = [None] * n_ssd
    reduced, reduced_items = {}, []

    def start_rs(layer):
        gs, items, dest = [], [], []
        for b, l in layer_items(layer):
            n = BIG[b][0]
            if n == "ssd_w_in":
                gs.append(ssd_in_g[l].reshape(d, N_CHIP, -1).transpose(1, 0, 2).astype(BF16))
                items.append((len(gs) - 1, 0, d))
                dest.append(((n, l), (d, gs[-1].shape[2]), 0))
            else:
                shape, rows_per_layer = gshape[n]
                gs.append(gbuf[n])
                items.append((len(gs) - 1, l * rows_per_layer, rows_per_layer))
                dest.append((n, shape[1:], l * rows_per_layer))
        got = grads_pair_exchange(gs, items, f"rs_pair_exchange_l{layer}")
        both = [pair_add(gs[a], o, ic, chip, row0, f"rs_pair_add_l{layer}_{q}") for q, ((a, row0, _), o) in enumerate(zip(items, got))]
        return [p for p, _ in both], [ld for _, ld in both], dest

    def finish_rs(pend, carried):
        if not pend:
            return None
        landed = carried[len(pend[0]):]
        for q, (a, (key, shape, row0)) in enumerate(zip(landed, pend[2])):
            reduced[key] = sum_chips(a, ic, reduced.get(key), shape, row0, f"rs_chip_sum_{q}_r{row0}_{key if isinstance(key, str) else key[0] + str(key[1])}")
            reduced_items.append((key, row0, 2 * a.shape[1]))
        return None

    pending = None
    st_norm1, st_norm2, st_gate1, st_gate2 = ([None] * depth for _ in range(4))
    st_sconv, st_snorm, st_sd, st_sdt = ([None] * n_ssd for _ in range(4))
    st_gain, st_sink = [None] * n_att, [None] * n_att
    st_fconv = [None] * depth
    for i in reversed(range(depth)):
        j = i // 2
        s = saved[i]
        sh1, sc1, g1, sh2, sc2, g2 = [mods[i, :, q] for q in range(6)]
        df, st_gate2[i] = gate_bwd(dxs, s["f"], g2, n_ctx, f"l{i}_ffn_gate_bwd")
        dact = matmul(df, w_row("ffn_w_down", i), "nt", f"l{i}_ffn_down_dx")
        dw_into("ffn_w_down", "row", i, s["act"], df, f"l{i}_ffn_down_dw")
        du3, st_fconv[i] = dwconv_act_bwd(dact, 0, s["u"], 0, wb_ffn[i], 0, d_ff, n_ctx, "glu", f"l{i}_ffn_conv_bwd")
        du = Mat(du3, "cols3", nparts=2)
        dh2 = matmul(du, w_col("ffn_w_up", i), "nt", f"l{i}_ffn_up_dx")
        dw_into("ffn_w_up", "col", i, s["h2"], du, f"l{i}_ffn_up_dw")
        dx1, st_norm2[i] = norm_mod_bwd(dh2, dxs, s["x1"], norm2_w[i:i + 1], sc2, n_ctx, f"l{i}_norm2_bwd")
        dmix, st_gate1[i] = gate_bwd(dx1, s["mix"], g1, n_ctx, f"l{i}_mix_gate_bwd")
        if i % 2 == 0:
            zx = s["zx"]
            dyn = matmul(dmix, w_row("ssd_w_out", j), "nt", f"l{i}_ssd_out_dx")
            dw_into("ssd_w_out", "row", j, s["yn"], dmix, f"l{i}_ssd_out_dw")
            dy, dzx, st_snorm[j] = ssd_gate_norm_bwd(dyn, s["ytot"], zx, ssd_norm_w[j:j + 1], f"l{i}_ssd_gate_norm_bwd")
            o1 = ssd_scan_bwd(dy, s["xbc"], s["dtg"], s["ag"], s["agt"], s["hf"], dskip[j], None, d_inner, n_ctx, False, f"l{i}_ssd_scan_f_bwd",
                              carry=rs_carry(*pending[:2]) if pending else None)
            o1, pending = o1[:7], finish_rs(pending, o1[7:])
            o2 = ssd_scan_bwd(dy, s["xbc"], s["dtg"], s["ag"], s["agt"], s["hb"], None, o1[:3], d_inner, n_ctx, True, f"l{i}_ssd_scan_b_bwd")
            st_sd[j] = o1[6]
            gn = SSD_GROUPS * SSD_STATE
            conv_st = []
            for src, width, col, tag in ((o2[0], d_inner, 0, "x"), (o2[1], gn, d_inner, "b"), (o2[2], gn, d_inner + gn, "c")):
                dzx, st = dwconv_act_bwd(src, 0, zx, d_inner + col, wb_ssd[j], col, width, n_ctx, "silu",
                                         f"l{i}_ssd_conv_bwd_{tag}", into=dzx, ocol0=d_inner + col)
                conv_st.append(st)
            st_sconv[j] = jnp.concatenate(conv_st, axis=1)
            dzx, st_sdt[j] = ssd_prep_bwd(zx, dt_col, bias128[j], alog128[j], sel, o1[3:6], o2[3:6], s["da"], heads, dzx, f"l{i}_ssd_prep_bwd")
            dh1 = matmul(dzx, s["w_in"], "nt", f"l{i}_ssd_in_dx")
            ssd_in_g[j] = matmul(s["h1"], dzx, "tn", f"l{i}_ssd_in_dw")
        else:
            do = matmul(dmix, w_row("attn_w_o", j), "nt", f"l{i}_attn_out_dx")
            dw_into("attn_w_o", "row", j, s["o"], dmix, f"l{i}_attn_out_dw")
            ab = attention_bwd(do, s["o"], s["lse"], s["qkvr"], sinks3[j], n_ctx, n_qh, f"l{i}_attn_bwd",
                               carry=rs_carry(*pending[:2]) if pending else None)
            (dq, dkc, dvc, dkp, dvp, st_sink[j]), pending = ab[:6], finish_rs(pending, ab[6:])
            dk = band_reduce(dkc, dkp, n_ctx, f"l{i}_dk_reduce")
            dv = band_reduce(dvc, dvp, n_ctx, f"l{i}_dv_reduce")
            dqkv, st_gain[j] = qk_prep_bwd(dq, dk, dv, s["qkv"], gains[j], cos, sin, f"l{i}_qk_prep_bwd")
            dh1 = matmul(dqkv, w_col("attn_w_qkv", j), "nt", f"l{i}_attn_qkv_dx")
            dw_into("attn_w_qkv", "col", j, s["h1"], dqkv, f"l{i}_attn_qkv_dw")
        dxs, st_norm1[i] = norm_mod_bwd(dh1, dx1, s["x"], norm1_w[i:i + 1], sc1, n_ctx, f"l{i}_norm1_bwd")
        pending = start_rs(i)
    pairs, lands, _ = pending
    finish_rs(pending, list(pairs) + list(grads_chip_exchange(pairs, lands, "rs_chip_exchange_l0")))
    grad_x = dxs[n_ctx:][None]

    rows_d = ([st_norm1[i][4:5] for i in range(depth)] + [st_norm2[i][4:5] for i in range(depth)]
              + [st[seg:seg + 1] for seg in (0, 1) for i in range(depth)
                 for st in (st_norm1[i][0:2], st_norm1[i][2:4], st_gate1[i][0:2], st_norm2[i][0:2], st_norm2[i][2:4], st_gate2[i][0:2])])
    a_d = jnp.concatenate(rows_d, axis=0)
    a_sconv = _rows8(jnp.concatenate([st[0:4] for st in st_sconv], axis=0))
    a_fconv = _rows8(jnp.concatenate([jnp.concatenate([st[0, 0:4], st[1, 0:4]], axis=1) for st in st_fconv], axis=0))
    a_di = _rows8(jnp.concatenate([st[0:1] for st in st_snorm] + [st[0:1] for st in st_sd], axis=0))
    a_128 = _rows8(jnp.concatenate([st[0:2] for st in st_sdt] + [st[0:2] for st in st_gain] + [st[:, 0] for st in st_sink], axis=0))
    gathered = small_allgather([a_d, a_sconv, a_fconv, a_di, a_128], "gather_small_grads")
    s_d, s_sconv, s_fconv, s_di, s_128 = [sum_leading(g, f"sum_small_grads_{q}") for q, g in enumerate(gathered)]
    grads = {"norm1_w": s_d[0:depth], "norm2_w": s_d[depth:2 * depth]}
    dctx_sum = s_d[2 * depth:8 * depth].reshape(depth, 6 * d)
    grads["ada_b"] = dctx_sum + s_d[8 * depth:14 * depth].reshape(depth, 6 * d)
    sc = s_sconv[:4 * n_ssd].reshape(n_ssd, 4, -1)
    own_cols = lambda a, width: lax.dynamic_slice_in_dim(a, chip * width, width, axis=a.ndim - 1)
    grads["ssd_conv_w"], grads["ssd_conv_b"] = own_cols(sc[:, 0:3], ssd_conv_w.shape[-1]), sc[:, 3]
    fc = s_fconv[:4 * depth].reshape(depth, 4, -1)
    grads["ffn_conv_w"], grads["ffn_conv_b"] = own_cols(fc[:, 0:3], ffn_conv_w.shape[-1]), fc[:, 3]
    grads["ssd_norm_w"] = s_di[0:n_ssd]
    grads["ssd_d"] = jnp.sum(s_di[n_ssd:2 * n_ssd].reshape(n_ssd, heads, SSD_HEAD_DIM), axis=-1)
    dt_st = s_128[0:2 * n_ssd].reshape(n_ssd, 2, LANES)
    grads["ssd_dt_bias_f"], grads["ssd_dt_bias_b"] = dt_st[:, 0, :heads], dt_st[:, 0, heads:]
    grads["ssd_a_log_f"], grads["ssd_a_log_b"] = dt_st[:, 1, :heads], dt_st[:, 1, heads:]
    gain_st = s_128[2 * n_ssd:2 * n_ssd + 2 * n_att].reshape(n_att, 2, LANES)
    grads["attn_q_gain"], grads["attn_k_gain"] = gain_st[:, 0], gain_st[:, 1]
    sink_st = s_128[2 * n_ssd + 2 * n_att:2 * n_ssd + 2 * n_att + ATTN_KV_HEADS * n_att].reshape(n_att, ATTN_KV_HEADS, LANES)
    grads["attn_sinks"] = sink_st[:, :, :grp].reshape(n_att, n_qh)

    dlat_rows = gathered[0][:, 8 * depth:14 * depth].reshape(N_DEV, depth, 6 * d)
    gmod = jnp.concatenate([dlat_rows, dctx_sum[None], jnp.zeros((ADA_ROWS - N_DEV - 1, depth, 6 * d), F32)], axis=0).transpose(1, 0, 2)
    grads["ada_w"], dcs = ada_bwd(cs16, own_cols(gmod, ns_ada), ada_w, "ada_bwd")
    dcc = (dcs[N_DEV] * _dsilu(c_ctx))[None, :]
    (g_dcc,) = small_allgather([_rows8(dcc)], "gather_dc_ctx")
    grads["c_ctx"] = sum_leading(g_dcc[0::2], "sum_dc_ctx")[0]

    keys = list(reduced)
    red = dict(zip(keys, halves_exchange([reduced[k] for k in keys], [(keys.index(k), r0, nr) for k, r0, nr in reduced_items],
                                         "rs_halves_exchange")))
    for n, _ in BIG:
        grads[n] = jnp.stack([red[(n, l)] for l in range(n_ssd)]) if n == "ssd_w_in" else red[n].reshape(w[n].shape)

    delta, new_m, new_v = {}, {}, {}
    for n in WEIGHTS:
        shp = w[n].shape
        two = lambda a: a.reshape(-1, shp[-1])
        dl, nm, nv = adamw(two(w[n]), two(grads[n]), two(mom[n]), two(var[n]), f"adamw_{n}")
        delta[n], new_m[n], new_v[n] = dl.reshape(shp), nm.reshape(shp), nv.reshape(shp)
    grads = {n: grads[n].reshape(w[n].shape) for n in WEIGHTS}
    return (loss, grad_x, *[grads[n] for n in WEIGHTS], *[delta[n] for n in WEIGHTS], *[new_m[n] for n in WEIGHTS], *[new_v[n] for n in WEIGHTS])
```

```python
import functools
import math

import jax
import jax.numpy as jnp
from jax import lax
from jax.experimental import pallas as pl
from jax.experimental.pallas import tpu as pltpu

F32 = jnp.float32
BF16 = jnp.bfloat16

SSD_HEAD_DIM = 64
SSD_GROUPS = 8
SSD_STATE = 128
CHUNK = 128
ATTN_HEAD_DIM = 128
ATTN_KV_HEADS = 4
GRID_W = 64
ROPE_THETA = 10000.0
NORM_EPS = 1e-6
ADAM_LR, ADAM_B1, ADAM_B2, ADAM_EPS, ADAM_WD, ADAM_STEP = 0.001, 0.9, 0.999, 1e-08, 0.01, 10

LANES = 128
SUBLANES = 8
VMEM_LIMIT = 56 * 1024 * 1024
MESH = pl.DeviceIdType.MESH
N_DEV = 8
N_CHIP = 4


def _cp(**kw):
    return pltpu.CompilerParams(vmem_limit_bytes=VMEM_LIMIT, **kw)


def _pallas(body, name, grid, in_specs, out_specs, out_shape, scratch, args, dims, aliases=None, carry=None):
    in_specs, out_specs, out_shape, scratch, args = list(in_specs), list(out_specs), list(out_shape), list(scratch), list(args)
    aliases = dict(aliases or {})
    if carry is not None:
        bufs, n_sems, sends, recv_waits = carry
        n_in, n_out, nb = len(in_specs), len(out_specs), len(bufs)
        inner = body

        def body(*refs):
            ins, outs = refs[:n_in], refs[n_in + nb:n_in + nb + n_out]
            cbufs = refs[n_in + nb + n_out:n_in + 2 * nb + n_out]
            rest = refs[n_in + 2 * nb + n_out:]
            sems = rest[-2:]
            ids = [pl.program_id(q) for q in range(len(grid))]
            first = functools.reduce(jnp.logical_and, [i == 0 for i in ids])
            last = functools.reduce(jnp.logical_and, [i == g - 1 for i, g in zip(ids, grid)])

            @pl.when(first)
            def _():
                for cp in sends(cbufs, *sems):
                    cp.start()

            inner(*ins, *outs, *rest[:-2])

            @pl.when(last)
            def _():
                recv_waits(cbufs, *sems)
                for cp in sends(cbufs, *sems):
                    cp.wait_send()

        aliases.update({n_in + q: n_out + q for q in range(nb)})
        in_specs += [pl.BlockSpec(memory_space=pltpu.HBM)] * nb
        out_specs += [pl.BlockSpec(memory_space=pltpu.HBM)] * nb
        out_shape += [jax.ShapeDtypeStruct(b.shape, b.dtype) for b in bufs]
        scratch += [pltpu.SemaphoreType.DMA((n_sems,))] * 2
        args += list(bufs)
    return pl.pallas_call(
        body, name=name, grid=grid, in_specs=in_specs, out_specs=out_specs, out_shape=out_shape, scratch_shapes=scratch,
        input_output_aliases=aliases, compiler_params=_cp(dimension_semantics=dims),
    )(*args)


def _pick(n, cands):
    for c in cands:
        if n % c == 0:
            return c
    return n


def _silu(x):
    return x * jax.nn.sigmoid(x)


def _dsilu(x):
    s = jax.nn.sigmoid(x)
    return s * (1.0 + x * (1.0 - s))


_DIMS = {"nn": (((1,), (0,)), ((), ())), "nt": (((1,), (1,)), ((), ())), "tn": (((0,), (0,)), ((), ()))}


TILES_M = (1408, 768, 512, 384, 256, 128)
TILES_N = (1408, 1024, 1152, 768, 512, 384, 256, 128)
TILES_K = (2048, 1408, 1152, 1024, 768, 512, 384, 256, 128)


class Mat:
    def __init__(self, arr, kind="plain", rows=None, row0=0, base=0, nparts=1):
        self.arr, self.kind, self.row0, self.base, self.nparts = arr, kind, row0, base, nparts
        if kind == "cols3":
            self.r, self.s = arr.shape[1], arr.shape[2] * nparts
        else:
            self.r, self.s = (rows if rows is not None else arr.shape[0]), arr.shape[1]

    def s_unit(self):
        return self.s // self.nparts

    def spec(self, tr, ts, r_of, s_of):
        if self.kind == "cols3":
            nps = self.s // self.nparts // ts
            return pl.BlockSpec((None, tr, ts), lambda *ids: (self.base + s_of(*ids) // nps, r_of(*ids), s_of(*ids) % nps))
        off = self.row0 // tr
        return pl.BlockSpec((tr, ts), lambda *ids: (off + r_of(*ids), s_of(*ids)))


MATMUL_VMEM_BUDGET = 36 * 1024 * 1024


def _fit_tiles(m, n, k, m_unit, n_unit, k_unit, result_bytes):
    cands = [[c for c in tiles if dim % c == 0 and unit % c == 0]
             for tiles, dim, unit in ((TILES_M, m, m_unit), (TILES_N, n, n_unit), (TILES_K, k, k_unit))]
    assert all(cands), (m, n, k, m_unit, n_unit, k_unit)
    idx = [0, 0, 0]
    while True:
        tm, tn, tk = (c[i] for c, i in zip(cands, idx))
        if 2 * 2 * (tm * tk + tk * tn) + tm * tn * (4 + 2 * result_bytes) <= MATMUL_VMEM_BUDGET:
            return tm, tn, tk
        shrinkable = [q for q in range(3) if idx[q] + 1 < len(cands[q])]
        assert shrinkable, (m, n, k)
        q = max(shrinkable, key=lambda q: cands[q][idx[q]])
        idx[q] += 1


def matmul(a, b, mode, name, out_dtype=F32, into=None, carry=None):
    a = a if isinstance(a, Mat) else Mat(a)
    b = b if isinstance(b, Mat) else Mat(b)
    m, k = (a.s, a.r) if mode == "tn" else (a.r, a.s)
    n, kb = (b.r, b.s) if mode == "nt" else (b.s, b.r)
    assert k == kb, (name, a.r, a.s, b.r, b.s)
    m_unit = a.s_unit() if mode == "tn" else m
    k_unit = math.gcd(a.s_unit() if mode != "tn" else k, b.s_unit() if mode == "nt" else k)
    n_unit = n if mode == "nt" else b.s_unit()
    if into is not None:
        kind, buf, shape, row0 = into
        if kind == "col":
            n_unit = math.gcd(n_unit, shape[2])
        else:
            m_unit = math.gcd(m_unit, m // N_CHIP)
    tm, tn, tk = _fit_tiles(m, n, k, m_unit, n_unit, k_unit, jnp.dtype(out_dtype).itemsize)
    nk = k // tk
    ii, jj, kk_ = (lambda j, i, kk: i), (lambda j, i, kk: j), (lambda j, i, kk: kk)
    a_spec = a.spec(tk, tm, kk_, ii) if mode == "tn" else a.spec(tm, tk, ii, kk_)
    b_spec = b.spec(tn, tk, jj, kk_) if mode == "nt" else b.spec(tk, tn, kk_, jj)
    in_specs, args, aliases = [a_spec, b_spec], [a.arr, b.arr], {}
    if into is None:
        out_spec = pl.BlockSpec((tm, tn), lambda j, i, kk: (i, j))
        out_shape = jax.ShapeDtypeStruct((m, n), out_dtype)
    else:
        assert row0 % tm == 0
        r0 = row0 // tm
        if kind == "col":
            npn = shape[2] // tn
            out_spec = pl.BlockSpec((None, tm, tn), lambda j, i, kk: (j // npn, r0 + i, j % npn))
        else:
            npm = m // N_CHIP // tm
            out_spec = pl.BlockSpec((None, tm, tn), lambda j, i, kk: (i // npm, r0 + i % npm, j))
        out_shape = jax.ShapeDtypeStruct(shape, out_dtype)
        if buf is not None:
            in_specs.append(pl.BlockSpec(memory_space=pl.ANY))
            args.append(buf)
            aliases = {2: 0}

    def body(a_ref, b_ref, *rest):
        o_ref, acc_ref = rest[-2:]
        kk = pl.program_id(2)
        part = lax.dot_general(a_ref[...], b_ref[...], _DIMS[mode], preferred_element_type=F32)
        if nk == 1:
            o_ref[...] = part.astype(out_dtype)
        else:

            @pl.when(kk == 0)
            def _():
                acc_ref[...] = part

            @pl.when(kk > 0)
            def _():
                acc_ref[...] += part

            @pl.when(kk == nk - 1)
            def _():
                o_ref[...] = acc_ref[...].astype(out_dtype)

    res = _pallas(body, name, (n // tn, m // tm, nk), in_specs, [out_spec], [out_shape],
                  [pltpu.VMEM((tm, tn) if nk > 1 else (SUBLANES, LANES), F32)], args, ("parallel", "parallel", "arbitrary"),
                  aliases=aliases, carry=carry)
    return res[0] if carry is None else (res[0], res[1:])


def matmul_gate_res(a, w, res, gate, n_ctx, name):
    w = w if isinstance(w, Mat) else Mat(w)
    (m, k), n = a.shape, w.s
    assert k == w.r
    tm, tn, tk = _fit_tiles(m, n, k, m, w.s_unit(), k, 3 * 4)
    nk = k // tk

    def body(a_ref, b_ref, r_ref, g_ref, x_ref, y_ref, acc_ref):
        kk = pl.program_id(2)
        row0 = pl.program_id(1) * tm
        part = jnp.dot(a_ref[...], b_ref[...], preferred_element_type=F32)

        @pl.when(kk == 0)
        def _():
            acc_ref[...] = part

        @pl.when(kk > 0)
        def _():
            acc_ref[...] += part

        @pl.when(kk == nk - 1)
        def _():
            y = acc_ref[...]
            row = row0 + lax.broadcasted_iota(jnp.int32, (tm, 1), 0)
            g = jnp.where(row < n_ctx, g_ref[0:1, :], g_ref[1:2, :])
            y_ref[...] = y.astype(y_ref.dtype)
            x_ref[...] = r_ref[...] + g * y

    return pl.pallas_call(
        body, name=name, grid=(n // tn, m // tm, nk),
        in_specs=[pl.BlockSpec((tm, tk), lambda j, i, kk: (i, kk)), w.spec(tk, tn, lambda j, i, kk: kk, lambda j, i, kk: j),
                  pl.BlockSpec((tm, tn), lambda j, i, kk: (i, j)), pl.BlockSpec((2, tn), lambda j, i, kk: (0, j))],
        out_specs=[pl.BlockSpec((tm, tn), lambda j, i, kk: (i, j)), pl.BlockSpec((tm, tn), lambda j, i, kk: (i, j))],
        out_shape=[jax.ShapeDtypeStruct((m, n), F32), jax.ShapeDtypeStruct((m, n), F32)],
        scratch_shapes=[pltpu.VMEM((tm, tn), F32)],
        compiler_params=_cp(dimension_semantics=("parallel", "parallel", "arbitrary")),
    )(a, w.arr, res, gate)


ROW_TILE = 256


def _seg_row(ref2, i, n_ctx_tiles):
    return jnp.where(i < n_ctx_tiles, ref2[0:1, :], ref2[1:2, :])


def _acc_rows(ref, step, rows):
    @pl.when(step == 0)
    def _():
        ref[...] = jnp.zeros_like(ref)

    for r, v in enumerate(rows):
        ref[r:r + 1, :] += v


def norm_mod(x, w, shift, scale, n_ctx, name):
    t, d = x.shape
    tm = _pick(n_ctx, (ROW_TILE, 128))
    nct = n_ctx // tm

    def body(x_ref, w_ref, sh_ref, sc_ref, h_ref):
        i = pl.program_id(0)
        xv = x_ref[...]
        r = lax.rsqrt(jnp.mean(xv * xv, axis=-1, keepdims=True) + NORM_EPS)
        h_ref[...] = ((xv * r) * w_ref[...] * (1.0 + _seg_row(sc_ref, i, nct)) + _seg_row(sh_ref, i, nct)).astype(BF16)

    return pl.pallas_call(
        body, name=name, grid=(t // tm,),
        in_specs=[pl.BlockSpec((tm, d), lambda i: (i, 0)), pl.BlockSpec((1, d), lambda i: (0, 0)),
                  pl.BlockSpec((2, d), lambda i: (0, 0)), pl.BlockSpec((2, d), lambda i: (0, 0))],
        out_specs=pl.BlockSpec((tm, d), lambda i: (i, 0)),
        out_shape=jax.ShapeDtypeStruct((t, d), BF16), compiler_params=_cp(),
    )(x, w, shift, scale)


def norm_mod_bwd(dh, dres, x, w, scale, n_ctx, name):
    t, d = x.shape
    tm = _pick(n_ctx, (ROW_TILE, 128))
    nct = n_ctx // tm

    def body(dh_ref, dr_ref, x_ref, w_ref, sc_ref, dx_ref, st_ref):
        i = pl.program_id(0)
        xv, g = x_ref[...], dh_ref[...]
        r = lax.rsqrt(jnp.mean(xv * xv, axis=-1, keepdims=True) + NORM_EPS)
        xn = xv * r
        one_sc = 1.0 + _seg_row(sc_ref, i, nct)
        dxn = g * (w_ref[...] * one_sc)
        dx_ref[...] = dr_ref[...] + r * (dxn - xn * jnp.mean(dxn * xn, axis=-1, keepdims=True))
        gx = g * xn
        s_shift = jnp.sum(g, axis=0, keepdims=True)
        s_scale = jnp.sum(gx * w_ref[...], axis=0, keepdims=True)
        s_w = jnp.sum(gx * one_sc, axis=0, keepdims=True)
        _acc_rows(st_ref, i, [jnp.where(i < nct, s_shift, 0.0), jnp.where(i < nct, 0.0, s_shift),
                              jnp.where(i < nct, s_scale, 0.0), jnp.where(i < nct, 0.0, s_scale), s_w])

    return pl.pallas_call(
        body, name=name, grid=(t // tm,),
        in_specs=[pl.BlockSpec((tm, d), lambda i: (i, 0)), pl.BlockSpec((tm, d), lambda i: (i, 0)),
                  pl.BlockSpec((tm, d), lambda i: (i, 0)), pl.BlockSpec((1, d), lambda i: (0, 0)),
                  pl.BlockSpec((2, d), lambda i: (0, 0))],
        out_specs=[pl.BlockSpec((tm, d), lambda i: (i, 0)), pl.BlockSpec((SUBLANES, d), lambda i: (0, 0))],
        out_shape=[jax.ShapeDtypeStruct((t, d), F32), jax.ShapeDtypeStruct((SUBLANES, d), F32)],
        compiler_params=_cp(dimension_semantics=("arbitrary",)),
    )(dh, dres, x, w, scale)


def gate_bwd(dx, y, gate, n_ctx, name):
    t, d = dx.shape
    tm = _pick(n_ctx, (ROW_TILE, 128))
    nct = n_ctx // tm

    def body(dx_ref, y_ref, g_ref, dy_ref, dg_ref):
        i = pl.program_id(0)
        dxv = dx_ref[...]
        dy_ref[...] = (dxv * _seg_row(g_ref, i, nct)).astype(BF16)
        s = jnp.sum(dxv * y_ref[...], axis=0, keepdims=True)
        _acc_rows(dg_ref, i, [jnp.where(i < nct, s, 0.0), jnp.where(i < nct, 0.0, s)])

    return pl.pallas_call(
        body, name=name, grid=(t // tm,),
        in_specs=[pl.BlockSpec((tm, d), lambda i: (i, 0)), pl.BlockSpec((tm, d), lambda i: (i, 0)),
                  pl.BlockSpec((2, d), lambda i: (0, 0))],
        out_specs=[pl.BlockSpec((tm, d), lambda i: (i, 0)), pl.BlockSpec((SUBLANES, d), lambda i: (0, 0))],
        out_shape=[jax.ShapeDtypeStruct((t, d), BF16), jax.ShapeDtypeStruct((SUBLANES, d), F32)],
        compiler_params=_cp(dimension_semantics=("arbitrary",)),
    )(dx, y, gate)


def _seg_edges(i, tm, n_ctx, t):
    first = (i == 0) | (i == n_ctx // tm)
    last = (i == n_ctx // tm - 1) | (i == t // tm - 1)
    return first, last


def _shift_rows(x, prev_row, next_row, first, last):
    tm = x.shape[0]
    row = lax.broadcasted_iota(jnp.int32, (tm, 1), 0)
    xp = jnp.where(row == 0, jnp.where(first, 0.0, prev_row), pltpu.roll(x, 1, 0))
    xn = jnp.where(row == tm - 1, jnp.where(last, 0.0, next_row), pltpu.roll(x, tm - 1, 0))
    return xp, xn


def _halo_rows(dtype):
    return SUBLANES * 4 // jnp.dtype(dtype).itemsize


def _halo_specs(tm, tc, t, col, hb):
    r, nblk = tm // hb, t // hb
    return [pl.BlockSpec((tm, tc), lambda j, i: (i, col(j))),
            pl.BlockSpec((hb, tc), lambda j, i: (jnp.maximum(i * r - 1, 0), col(j))),
            pl.BlockSpec((hb, tc), lambda j, i: (jnp.minimum((i + 1) * r, nblk - 1), col(j)))]


def _conv_rows(w_ref, xm, prev, nxt, first, last):
    hb, tm = prev.shape[0], xm.shape[0]
    w0, w1, w2, b = w_ref[0:1, :], w_ref[1:2, :], w_ref[2:3, :], w_ref[3:4, :]
    xp, xn = _shift_rows(xm, prev[hb - 1:hb, :], nxt[0:1, :], first, last)
    pre = w0 * xp + w1 * xm + w2 * xn + b
    pre_before = w0 * prev[hb - 2:hb - 1, :] + w1 * prev[hb - 1:hb, :] + w2 * xm[0:1, :] + b
    pre_after = w0 * xm[tm - 1:tm, :] + w1 * nxt[0:1, :] + w2 * nxt[1:2, :] + b
    return pre, pre_before, pre_after


def dwconv_act(x, col0, c, wb, n_ctx, mode, act_dtype, name):
    t = x.shape[0]
    tm = _pick(n_ctx, (ROW_TILE, 128))
    nparts = 2 if mode == "glu" else 1
    cw = c // nparts
    tc = _pick(math.gcd(cw, col0), (512, 384, 256, 128))
    hb = _halo_rows(x.dtype)

    def body(*refs):
        first, last = _seg_edges(pl.program_id(1), tm, n_ctx, t)
        pres = [_conv_rows(refs[4 * p + 3], *[r[...].astype(F32) for r in refs[4 * p:4 * p + 3]], first, last)[0] for p in range(nparts)]
        refs[-1][...] = (_silu(pres[0]) if mode == "silu" else _silu(pres[1]) * pres[0]).astype(act_dtype)

    in_specs, args = [], []
    for p in range(nparts):
        in_specs += _halo_specs(tm, tc, t, lambda j, p=p: (col0 + p * cw) // tc + j, hb)
        in_specs.append(pl.BlockSpec((SUBLANES, tc), lambda j, i, p=p: (0, p * cw // tc + j)))
        args += [x, x, x, wb]
    return pl.pallas_call(
        body, name=name, grid=(cw // tc, t // tm), in_specs=in_specs, out_specs=pl.BlockSpec((tm, tc), lambda j, i: (i, j)),
        out_shape=jax.ShapeDtypeStruct((t, cw), act_dtype), compiler_params=_cp(dimension_semantics=("parallel", "parallel")),
    )(*args)


def dwconv_act_bwd(dact, dcol0, x, xcol0, wb, wcol0, cw, n_ctx, mode, name, into=None, ocol0=0):
    t = x.shape[0]
    tm = _pick(n_ctx, (ROW_TILE, 128))
    nparts = 2 if mode == "glu" else 1
    tc = _pick(math.gcd(cw, dcol0, xcol0, wcol0, ocol0), (512, 384, 256, 128))
    hb_d, hb_x = _halo_rows(dact.dtype), _halo_rows(x.dtype)

    def dpre_of(dact_v, pres_v):
        if mode == "silu":
            return [dact_v * _dsilu(pres_v[0])]
        val, gat = pres_v
        sg = jax.nn.sigmoid(gat)
        return [dact_v * (gat * sg), dact_v * val * (sg * (1.0 + gat * (1.0 - sg)))]

    def body(*refs):
        i = pl.program_id(1)
        first, last = _seg_edges(i, tm, n_ctx, t)
        da = [r[...].astype(F32) for r in refs[0:3]]
        xs = [[r[...].astype(F32) for r in refs[3 + 3 * p:6 + 3 * p]] for p in range(nparts)]
        ws = refs[3 + 3 * nparts:3 + 4 * nparts]
        dx_ref, dw_ref = refs[-2:]
        pres = [_conv_rows(ws[p], *xs[p], first, last) for p in range(nparts)]
        dm = dpre_of(da[0], [pr[0] for pr in pres])
        d_before = dpre_of(da[1][hb_d - 1:hb_d, :], [pr[1] for pr in pres])
        d_after = dpre_of(da[2][0:1, :], [pr[2] for pr in pres])
        for p in range(nparts):
            d_prev, d_next = _shift_rows(dm[p], d_before[p], d_after[p], first, last)
            w_ref, xv = ws[p], xs[p][0]
            dxv = (w_ref[0:1, :] * d_next + w_ref[1:2, :] * dm[p] + w_ref[2:3, :] * d_prev).astype(BF16)
            if mode == "glu":
                dx_ref[p] = dxv
            else:
                dx_ref[...] = dxv
            _acc_rows(dw_ref.at[p] if mode == "glu" else dw_ref, i,
                      [jnp.sum(d_next * xv, axis=0, keepdims=True), jnp.sum(dm[p] * xv, axis=0, keepdims=True),
                       jnp.sum(d_prev * xv, axis=0, keepdims=True), jnp.sum(dm[p], axis=0, keepdims=True)])

    in_specs = _halo_specs(tm, tc, t, lambda j: dcol0 // tc + j, hb_d)
    args = [dact] * 3
    for p in range(nparts):
        in_specs += _halo_specs(tm, tc, t, lambda j, p=p: (xcol0 + p * cw) // tc + j, hb_x)
        args += [x] * 3
    for p in range(nparts):
        in_specs.append(pl.BlockSpec((SUBLANES, tc), lambda j, i, p=p: (0, (wcol0 + p * cw) // tc + j)))
        args.append(wb)
    aliases = {}
    if mode == "glu":
        out_specs = [pl.BlockSpec((2, tm, tc), lambda j, i: (0, i, j)), pl.BlockSpec((2, SUBLANES, tc), lambda j, i: (0, 0, j))]
        out_shape = [jax.ShapeDtypeStruct((2, t, cw), BF16), jax.ShapeDtypeStruct((2, SUBLANES, cw), F32)]
    else:
        out_specs = [pl.BlockSpec((tm, tc), lambda j, i: (i, ocol0 // tc + j)), pl.BlockSpec((SUBLANES, tc), lambda j, i: (0, j))]
        out_shape = [jax.ShapeDtypeStruct((t, cw) if into is None else into.shape, BF16), jax.ShapeDtypeStruct((SUBLANES, cw), F32)]
        if into is not None:
            aliases = {len(args): 0}
            in_specs.append(pl.BlockSpec(memory_space=pl.ANY))
            args.append(into)
    return pl.pallas_call(
        body, name=name, grid=(cw // tc, t // tm), in_specs=in_specs, out_specs=out_specs, out_shape=out_shape,
        input_output_aliases=aliases, compiler_params=_cp(dimension_semantics=("parallel", "arbitrary")),
    )(*args)


HI = lax.Precision.HIGHEST
_NT = (((1,), (1,)), ((), ()))
_TN = (((0,), (0,)), ((), ()))


def _dot(a, b, dims=None, precision=None):
    if dims is None:
        return jnp.dot(a, b, preferred_element_type=F32, precision=precision)
    return lax.dot_general(a, b, dims, preferred_element_type=F32, precision=precision)


def _softplus(x):
    y = jnp.exp(-jnp.abs(x))
    u = 1.0 + y
    log1p = jnp.where(u == 1.0, y, y * jnp.log(u) / jnp.where(u == 1.0, 1.0, u - 1.0))
    return jnp.maximum(x, 0.0) + log1p


def _tri(n, upper):
    r = lax.broadcasted_iota(jnp.int32, (n, n), 0)
    c = lax.broadcasted_iota(jnp.int32, (n, n), 1)
    return ((r <= c) if upper else (r >= c)).astype(F32)


def _group_select(heads, e):
    g = jnp.arange(SSD_GROUPS)[:, None, None]
    src = jnp.arange(LANES)[None, :, None]
    dst = jnp.arange(LANES)[None, None, :]
    d, k = dst // e, dst % e
    return ((dst < 2 * e) & (src == d * heads + g * e + k)).astype(F32)


def ssd_prep(zx, col0, bias, alog, sel, heads, name):
    t = zx.shape[0]
    assert 2 * heads == LANES and col0 % LANES == 0
    nc = t // CHUNK

    def body(zx_ref, b_ref, al_ref, sel_ref, dt_ref, da_ref, dtg_ref, ag_ref, agt_ref):
        dt = _softplus(zx_ref[...] + b_ref[...])
        da = -jnp.exp(al_ref[...]) * dt
        lane = lax.broadcasted_iota(jnp.int32, (CHUNK, LANES), 1)
        a = jnp.where(lane < heads, _dot(_tri(CHUNK, False), da, precision=HI), _dot(_tri(CHUNK, True), da, precision=HI))
        dt_ref[...] = dt
        da_ref[...] = da
        for g in range(SSD_GROUPS):
            s = sel_ref[g]
            dtg_ref[g] = _dot(dt, s, precision=HI)
            a_g = _dot(a, s, precision=HI)
            ag_ref[g] = a_g
            agt_ref[g, 0] = a_g.T

    return pl.pallas_call(
        body, name=name, grid=(nc,),
        in_specs=[pl.BlockSpec((CHUNK, LANES), lambda c: (c, col0 // LANES)), pl.BlockSpec((1, LANES), lambda c: (0, 0)),
                  pl.BlockSpec((1, LANES), lambda c: (0, 0)), pl.BlockSpec((SSD_GROUPS, LANES, LANES), lambda c: (0, 0, 0))],
        out_specs=[pl.BlockSpec((CHUNK, LANES), lambda c: (c, 0)), pl.BlockSpec((CHUNK, LANES), lambda c: (c, 0)),
                   pl.BlockSpec((SSD_GROUPS, CHUNK, LANES), lambda c: (0, c, 0)), pl.BlockSpec((SSD_GROUPS, CHUNK, LANES), lambda c: (0, c, 0)),
                   pl.BlockSpec((SSD_GROUPS, 1, LANES, CHUNK), lambda c: (0, c, 0, 0))],
        out_shape=[jax.ShapeDtypeStruct((t, LANES), F32), jax.ShapeDtypeStruct((t, LANES), F32),
                   jax.ShapeDtypeStruct((SSD_GROUPS, t, LANES), F32), jax.ShapeDtypeStruct((SSD_GROUPS, t, LANES), F32),
                   jax.ShapeDtypeStruct((SSD_GROUPS, nc, LANES, CHUNK), F32)],
        compiler_params=_cp(),
    )(zx, bias, alog, sel)


def ssd_prep_bwd(zx, col0, bias, alog, sel, grads_f, grads_b, da_comp, heads, into, name):
    t = zx.shape[0]
    nc = t // CHUNK

    def body(zx_ref, b_ref, al_ref, sel_ref, ddtg_ref, dag_ref, dagt_ref, ddtg2_ref, dag2_ref, dagt2_ref, da_ref, _, draw_ref, st_ref):
        c = pl.program_id(0)
        ddt = jnp.zeros((CHUNK, LANES), F32)
        dacc = jnp.zeros((CHUNK, LANES), F32)
        for g in range(SSD_GROUPS):
            s = sel_ref[g]
            ddt += _dot(ddtg_ref[g] + ddtg2_ref[g], s, _NT, precision=HI)
            dacc += _dot(dag_ref[g] + dag2_ref[g] + (dagt_ref[g, 0] + dagt2_ref[g, 0]).T, s, _NT, precision=HI)
        lane = lax.broadcasted_iota(jnp.int32, (CHUNK, LANES), 1)
        dda = jnp.where(lane < heads, _dot(_tri(CHUNK, True), dacc, precision=HI), _dot(_tri(CHUNK, False), dacc, precision=HI))
        xin = zx_ref[...] + b_ref[...]
        ddt_tot = ddt - dda * jnp.exp(al_ref[...])
        draw = ddt_tot * jax.nn.sigmoid(xin)
        draw_ref[...] = draw.astype(BF16)
        _acc_rows(st_ref, c, [jnp.sum(draw, axis=0, keepdims=True), jnp.sum(dda * da_ref[...], axis=0, keepdims=True)])

    g3 = pl.BlockSpec((SSD_GROUPS, CHUNK, LANES), lambda c: (0, c, 0))
    g4 = pl.BlockSpec((SSD_GROUPS, 1, LANES, CHUNK), lambda c: (0, c, 0, 0))
    return pl.pallas_call(
        body, name=name, grid=(nc,),
        in_specs=[pl.BlockSpec((CHUNK, LANES), lambda c: (c, col0 // LANES)), pl.BlockSpec((1, LANES), lambda c: (0, 0)),
                  pl.BlockSpec((1, LANES), lambda c: (0, 0)), pl.BlockSpec((SSD_GROUPS, LANES, LANES), lambda c: (0, 0, 0)),
                  g3, g3, g4, g3, g3, g4, pl.BlockSpec((CHUNK, LANES), lambda c: (c, 0)), pl.BlockSpec(memory_space=pl.ANY)],
        out_specs=[pl.BlockSpec((CHUNK, LANES), lambda c: (c, col0 // LANES)), pl.BlockSpec((SUBLANES, LANES), lambda c: (0, 0))],
        out_shape=[jax.ShapeDtypeStruct(into.shape, BF16), jax.ShapeDtypeStruct((SUBLANES, LANES), F32)],
        input_output_aliases={11: 0}, compiler_params=_cp(dimension_semantics=("arbitrary",)),
    )(zx, bias, alog, sel, *grads_f, *grads_b, da_comp, into)


def _chunk_row(k, nctx_c, nc, rev):
    if not rev:
        return k
    return jnp.where(k < nctx_c, nctx_c - 1 - k, nc + nctx_c - 1 - k)


def _pair_consts(dtg, ag, agt, l0, end):
    lane = lax.broadcasted_iota(jnp.int32, (CHUNK, LANES), 1)
    lo = lane < SSD_HEAD_DIM
    a0, a1 = ag[:, l0:l0 + 1], ag[:, l0 + 1:l0 + 2]
    dtp = jnp.where(lo, dtg[:, l0:l0 + 1], dtg[:, l0 + 1:l0 + 2])
    acol = jnp.where(lo, a0, a1)
    aend = acol[end:end + 1, :]
    return lo, a0, a1, agt[l0:l0 + 1, :], agt[l0 + 1:l0 + 2, :], dtp, acol, aend


def ssd_scan(xbc, dtg, ag, agt, d_inner, n_ctx, rev, name, carry=None):
    t = xbc.shape[0]
    e = d_inner // SSD_HEAD_DIM // SSD_GROUPS
    npair, gw = e // 2, e * SSD_HEAD_DIM
    assert e % 2 == 0 and gw % LANES == 0
    nc, nctx_c = t // CHUNK, n_ctx // CHUNK
    dirn = 1 if rev else 0
    end = 0 if rev else CHUNK - 1
    ridx = lambda k: _chunk_row(k, nctx_c, nc, rev)

    gps = 2
    xcol = d_inner // LANES
    assert SSD_GROUPS % gps == 0 and xcol % gps == 0
    srows = npair * LANES

    def body(x_ref, b_ref, c_ref, dtg_ref, ag_ref, agt_ref, y_ref, h_ref, state):
        k = pl.program_id(1)

        @pl.when(k == 0)
        def _():
            state[...] = jnp.zeros_like(state)

        row = lax.broadcasted_iota(jnp.int32, (CHUNK, CHUNK), 0)
        col = lax.broadcasted_iota(jnp.int32, (CHUNK, CHUNK), 1)
        mask = (row <= col) if rev else (row >= col)
        rlo = lax.broadcasted_iota(jnp.int32, (LANES, 1), 0) < SSD_HEAD_DIM
        for gi in range(gps):
            gs, ss = slice(gi * gw, (gi + 1) * gw), slice(gi * srows, (gi + 1) * srows)
            ns = slice(gi * SSD_STATE, (gi + 1) * SSD_STATE)
            bb, cbf = b_ref[:, ns].astype(BF16), c_ref[:, ns].astype(BF16)
            cb = _dot(cbf, bb, _NT)
            dtg_v, ag_v, agt_v = dtg_ref[gi], ag_ref[gi], agt_ref[gi, 0]
            consts = [_pair_consts(dtg_v, ag_v, agt_v, dirn * e + 2 * p, end) for p in range(npair)]
            dtp = jnp.concatenate([c[5] for c in consts], axis=1)
            acol = jnp.concatenate([c[6] for c in consts], axis=1)
            aend = jnp.concatenate([c[7] for c in consts], axis=1)
            xdt = x_ref[:, gs] * dtp
            xdtb = xdt.astype(BF16)
            s_in = state[ss, :]
            h_ref[0, gi] = s_in
            yo = _dot(cbf, s_in.astype(BF16), _NT)
            st = _dot((xdt * jnp.exp(aend - acol)).astype(BF16), bb, _TN)
            yd = []
            for p, (lo, a0, a1, a0r, a1r, _, _, _) in enumerate(consts):
                xp = xdtb[:, p * LANES:(p + 1) * LANES]
                w01 = jnp.concatenate([(cb * jnp.exp(jnp.where(mask, a0 - a0r, -jnp.inf))).astype(BF16),
                                       (cb * jnp.exp(jnp.where(mask, a1 - a1r, -jnp.inf))).astype(BF16)], axis=1)
                yd.append(_dot(w01, jnp.concatenate([jnp.where(lo, xp, 0), jnp.where(lo, 0, xp)], axis=0)))
            y_ref[:, gs] = jnp.concatenate(yd, axis=1) + yo * jnp.exp(acol)
            cd = jnp.concatenate([jnp.where(rlo, jnp.exp(c[7][:, 0:1]), jnp.exp(c[7][:, LANES - 1:LANES])) for c in consts], axis=0)
            state[ss, :] = cd * s_in + st

    g3 = pl.BlockSpec((gps, CHUNK, LANES), lambda g, k: (g, ridx(k), 0))
    return _pallas(
        body, name, (SSD_GROUPS // gps, nc),
        [pl.BlockSpec((CHUNK, gps * gw), lambda g, k: (ridx(k), g)),
         pl.BlockSpec((CHUNK, gps * SSD_STATE), lambda g, k: (ridx(k), xcol // gps + g)),
         pl.BlockSpec((CHUNK, gps * SSD_STATE), lambda g, k: (ridx(k), (xcol + SSD_GROUPS) // gps + g)),
         g3, g3, pl.BlockSpec((gps, 1, LANES, CHUNK), lambda g, k: (g, ridx(k), 0, 0))],
        [pl.BlockSpec((CHUNK, gps * gw), lambda g, k: (ridx(k), g)),
         pl.BlockSpec((1, gps, srows, SSD_STATE), lambda g, k: (ridx(k), g, 0, 0))],
        [jax.ShapeDtypeStruct((t, d_inner), F32), jax.ShapeDtypeStruct((nc, SSD_GROUPS, srows, SSD_STATE), F32)],
        [pltpu.VMEM((gps * srows, SSD_STATE), F32)], (xbc, xbc, xbc, dtg, ag, agt), ("parallel", "arbitrary"), carry=carry)


def ssd_scan_bwd(dy, xbc, dtg, ag, agt, hst, dskip, prev, d_inner, n_ctx, rev, name, carry=None):
    t = xbc.shape[0]
    e = d_inner // SSD_HEAD_DIM // SSD_GROUPS
    npair, gw = e // 2, e * SSD_HEAD_DIM
    nc, nctx_c = t // CHUNK, n_ctx // CHUNK
    dirn = 1 if rev else 0
    end = 0 if rev else CHUNK - 1
    ridx = lambda kk: _chunk_row(nc - 1 - kk, nctx_c, nc, rev)
    has_skip, has_prev = dskip is not None, prev is not None
    gps = 2
    xcol = d_inner // LANES
    assert SSD_GROUPS % gps == 0 and xcol % gps == 0
    srows = npair * LANES
    kinds = (["xcols", "xcols", "ncols", "ncols", "lead", "lead", "lead", "second"] + (["xcols"] if has_skip else [])
             + (["xcols", "ncols", "ncols"] if has_prev else []) + ["xcols", "ncols", "ncols", "lead", "lead", "lead", "xcols", "srows"])

    def body(*refs):
        kk = pl.program_id(1)

        @pl.when(kk == 0)
        def _():
            refs[-1][...] = jnp.zeros_like(refs[-1])

        for gi in range(gps):
            view = {"xcols": lambda r: r.at[:, gi * gw:(gi + 1) * gw], "ncols": lambda r: r.at[:, gi * SSD_STATE:(gi + 1) * SSD_STATE],
                    "lead": lambda r: r.at[gi:gi + 1], "second": lambda r: r.at[:, gi:gi + 1],
                    "srows": lambda r: r.at[gi * srows:(gi + 1) * srows]}
            one_group(kk, *[view[kind](r) for kind, r in zip(kinds, refs)])

    def one_group(kk, *refs):
        dy_ref, x_ref, b_ref, c_ref, dtg_ref, ag_ref, agt_ref, h_ref = refs[:8]
        pos = 8
        if has_skip:
            ds_ref = refs[pos]
            pos += 1
        if has_prev:
            pdx_ref, pdb_ref, pdc_ref = refs[pos:pos + 3]
            pos += 3
        dx_ref, db_ref, dc_ref, ddtg_ref, dag_ref, dagt_ref, dd_ref, dstate = refs[pos:]
        bb, cbf = b_ref[...].astype(BF16), c_ref[...].astype(BF16)
        cb = _dot(cbf, bb, _NT)
        row = lax.broadcasted_iota(jnp.int32, (CHUNK, CHUNK), 0)
        col = lax.broadcasted_iota(jnp.int32, (CHUNK, CHUNK), 1)
        mask = (row <= col) if rev else (row >= col)
        rlo = lax.broadcasted_iota(jnp.int32, (LANES, 1), 0) < SSD_HEAD_DIM
        is_end = lax.broadcasted_iota(jnp.int32, (CHUNK, 1), 0) == end
        dtg_v, ag_v, agt_v = dtg_ref[0], ag_ref[0], agt_ref[0, 0]
        consts = [_pair_consts(dtg_v, ag_v, agt_v, dirn * e + 2 * p, end) for p in range(npair)]
        dtp = jnp.concatenate([c[5] for c in consts], axis=1)
        acol = jnp.concatenate([c[6] for c in consts], axis=1)
        aend = jnp.concatenate([c[7] for c in consts], axis=1)
        x, dyv = x_ref[...], dy_ref[...]
        xdt = x * dtp
        xdtb = xdt.astype(BF16)
        efs, dte = jnp.exp(acol), jnp.exp(aend - acol)
        s_in, d_s = h_ref[0, 0], dstate[...]
        sb, dsb = s_in.astype(BF16), d_s.astype(BF16)
        dyo = (dyv * efs).astype(BF16)
        da_exp = dyv * _dot(cbf, sb, _NT) * efs
        d_c = _dot(dyo, sb)
        ds_y = _dot(dyo, cbf, _TN)
        dxw = _dot(bb, dsb, _NT)
        d_b = _dot((xdt * dte).astype(BF16), dsb)
        tmp = dxw * xdt * dte
        da_exp = da_exp - tmp
        end_row = jnp.sum(tmp, axis=0, keepdims=True)
        prod = d_s * s_in
        cd = jnp.concatenate([jnp.where(rlo, jnp.exp(c[7][:, 0:1]), jnp.exp(c[7][:, LANES - 1:LANES])) for c in consts], axis=0)
        dstate[...] = cd * d_s + ds_y
        dcb = jnp.zeros((CHUNK, CHUNK), F32)
        ddt_out = jnp.zeros((CHUNK, LANES), F32)
        da_out = jnp.zeros((CHUNK, LANES), F32)
        dat_out = jnp.zeros((LANES, CHUNK), F32)
        lane = lax.broadcasted_iota(jnp.int32, (CHUNK, LANES), 1)
        sub = lax.broadcasted_iota(jnp.int32, (LANES, CHUNK), 0)
        dxdt_parts = []
        for p, (lo, a0, a1, a0r, a1r, _, _, aend_p) in enumerate(consts):
            sl = slice(p * LANES, (p + 1) * LANES)
            l0 = dirn * e + 2 * p
            seg0 = jnp.exp(jnp.where(mask, a0 - a0r, -jnp.inf))
            seg1 = jnp.exp(jnp.where(mask, a1 - a1r, -jnp.inf))
            w0, w1 = cb * seg0, cb * seg1
            dyp = dyv[:, sl]
            dy01 = jnp.concatenate([jnp.where(lo, dyp, 0.0), jnp.where(lo, 0.0, dyp)], axis=0).astype(BF16)
            dw01 = _dot(dy01, xdtb[:, sl], _NT)
            dw0, dw1 = dw01[:CHUNK], dw01[CHUNK:]
            dxdt_p = _dot(jnp.concatenate([w0.astype(BF16), w1.astype(BF16)], axis=0), dy01, _TN) + dxw[:, sl] * dte[:, sl]
            dxdt_parts.append(dxdt_p)
            dcb += dw0 * seg0 + dw1 * seg1
            t0, t1 = dw0 * w0, dw1 * w1
            prod_p = prod[sl, :]
            sc0 = jnp.sum(jnp.where(rlo, prod_p, 0.0), keepdims=True) * jnp.exp(aend_p[:, 0:1])
            sc1 = jnp.sum(jnp.where(rlo, 0.0, prod_p), keepdims=True) * jnp.exp(aend_p[:, LANES - 1:LANES])
            ddt_exp, da_exp_p, end_row_p = dxdt_p * x[:, sl], da_exp[:, sl], end_row[:, sl]
            for j, (sel, tj, scj) in enumerate(((lo, t0, sc0), (~lo, t1, sc1))):
                ddt_col = jnp.sum(jnp.where(sel, ddt_exp, 0.0), axis=1, keepdims=True)
                da_col = jnp.sum(jnp.where(sel, da_exp_p, 0.0), axis=1, keepdims=True) + jnp.sum(tj, axis=1, keepdims=True)
                da_end = jnp.sum(jnp.where(sel[0:1, :], end_row_p, 0.0), axis=1, keepdims=True) + scj
                da_col = da_col + jnp.where(is_end, da_end, 0.0)
                ddt_out += jnp.where(lane == l0 + j, ddt_col, 0.0)
                da_out += jnp.where(lane == l0 + j, da_col, 0.0)
                dat_out -= jnp.where(sub == l0 + j, jnp.sum(tj, axis=0, keepdims=True), 0.0)
        dxv = jnp.concatenate(dxdt_parts, axis=1) * dtp
        if has_skip:
            dxv += dyv * ds_ref[...]
            _acc_rows(dd_ref, kk, [jnp.sum(dyv * x, axis=0, keepdims=True)])
        if has_prev:
            dxv += pdx_ref[...]
        dx_ref[...] = dxv
        dcbb = dcb.astype(BF16)
        d_c += _dot(dcbb, bb)
        d_b += _dot(dcbb, cbf, _TN)
        if has_prev:
            d_b += pdb_ref[...]
            d_c += pdc_ref[...]
        db_ref[...] = d_b
        dc_ref[...] = d_c
        ddtg_ref[0] = ddt_out
        dag_ref[0] = da_out
        dagt_ref[0, 0] = dat_out
        if not has_skip:
            dd_ref[...] = jnp.zeros_like(dd_ref)

    xs_spec = pl.BlockSpec((CHUNK, gps * gw), lambda g, kk: (ridx(kk), g))
    bc_spec = pl.BlockSpec((CHUNK, gps * SSD_STATE), lambda g, kk: (ridx(kk), g))
    g3 = pl.BlockSpec((gps, CHUNK, LANES), lambda g, kk: (g, ridx(kk), 0))
    g4 = pl.BlockSpec((gps, 1, LANES, CHUNK), lambda g, kk: (g, ridx(kk), 0, 0))
    in_specs = [xs_spec, xs_spec,
                pl.BlockSpec((CHUNK, gps * SSD_STATE), lambda g, kk: (ridx(kk), xcol // gps + g)),
                pl.BlockSpec((CHUNK, gps * SSD_STATE), lambda g, kk: (ridx(kk), (xcol + SSD_GROUPS) // gps + g)),
                g3, g3, g4, pl.BlockSpec((1, gps, srows, SSD_STATE), lambda g, kk: (ridx(kk), g, 0, 0))]
    args = [dy, xbc, xbc, xbc, dtg, ag, agt, hst]
    if has_skip:
        in_specs.append(pl.BlockSpec((1, gps * gw), lambda g, kk: (0, g)))
        args.append(dskip)
    if has_prev:
        in_specs += [xs_spec, bc_spec, bc_spec]
        args += list(prev)
    gn = SSD_GROUPS * SSD_STATE
    return _pallas(
        body, name, (SSD_GROUPS // gps, nc), in_specs,
        [xs_spec, bc_spec, bc_spec, g3, g3, g4, pl.BlockSpec((SUBLANES, gps * gw), lambda g, kk: (0, g))],
        [jax.ShapeDtypeStruct((t, d_inner), F32), jax.ShapeDtypeStruct((t, gn), F32), jax.ShapeDtypeStruct((t, gn), F32),
         jax.ShapeDtypeStruct((SSD_GROUPS, t, LANES), F32), jax.ShapeDtypeStruct((SSD_GROUPS, t, LANES), F32),
         jax.ShapeDtypeStruct((SSD_GROUPS, nc, LANES, CHUNK), F32), jax.ShapeDtypeStruct((SUBLANES, d_inner), F32)],
        [pltpu.VMEM((gps * srows, SSD_STATE), F32)], args, ("parallel", "arbitrary"), carry=carry)


def ssd_gate_norm(yf, yb, xbc, zx, dskip, w, name):
    t, di = yf.shape
    tm = CHUNK

    def body(yf_ref, yb_ref, x_ref, z_ref, d_ref, w_ref, y_ref, o_ref):
        y = yf_ref[...] + yb_ref[...] + x_ref[...] * d_ref[...]
        y_ref[...] = y
        gz = y * _silu(z_ref[...])
        o_ref[...] = (gz * lax.rsqrt(jnp.mean(gz * gz, axis=-1, keepdims=True) + NORM_EPS) * w_ref[...]).astype(BF16)

    blk = pl.BlockSpec((tm, di), lambda i: (i, 0))
    vec = pl.BlockSpec((1, di), lambda i: (0, 0))
    return pl.pallas_call(
        body, name=name, grid=(t // tm,), in_specs=[blk, blk, blk, blk, vec, vec], out_specs=[blk, blk],
        out_shape=[jax.ShapeDtypeStruct((t, di), F32), jax.ShapeDtypeStruct((t, di), BF16)], compiler_params=_cp(),
    )(yf, yb, xbc, zx, dskip, w)


def ssd_gate_norm_bwd(dout, y, zx, w, name):
    t, di = y.shape
    tm = CHUNK

    def body(do_ref, y_ref, z_ref, w_ref, dy_ref, dz_ref, st_ref):
        i = pl.program_id(0)
        z, yv, g = z_ref[...], y_ref[...], do_ref[...]
        sz = _silu(z)
        gz = yv * sz
        r = lax.rsqrt(jnp.mean(gz * gz, axis=-1, keepdims=True) + NORM_EPS)
        n = gz * r
        dn = g * w_ref[...]
        dgz = r * (dn - n * jnp.mean(dn * n, axis=-1, keepdims=True))
        dy_ref[...] = dgz * sz
        dz_ref[...] = (dgz * yv * _dsilu(z)).astype(BF16)
        _acc_rows(st_ref, i, [jnp.sum(g * n, axis=0, keepdims=True)])

    blk = pl.BlockSpec((tm, di), lambda i: (i, 0))
    return pl.pallas_call(
        body, name=name, grid=(t // tm,), in_specs=[blk, blk, blk, pl.BlockSpec((1, di), lambda i: (0, 0))],
        out_specs=[blk, blk, pl.BlockSpec((SUBLANES, di), lambda i: (0, 0))],
        out_shape=[jax.ShapeDtypeStruct((t, di), F32), jax.ShapeDtypeStruct(zx.shape, BF16), jax.ShapeDtypeStruct((SUBLANES, di), F32)],
        compiler_params=_cp(dimension_semantics=("arbitrary",)),
    )(dout, y, zx, w)


def rope_tables(n_ctx, seq):
    pos = jnp.arange(seq)
    half = ATTN_HEAD_DIM // 4
    inv = ROPE_THETA ** (-jnp.arange(0, 2 * half, 2, dtype=F32) / (2 * half))
    ar = (pos // GRID_W).astype(F32)[:, None] * inv[None, :]
    ac = (pos % GRID_W).astype(F32)[:, None] * inv[None, :]
    cos = jnp.concatenate([jnp.cos(ar), jnp.cos(ar), jnp.cos(ac), jnp.cos(ac)], axis=-1)
    sin = jnp.concatenate([-jnp.sin(ar), jnp.sin(ar), -jnp.sin(ac), jnp.sin(ac)], axis=-1)
    cos = jnp.concatenate([jnp.ones((n_ctx, ATTN_HEAD_DIM), F32), cos], axis=0)
    sin = jnp.concatenate([jnp.zeros((n_ctx, ATTN_HEAD_DIM), F32), sin], axis=0)
    return cos, sin


def _rot(x):
    lane = lax.broadcasted_iota(jnp.int32, x.shape, 1)
    q = ATTN_HEAD_DIM // 4
    return jnp.where((lane % (2 * q)) < q, pltpu.roll(x, ATTN_HEAD_DIM - q, 1), pltpu.roll(x, q, 1))


def qk_prep(qkv, gains, cos, sin, n_q, n_k, name):
    t, c = qkv.shape
    tm = ROW_TILE
    hd = ATTN_HEAD_DIM

    def body(x_ref, g_ref, cos_ref, sin_ref, o_ref):
        cs, sn = cos_ref[...], sin_ref[...]
        for h in range(c // hd):
            sl = slice(h * hd, (h + 1) * hd)
            x = x_ref[:, sl]
            if h < n_q + n_k:
                gain = g_ref[0:1, :] if h < n_q else g_ref[1:2, :]
                xn = x * lax.rsqrt(jnp.mean(x * x, axis=-1, keepdims=True) + NORM_EPS) * gain
                x = xn * cs + _rot(xn) * sn
            o_ref[:, sl] = x.astype(BF16)

    return pl.pallas_call(
        body, name=name, grid=(t // tm,),
        in_specs=[pl.BlockSpec((tm, c), lambda i: (i, 0)), pl.BlockSpec((SUBLANES, hd), lambda i: (0, 0)),
                  pl.BlockSpec((tm, hd), lambda i: (i, 0)), pl.BlockSpec((tm, hd), lambda i: (i, 0))],
        out_specs=pl.BlockSpec((tm, c), lambda i: (i, 0)), out_shape=jax.ShapeDtypeStruct((t, c), BF16), compiler_params=_cp(),
    )(qkv, gains, cos, sin)


def qk_prep_bwd(dq, dk, dv, qkv, gains, cos, sin, name):
    t, c = qkv.shape
    tm = ROW_TILE
    hd = ATTN_HEAD_DIM
    n_q, n_k = dq.shape[1] // hd, dk.shape[1] // hd

    def body(dq_ref, dk_ref, dv_ref, x_ref, g_ref, cos_ref, sin_ref, o_ref, st_ref):
        i = pl.program_id(0)
        cs, sn = cos_ref[...], sin_ref[...]
        dgq = jnp.zeros((1, hd), F32)
        dgk = jnp.zeros((1, hd), F32)
        for h in range(c // hd):
            sl = slice(h * hd, (h + 1) * hd)
            if h >= n_q + n_k:
                hv = h - n_q - n_k
                o_ref[:, sl] = dv_ref[:, hv * hd:(hv + 1) * hd].astype(BF16)
                continue
            is_q = h < n_q
            dy = dq_ref[:, sl] if is_q else dk_ref[:, (h - n_q) * hd:(h - n_q + 1) * hd]
            gain = g_ref[0:1, :] if is_q else g_ref[1:2, :]
            x = x_ref[:, sl]
            r = lax.rsqrt(jnp.mean(x * x, axis=-1, keepdims=True) + NORM_EPS)
            xh = x * r
            dxn = dy * cs + _rot(dy * sn)
            dg = jnp.sum(dxn * xh, axis=0, keepdims=True)
            if is_q:
                dgq += dg
            else:
                dgk += dg
            dxh = dxn * gain
            o_ref[:, sl] = (r * (dxh - xh * jnp.mean(dxh * xh, axis=-1, keepdims=True))).astype(BF16)
        _acc_rows(st_ref, i, [dgq, dgk])

    return pl.pallas_call(
        body, name=name, grid=(t // tm,),
        in_specs=[pl.BlockSpec((tm, n_q * hd), lambda i: (i, 0)), pl.BlockSpec((tm, n_k * hd), lambda i: (i, 0)),
                  pl.BlockSpec((tm, n_k * hd), lambda i: (i, 0)), pl.BlockSpec((tm, c), lambda i: (i, 0)),
                  pl.BlockSpec((SUBLANES, hd), lambda i: (0, 0)), pl.BlockSpec((tm, hd), lambda i: (i, 0)), pl.BlockSpec((tm, hd), lambda i: (i, 0))],
        out_specs=[pl.BlockSpec((tm, c), lambda i: (i, 0)), pl.BlockSpec((SUBLANES, hd), lambda i: (0, 0))],
        out_shape=[jax.ShapeDtypeStruct((t, c), BF16), jax.ShapeDtypeStruct((SUBLANES, hd), F32)],
        compiler_params=_cp(dimension_semantics=("arbitrary",)),
    )(dq, dk, dv, qkv, gains, cos, sin)


def _attn_specs(n_ctx, nb, grp, n_qh):
    hd, blk = ATTN_HEAD_DIM, CHUNK
    kc, vc = n_qh, n_qh + ATTN_KV_HEADS
    specs = [pl.BlockSpec((blk, grp * hd), lambda h, b: (b, h))]
    for c0 in (kc, vc):
        specs += [pl.BlockSpec((n_ctx, hd), lambda h, b, c0=c0: (0, c0 + h)),
                  pl.BlockSpec((blk, hd), lambda h, b, c0=c0: (jnp.maximum(b - 1, 0), c0 + h)),
                  pl.BlockSpec((blk, hd), lambda h, b, c0=c0: (b, c0 + h)),
                  pl.BlockSpec((blk, hd), lambda h, b, c0=c0: (jnp.minimum(b + 1, nb - 1), c0 + h))]
    return specs


def _attn_masks(b, nctx_b, nb):
    row = lax.broadcasted_iota(jnp.int32, (CHUNK, CHUNK), 0)
    col = lax.broadcasted_iota(jnp.int32, (CHUNK, CHUNK), 1)
    lat = b >= nctx_b
    return [(col >= row) & lat & (b - 1 >= nctx_b), jnp.broadcast_to(lat, (CHUNK, CHUNK)), (col <= row) & lat & (b + 1 <= nb - 1)]


def _attn_bias(b, n_ctx, nctx_b, nb):
    band = [jnp.where(m, 0.0, -jnp.inf) for m in _attn_masks(b, nctx_b, nb)]
    return jnp.concatenate([jnp.zeros((CHUNK, n_ctx), F32)] + band, axis=1)


def attention(qkvr, sinks, n_ctx, n_qh, name, carry=None):
    t = qkvr.shape[0]
    hd, blk = ATTN_HEAD_DIM, CHUNK
    grp = n_qh // ATTN_KV_HEADS
    nb, nctx_b = t // blk, n_ctx // blk
    scale = hd ** -0.5

    def body(q_ref, kc_ref, kp_ref, ko_ref, kn_ref, vc_ref, vp_ref, vo_ref, vn_ref, s_ref, o_ref, lse_ref):
        b = pl.program_id(1)
        k_all = jnp.concatenate([kc_ref[...], kp_ref[...], ko_ref[...], kn_ref[...]], axis=0)
        v_all = jnp.concatenate([vc_ref[...], vp_ref[...], vo_ref[...], vn_ref[...]], axis=0)
        bias = jnp.concatenate([_attn_bias(b, n_ctx, nctx_b, nb)] * grp, axis=0)
        q = jnp.concatenate([q_ref[:, g * hd:(g + 1) * hd] for g in range(grp)], axis=0)
        sink = jnp.concatenate([jnp.broadcast_to(s_ref[0, 0:1, g:g + 1], (blk, 1)) for g in range(grp)], axis=0)
        s = _dot(q, k_all, _NT) * scale + bias
        m = jnp.maximum(sink, jnp.max(s, axis=-1, keepdims=True))
        p = jnp.exp(s - m)
        l = jnp.exp(sink - m) + jnp.sum(p, axis=-1, keepdims=True)
        o = _dot((p * (1.0 / l)).astype(BF16), v_all).astype(BF16)
        lse = m + jnp.log(l)
        lane = lax.broadcasted_iota(jnp.int32, (blk, LANES), 1)
        lse_out = jnp.zeros((blk, LANES), F32)
        for g in range(grp):
            o_ref[:, g * hd:(g + 1) * hd] = o[g * blk:(g + 1) * blk]
            lse_out = jnp.where(lane == g, lse[g * blk:(g + 1) * blk], lse_out)
        lse_ref[...] = lse_out

    return _pallas(
        body, name, (ATTN_KV_HEADS, nb),
        _attn_specs(n_ctx, nb, grp, n_qh) + [pl.BlockSpec((1, SUBLANES, LANES), lambda h, b: (h, 0, 0))],
        [pl.BlockSpec((blk, grp * hd), lambda h, b: (b, h)), pl.BlockSpec((blk, LANES), lambda h, b: (b, h))],
        [jax.ShapeDtypeStruct((t, n_qh * hd), BF16), jax.ShapeDtypeStruct((t, ATTN_KV_HEADS * LANES), F32)],
        [], (qkvr,) * 9 + (sinks,), ("parallel", "arbitrary"), carry=carry)


def attention_bwd(do, o, lse, qkvr, sinks, n_ctx, n_qh, name, carry=None):
    t = qkvr.shape[0]
    hd, blk = ATTN_HEAD_DIM, CHUNK
    grp = n_qh // ATTN_KV_HEADS
    nb, nctx_b = t // blk, n_ctx // blk
    scale = hd ** -0.5
    kvw = ATTN_KV_HEADS * hd

    def body(do_ref, o_ref, lse_ref, q_ref, kc_ref, kp_ref, ko_ref, kn_ref, vc_ref, vp_ref, vo_ref, vn_ref, s_ref,
             dq_ref, dkc_ref, dvc_ref, dkp_ref, dvp_ref, dsk_ref):
        b = pl.program_id(1)
        k_all = jnp.concatenate([kc_ref[...], kp_ref[...], ko_ref[...], kn_ref[...]], axis=0)
        v_all = jnp.concatenate([vc_ref[...], vp_ref[...], vo_ref[...], vn_ref[...]], axis=0)
        bias = _attn_bias(b, n_ctx, nctx_b, nb)
        lane = lax.broadcasted_iota(jnp.int32, (1, LANES), 1)
        dk_all = jnp.zeros(k_all.shape, F32)
        dv_all = jnp.zeros(v_all.shape, F32)
        dsk = jnp.zeros((1, LANES), F32)
        for g in range(grp):
            sl = slice(g * hd, (g + 1) * hd)
            q = q_ref[:, sl]
            dof = do_ref[:, sl]
            dob = dof.astype(BF16)
            lse = lse_ref[:, g:g + 1]
            delta = jnp.sum(dof * o_ref[:, sl].astype(F32), axis=-1, keepdims=True)
            p = jnp.exp(_dot(q, k_all, _NT) * scale + bias - lse)
            ds = (p * (_dot(dob, v_all, _NT) - delta)).astype(BF16)
            dq_ref[:, sl] = _dot(ds, k_all) * scale
            dk_all += _dot(ds, q, _TN)
            dv_all += _dot(p.astype(BF16), dob, _TN)
            p_sink = jnp.exp(s_ref[0, 0:1, g:g + 1] - lse)
            dsk = dsk + jnp.where(lane == g, -jnp.sum(p_sink * delta, axis=0, keepdims=True), 0.0)

        @pl.when(b == 0)
        def _():
            dkc_ref[...] = jnp.zeros_like(dkc_ref)
            dvc_ref[...] = jnp.zeros_like(dvc_ref)
            dsk_ref[...] = jnp.zeros_like(dsk_ref)

        dkc_ref[...] += dk_all[:n_ctx] * scale
        dvc_ref[...] += dv_all[:n_ctx]
        dsk_ref[0, 0:1, :] += dsk
        for x in range(3):
            rows = slice(n_ctx + x * blk, n_ctx + (x + 1) * blk)
            dkp_ref[0, x] = dk_all[rows] * scale
            dvp_ref[0, x] = dv_all[rows]

    part = pl.BlockSpec((1, 3, blk, hd), lambda h, b: (b, 0, 0, h))
    ctxo = pl.BlockSpec((n_ctx, hd), lambda h, b: (0, h))
    return _pallas(
        body, name, (ATTN_KV_HEADS, nb),
        [pl.BlockSpec((blk, grp * hd), lambda h, b: (b, h)), pl.BlockSpec((blk, grp * hd), lambda h, b: (b, h)),
         pl.BlockSpec((blk, LANES), lambda h, b: (b, h))] + _attn_specs(n_ctx, nb, grp, n_qh)
        + [pl.BlockSpec((1, SUBLANES, LANES), lambda h, b: (h, 0, 0))],
        [pl.BlockSpec((blk, grp * hd), lambda h, b: (b, h)), ctxo, ctxo, part, part,
         pl.BlockSpec((1, SUBLANES, LANES), lambda h, b: (h, 0, 0))],
        [jax.ShapeDtypeStruct((t, n_qh * hd), F32), jax.ShapeDtypeStruct((n_ctx, kvw), F32), jax.ShapeDtypeStruct((n_ctx, kvw), F32),
         jax.ShapeDtypeStruct((nb, 3, blk, kvw), F32), jax.ShapeDtypeStruct((nb, 3, blk, kvw), F32),
         jax.ShapeDtypeStruct((ATTN_KV_HEADS, SUBLANES, LANES), F32)],
        [], (do, o, lse) + (qkvr,) * 9 + (sinks,), ("parallel", "arbitrary"), carry=carry)


def band_reduce(ctx_part, band_part, n_ctx, name):
    nb, _, blk, w = band_part.shape
    nctx_b = n_ctx // blk

    def body(c_ref, p_ref, o_ref, n_ref, out_ref):
        b = pl.program_id(0)
        band = p_ref[0, 0] + o_ref[0, 0] + jnp.where(b + 1 <= nb - 1, n_ref[0, 0], 0.0)
        out_ref[...] = jnp.where(b < nctx_b, c_ref[...], band)

    return pl.pallas_call(
        body, name=name, grid=(nb,),
        in_specs=[pl.BlockSpec((blk, w), lambda b: (jnp.minimum(b, nctx_b - 1), 0)),
                  pl.BlockSpec((1, 1, blk, w), lambda b: (jnp.maximum(b - 1, 0), 2, 0, 0)),
                  pl.BlockSpec((1, 1, blk, w), lambda b: (b, 1, 0, 0)),
                  pl.BlockSpec((1, 1, blk, w), lambda b: (jnp.minimum(b + 1, nb - 1), 0, 0, 0))],
        out_specs=pl.BlockSpec((blk, w), lambda b: (b, 0)), out_shape=jax.ShapeDtypeStruct((nb * blk, w), F32),
        compiler_params=_cp(),
    )(ctx_part, band_part, band_part, band_part)


def loss_grad(xf, target, n_ctx, name):
    t, d = xf.shape
    tm = _pick(n_ctx, (ROW_TILE, 128))
    nct = n_ctx // tm

    def body(x_ref, t_ref, dy_ref, s_ref):
        i = pl.program_id(0)
        err = jnp.where(i < nct, 0.0, x_ref[...] - t_ref[...])
        dy_ref[...] = err * (1.0 / d)
        _acc_rows(s_ref, i, [jnp.sum(err * err, axis=0, keepdims=True)])

    return pl.pallas_call(
        body, name=name, grid=(t // tm,),
        in_specs=[pl.BlockSpec((tm, d), lambda i: (i, 0)), pl.BlockSpec((tm, d), lambda i: (jnp.maximum(i - nct, 0), 0))],
        out_specs=[pl.BlockSpec((tm, d), lambda i: (i, 0)), pl.BlockSpec((SUBLANES, d), lambda i: (0, 0))],
        out_shape=[jax.ShapeDtypeStruct((t, d), F32), jax.ShapeDtypeStruct((SUBLANES, d), F32)],
        compiler_params=_cp(dimension_semantics=("arbitrary",)),
    )(xf, target)


def adamw(w, g, m, v, name):
    r, c = w.shape
    tr = r
    while tr % 2 == 0 and tr * c * 4 > (1 << 20) and (tr // 2) % SUBLANES == 0:
        tr //= 2
    bc1, bc2 = 1.0 - ADAM_B1 ** ADAM_STEP, 1.0 - ADAM_B2 ** ADAM_STEP

    def body(w_ref, g_ref, m_ref, v_ref, d_ref, nm_ref, nv_ref):
        gv = g_ref[...]
        nm = ADAM_B1 * m_ref[...] + (1.0 - ADAM_B1) * gv
        nv = ADAM_B2 * v_ref[...] + (1.0 - ADAM_B2) * (gv * gv)
        nm_ref[...] = nm
        nv_ref[...] = nv
        d_ref[...] = -ADAM_LR * ((nm / bc1) / (jnp.sqrt(nv / bc2) + ADAM_EPS) + ADAM_WD * w_ref[...])

    blk = pl.BlockSpec((tr, c), lambda i: (i, 0))
    return pl.pallas_call(
        body, name=name, grid=(r // tr,), in_specs=[blk] * 4, out_specs=[blk] * 3,
        out_shape=[jax.ShapeDtypeStruct((r, c), F32)] * 3, compiler_params=_cp(dimension_semantics=("parallel",)),
    )(w, g, m, v)


ADA_ROWS = 16


def ada_fwd(cs, w, name):
    l, d, ns = w.shape
    tn = _pick(ns, (512, 256, 128))

    def body(c_ref, w_ref, o_ref):
        o_ref[0] = _dot(c_ref[...], w_ref[0].astype(BF16))

    return pl.pallas_call(
        body, name=name, grid=(l, ns // tn),
        in_specs=[pl.BlockSpec((ADA_ROWS, d), lambda i, j: (0, 0)), pl.BlockSpec((1, d, tn), lambda i, j: (i, 0, j))],
        out_specs=pl.BlockSpec((1, ADA_ROWS, tn), lambda i, j: (i, 0, j)),
        out_shape=jax.ShapeDtypeStruct((l, ADA_ROWS, ns), F32), compiler_params=_cp(),
    )(cs, w)


def ada_bwd(cs, gmod, w, name):
    l, d, ns = w.shape
    tn = _pick(ns, (512, 256, 128))

    def body(c_ref, g_ref, w_ref, dw_ref, dc_ref):
        first = (pl.program_id(0) == 0) & (pl.program_id(1) == 0)
        gb = g_ref[0].astype(BF16)
        dw_ref[0] = _dot(c_ref[...], gb, _TN)
        part = _dot(gb, w_ref[0].astype(BF16), _NT)

        @pl.when(first)
        def _():
            dc_ref[...] = part

        @pl.when(jnp.logical_not(first))
        def _():
            dc_ref[...] += part

    return pl.pallas_call(
        body, name=name, grid=(l, ns // tn),
        in_specs=[pl.BlockSpec((ADA_ROWS, d), lambda i, j: (0, 0)), pl.BlockSpec((1, ADA_ROWS, tn), lambda i, j: (i, 0, j)),
                  pl.BlockSpec((1, d, tn), lambda i, j: (i, 0, j))],
        out_specs=[pl.BlockSpec((1, d, tn), lambda i, j: (i, 0, j)), pl.BlockSpec((ADA_ROWS, d), lambda i, j: (0, 0))],
        out_shape=[jax.ShapeDtypeStruct((l, d, ns), F32), jax.ShapeDtypeStruct((ADA_ROWS, d), F32)],
        compiler_params=_cp(dimension_semantics=("arbitrary", "arbitrary")),
    )(cs, gmod, w)


def sum_leading(a, name):
    k, r, c = a.shape
    tr = _pick(r, (256, 128, 64, 32, 16, 8))

    def body(a_ref, o_ref):
        acc = a_ref[0]
        for q in range(1, k):
            acc = acc + a_ref[q]
        o_ref[...] = acc

    return pl.pallas_call(
        body, name=name, grid=(r // tr,), in_specs=[pl.BlockSpec((k, tr, c), lambda i: (0, i, 0))],
        out_specs=pl.BlockSpec((tr, c), lambda i: (i, 0)), out_shape=jax.ShapeDtypeStruct((r, c), F32), compiler_params=_cp(),
    )(a)


def _mesh_pos():
    return lax.axis_index("x"), lax.axis_index("y"), lax.axis_index("c")


def _other_chips(x, y):
    return [(1 - x, y), (x, 1 - y), (1 - x, 1 - y)]


def _rcopy(src, dst, send_sems, recv_sems, k, to):
    return pltpu.make_async_remote_copy(src_ref=src, dst_ref=dst, send_sem=send_sems.at[k], recv_sem=recv_sems.at[k],
                                        device_id=to, device_id_type=MESH)


def small_allgather(vs, name):
    nv = len(vs)

    def body(*refs):
        v_refs, out_refs = refs[:nv], refs[nv:2 * nv]
        send_sems, recv_sems, local_sems = refs[2 * nv:]
        x, y, c = _mesh_pos()
        sibling = (x, y, 1 - c)
        chips = _other_chips(x, y)

        def blk(q, px, py, pc):
            return out_refs[q].at[4 * px + 2 * py + pc]

        mine = [pltpu.make_async_copy(v_refs[q], blk(q, x, y, c), local_sems.at[q]) for q in range(nv)]
        first, passed = [], []
        for q in range(nv):
            mine[q].start()
            first.append(_rcopy(v_refs[q], blk(q, x, y, c), send_sems, recv_sems, 7 * q, sibling))
            first += [_rcopy(v_refs[q], blk(q, x, y, c), send_sems, recv_sems, 7 * q + 1 + j, (*chip, c)) for j, chip in enumerate(chips)]
        for cp in first:
            cp.start()
        for q in range(nv):
            for j, chip in enumerate(chips):
                _rcopy(blk(q, *chip, c), blk(q, *chip, c), send_sems, recv_sems, 7 * q + 1 + j, (x, y, c)).wait_recv()
                passed.append(_rcopy(blk(q, *chip, c), blk(q, *chip, c), send_sems, recv_sems, 7 * q + 4 + j, sibling))
                passed[-1].start()
        for q in range(nv):
            _rcopy(blk(q, x, y, 1 - c), blk(q, x, y, 1 - c), send_sems, recv_sems, 7 * q, (x, y, c)).wait_recv()
            for j, chip in enumerate(chips):
                _rcopy(blk(q, *chip, 1 - c), blk(q, *chip, 1 - c), send_sems, recv_sems, 7 * q + 4 + j, (x, y, c)).wait_recv()
        for cp in first + passed:
            cp.wait_send()
        for cp in mine:
            cp.wait()

    vm = pl.BlockSpec(memory_space=pltpu.VMEM)
    return pl.pallas_call(
        body, name=name, out_shape=[jax.ShapeDtypeStruct((N_DEV, *v.shape), v.dtype) for v in vs],
        in_specs=[vm] * nv, out_specs=[vm] * nv,
        scratch_shapes=[pltpu.SemaphoreType.DMA((7 * nv,)), pltpu.SemaphoreType.DMA((7 * nv,)), pltpu.SemaphoreType.DMA((nv,))],
        compiler_params=_cp(),
    )(*vs)


_HBM = pl.BlockSpec(memory_space=pltpu.HBM)


STREAM_TILE_BYTES = 2 * 1024 * 1024


def _stream_rows(rows, row_bytes):
    tr = 16
    while rows % (2 * tr) == 0 and 2 * tr * row_bytes <= STREAM_TILE_BYTES:
        tr *= 2
    assert rows % tr == 0
    return tr


def _scalars(*vals):
    return jnp.stack([jnp.asarray(v, jnp.int32) for v in vals])


def place_own(w, chip, name):
    l, k, ns = w.shape
    tk = _pick(k, (256, 128, 64))

    def body(s_ref, w_ref, o_ref):
        o_ref[...] = w_ref[...].astype(BF16)

    grid_spec = pltpu.PrefetchScalarGridSpec(
        num_scalar_prefetch=1, grid=(l, k // tk),
        in_specs=[pl.BlockSpec((None, tk, ns), lambda i, j, s: (i, j, 0))],
        out_specs=pl.BlockSpec((None, None, tk, ns), lambda i, j, s: (i, s[0], j, 0)))
    return pl.pallas_call(body, name=name, grid_spec=grid_spec, out_shape=jax.ShapeDtypeStruct((l, N_CHIP, k, ns), BF16),
                          compiler_params=_cp())(_scalars(chip), w)


def _half(ref, layer, px, py, pc):
    hk = ref.shape[2] // 2
    return ref.at[layer, 2 * px + py, pl.ds(pc * hk, hk)]


def _chip_sends(refs, items, send_sems, recv_sems):
    x, y, c = _mesh_pos()
    return [_rcopy(_half(refs[b], l, x, y, c), _half(refs[b], l, x, y, c), send_sems, recv_sems, 3 * q + j, (*chip, c))
            for q, (b, l) in enumerate(items) for j, chip in enumerate(_other_chips(x, y))]


def _chip_recv_waits(refs, items, send_sems, recv_sems):
    x, y, c = _mesh_pos()
    for q, (b, l) in enumerate(items):
        for j, chip in enumerate(_other_chips(x, y)):
            _rcopy(_half(refs[b], l, *chip, c), _half(refs[b], l, *chip, c), send_sems, recv_sems, 3 * q + j, (x, y, c)).wait_recv()


def _sibling_forward(refs, items, send_sems, recv_sems):
    x, y, c = _mesh_pos()
    passed = [_rcopy(_half(refs[b], l, *chip, c), _half(refs[b], l, *chip, c), send_sems, recv_sems, 3 * q + j, (x, y, 1 - c))
              for q, (b, l) in enumerate(items) for j, chip in enumerate(_other_chips(x, y))]
    for cp in passed:
        cp.start()
    for q, (b, l) in enumerate(items):
        for j, chip in enumerate(_other_chips(x, y)):
            _rcopy(_half(refs[b], l, *chip, 1 - c), _half(refs[b], l, *chip, 1 - c), send_sems, recv_sems, 3 * q + j, (x, y, c)).wait_recv()
    for cp in passed:
        cp.wait_send()


def _inplace_comm_call(body, bufs, n_sems, name):
    nb = len(bufs)
    return pl.pallas_call(
        body, name=name, out_shape=[jax.ShapeDtypeStruct(b.shape, b.dtype) for b in bufs],
        in_specs=[_HBM] * nb, out_specs=[_HBM] * nb, input_output_aliases={q: q for q in range(nb)},
        scratch_shapes=[pltpu.SemaphoreType.DMA((n_sems,))] * 4, compiler_params=_cp(),
    )(*bufs)


def weights_allgather(bufs, items, name):
    nb = len(bufs)

    def body(*refs):
        out_refs = refs[nb:2 * nb]
        s1, r1, s2, r2 = refs[2 * nb:]
        sends = _chip_sends(out_refs, items, s1, r1)
        for cp in sends:
            cp.start()
        _chip_recv_waits(out_refs, items, s1, r1)
        _sibling_forward(out_refs, items, s2, r2)
        for cp in sends:
            cp.wait_send()

    return _inplace_comm_call(body, bufs, 3 * len(items), name)


def weights_forward(bufs, items, name):
    nb = len(bufs)

    def body(*refs):
        s1, r1, _, _ = refs[2 * nb:]
        _sibling_forward(refs[nb:2 * nb], items, s1, r1)

    return _inplace_comm_call(body, bufs, 3 * len(items), name)


def grads_pair_exchange(gs, items, name):
    ng, ni = len(gs), len(items)

    def body(*refs):
        g_refs, out_refs = refs[:ng], refs[ng:ng + ni]
        send_sems, recv_sems = refs[ng + ni:]
        x, y, c = _mesh_pos()
        cps = []
        for q, (a, row0, nrows) in enumerate(items):
            h = nrows // 2
            cps.append(_rcopy(g_refs[a].at[:, pl.ds(row0 + (1 - c) * h, h)], out_refs[q], send_sems, recv_sems, q, (x, y, 1 - c)))
            cps[-1].start()
        for cp in cps:
            cp.wait()

    return pl.pallas_call(
        body, name=name, out_shape=[jax.ShapeDtypeStruct((N_CHIP, nrows // 2, gs[a].shape[2]), BF16) for a, _, nrows in items],
        in_specs=[_HBM] * ng, out_specs=[_HBM] * ni,
        scratch_shapes=[pltpu.SemaphoreType.DMA((ni,)), pltpu.SemaphoreType.DMA((ni,))], compiler_params=_cp(),
    )(*gs)


def pair_add(g, got, c, chip, row0, name):
    n, h, c_ = got.shape
    tr = _stream_rows(math.gcd(h, row0) if row0 else h, n * c_ * 2)
    nblk = h // tr

    def body(s_ref, g_ref, o_ref, pair_ref, land_ref):
        pair_ref[...] = (g_ref[...].astype(F32) + o_ref[...].astype(F32)).astype(BF16)
        me = s_ref[1]
        land_ref[...] = (g_ref[me].astype(F32) + o_ref[me].astype(F32)).astype(BF16)

    grid_spec = pltpu.PrefetchScalarGridSpec(
        num_scalar_prefetch=1, grid=(nblk,),
        in_specs=[pl.BlockSpec((n, tr, c_), lambda i, s: (0, row0 // tr + s[0] * nblk + i, 0)), pl.BlockSpec((n, tr, c_), lambda i, s: (0, i, 0))],
        out_specs=[pl.BlockSpec((n, tr, c_), lambda i, s: (0, i, 0)), pl.BlockSpec((None, tr, c_), lambda i, s: (s[1], i, 0))])
    return pl.pallas_call(body, name=name, grid_spec=grid_spec, out_shape=[jax.ShapeDtypeStruct((n, h, c_), BF16)] * 2,
                          compiler_params=_cp())(_scalars(c, chip), g, got)


def _rs_sends(refs, send_sems, recv_sems):
    ng = len(refs) // 2
    x, y, c = _mesh_pos()
    return [_rcopy(refs[q].at[2 * px + py], refs[ng + q].at[2 * x + y], send_sems, recv_sems, 3 * q + j, (px, py, c))
            for q in range(ng) for j, (px, py) in enumerate(_other_chips(x, y))]


def _rs_recv_waits(refs, send_sems, recv_sems):
    ng = len(refs) // 2
    x, y, c = _mesh_pos()
    for q in range(ng):
        for j, (px, py) in enumerate(_other_chips(x, y)):
            _rcopy(refs[q].at[2 * x + y], refs[ng + q].at[2 * px + py], send_sems, recv_sems, 3 * q + j, (x, y, c)).wait_recv()


def rs_carry(pairs, lands):
    return (list(pairs) + list(lands), 3 * len(pairs), _rs_sends, _rs_recv_waits)


def gather_carry(bufs, items):
    return (list(bufs), 3 * len(items), lambda refs, ss, rs: _chip_sends(refs, items, ss, rs),
            lambda refs, ss, rs: _chip_recv_waits(refs, items, ss, rs))


def grads_chip_exchange(pairs, lands, name):
    ng = len(pairs)

    def body(*refs):
        cbufs, sems = refs[2 * ng:4 * ng], refs[4 * ng:]
        sends = _rs_sends(cbufs, *sems)
        for cp in sends:
            cp.start()
        _rs_recv_waits(cbufs, *sems)
        for cp in sends:
            cp.wait_send()

    res = pl.pallas_call(
        body, name=name, out_shape=[jax.ShapeDtypeStruct(a.shape, a.dtype) for a in list(pairs) + list(lands)],
        in_specs=[_HBM] * (2 * ng), out_specs=[_HBM] * (2 * ng), input_output_aliases={q: q for q in range(2 * ng)},
        scratch_shapes=[pltpu.SemaphoreType.DMA((3 * ng,)), pltpu.SemaphoreType.DMA((3 * ng,))], compiler_params=_cp(),
    )(*pairs, *lands)
    return res[ng:]


def sum_chips(a, c, into, shape, row0, name):
    k, h, c_ = a.shape
    tr = _stream_rows(math.gcd(h, row0) if row0 else h, k * c_ * 2)
    nblk = h // tr

    def body(s_ref, a_ref, *rest):
        acc = a_ref[0].astype(F32)
        for q in range(1, k):
            acc = acc + a_ref[q].astype(F32)
        rest[-1][...] = acc

    in_specs, args, aliases = [pl.BlockSpec((k, tr, c_), lambda i, s: (0, i, 0))], [_scalars(c), a], {}
    if into is not None:
        in_specs.append(pl.BlockSpec(memory_space=pl.ANY))
        args.append(into)
        aliases = {2: 0}
    grid_spec = pltpu.PrefetchScalarGridSpec(
        num_scalar_prefetch=1, grid=(nblk,), in_specs=in_specs,
        out_specs=pl.BlockSpec((tr, c_), lambda i, s: (row0 // tr + s[0] * nblk + i, 0)))
    return pl.pallas_call(body, name=name, grid_spec=grid_spec, out_shape=jax.ShapeDtypeStruct(shape, F32),
                          input_output_aliases=aliases, compiler_params=_cp())(*args)


def halves_exchange(outs, items, name):
    ng = len(outs)

    def body(*refs):
        out_refs = refs[ng:2 * ng]
        send_sems, recv_sems = refs[2 * ng:]
        x, y, c = _mesh_pos()
        rows = lambda a, row0, nrows, half: out_refs[a].at[pl.ds(row0 + half * (nrows // 2), nrows // 2)]
        cps = [_rcopy(rows(*it, c), rows(*it, c), send_sems, recv_sems, q, (x, y, 1 - c)) for q, it in enumerate(items)]
        for cp in cps:
            cp.start()
        for q, it in enumerate(items):
            _rcopy(rows(*it, 1 - c), rows(*it, 1 - c), send_sems, recv_sems, q, (x, y, c)).wait_recv()
        for cp in cps:
            cp.wait_send()

    return pl.pallas_call(
        body, name=name, out_shape=[jax.ShapeDtypeStruct(a.shape, a.dtype) for a in outs],
        in_specs=[_HBM] * ng, out_specs=[_HBM] * ng, input_output_aliases={q: q for q in range(ng)},
        scratch_shapes=[pltpu.SemaphoreType.DMA((len(items),)), pltpu.SemaphoreType.DMA((len(items),))], compiler_params=_cp(),
    )(*outs)


def _rows8(a):
    return jnp.pad(a, ((0, -a.shape[0] % SUBLANES), (0, 0)))


def _chips_cols(g, rows):
    return jnp.concatenate([g[2 * j, :rows] for j in range(N_CHIP)], axis=-1)


BIG = (("ssd_w_in", "col"), ("ssd_w_out", "row"), ("attn_w_qkv", "col"), ("attn_w_o", "row"), ("ffn_w_up", "col"), ("ffn_w_down", "row"))
WEIGHTS = ("c_ctx", "ada_w", "ada_b", "norm1_w", "norm2_w", "ssd_w_in", "ssd_conv_w", "ssd_conv_b", "ssd_dt_bias_f", "ssd_dt_bias_b",
           "ssd_a_log_f", "ssd_a_log_b", "ssd_d", "ssd_norm_w", "ssd_w_out", "attn_w_qkv", "attn_q_gain", "attn_k_gain", "attn_sinks",
           "attn_w_o", "ffn_w_up", "ffn_conv_w", "ffn_conv_b", "ffn_w_down")


def _taps_bias(w3, b):
    return jnp.concatenate([w3, b[None, :], jnp.zeros((SUBLANES - 4, w3.shape[1]), F32)], axis=0)


def kernel(x, c, ctx, c_ctx, ada_w, ada_b, norm1_w, norm2_w, ssd_w_in, ssd_conv_w, ssd_conv_b, ssd_dt_bias_f, ssd_dt_bias_b, ssd_a_log_f, ssd_a_log_b, ssd_d, ssd_norm_w, ssd_w_out, attn_w_qkv, attn_q_gain, attn_k_gain, attn_sinks, attn_w_o, ffn_w_up, ffn_conv_w, ffn_conv_b, ffn_w_down, loss_target, m_c_ctx, m_ada_w, m_ada_b, m_norm1_w, m_norm2_w, m_ssd_w_in, m_ssd_conv_w, m_ssd_conv_b, m_ssd_dt_bias_f, m_ssd_dt_bias_b, m_ssd_a_log_f, m_ssd_a_log_b, m_ssd_d, m_ssd_norm_w, m_ssd_w_out, m_attn_w_qkv, m_attn_q_gain, m_attn_k_gain, m_attn_sinks, m_attn_w_o, m_ffn_w_up, m_ffn_conv_w, m_ffn_conv_b, m_ffn_w_down, v_c_ctx, v_ada_w, v_ada_b, v_norm1_w, v_norm2_w, v_ssd_w_in, v_ssd_conv_w, v_ssd_conv_b, v_ssd_dt_bias_f, v_ssd_dt_bias_b, v_ssd_a_log_f, v_ssd_a_log_b, v_ssd_d, v_ssd_norm_w, v_ssd_w_out, v_attn_w_qkv, v_attn_q_gain, v_attn_k_gain, v_attn_sinks, v_attn_w_o, v_ffn_w_up, v_ffn_conv_w, v_ffn_conv_b, v_ffn_w_down):
    args = locals()
    w = {n: args[n] for n in WEIGHTS}
    mom = {n: args["m_" + n] for n in WEIGHTS}
    var = {n: args["v_" + n] for n in WEIGHTS}

    ix, iy, ic = _mesh_pos()
    chip = 2 * ix + iy
    dev = 2 * chip + ic
    depth, d = norm1_w.shape
    n_ctx, seq = ctx.shape[1], x.shape[1]
    t = n_ctx + seq
    d_inner = ssd_norm_w.shape[1]
    heads = ssd_d.shape[1]
    n_qh = attn_sinks.shape[1]
    grp = n_qh // ATTN_KV_HEADS
    d_ff = ffn_w_down.shape[1] * N_CHIP
    xbc_w = ssd_conv_b.shape[1]
    dt_col = d_inner + xbc_w
    n_ssd, n_att = ssd_w_in.shape[0], attn_w_qkv.shape[0]

    sconv_rows, fconv_rows = n_ssd * 3, depth * 3
    g_c, g_sconv, g_fconv = small_allgather(
        [_rows8(c), _rows8(ssd_conv_w.reshape(sconv_rows, -1)), _rows8(ffn_conv_w.reshape(fconv_rows, -1))], "gather_cond")
    c_all = g_c[:, 0]
    ssd_conv_full = _chips_cols(g_sconv, sconv_rows).reshape(n_ssd, 3, -1)
    ffn_conv_full = _chips_cols(g_fconv, fconv_rows).reshape(depth, 3, -1)

    cvec = jnp.concatenate([c_all, c_ctx[None, :], jnp.zeros((ADA_ROWS - N_DEV - 1, d), F32)], axis=0)
    cs16 = _silu(cvec).astype(BF16)
    mod_cols = ada_fwd(cs16, ada_w, "ada_fwd")
    ns_ada = mod_cols.shape[-1]
    (g_mod,) = small_allgather([mod_cols.reshape(depth * ADA_ROWS, ns_ada)], "gather_mod")
    mod_all = _chips_cols(g_mod, depth * ADA_ROWS).reshape(depth, ADA_ROWS, -1) + ada_b[:, None, :]
    mod_lat = lax.dynamic_index_in_dim(mod_all, dev, axis=1, keepdims=False)
    mod_ctx = mod_all[:, N_DEV]
    mods = jnp.stack([mod_ctx, mod_lat], axis=1).reshape(depth, 2, 6, d)

    bidx = {n: q for q, (n, _) in enumerate(BIG)}
    bufs = [place_own(w[n], chip, f"place_{n}") for n, _ in BIG]

    def layer_items(layer):
        mixer = ("ssd_w_in", "ssd_w_out") if layer % 2 == 0 else ("attn_w_qkv", "attn_w_o")
        return [(bidx[n], layer // 2) for n in mixer] + [(bidx["ffn_w_up"], layer), (bidx["ffn_w_down"], layer)]

    def ssd_in_full(layer):
        return bufs[bidx["ssd_w_in"]][layer].transpose(1, 0, 2).reshape(d, -1)

    def w_col(n, layer):
        b = bufs[bidx[n]]
        return Mat(b.reshape(-1, *b.shape[2:]), "cols3", base=layer * N_CHIP, nparts=N_CHIP)

    def w_row(n, layer):
        b = bufs[bidx[n]]
        rows = N_CHIP * b.shape[2]
        return Mat(b.reshape(-1, b.shape[3]), "rows", rows=rows, row0=layer * rows)

    def subset(items):
        used = sorted({b for b, _ in items})
        return [bufs[b] for b in used], [(used.index(b), l) for b, l in items], used

    def put_back(used, new):
        for b, a in zip(used, new):
            bufs[b] = a

    sub, its, used = subset(layer_items(0)[:1])
    put_back(used, weights_allgather(sub, its, "gather_weights_first"))

    cos, sin = rope_tables(n_ctx, seq)
    sel = _group_select(heads, heads // SSD_GROUPS)
    bias128 = jnp.concatenate([ssd_dt_bias_f, ssd_dt_bias_b], axis=-1)[:, None, :]
    alog128 = jnp.concatenate([ssd_a_log_f, ssd_a_log_b], axis=-1)[:, None, :]
    dskip = jnp.repeat(ssd_d, SSD_HEAD_DIM, axis=-1)[:, None, :]
    gains = jnp.zeros((n_att, SUBLANES, ATTN_HEAD_DIM), F32).at[:, 0].set(attn_q_gain).at[:, 1].set(attn_k_gain)
    sinks3 = jnp.zeros((n_att, ATTN_KV_HEADS, SUBLANES, LANES), F32).at[:, :, 0, :grp].set(attn_sinks.reshape(n_att, ATTN_KV_HEADS, grp))
    wb_ssd = [_taps_bias(ssd_conv_full[j], ssd_conv_b[j]) for j in range(n_ssd)]
    wb_ffn = [_taps_bias(ffn_conv_full[i], ffn_conv_b[i]) for i in range(depth)]

    xs = jnp.concatenate([ctx[0], x[0]], axis=0)
    saved = []
    for i in range(depth):
        j = i // 2
        sh1, sc1, g1, sh2, sc2, g2 = [mods[i, :, q] for q in range(6)]
        s = {"x": xs}
        h1 = norm_mod(xs, norm1_w[i:i + 1], sh1, sc1, n_ctx, f"l{i}_norm1")
        s["h1"] = h1
        if i % 2 == 0:
            w_in = ssd_in_full(j)
            if i == 0:
                sub, its, used = subset(layer_items(0)[1:])
                zx, carried = matmul(h1, w_in, "nn", f"l{i}_ssd_in", carry=gather_carry(sub, its))
                put_back(used, weights_forward(carried, its, "forward_weights_l0"))
            else:
                zx = matmul(h1, w_in, "nn", f"l{i}_ssd_in")
        sub, its, used = subset(layer_items(i + 1)) if i + 1 < depth else (None, None, None)
        nxt = gather_carry(sub, its) if sub is not None else None
        if i % 2 == 0:
            xbc = dwconv_act(zx, d_inner, xbc_w, wb_ssd[j], n_ctx, "silu", F32, f"l{i}_ssd_conv")
            _, da, dtg, ag, agt = ssd_prep(zx, dt_col, bias128[j], alog128[j], sel, heads, f"l{i}_ssd_prep")
            yf, hf, *carried = ssd_scan(xbc, dtg, ag, agt, d_inner, n_ctx, False, f"l{i}_ssd_scan_f", carry=nxt)
            if nxt is not None:
                put_back(used, weights_forward(carried, its, f"forward_weights_l{i + 1}"))
            yb, hb = ssd_scan(xbc, dtg, ag, agt, d_inner, n_ctx, True, f"l{i}_ssd_scan_b")
            ytot, yn = ssd_gate_norm(yf, yb, xbc, zx, dskip[j], ssd_norm_w[j:j + 1], f"l{i}_ssd_gate_norm")
            x1, mix = matmul_gate_res(yn, w_row("ssd_w_out", j), xs, g1, n_ctx, f"l{i}_ssd_out")
            s.update(w_in=w_in, zx=zx, xbc=xbc, da=da, dtg=dtg, ag=ag, agt=agt, hf=hf, hb=hb, ytot=ytot, yn=yn)
        else:
            qkv = matmul(h1, w_col("attn_w_qkv", j), "nn", f"l{i}_attn_qkv")
            qkvr = qk_prep(qkv, gains[j], cos, sin, n_qh, ATTN_KV_HEADS, f"l{i}_qk_prep")
            o, lse, *carried = attention(qkvr, sinks3[j], n_ctx, n_qh, f"l{i}_attn", carry=nxt)
            if nxt is not None:
                put_back(used, weights_forward(carried, its, f"forward_weights_l{i + 1}"))
            x1, mix = matmul_gate_res(o, w_row("attn_w_o", j), xs, g1, n_ctx, f"l{i}_attn_out")
            s.update(qkv=qkv, qkvr=qkvr, o=o, lse=lse)
        h2 = norm_mod(x1, norm2_w[i:i + 1], sh2, sc2, n_ctx, f"l{i}_norm2")
        u = matmul(h2, w_col("ffn_w_up", i), "nn", f"l{i}_ffn_up", out_dtype=BF16)
        act = dwconv_act(u, 0, 2 * d_ff, wb_ffn[i], n_ctx, "glu", BF16, f"l{i}_ffn_conv")
        x2, f = matmul_gate_res(act, w_row("ffn_w_down", i), x1, g2, n_ctx, f"l{i}_ffn_down")
        s.update(mix=mix, x1=x1, h2=h2, u=u, act=act, f=f)
        saved.append(s)
        xs = x2

    dxs, sq = loss_grad(xs, loss_target[0], n_ctx, "loss")
    loss = lax.psum(0.5 / d * jnp.sum(sq[0]), ("x", "y", "c"))

    gbuf = {n: None for n, _ in BIG}
    gshape = {n: ((N_CHIP, b.shape[0] * b.shape[2], b.shape[3]), b.shape[2]) for (n, _), b in zip(BIG, bufs)}

    def dw_into(n, kind, layer, a, b, name):
        shape, rows_per_layer = gshape[n]
        gbuf[n] = matmul(a, b, "tn", name, out_dtype=BF16, into=(kind, gbuf[n], shape, layer * rows_per_layer))

    ssd_in_g = [None] * n_ssd
    reduced, reduced_items = {}, []

    def start_rs(layer):
        gs, items, dest = [], [], []
        for b, l in layer_items(layer):
            n = BIG[b][0]
            if n == "ssd_w_in":
                gs.append(ssd_in_g[l].reshape(d, N_CHIP, -1).transpose(1, 0, 2).astype(BF16))
                items.append((len(gs) - 1, 0, d))
                dest.append(((n, l), (d, gs[-1].shape[2]), 0))
            else:
                shape, rows_per_layer = gshape[n]
                gs.append(gbuf[n])
                items.append((len(gs) - 1, l * rows_per_layer, rows_per_layer))
                dest.append((n, shape[1:], l * rows_per_layer))
        got = grads_pair_exchange(gs, items, f"rs_pair_exchange_l{layer}")
        both = [pair_add(gs[a], o, ic, chip, row0, f"rs_pair_add_l{layer}_{q}") for q, ((a, row0, _), o) in enumerate(zip(items, got))]
        return [p for p, _ in both], [ld for _, ld in both], dest

    def finish_rs(pend, carried):
        if not pend:
            return None
        landed = carried[len(pend[0]):]
        for q, (a, (key, shape, row0)) in enumerate(zip(landed, pend[2])):
            reduced[key] = sum_chips(a, ic, reduced.get(key), shape, row0, f"rs_chip_sum_{q}_r{row0}_{key if isinstance(key, str) else key[0] + str(key[1])}")
            reduced_items.append((key, row0, 2 * a.shape[1]))
        return None

    pending = None
    st_norm1, st_norm2, st_gate1, st_gate2 = ([None] * depth for _ in range(4))
    st_sconv, st_snorm, st_sd, st_sdt = ([None] * n_ssd for _ in range(4))
    st_gain, st_sink = [None] * n_att, [None] * n_att
    st_fconv = [None] * depth
    for i in reversed(range(depth)):
        j = i // 2
        s = saved[i]
        sh1, sc1, g1, sh2, sc2, g2 = [mods[i, :, q] for q in range(6)]
        df, st_gate2[i] = gate_bwd(dxs, s["f"], g2, n_ctx, f"l{i}_ffn_gate_bwd")
        dact = matmul(df, w_row("ffn_w_down", i), "nt", f"l{i}_ffn_down_dx")
        dw_into("ffn_w_down", "row", i, s["act"], df, f"l{i}_ffn_down_dw")
        du3, st_fconv[i] = dwconv_act_bwd(dact, 0, s["u"], 0, wb_ffn[i], 0, d_ff, n_ctx, "glu", f"l{i}_ffn_conv_bwd")
        du = Mat(du3, "cols3", nparts=2)
        dh2 = matmul(du, w_col("ffn_w_up", i), "nt", f"l{i}_ffn_up_dx")
        dw_into("ffn_w_up", "col", i, s["h2"], du, f"l{i}_ffn_up_dw")
        dx1, st_norm2[i] = norm_mod_bwd(dh2, dxs, s["x1"], norm2_w[i:i + 1], sc2, n_ctx, f"l{i}_norm2_bwd")
        dmix, st_gate1[i] = gate_bwd(dx1, s["mix"], g1, n_ctx, f"l{i}_mix_gate_bwd")
        if i % 2 == 0:
            zx = s["zx"]
            dyn = matmul(dmix, w_row("ssd_w_out", j), "nt", f"l{i}_ssd_out_dx")
            dw_into("ssd_w_out", "row", j, s["yn"], dmix, f"l{i}_ssd_out_dw")
            dy, dzx, st_snorm[j] = ssd_gate_norm_bwd(dyn, s["ytot"], zx, ssd_norm_w[j:j + 1], f"l{i}_ssd_gate_norm_bwd")
            o1 = ssd_scan_bwd(dy, s["xbc"], s["dtg"], s["ag"], s["agt"], s["hf"], dskip[j], None, d_inner, n_ctx, False, f"l{i}_ssd_scan_f_bwd",
                              carry=rs_carry(*pending[:2]) if pending else None)
            o1, pending = o1[:7], finish_rs(pending, o1[7:])
            o2 = ssd_scan_bwd(dy, s["xbc"], s["dtg"], s["ag"], s["agt"], s["hb"], None, o1[:3], d_inner, n_ctx, True, f"l{i}_ssd_scan_b_bwd")
            st_sd[j] = o1[6]
            gn = SSD_GROUPS * SSD_STATE
            conv_st = []
            for src, width, col, tag in ((o2[0], d_inner, 0, "x"), (o2[1], gn, d_inner, "b"), (o2[2], gn, d_inner + gn, "c")):
                dzx, st = dwconv_act_bwd(src, 0, zx, d_inner + col, wb_ssd[j], col, width, n_ctx, "silu",
                                         f"l{i}_ssd_conv_bwd_{tag}", into=dzx, ocol0=d_inner + col)
                conv_st.append(st)
            st_sconv[j] = jnp.concatenate(conv_st, axis=1)
            dzx, st_sdt[j] = ssd_prep_bwd(zx, dt_col, bias128[j], alog128[j], sel, o1[3:6], o2[3:6], s["da"], heads, dzx, f"l{i}_ssd_prep_bwd")
            dh1 = matmul(dzx, s["w_in"], "nt", f"l{i}_ssd_in_dx")
            ssd_in_g[j] = matmul(s["h1"], dzx, "tn", f"l{i}_ssd_in_dw")
        else:
            do = matmul(dmix, w_row("attn_w_o", j), "nt", f"l{i}_attn_out_dx")
            dw_into("attn_w_o", "row", j, s["o"], dmix, f"l{i}_attn_out_dw")
            ab = attention_bwd(do, s["o"], s["lse"], s["qkvr"], sinks3[j], n_ctx, n_qh, f"l{i}_attn_bwd",
                               carry=rs_carry(*pending[:2]) if pending else None)
            (dq, dkc, dvc, dkp, dvp, st_sink[j]), pending = ab[:6], finish_rs(pending, ab[6:])
            dk = band_reduce(dkc, dkp, n_ctx, f"l{i}_dk_reduce")
            dv = band_reduce(dvc, dvp, n_ctx, f"l{i}_dv_reduce")
            dqkv, st_gain[j] = qk_prep_bwd(dq, dk, dv, s["qkv"], gains[j], cos, sin, f"l{i}_qk_prep_bwd")
            dh1 = matmul(dqkv, w_col("attn_w_qkv", j), "nt", f"l{i}_attn_qkv_dx")
            dw_into("attn_w_qkv", "col", j, s["h1"], dqkv, f"l{i}_attn_qkv_dw")
        dxs, st_norm1[i] = norm_mod_bwd(dh1, dx1, s["x"], norm1_w[i:i + 1], sc1, n_ctx, f"l{i}_norm1_bwd")
        pending = start_rs(i)
    pairs, lands, _ = pending
    finish_rs(pending, list(pairs) + list(grads_chip_exchange(pairs, lands, "rs_chip_exchange_l0")))
    grad_x = dxs[n_ctx:][None]

    rows_d = ([st_norm1[i][4:5] for i in range(depth)] + [st_norm2[i][4:5] for i in range(depth)]
              + [st[seg:seg + 1] for seg in (0, 1) for i in range(depth)
                 for st in (st_norm1[i][0:2], st_norm1[i][2:4], st_gate1[i][0:2], st_norm2[i][0:2], st_norm2[i][2:4], st_gate2[i][0:2])])
    a_d = jnp.concatenate(rows_d, axis=0)
    a_sconv = _rows8(jnp.concatenate([st[0:4] for st in st_sconv], axis=0))
    a_fconv = _rows8(jnp.concatenate([jnp.concatenate([st[0, 0:4], st[1, 0:4]], axis=1) for st in st_fconv], axis=0))
    a_di = _rows8(jnp.concatenate([st[0:1] for st in st_snorm] + [st[0:1] for st in st_sd], axis=0))
    a_128 = _rows8(jnp.concatenate([st[0:2] for st in st_sdt] + [st[0:2] for st in st_gain] + [st[:, 0] for st in st_sink], axis=0))
    gathered = small_allgather([a_d, a_sconv, a_fconv, a_di, a_128], "gather_small_grads")
    s_d, s_sconv, s_fconv, s_di, s_128 = [sum_leading(g, f"sum_small_grads_{q}") for q, g in enumerate(gathered)]
    grads = {"norm1_w": s_d[0:depth], "norm2_w": s_d[depth:2 * depth]}
    dctx_sum = s_d[2 * depth:8 * depth].reshape(depth, 6 * d)
    grads["ada_b"] = dctx_sum + s_d[8 * depth:14 * depth].reshape(depth, 6 * d)
    sc = s_sconv[:4 * n_ssd].reshape(n_ssd, 4, -1)
    own_cols = lambda a, width: lax.dynamic_slice_in_dim(a, chip * width, width, axis=a.ndim - 1)
    grads["ssd_conv_w"], grads["ssd_conv_b"] = own_cols(sc[:, 0:3], ssd_conv_w.shape[-1]), sc[:, 3]
    fc = s_fconv[:4 * depth].reshape(depth, 4, -1)
    grads["ffn_conv_w"], grads["ffn_conv_b"] = own_cols(fc[:, 0:3], ffn_conv_w.shape[-1]), fc[:, 3]
    grads["ssd_norm_w"] = s_di[0:n_ssd]
    grads["ssd_d"] = jnp.sum(s_di[n_ssd:2 * n_ssd].reshape(n_ssd, heads, SSD_HEAD_DIM), axis=-1)
    dt_st = s_128[0:2 * n_ssd].reshape(n_ssd, 2, LANES)
    grads["ssd_dt_bias_f"], grads["ssd_dt_bias_b"] = dt_st[:, 0, :heads], dt_st[:, 0, heads:]
    grads["ssd_a_log_f"], grads["ssd_a_log_b"] = dt_st[:, 1, :heads], dt_st[:, 1, heads:]
    gain_st = s_128[2 * n_ssd:2 * n_ssd + 2 * n_att].reshape(n_att, 2, LANES)
    grads["attn_q_gain"], grads["attn_k_gain"] = gain_st[:, 0], gain_st[:, 1]
    sink_st = s_128[2 * n_ssd + 2 * n_att:2 * n_ssd + 2 * n_att + ATTN_KV_HEADS * n_att].reshape(n_att, ATTN_KV_HEADS, LANES)
    grads["attn_sinks"] = sink_st[:, :, :grp].reshape(n_att, n_qh)

    dlat_rows = gathered[0][:, 8 * depth:14 * depth].reshape(N_DEV, depth, 6 * d)
    gmod = jnp.concatenate([dlat_rows, dctx_sum[None], jnp.zeros((ADA_ROWS - N_DEV - 1, depth, 6 * d), F32)], axis=0).transpose(1, 0, 2)
    grads["ada_w"], dcs = ada_bwd(cs16, own_cols(gmod, ns_ada), ada_w, "ada_bwd")
    dcc = (dcs[N_DEV] * _dsilu(c_ctx))[None, :]
    (g_dcc,) = small_allgather([_rows8(dcc)], "gather_dc_ctx")
    grads["c_ctx"] = sum_leading(g_dcc[0::2], "sum_dc_ctx")[0]

    keys = list(reduced)
    red = dict(zip(keys, halves_exchange([reduced[k] for k in keys], [(keys.index(k), r0, nr) for k, r0, nr in reduced_items],
                                         "rs_halves_exchange")))
    for n, _ in BIG:
        grads[n] = jnp.stack([red[(n, l)] for l in range(n_ssd)]) if n == "ssd_w_in" else red[n].reshape(w[n].shape)

    delta, new_m, new_v = {}, {}, {}
    for n in WEIGHTS:
        shp = w[n].shape
        two = lambda a: a.reshape(-1, shp[-1])
        dl, nm, nv = adamw(two(w[n]), two(grads[n]), two(mom[n]), two(var[n]), f"adamw_{n}")
        delta[n], new_m[n], new_v[n] = dl.reshape(shp), nm.reshape(shp), nv.reshape(shp)
    grads = {n: grads[n].reshape(w[n].shape) for n in WEIGHTS}
    return (loss, grad_x, *[grads[n] for n in WEIGHTS], *[delta[n] for n in WEIGHTS], *[new_m[n] for n in WEIGHTS], *[new_v[n] for n in WEIGHTS])
```
